```python
import jax, jax.numpy as jnp
from jax import lax
import numpy as np

D_MODEL = 1024
BATCH = 4
SEQ = 8192
DEPTH = 2

GRID_W = 64
HEAD_DIM = 64
ROPE_THETA = 10000.0
PLE_DIM = 256
NORM_EPS = 1e-6
NEG_INF = -1e30
Q_BLOCK = 128

A_HEADS = 4
A_Q_RANK = 256
A_KV_RANK = 128
A_NOPE = 64
A_ROPE = 32
A_V = 64
B_HEADS = 4
B_KV_HEADS = 2
C_PATTERNS = ((128, 1), (512, 4), (2048, 16))
C_HEADS_PER_GROUP = 4
C_BLOCK = 64
D_HEADS = 4
D_KV_HEADS = 2
D_HALF_WINDOW = 128
D_BLOCK = 128

N_BRANCHES = 4
BRANCH_WIDTH = 256

A_COLS = A_Q_RANK + A_KV_RANK + A_ROPE
B_COLS = (B_HEADS + 2 * B_KV_HEADS) * HEAD_DIM
C_COLS = len(C_PATTERNS) * 3 * C_HEADS_PER_GROUP * HEAD_DIM
D_COLS = (D_HEADS + 2 * D_KV_HEADS) * HEAD_DIM
GATE_COLS = N_BRANCHES * D_MODEL
IN_COLS = A_COLS + B_COLS + C_COLS + D_COLS + GATE_COLS
SPLIT_POINTS = (A_COLS, A_COLS + B_COLS, A_COLS + B_COLS + C_COLS, A_COLS + B_COLS + C_COLS + D_COLS)

D_FF = 4096
N_EXPERTS = 8
TOP_K = 2
MOE_FF = 3584
MOE_BLOCK = 128
N_DENSE = (DEPTH + 1) // 2
N_MOE = DEPTH // 2

kernel_name = "hybrid_gated_parallel_mixers_encoder"


def rms_norm(x, g):
    xf = x.astype(jnp.float32)
    y = xf * lax.rsqrt(jnp.mean(xf * xf, axis=-1, keepdims=True) + NORM_EPS)
    return (y * g.astype(jnp.float32)).astype(x.dtype)


def rope_tables(pos, dim):
    half = dim // 2
    inv = jnp.power(ROPE_THETA, -jnp.arange(half, dtype=jnp.float32) / half)
    ang = pos.astype(jnp.float32)[:, None] * inv[None, :]
    return jnp.cos(ang), jnp.sin(ang)


def apply_rope(x, cos, sin):
    half = x.shape[-1] // 2
    x1 = x[..., :half].astype(jnp.float32)
    x2 = x[..., half:].astype(jnp.float32)
    c = cos[:, None, :]
    s = sin[:, None, :]
    return jnp.concatenate([x1 * c - x2 * s, x2 * c + x1 * s], axis=-1).astype(x.dtype)


def axial_rope(x, row_tab, col_tab):
    half = x.shape[-1] // 2
    return jnp.concatenate([apply_rope(x[..., :half], *row_tab), apply_rope(x[..., half:], *col_tab)], axis=-1)


def dense_attention(q, k, v):
    bsz, seq, n_heads, dq = q.shape
    n_kv = k.shape[2]
    rep = n_heads // n_kv
    dv = v.shape[-1]
    nb = seq // Q_BLOCK
    qb = q.reshape(bsz, nb, Q_BLOCK, n_kv, rep, dq).transpose(1, 0, 2, 3, 4, 5)
    scale = dq ** -0.5

    def block(qblk):
        s = jnp.einsum('bqgrd,bkgd->bgrqk', qblk, k).astype(jnp.float32) * scale
        p = jax.nn.softmax(s, axis=-1).astype(v.dtype)
        return jnp.einsum('bgrqk,bkgd->bqgrd', p, v)

    o = lax.map(block, qb)
    return o.transpose(1, 0, 2, 3, 4, 5).reshape(bsz, seq, n_heads, dv)


def banded_attention(q, k, v, half_w, blk, sink):
    bsz, length, n_heads, dim = q.shape
    n_kv = k.shape[2]
    rep = n_heads // n_kv
    nb = -(-length // blk)
    lp = nb * blk
    qp = jnp.pad(q, ((0, 0), (0, lp - length), (0, 0), (0, 0)))
    kv_pad = ((0, 0), (half_w, lp - length + half_w), (0, 0), (0, 0))
    kp = jnp.pad(k, kv_pad)
    vp = jnp.pad(v, kv_pad)
    kw = blk + 2 * half_w
    idx = jnp.arange(nb)[:, None] * blk + jnp.arange(kw)[None, :]
    kb = kp[:, idx]
    vb = vp[:, idx]
    qb = qp.reshape(bsz, nb, blk, n_kv, rep, dim)
    s = jnp.einsum('bnqgrd,bnkgd->bngrqk', qb, kb).astype(jnp.float32) * (dim ** -0.5)
    qpos = jnp.arange(nb)[:, None] * blk + jnp.arange(blk)[None, :]
    kpos = idx - half_w
    valid = ((jnp.abs(qpos[:, :, None] - kpos[:, None, :]) <= half_w)
             & (kpos[:, None, :] >= 0) & (kpos[:, None, :] < length))
    s = jnp.where(valid[None, :, None, None], s, NEG_INF)
    m = jnp.max(s, axis=-1)
    if sink is not None:
        sink_logit = sink.astype(jnp.float32).reshape(n_kv, rep)[None, None, :, :, None]
        m = jnp.maximum(m, sink_logit)
    e = jnp.exp(s - m[..., None])
    l = jnp.sum(e, axis=-1)
    if sink is not None:
        l = l + jnp.exp(sink_logit - m)
    o = jnp.einsum('bngrqk,bnkgd->bnqgrd', e.astype(v.dtype), vb).astype(jnp.float32)
    o = o / jnp.transpose(l, (0, 1, 4, 2, 3))[..., None]
    o = o.reshape(bsz, lp, n_heads, dim)[:, :length].astype(q.dtype)
    lse = jnp.transpose(m + jnp.log(l), (0, 1, 4, 2, 3)).reshape(bsz, lp, n_heads)[:, :length]
    return o, lse


def dilated_attention(q, k, v, window, dilation):
    bsz, seq, n_heads, dim = q.shape
    ls = seq // dilation

    def to_classes(x):
        return x.reshape(bsz, ls, dilation, n_heads, dim).transpose(0, 2, 1, 3, 4).reshape(bsz * dilation, ls, n_heads, dim)

    o, lse = banded_attention(to_classes(q), to_classes(k), to_classes(v), window // (2 * dilation), C_BLOCK, None)
    o = o.reshape(bsz, dilation, ls, n_heads, dim).transpose(0, 2, 1, 3, 4).reshape(bsz, seq, n_heads, dim)
    lse = lse.reshape(bsz, dilation, ls, n_heads).transpose(0, 2, 1, 3).reshape(bsz, seq, n_heads)
    return o, lse


def token_mixers(h, tabs, w_in, a_qa_g, a_kva_g, a_w_uq, a_w_ukv, b_q_g, b_k_g, d_sink, w_branch, w_out):
    bsz, seq, _ = h.shape
    rope_a, rope_row, rope_col, rope_full = tabs
    z = h @ w_in
    z_a, z_b, z_c, z_d, z_gate = jnp.split(z, list(SPLIT_POINTS), axis=-1)

    c_q, c_kv, k_r = jnp.split(z_a, [A_Q_RANK, A_Q_RANK + A_KV_RANK], axis=-1)
    q_a = (rms_norm(c_q, a_qa_g) @ a_w_uq).reshape(bsz, seq, A_HEADS, A_NOPE + A_ROPE)
    q_a = jnp.concatenate([q_a[..., :A_NOPE], apply_rope(q_a[..., A_NOPE:], *rope_a)], axis=-1)
    kv_a = (rms_norm(c_kv, a_kva_g) @ a_w_ukv).reshape(bsz, seq, A_HEADS, A_NOPE + A_V)
    k_rope = apply_rope(k_r[:, :, None, :], *rope_a)
    k_a = jnp.concatenate([kv_a[..., :A_NOPE], jnp.broadcast_to(k_rope, (bsz, seq, A_HEADS, A_ROPE))], axis=-1)
    o_a = dense_attention(q_a, k_a, kv_a[..., A_NOPE:])

    q_b, k_b, v_b = jnp.split(z_b, [B_HEADS * HEAD_DIM, (B_HEADS + B_KV_HEADS) * HEAD_DIM], axis=-1)
    q_b = axial_rope(rms_norm(q_b.reshape(bsz, seq, B_HEADS, HEAD_DIM), b_q_g), rope_row, rope_col)
    k_b = axial_rope(rms_norm(k_b.reshape(bsz, seq, B_KV_HEADS, HEAD_DIM), b_k_g), rope_row, rope_col)
    o_b = dense_attention(q_b, k_b, v_b.reshape(bsz, seq, B_KV_HEADS, HEAD_DIM))

    zc = z_c.reshape(bsz, seq, len(C_PATTERNS), 3, C_HEADS_PER_GROUP, HEAD_DIM)
    outs, lses = [], []
    for gi, (win, dil) in enumerate(C_PATTERNS):
        og, lg = dilated_attention(apply_rope(zc[:, :, gi, 0], *rope_full), apply_rope(zc[:, :, gi, 1], *rope_full),
                                   zc[:, :, gi, 2], win, dil)
        outs.append(og)
        lses.append(lg)
    alpha = jax.nn.softmax(jnp.stack(lses, axis=0), axis=0)
    o_c = jnp.einsum('gbsh,gbshd->bshd', alpha, jnp.stack(outs, axis=0).astype(jnp.float32)).astype(h.dtype)

    q_d, k_d, v_d = jnp.split(z_d, [D_HEADS * HEAD_DIM, (D_HEADS + D_KV_HEADS) * HEAD_DIM], axis=-1)
    q_d = apply_rope(q_d.reshape(bsz, seq, D_HEADS, HEAD_DIM), *rope_full)
    k_d = apply_rope(k_d.reshape(bsz, seq, D_KV_HEADS, HEAD_DIM), *rope_full)
    o_d, _ = banded_attention(q_d, k_d, v_d.reshape(bsz, seq, D_KV_HEADS, HEAD_DIM), D_HALF_WINDOW, D_BLOCK, d_sink)

    gates = jax.nn.sigmoid(z_gate.astype(jnp.float32)).astype(h.dtype).reshape(bsz, seq, N_BRANCHES, D_MODEL)
    merged = jnp.zeros((bsz, seq, D_MODEL), h.dtype)
    for n, o in enumerate((o_a, o_b, o_c, o_d)):
        merged = merged + gates[:, :, n] * (o.reshape(bsz, seq, BRANCH_WIDTH) @ w_branch[n])
    return merged @ w_out


def swiglu(h, wg, wu, wd):
    return (jax.nn.silu(h @ wg) * (h @ wu)) @ wd


def moe_swiglu(h, w_router, b_router, wg, wu, wd):
    bsz, seq, dm = h.shape
    n_tok = bsz * seq
    xf = h.reshape(n_tok, dm)
    logits = (xf @ w_router).astype(jnp.float32) + b_router.astype(jnp.float32)
    top_v, top_e = lax.top_k(logits, TOP_K)
    gate = jax.nn.softmax(top_v, axis=-1)
    n_asg = n_tok * TOP_K
    e_flat = top_e.reshape(n_asg)
    tok_flat = jnp.repeat(jnp.arange(n_tok, dtype=jnp.int32), TOP_K)
    g_flat = gate.reshape(n_asg)
    order = jnp.argsort(e_flat)
    e_s, tok_s, g_s = e_flat[order], tok_flat[order], g_flat[order]
    counts = jnp.bincount(e_flat, length=N_EXPERTS)
    start = jnp.cumsum(counts) - counts
    padded = ((counts + MOE_BLOCK - 1) // MOE_BLOCK) * MOE_BLOCK
    pend = jnp.cumsum(padded)
    pstart = pend - padded
    dest = pstart[e_s] + (jnp.arange(n_asg) - start[e_s])
    nb = -(-n_asg // MOE_BLOCK) + N_EXPERTS
    n_rows = nb * MOE_BLOCK
    buf_tok = jnp.zeros((n_rows,), jnp.int32).at[dest].set(tok_s)
    buf_g = jnp.zeros((n_rows,), jnp.float32).at[dest].set(g_s)
    blk_e = jnp.clip(jnp.searchsorted(pend, jnp.arange(nb) * MOE_BLOCK, side='right'), 0, N_EXPERTS - 1)
    xb = xf[buf_tok].reshape(nb, MOE_BLOCK, dm)

    def expert_block(args):
        xblk, e = args
        return swiglu(xblk, wg[e], wu[e], wd[e])

    yb = lax.map(expert_block, (xb, blk_e)).reshape(n_rows, dm)
    out = jnp.zeros((n_tok, dm), h.dtype).at[buf_tok].add(yb * buf_g[:, None].astype(h.dtype))
    return out.reshape(bsz, seq, dm)


def setup_inputs(seed: int = 0) -> dict:
    key = jax.random.key(seed)
    ks = iter(jax.random.split(key, 32))
    f32 = jnp.float32

    def w(shape, fan_in):
        return jax.random.normal(next(ks), shape, f32) * (fan_in ** -0.5)

    def gain(shape):
        return 1.0 + 0.05 * jax.random.normal(next(ks), shape, f32)

    return {
        "x": jax.random.normal(next(ks), (BATCH, SEQ, D_MODEL), f32),
        "p": jax.random.normal(next(ks), (DEPTH, BATCH, SEQ, PLE_DIM), f32),
        "w_in": w((DEPTH, D_MODEL, IN_COLS), D_MODEL),
        "a_qa_g": gain((DEPTH, A_Q_RANK)),
        "a_kva_g": gain((DEPTH, A_KV_RANK)),
        "a_w_uq": w((DEPTH, A_Q_RANK, A_HEADS * (A_NOPE + A_ROPE)), A_Q_RANK),
        "a_w_ukv": w((DEPTH, A_KV_RANK, A_HEADS * (A_NOPE + A_V)), A_KV_RANK),
        "b_q_g": gain((DEPTH, HEAD_DIM)),
        "b_k_g": gain((DEPTH, HEAD_DIM)),
        "d_sink": 0.5 * jax.random.normal(next(ks), (DEPTH, D_HEADS), f32),
        "w_branch": w((DEPTH, N_BRANCHES, BRANCH_WIDTH, D_MODEL), BRANCH_WIDTH),
        "w_out": w((DEPTH, D_MODEL, D_MODEL), D_MODEL),
        "mix_pre_g": gain((DEPTH, D_MODEL)),
        "mix_post_g": gain((DEPTH, D_MODEL)),
        "ffn_pre_g": gain((DEPTH, D_MODEL)),
        "ffn_post_g": gain((DEPTH, D_MODEL)),
        "ffn_w_gate": w((N_DENSE, D_MODEL, D_FF), D_MODEL),
        "ffn_w_up": w((N_DENSE, D_MODEL, D_FF), D_MODEL),
        "ffn_w_down": w((N_DENSE, D_FF, D_MODEL), D_FF),
        "router_w": w((N_MOE, D_MODEL, N_EXPERTS), D_MODEL),
        "router_b": 0.01 * jax.random.normal(next(ks), (N_MOE, N_EXPERTS), f32),
        "moe_w_gate": w((N_MOE, N_EXPERTS, D_MODEL, MOE_FF), D_MODEL),
        "moe_w_up": w((N_MOE, N_EXPERTS, D_MODEL, MOE_FF), D_MODEL),
        "moe_w_down": w((N_MOE, N_EXPERTS, MOE_FF, D_MODEL), MOE_FF),
        "ple_w_proj": w((DEPTH, PLE_DIM, D_MODEL), PLE_DIM),
        "ple_w_gate": w((DEPTH, D_MODEL, D_MODEL), D_MODEL),
        "ple_post_g": gain((DEPTH, D_MODEL)),
    }


def reference(x, p, w_in, a_qa_g, a_kva_g, a_w_uq, a_w_ukv, b_q_g, b_k_g, d_sink, w_branch, w_out,
              mix_pre_g, mix_post_g, ffn_pre_g, ffn_post_g, ffn_w_gate, ffn_w_up, ffn_w_down,
              router_w, router_b, moe_w_gate, moe_w_up, moe_w_down, ple_w_proj, ple_w_gate, ple_post_g):
    seq = x.shape[1]
    rows = seq // GRID_W
    pos = jnp.arange(seq, dtype=jnp.int32)
    row_idx = jnp.repeat(jnp.arange(rows, dtype=jnp.int32), GRID_W)
    col_idx = jnp.tile(jnp.arange(GRID_W, dtype=jnp.int32), rows)
    tabs = (rope_tables(pos, A_ROPE), rope_tables(row_idx, HEAD_DIM // 2),
            rope_tables(col_idx, HEAD_DIM // 2), rope_tables(pos, HEAD_DIM))

    for i in range(DEPTH):
        h = rms_norm(x, mix_pre_g[i])
        mix = token_mixers(h, tabs, w_in[i], a_qa_g[i], a_kva_g[i], a_w_uq[i], a_w_ukv[i],
                           b_q_g[i], b_k_g[i], d_sink[i], w_branch[i], w_out[i])
        x = x + rms_norm(mix, mix_post_g[i])

        h = rms_norm(x, ffn_pre_g[i])
        j = i // 2
        if i % 2 == 0:
            f = swiglu(h, ffn_w_gate[j], ffn_w_up[j], ffn_w_down[j])
        else:
            f = moe_swiglu(h, router_w[j], router_b[j], moe_w_gate[j], moe_w_up[j], moe_w_down[j])
        x = x + rms_norm(f, ffn_post_g[i])

        gate = jax.nn.sigmoid((x @ ple_w_gate[i]).astype(jnp.float32)).astype(x.dtype)
        e = (p[i] @ ple_w_proj[i]) * gate
        x = x + rms_norm(e, ple_post_g[i])
    return x
```

```python
import functools

import jax
import jax.numpy as jnp
from jax import lax
from jax.experimental import pallas as pl
from jax.experimental.pallas import tpu as pltpu

F32 = jnp.float32
BF16 = jnp.bfloat16

GRID_W = 64
HEAD_DIM = 64
ROPE_THETA = 10000.0
NORM_EPS = 1e-6
NEG_INF = -1e30
A_HEADS = 4
A_Q_RANK = 256
A_KV_RANK = 128
A_NOPE = 64
A_ROPE = 32
A_V = 64
C_PATTERNS = ((128, 1), (512, 4), (2048, 16))
D_HALF_WINDOW = 128
N_BRANCHES = 4
N_EXPERTS = 8
TOP_K = 2

LANES = 128
HALF = 64
VMEM_MB = 1024 * 1024

ZB = 0
ZC = 768
ZD = 768 + 3 * 768
ZMAIN_COLS = ZD + 768
A_SEG = 512
W_IN_COLS = A_SEG + ZMAIN_COLS


def _cparams(sem, vmem_mb):
    return pltpu.CompilerParams(dimension_semantics=sem, vmem_limit_bytes=vmem_mb * VMEM_MB)


def _resident(shape):
    nd = len(shape)
    return pl.BlockSpec(shape, lambda *_: (0,) * nd, pipeline_mode=pl.Buffered(1))


def _rms(xf, g):
    return xf * lax.rsqrt(jnp.mean(xf * xf, axis=-1, keepdims=True) + NORM_EPS) * g


def _sigmoid(x):
    return 1.0 / (1.0 + jnp.exp(-x))


def _swap32(a):
    lane = lax.broadcasted_iota(jnp.int32, a.shape, 1)
    fwd = pltpu.roll(a, LANES - 32, 1)
    bwd = pltpu.roll(a, 32, 1)
    return jnp.where((lane & 32) == 0, fwd, bwd)


def _rope(a, cos, sin):
    outs = []
    for c in range(a.shape[1] // LANES):
        ch = a[:, c * LANES:(c + 1) * LANES]
        outs.append(ch * cos + _swap32(ch) * sin)
    return outs[0] if len(outs) == 1 else jnp.concatenate(outs, axis=1)


def _head_norm(a, g, bd):
    outs = []
    for c in range(a.shape[1] // LANES):
        ch = a[:, c * LANES:(c + 1) * LANES]
        sq = ch * ch
        hi = sq.astype(BF16)
        lo = (sq - hi.astype(F32)).astype(BF16)
        ms = (jnp.dot(hi, bd, preferred_element_type=F32)
              + jnp.dot(lo, bd, preferred_element_type=F32))
        outs.append(ch * lax.rsqrt(ms + NORM_EPS) * g)
    return outs[0] if len(outs) == 1 else jnp.concatenate(outs, axis=1)


def _in_proj_kernel(x_ref, g_ref, w_ref, tabf_ref, tabx_ref, taba_ref, tabq_ref,
                    bqg_ref, bkg_ref, aqg_ref, akvg_ref, wuq_ref, wk_ref, wv_ref,
                    z_ref, qa_ref, ka_ref, va_ref):
    h = _rms(x_ref[...], g_ref[...]).astype(BF16)
    cf, sf = tabf_ref[0], tabf_ref[1]
    cx, sx = tabx_ref[0], tabx_ref[1]
    q_scale = HEAD_DIM ** -0.5

    acc = jnp.dot(h, w_ref[:, 0:A_SEG], preferred_element_type=F32)
    nq = _rms(acc[:, 0:A_Q_RANK], aqg_ref[...]).astype(BF16)
    nkv = _rms(acc[:, A_Q_RANK:A_Q_RANK + A_KV_RANK], akvg_ref[...]).astype(BF16)
    kr = _rope(acc[:, 384:512], taba_ref[0], taba_ref[1]).astype(BF16)
    qa = jnp.dot(nq, wuq_ref[...], preferred_element_type=F32)
    qa = _rope(qa, tabq_ref[0], tabq_ref[1]) * ((A_NOPE + A_ROPE) ** -0.5)
    qa_ref[...] = qa.astype(qa_ref.dtype)
    r = lax.broadcasted_iota(jnp.int32, (LANES, A_HEADS * LANES), 0)
    c = lax.broadcasted_iota(jnp.int32, (LANES, A_HEADS * LANES), 1)
    place = jnp.where((r < HALF) & ((c & (LANES - 1)) == r + HALF), 1.0, 0.0).astype(BF16)
    ka = (jnp.dot(nkv, wk_ref[...], preferred_element_type=F32)
          + jnp.dot(kr, place, preferred_element_type=F32))
    ka_ref[...] = ka.astype(ka_ref.dtype)
    va_ref[...] = jnp.dot(nkv, wv_ref[...], preferred_element_type=F32).astype(va_ref.dtype)

    rr = lax.broadcasted_iota(jnp.int32, (LANES, LANES), 0)
    cc = lax.broadcasted_iota(jnp.int32, (LANES, LANES), 1)
    bd = jnp.where((rr >> 6) == (cc >> 6), 1.0 / HEAD_DIM, 0.0).astype(BF16)
    base = A_SEG + ZB
    acc = jnp.dot(h, w_ref[:, base:base + 768], preferred_element_type=F32)
    q = _rope(_head_norm(acc[:, 0:256], bqg_ref[...], bd), cx, sx) * q_scale
    k = _rope(_head_norm(acc[:, 256:512], bkg_ref[...], bd), cx, sx)
    z_ref[:, ZB:ZB + 256] = q.astype(z_ref.dtype)
    z_ref[:, ZB + 256:ZB + 512] = k.astype(z_ref.dtype)
    z_ref[:, ZB + 512:ZB + 768] = acc[:, 512:768].astype(z_ref.dtype)

    for zoff in (ZC, ZC + 768, ZC + 1536, ZD):
        base = A_SEG + zoff
        acc = jnp.dot(h, w_ref[:, base:base + 768], preferred_element_type=F32)
        z_ref[:, zoff:zoff + 256] = (_rope(acc[:, 0:256], cf, sf) * q_scale).astype(z_ref.dtype)
        z_ref[:, zoff + 256:zoff + 512] = _rope(acc[:, 256:512], cf, sf).astype(z_ref.dtype)
        z_ref[:, zoff + 512:zoff + 768] = acc[:, 512:768].astype(z_ref.dtype)


def _in_proj(x2, gain, w, tabs, bqg, bkg, aqg, akvg, wuq, wk, wv, seq, tm):
    n, dm = x2.shape
    nt = seq // tm
    tab_spec = pl.BlockSpec((2, tm, LANES), lambda i: (0, i % nt, 0))
    row = lambda cols: pl.BlockSpec((tm, cols), lambda i: (i, 0))
    return pl.pallas_call(
        _in_proj_kernel,
        grid=(n // tm,),
        in_specs=[row(dm), _resident((1, dm)), _resident(w.shape),
                  tab_spec, tab_spec, tab_spec, tab_spec,
                  _resident((1, LANES)), _resident((1, LANES)),
                  _resident((1, A_Q_RANK)), _resident((1, A_KV_RANK)),
                  _resident(wuq.shape), _resident(wk.shape), _resident(wv.shape)],
        out_specs=[row(ZMAIN_COLS), row(A_HEADS * LANES), row(A_HEADS * LANES), row(A_HEADS * A_V)],
        out_shape=[jax.ShapeDtypeStruct((n, ZMAIN_COLS), BF16),
                   jax.ShapeDtypeStruct((n, A_HEADS * LANES), BF16),
                   jax.ShapeDtypeStruct((n, A_HEADS * LANES), BF16),
                   jax.ShapeDtypeStruct((n, A_HEADS * A_V), BF16)],
        compiler_params=_cparams(("parallel",), 48),
        name="in_proj",
    )(x2, gain, w, tabs["full"], tabs["axial"], tabs["a_k"], tabs["a_q"],
      bqg, bkg, aqg, akvg, wuq, wk, wv)


def _flash_kernel(q_ref, k_ref, vt_ref, o_ref, *, tk, packed):
    tq = q_ref.shape[0]
    nk = k_ref.shape[0] // tk
    lane = lax.broadcasted_iota(jnp.int32, (1, LANES), 1)
    halves = []
    for r in range(2):
        if packed:
            q = jnp.where((lane < HALF) == (r == 0), q_ref[...], 0)
            kc = 0
        else:
            q = q_ref[:, r * LANES:(r + 1) * LANES]
            kc = r * LANES

        def body(kk, carry, q=q, kc=kc, r=r):
            m, l, acc = carry
            ks = pl.multiple_of(kk * tk, tk)
            k = k_ref[pl.ds(ks, tk), kc:kc + LANES]
            vt = vt_ref[r * HALF:(r + 1) * HALF, pl.ds(ks, tk)]
            s = lax.dot_general(k, q, (((1,), (1,)), ((), ())), preferred_element_type=F32)
            m_new = jnp.maximum(m, jnp.max(s, axis=0, keepdims=True))
            alpha = jnp.exp(m - m_new)
            p = jnp.exp(s - m_new)
            l = alpha * l + jnp.sum(p, axis=0, keepdims=True)
            acc = alpha * acc + jnp.dot(vt, p.astype(BF16), preferred_element_type=F32)
            return m_new, l, acc

        m0 = jnp.full((1, tq), NEG_INF, F32)
        l0 = jnp.zeros((1, tq), F32)
        a0 = jnp.zeros((HALF, tq), F32)
        m, l, acc = lax.fori_loop(0, nk, body, (m0, l0, a0))
        halves.append(acc / l)
    o_ref[...] = jnp.concatenate(halves, axis=0).T.astype(o_ref.dtype)


def _flash(q, k, vt, *, q_col0, k_col0, v_row0, seq, packed, tq, tk):
    n = q.shape[0]
    bsz = n // seq
    qw = LANES if packed else 2 * LANES
    nq = seq // tq
    k3 = k.reshape(bsz, seq, k.shape[1])
    qb, kb, vb = q_col0 // qw, k_col0 // qw, v_row0 // LANES
    return pl.pallas_call(
        functools.partial(_flash_kernel, tk=tk, packed=packed),
        grid=(bsz, 2, nq),
        in_specs=[pl.BlockSpec((tq, qw), lambda b, j, i: (b * nq + i, qb + j)),
                  pl.BlockSpec((None, seq, qw), lambda b, j, i: (b, 0, kb + j)),
                  pl.BlockSpec((None, LANES, seq), lambda b, j, i: (b, vb + j, 0))],
        out_specs=pl.BlockSpec((tq, LANES), lambda b, j, i: (b * nq + i, j)),
        out_shape=jax.ShapeDtypeStruct((n, 2 * LANES), BF16),
        compiler_params=_cparams(("parallel", "parallel", "parallel"), 48),
        name="flash_packed" if packed else "flash_slots",
    )(q, k3, vt)


def _banded_kernel(*refs, hw, has_sink, want_lse):
    if has_sink:
        sink_ref, q_ref, k_ref, v_ref = refs[:4]
        outs = refs[4:]
    else:
        q_ref, k_ref, v_ref = refs[:3]
        outs = refs[3:]
    o_ref = outs[0]
    tq = q_ref.shape[0]
    length = k_ref.shape[0]
    win = min(tq + 2 * hw, length)
    j = pl.program_id(1)
    i = pl.program_id(2)
    ks = jnp.clip(i * tq - hw, 0, length - win)
    ks = pl.multiple_of(ks, HALF)
    kw = k_ref[pl.ds(ks, win), :]
    vw = v_ref[pl.ds(ks, win), :]
    qpos = i * tq + lax.broadcasted_iota(jnp.int32, (tq, win), 0)
    kpos = ks + lax.broadcasted_iota(jnp.int32, (tq, win), 1)
    valid = jnp.abs(qpos - kpos) <= hw
    lane = lax.broadcasted_iota(jnp.int32, (1, LANES), 1)
    o_halves, lse_halves = [], []
    for r in range(2):
        q = jnp.where((lane < HALF) == (r == 0), q_ref[...], 0)
        s = lax.dot_general(q, kw, (((1,), (1,)), ((), ())), preferred_element_type=F32)
        s = jnp.where(valid, s, NEG_INF)
        m = jnp.max(s, axis=-1, keepdims=True)
        if has_sink:
            sink = sink_ref[2 * j + r]
            m = jnp.maximum(m, sink)
        e = jnp.exp(s - m)
        l = jnp.sum(e, axis=-1, keepdims=True)
        if has_sink:
            l = l + jnp.exp(sink - m)
        o = jnp.dot(e.astype(BF16), vw, preferred_element_type=F32) / l
        o_halves.append(o)
        if want_lse:
            lse_halves.append(jnp.broadcast_to(m + jnp.log(l), (tq, LANES)))
    first = lane < HALF
    o_ref[...] = jnp.where(first, o_halves[0], o_halves[1]).astype(o_ref.dtype)
    if want_lse:
        outs[1][...] = jnp.where(first, lse_halves[0], lse_halves[1])


def _banded(q, k, v, *, q_col0, k_col0, v_col0, row_cols, dil, seq, hw, tq, sink=None,
            want_lse=False, out_dtype=BF16):
    bsz, ls, _ = q.shape
    nq = ls // tq
    rb = row_cols // LANES
    qb, kb, vb = q_col0 // LANES, k_col0 // LANES, v_col0 // LANES
    in_specs = [pl.BlockSpec((None, tq, LANES), lambda bc, j, i: (bc // dil, i, (bc % dil) * rb + qb + j)),
                pl.BlockSpec((None, ls, LANES), lambda bc, j, i: (bc // dil, 0, (bc % dil) * rb + kb + j)),
                pl.BlockSpec((None, ls, LANES), lambda bc, j, i: (bc // dil, 0, (bc % dil) * rb + vb + j))]
    args = [q, k, v]
    if sink is not None:
        in_specs = [pl.BlockSpec(memory_space=pltpu.SMEM)] + in_specs
        args = [sink] + args
    o_spec = pl.BlockSpec((None, tq, LANES), lambda bc, j, i: (bc // dil, i, (bc % dil) * 2 + j))
    out_specs = [o_spec]
    out_shape = [jax.ShapeDtypeStruct((bsz, ls, dil * 2 * LANES), out_dtype)]
    if want_lse:
        out_specs.append(o_spec)
        out_shape.append(jax.ShapeDtypeStruct((bsz, ls, dil * 2 * LANES), F32))
    res = pl.pallas_call(
        functools.partial(_banded_kernel, hw=hw, has_sink=sink is not None, want_lse=want_lse),
        grid=(bsz * dil, 2, nq),
        in_specs=in_specs,
        out_specs=out_specs,
        out_shape=out_shape,
        compiler_params=_cparams(("parallel", "parallel", "parallel"), 48),
        name="banded_d%d" % dil,
    )(*args)
    return res


def _merge_kernel(x_ref, oa_ref, ob_ref, oc0_ref, oc1_ref, oc2_ref, l0_ref, l1_ref, l2_ref,
                  od_ref, gpre_ref, wg_ref, wb_ref, wo_ref, gpost_ref, out_ref):
    xf = x_ref[...]
    dm = xf.shape[1]
    h = _rms(xf, gpre_ref[...]).astype(BF16)
    l0, l1, l2 = l0_ref[...], l1_ref[...], l2_ref[...]
    mx = jnp.maximum(jnp.maximum(l0, l1), l2)
    w0, w1, w2 = jnp.exp(l0 - mx), jnp.exp(l1 - mx), jnp.exp(l2 - mx)
    oc = (w0 * oc0_ref[...] + w1 * oc1_ref[...] + w2 * oc2_ref[...]) / (w0 + w1 + w2)
    branches = (oa_ref[...], ob_ref[...], oc.astype(BF16), od_ref[...])
    merged = None
    for n, o in enumerate(branches):
        gate = _sigmoid(jnp.dot(h, wg_ref[:, n * dm:(n + 1) * dm], preferred_element_type=F32))
        term = gate * jnp.dot(o, wb_ref[n], preferred_element_type=F32)
        merged = term if merged is None else merged + term
    y = jnp.dot(merged.astype(BF16), wo_ref[...], preferred_element_type=F32)
    out_ref[...] = xf + _rms(y, gpost_ref[...])


def _merge(x2, oa, ob, oc, lc, od, gpre, wg, wb, wo, gpost, tm):
    n, dm = x2.shape
    row = lambda cols: pl.BlockSpec((tm, cols), lambda i: (i, 0))
    bw = 2 * LANES
    return pl.pallas_call(
        _merge_kernel,
        grid=(n // tm,),
        in_specs=[row(dm), row(bw), row(bw), row(bw), row(bw), row(bw), row(bw), row(bw), row(bw),
                  row(bw), _resident((1, dm)), _resident(wg.shape), _resident(wb.shape),
                  _resident(wo.shape), _resident((1, dm))],
        out_specs=row(dm),
        out_shape=jax.ShapeDtypeStruct((n, dm), F32),
        compiler_params=_cparams(("parallel",), 56),
        name="merge",
    )(x2, oa, ob, oc[0], oc[1], oc[2], lc[0], lc[1], lc[2], od, gpre, wg, wb, wo, gpost)


def _ffn_kernel(x_ref, g_ref, wg_ref, wu_ref, wd_ref, f_ref, h_scr, acc_scr):
    j = pl.program_id(1)

    @pl.when(j == 0)
    def _():
        h_scr[...] = _rms(x_ref[...], g_ref[...]).astype(BF16)
        acc_scr[...] = jnp.zeros_like(acc_scr)

    h = h_scr[...]
    a = jnp.dot(h, wg_ref[...], preferred_element_type=F32)
    u = jnp.dot(h, wu_ref[...], preferred_element_type=F32)
    act = (a * _sigmoid(a) * u).astype(BF16)
    acc_scr[...] += jnp.dot(act, wd_ref[...], preferred_element_type=F32)

    @pl.when(j == pl.num_programs(1) - 1)
    def _():
        f_ref[...] = acc_scr[...]


def _ffn(x2, gain, wg, wu, wd, tm, tf):
    n, dm = x2.shape
    dff = wg.shape[1]
    return pl.pallas_call(
        _ffn_kernel,
        grid=(n // tm, dff // tf),
        in_specs=[pl.BlockSpec((tm, dm), lambda i, j: (i, 0)),
                  pl.BlockSpec((1, dm), lambda i, j: (0, 0)),
                  pl.BlockSpec((dm, tf), lambda i, j: (0, j)),
                  pl.BlockSpec((dm, tf), lambda i, j: (0, j)),
                  pl.BlockSpec((tf, dm), lambda i, j: (j, 0))],
        out_specs=pl.BlockSpec((tm, dm), lambda i, j: (i, 0)),
        out_shape=jax.ShapeDtypeStruct((n, dm), F32),
        scratch_shapes=[pltpu.VMEM((tm, dm), BF16), pltpu.VMEM((tm, dm), F32)],
        compiler_params=_cparams(("parallel", "arbitrary"), 48),
        name="ffn",
    )(x2, gain, wg, wu, wd)


def _post_math(xf, f, p_ref, gf_ref, wpg_ref, wpp_ref, gp_ref, out_ref):
    x2 = xf + _rms(f, gf_ref[...])
    gate = _sigmoid(jnp.dot(x2.astype(BF16), wpg_ref[...], preferred_element_type=F32))
    e = jnp.dot(p_ref[...].astype(BF16), wpp_ref[...], preferred_element_type=F32) * gate
    out_ref[...] = x2 + _rms(e, gp_ref[...])


def _post_kernel(x_ref, f_ref, p_ref, gf_ref, wpg_ref, wpp_ref, gp_ref, out_ref):
    _post_math(x_ref[...], f_ref[...], p_ref, gf_ref, wpg_ref, wpp_ref, gp_ref, out_ref)


def _post(x2, f, p2, gf, wpg, wpp, gp, tm):
    n, dm = x2.shape
    row = lambda cols: pl.BlockSpec((tm, cols), lambda i: (i, 0))
    return pl.pallas_call(
        _post_kernel,
        grid=(n // tm,),
        in_specs=[row(dm), row(dm), row(p2.shape[1]), _resident((1, dm)), _resident(wpg.shape),
                  _resident(wpp.shape), _resident((1, dm))],
        out_specs=row(dm),
        out_shape=jax.ShapeDtypeStruct((n, dm), F32),
        compiler_params=_cparams(("parallel",), 48),
        name="post",
    )(x2, f, p2, gf, wpg, wpp, gp)


def _router_kernel(x_ref, g_ref, wr_ref, br_ref, h_ref, info_ref):
    hf = _rms(x_ref[...], g_ref[...])
    hb = hf.astype(BF16)
    h_ref[...] = hb
    hl = (hf - hb.astype(F32)).astype(BF16)
    whi, wlo = wr_ref[0], wr_ref[1]
    logits = (jnp.dot(hb, whi, preferred_element_type=F32) + jnp.dot(hl, whi, preferred_element_type=F32)
              + jnp.dot(hb, wlo, preferred_element_type=F32)) + br_ref[...]
    lane = lax.broadcasted_iota(jnp.int32, logits.shape, 1).astype(F32)
    m1 = jnp.max(logits, axis=-1, keepdims=True)
    i1 = jnp.min(jnp.where(logits == m1, lane, float(LANES)), axis=-1, keepdims=True)
    rest = jnp.where(lane == i1, NEG_INF, logits)
    m2 = jnp.max(rest, axis=-1, keepdims=True)
    i2 = jnp.min(jnp.where(rest == m2, lane, float(LANES)), axis=-1, keepdims=True)
    e2 = jnp.exp(m2 - m1)
    g1 = 1.0 / (1.0 + e2)
    g2 = e2 / (1.0 + e2)
    info = jnp.where(lane == 0.0, i1, jnp.where(lane == 1.0, i2, jnp.where(lane == 2.0, g1, g2)))
    info_ref[...] = info


def _router(x2, gain, wr, br, tm):
    n, dm = x2.shape
    row = lambda cols: pl.BlockSpec((tm, cols), lambda i: (i, 0))
    return pl.pallas_call(
        _router_kernel,
        grid=(n // tm,),
        in_specs=[row(dm), _resident((1, dm)), _resident(wr.shape), _resident((1, LANES))],
        out_specs=[row(dm), row(LANES)],
        out_shape=[jax.ShapeDtypeStruct((n, dm), BF16), jax.ShapeDtypeStruct((n, LANES), F32)],
        compiler_params=_cparams(("parallel",), 32),
        name="router",
    )(x2, gain, wr, br)


def _dispatch_kernel(dest_ref, h_hbm, xb_in, xb_hbm, sem, *, tt):
    del xb_in
    t0 = pl.program_id(0) * tt

    def row_copy(a):
        return pltpu.make_async_copy(h_hbm.at[t0 + a // TOP_K], xb_hbm.at[dest_ref[0, a]], sem)

    def start(a, c):
        row_copy(a).start()
        return c

    def wait(a, c):
        row_copy(a).wait()
        return c

    lax.fori_loop(0, tt * TOP_K, start, 0)
    lax.fori_loop(0, tt * TOP_K, wait, 0)


def _dispatch(h3, dest, n_rows, tt):
    n = h3.shape[0]
    xb0 = jnp.zeros((n_rows,) + h3.shape[1:], h3.dtype)
    dest2 = dest.reshape(n // tt, 1, tt * TOP_K)
    return pl.pallas_call(
        functools.partial(_dispatch_kernel, tt=tt),
        grid=(n // tt,),
        in_specs=[pl.BlockSpec((None, 1, tt * TOP_K), lambda i: (i, 0, 0), memory_space=pltpu.SMEM),
                  pl.BlockSpec(memory_space=pl.ANY),
                  pl.BlockSpec(memory_space=pl.ANY)],
        out_specs=pl.BlockSpec(memory_space=pl.ANY),
        out_shape=jax.ShapeDtypeStruct(xb0.shape, xb0.dtype),
        scratch_shapes=[pltpu.SemaphoreType.DMA(())],
        input_output_aliases={2: 0},
        compiler_params=_cparams(("arbitrary",), 32),
        name="dispatch",
    )(dest2, h3, xb0)


def _experts_kernel(be_ref, nu_ref, x_ref, wg_ref, wu_ref, wd_ref, y_ref, acc_scr):
    del be_ref
    i = pl.program_id(0)
    j = pl.program_id(1)
    used = i < nu_ref[0]

    @pl.when(j == 0)
    def _():
        acc_scr[...] = jnp.zeros_like(acc_scr)

    @pl.when(used)
    def _():
        xb = x_ref[...]
        a = jnp.dot(xb, wg_ref[...], preferred_element_type=F32)
        u = jnp.dot(xb, wu_ref[...], preferred_element_type=F32)
        act = (a * _sigmoid(a) * u).astype(BF16)
        acc_scr[...] += jnp.dot(act, wd_ref[...], preferred_element_type=F32)

    @pl.when(j == pl.num_programs(1) - 1)
    def _():
        y_ref[...] = acc_scr[...]


def _experts(xb, blk_e, n_used, wg, wu, wd, tm, tf):
    n_rows, dm = xb.shape
    dff = wg.shape[2]
    grid_spec = pltpu.PrefetchScalarGridSpec(
        num_scalar_prefetch=2,
        grid=(n_rows // tm, dff // tf),
        in_specs=[pl.BlockSpec((tm, dm), lambda i, j, be, nu: (i, 0)),
                  pl.BlockSpec((None, dm, tf), lambda i, j, be, nu: (be[i], 0, j)),
                  pl.BlockSpec((None, dm, tf), lambda i, j, be, nu: (be[i], 0, j)),
                  pl.BlockSpec((None, tf, dm), lambda i, j, be, nu: (be[i], j, 0))],
        out_specs=pl.BlockSpec((tm, dm), lambda i, j, be, nu: (i, 0)),
        scratch_shapes=[pltpu.VMEM((tm, dm), F32)],
    )
    return pl.pallas_call(
        _experts_kernel,
        grid_spec=grid_spec,
        out_shape=jax.ShapeDtypeStruct((n_rows, dm), F32),
        compiler_params=_cparams(("parallel", "arbitrary"), 48),
        name="experts",
    )(blk_e, n_used, xb, wg, wu, wd)


def _combine_kernel(dest_ref, y_hbm, x_ref, gt_ref, p_ref, gf_ref, wpg_ref, wpp_ref, gp_ref,
                    out_ref, ybuf, sem, *, tt):
    sub = y_hbm.shape[1]

    def row_copy(a):
        slab = pl.multiple_of(((a % TOP_K) * tt + a // TOP_K) * sub, sub)
        return pltpu.make_async_copy(y_hbm.at[dest_ref[0, a]], ybuf.at[pl.ds(slab, sub)], sem)

    def start(a, c):
        row_copy(a).start()
        return c

    def wait(a, c):
        row_copy(a).wait()
        return c

    lax.fori_loop(0, tt * TOP_K, start, 0)
    lax.fori_loop(0, tt * TOP_K, wait, 0)
    gt = gt_ref[...]

    def rows(slot):
        return jnp.concatenate([ybuf[pl.ds(slot * tt * sub + s, tt, stride=sub), :] for s in range(sub)], axis=1)

    f = gt[:, 2:3] * rows(0) + gt[:, 3:4] * rows(1)
    _post_math(x_ref[...], f, p_ref, gf_ref, wpg_ref, wpp_ref, gp_ref, out_ref)


def _combine(y, dest, info, x2, p2, gf, wpg, wpp, gp, tt):
    n, dm = x2.shape
    dest2 = dest.reshape(n // tt, 1, tt * TOP_K)
    row = lambda cols: pl.BlockSpec((tt, cols), lambda i: (i, 0))
    return pl.pallas_call(
        functools.partial(_combine_kernel, tt=tt),
        grid=(n // tt,),
        in_specs=[pl.BlockSpec((None, 1, tt * TOP_K), lambda i: (i, 0, 0), memory_space=pltpu.SMEM),
                  pl.BlockSpec(memory_space=pl.ANY),
                  row(dm), row(LANES), row(p2.shape[1]), _resident((1, dm)),
                  _resident(wpg.shape), _resident(wpp.shape), _resident((1, dm))],
        out_specs=row(dm),
        out_shape=jax.ShapeDtypeStruct((n, dm), F32),
        scratch_shapes=[pltpu.VMEM((TOP_K * tt * (dm // LANES), LANES), F32), pltpu.SemaphoreType.DMA(())],
        compiler_params=_cparams(("arbitrary",), 48),
        name="combine",
    )(dest2, y, x2, info, p2, gf, wpg, wpp, gp)


def _moe(x2, p2, gain, wr, br, wg, wu, wd, gf, wpg, wpp, gp, *, tm_r, tt, tm_e, tf_e):
    n, _ = x2.shape
    h, info = _router(x2, gain, wr, br, tm_r)
    top_e = info[:, 0:TOP_K].astype(jnp.int32)
    e_flat = top_e.reshape(n * TOP_K)
    onehot = (e_flat[:, None] == jnp.arange(N_EXPERTS, dtype=jnp.int32)[None, :]).astype(jnp.int32)
    csum = jnp.cumsum(onehot, axis=0)
    rank = jnp.sum((csum - onehot) * onehot, axis=1)
    counts = csum[-1]
    padded = ((counts + tm_e - 1) // tm_e) * tm_e
    pend = jnp.cumsum(padded)
    pstart = pend - padded
    dest = (pstart[e_flat] + rank).astype(jnp.int32)
    n_blocks = -(-(n * TOP_K) // tm_e) + N_EXPERTS
    n_rows = n_blocks * tm_e
    blk_e = jnp.clip(jnp.searchsorted(pend, jnp.arange(n_blocks, dtype=jnp.int32) * tm_e, side="right"),
                     0, N_EXPERTS - 1).astype(jnp.int32)
    n_used = (pend[-1] // tm_e).astype(jnp.int32).reshape(1)
    dm = h.shape[1]
    xb = _dispatch(h.reshape(n, dm // LANES, LANES), dest, n_rows, tt)
    y = _experts(xb.reshape(n_rows, dm), blk_e, n_used, wg, wu, wd, tm_e, tf_e)
    return _combine(y.reshape(n_rows, dm // LANES, LANES), dest, info, x2, p2, gf, wpg, wpp, gp, tt)


def _tables(seq):
    pos = jnp.arange(seq, dtype=jnp.int32)

    def cs(p, half):
        inv = jnp.power(ROPE_THETA, -jnp.arange(half, dtype=F32) / half)
        ang = p.astype(F32)[:, None] * inv[None, :]
        return jnp.cos(ang), jnp.sin(ang)

    ones = lambda w: jnp.ones((seq, w), F32)
    zeros = lambda w: jnp.zeros((seq, w), F32)
    c, s = cs(pos, HEAD_DIM // 2)
    full = jnp.stack([jnp.tile(jnp.concatenate([c, c], 1), (1, 2)),
                      jnp.tile(jnp.concatenate([-s, s], 1), (1, 2))])
    cr, sr = cs(pos // GRID_W, HEAD_DIM // 4)
    cc, sc = cs(pos % GRID_W, HEAD_DIM // 4)
    axial = jnp.stack([jnp.tile(jnp.concatenate([cr, cc, cr, cc], 1), (1, 2)),
                       jnp.tile(jnp.concatenate([-sr, -sc, sr, sc], 1), (1, 2))])
    ca, sa = cs(pos, A_ROPE // 2)
    slot_c = jnp.concatenate([ca, ones(16), ca, ones(16)], 1)
    slot_s = jnp.concatenate([-sa, zeros(16), sa, zeros(16)], 1)
    a_k = jnp.stack([jnp.concatenate([slot_c, ones(HALF)], 1), jnp.concatenate([slot_s, zeros(HALF)], 1)])
    a_q = jnp.stack([jnp.concatenate([ones(HALF), slot_c], 1), jnp.concatenate([zeros(HALF), slot_s], 1)])
    return {"full": full, "axial": axial, "a_k": a_k, "a_q": a_q}


_AXIAL_PERM = tuple(list(range(0, 16)) + list(range(32, 48)) + list(range(16, 32)) + list(range(48, 64)))


def _dup_heads(w, n_heads, perm=None):
    rows = w.shape[0]
    w = w.reshape(rows, n_heads, HEAD_DIM)
    if perm is not None:
        w = w[:, :, perm]
    return jnp.stack([w, w], axis=2).reshape(rows, n_heads * 2 * HEAD_DIM)


def _assemble_w_in(w):
    dm = w.shape[0]
    perm = jnp.array(_AXIAL_PERM, jnp.int32)
    a, b, c, d = w[:, 0:416], w[:, 416:928], w[:, 928:3232], w[:, 3232:3744]
    z = lambda n: jnp.zeros((dm, n), w.dtype)
    kr = a[:, 384:416]
    a_seg = jnp.concatenate([a[:, 0:384], kr[:, 0:16], z(16), kr[:, 16:32], z(16), z(HALF)], axis=1)
    bq = b[:, 0:256].reshape(dm, 4, HEAD_DIM)[:, :, perm].reshape(dm, 256)
    b_seg = jnp.concatenate([bq, _dup_heads(b[:, 256:384], 2, perm), _dup_heads(b[:, 384:512], 2)], axis=1)
    d_seg = jnp.concatenate([d[:, 0:256], _dup_heads(d[:, 256:384], 2), _dup_heads(d[:, 384:512], 2)], axis=1)
    return jnp.concatenate([a_seg, b_seg, c, d_seg], axis=1).astype(BF16)


def _assemble_a(w_uq, w_ukv):
    zq = lambda n: jnp.zeros((w_uq.shape[0], n), w_uq.dtype)
    zk = lambda n: jnp.zeros((w_ukv.shape[0], n), w_ukv.dtype)
    dq = A_NOPE + A_ROPE
    q_cols, k_cols, v_cols = [], [], []
    for hh in range(A_HEADS):
        q = w_uq[:, hh * dq:(hh + 1) * dq]
        q_cols += [q[:, 0:A_NOPE], q[:, A_NOPE:A_NOPE + 16], zq(16), q[:, A_NOPE + 16:dq], zq(16)]
        kv = w_ukv[:, hh * (A_NOPE + A_V):(hh + 1) * (A_NOPE + A_V)]
        k_cols += [kv[:, 0:A_NOPE], zk(HALF)]
        v_cols += [kv[:, A_NOPE:A_NOPE + A_V]]
    cat = lambda cols: jnp.concatenate(cols, axis=1).astype(BF16)
    return cat(q_cols), cat(k_cols), cat(v_cols)


def _gain_pair(g, perm=None):
    if perm is not None:
        g = g[jnp.array(perm, jnp.int32)]
    return jnp.tile(g, 2).reshape(1, LANES).astype(F32)


def _class_view(z3, col0, dil):
    bsz, seq, _ = z3.shape
    return z3[:, :, col0:col0 + 768].reshape(bsz, seq // dil, dil * 768)


def kernel(x, p, w_in, a_qa_g, a_kva_g, a_w_uq, a_w_ukv, b_q_g, b_k_g, d_sink, w_branch, w_out,
           mix_pre_g, mix_post_g, ffn_pre_g, ffn_post_g, ffn_w_gate, ffn_w_up, ffn_w_down,
           router_w, router_b, moe_w_gate, moe_w_up, moe_w_down, ple_w_proj, ple_w_gate, ple_post_g):
    bsz, seq, dm = x.shape
    depth = w_in.shape[0]
    n = bsz * seq
    tm = min(512, seq)
    tq_flash = min(512, seq)
    tk_flash = min(512, seq)
    tabs = _tables(seq)
    row = lambda g: g.reshape(1, -1).astype(F32)
    x2 = x.reshape(n, dm)

    for i in range(depth):
        w_all = _assemble_w_in(w_in[i])
        wuq, wk, wv = _assemble_a(a_w_uq[i], a_w_ukv[i])
        z, qa, ka, va = _in_proj(x2, row(mix_pre_g[i]), w_all, tabs,
                                 _gain_pair(b_q_g[i], _AXIAL_PERM), _gain_pair(b_k_g[i], _AXIAL_PERM),
                                 row(a_qa_g[i]), row(a_kva_g[i]), wuq, wk, wv, seq, tm)
        z3 = z.reshape(bsz, seq, ZMAIN_COLS)
        vat = jnp.swapaxes(va.reshape(bsz, seq, 2 * LANES), 1, 2)
        o_a = _flash(qa, ka, vat, q_col0=0, k_col0=0, v_row0=0, seq=seq, packed=False,
                     tq=tq_flash, tk=tk_flash)
        vbt = jnp.swapaxes(z3[:, :, ZB + 512:ZB + 768], 1, 2)
        o_b = _flash(z, z, vbt, q_col0=ZB, k_col0=ZB + 256, v_row0=0, seq=seq, packed=True,
                     tq=tq_flash, tk=tk_flash)
        o_c, l_c = [], []
        for gi, (win, dil) in enumerate(C_PATTERNS):
            hw = win // (2 * dil)
            ls = seq // dil
            if dil == 1:
                src, col0, rc = z3, ZC, ZMAIN_COLS
            else:
                src, col0, rc = _class_view(z3, ZC + gi * 768, dil), 0, 768
            og, lg = _banded(src, src, src, q_col0=col0, k_col0=col0 + 256, v_col0=col0 + 512,
                             row_cols=rc, dil=dil, seq=seq, hw=hw, tq=min(256, ls),
                             want_lse=True, out_dtype=F32)
            o_c.append(og.reshape(n, 2 * LANES))
            l_c.append(lg.reshape(n, 2 * LANES))
        (o_d,) = _banded(z3, z3, z3, q_col0=ZD, k_col0=ZD + 256, v_col0=ZD + 512,
                         row_cols=ZMAIN_COLS, dil=1, seq=seq, hw=D_HALF_WINDOW, tq=min(256, seq),
                         sink=d_sink[i].astype(F32))
        o_d = o_d.reshape(n, 2 * LANES)
        wg_gate = w_in[i][:, 3744:].astype(BF16)
        x2 = _merge(x2, o_a, o_b, o_c, l_c, o_d, row(mix_pre_g[i]), wg_gate,
                    w_branch[i].astype(BF16), w_out[i].astype(BF16), row(mix_post_g[i]), tm)

        p2 = p[i].reshape(n, -1)
        wpg = ple_w_gate[i].astype(BF16)
        wpp = ple_w_proj[i].astype(BF16)
        j = i // 2
        if i % 2 == 0:
            f = _ffn(x2, row(ffn_pre_g[i]), ffn_w_gate[j].astype(BF16), ffn_w_up[j].astype(BF16),
                     ffn_w_down[j].astype(BF16), min(1024, n), 512)
            x2 = _post(x2, f, p2, row(ffn_post_g[i]), wpg, wpp, row(ple_post_g[i]), tm)
        else:
            wr32 = jnp.zeros((dm, LANES), F32).at[:, :N_EXPERTS].set(router_w[j].astype(F32))
            wr_hi = wr32.astype(BF16)
            wr = jnp.stack([wr_hi, (wr32 - wr_hi.astype(F32)).astype(BF16)])
            br = jnp.full((1, LANES), NEG_INF, F32).at[0, :N_EXPERTS].set(router_b[j].astype(F32))
            x2 = _moe(x2, p2, row(ffn_pre_g[i]), wr, br, moe_w_gate[j].astype(BF16),
                      moe_w_up[j].astype(BF16), moe_w_down[j].astype(BF16), row(ffn_post_g[i]),
                      wpg, wpp, row(ple_post_g[i]), tm_r=tm, tt=min(256, n), tm_e=512, tf_e=512)
    return x2.reshape(bsz, seq, dm)
```

```python
import functools

import jax
import jax.numpy as jnp
from jax import lax
from jax.experimental import pallas as pl
from jax.experimental.pallas import tpu as pltpu

F32 = jnp.float32
BF16 = jnp.bfloat16

GRID_W = 64
HEAD_DIM = 64
ROPE_THETA = 10000.0
NORM_EPS = 1e-6
NEG_INF = -1e30
A_HEADS = 4
A_Q_RANK = 256
A_KV_RANK = 128
A_NOPE = 64
A_ROPE = 32
A_V = 64
C_PATTERNS = ((128, 1), (512, 4), (2048, 16))
D_HALF_WINDOW = 128
N_BRANCHES = 4
N_EXPERTS = 8
TOP_K = 2

LANES = 128
HALF = 64
VMEM_MB = 1024 * 1024
LOG2E = 1.4426950408889634

ZB = 0
ZC = 512
ZD = 512 + 768
ZMAIN_COLS = ZD + 768
A_SEG = 512
B_SEG = 512


def _cparams(sem, vmem_mb):
    return pltpu.CompilerParams(dimension_semantics=sem, vmem_limit_bytes=vmem_mb * VMEM_MB)


def _resident(shape):
    nd = len(shape)
    return pl.BlockSpec(shape, lambda *_: (0,) * nd, pipeline_mode=pl.Buffered(1))


def _rms(xf, g):
    return xf * lax.rsqrt(jnp.mean(xf * xf, axis=-1, keepdims=True) + NORM_EPS) * g


def _sigmoid(x):
    return 1.0 / (1.0 + jnp.exp(-x))


def _swap32(a):
    lane = lax.broadcasted_iota(jnp.int32, a.shape, 1)
    fwd = pltpu.roll(a, LANES - 32, 1)
    bwd = pltpu.roll(a, 32, 1)
    return jnp.where((lane & 32) == 0, fwd, bwd)


def _rope(a, cos, sin):
    outs = []
    for c in range(a.shape[1] // LANES):
        ch = a[:, c * LANES:(c + 1) * LANES]
        outs.append(ch * cos + _swap32(ch) * sin)
    return outs[0] if len(outs) == 1 else jnp.concatenate(outs, axis=1)


def _head_norm(a, g, bd):
    outs = []
    for c in range(a.shape[1] // LANES):
        ch = a[:, c * LANES:(c + 1) * LANES]
        sq = ch * ch
        hi = sq.astype(BF16)
        lo = (sq - hi.astype(F32)).astype(BF16)
        ms = (jnp.dot(hi, bd, preferred_element_type=F32)
              + jnp.dot(lo, bd, preferred_element_type=F32))
        outs.append(ch * lax.rsqrt(ms + NORM_EPS) * g)
    return outs[0] if len(outs) == 1 else jnp.concatenate(outs, axis=1)


def _in_proj_kernel(x_ref, g_ref, w_ref, tabf_ref, tabx_ref, taba_ref, tabq_ref,
                    bqg_ref, bkg_ref, aqg_ref, akvg_ref, wuq_ref, wk_ref, wv_ref, wvb_ref,
                    z_ref, qa_ref, ka_ref, vat_ref, vbt_ref, zc1_ref, zc2_ref, cls_scr):
    h = _rms(x_ref[...], g_ref[...]).astype(BF16)
    cf, sf = tabf_ref[0], tabf_ref[1]
    cx, sx = tabx_ref[0], tabx_ref[1]
    q_scale = HEAD_DIM ** -0.5

    acc = jnp.dot(h, w_ref[:, 0:A_SEG], preferred_element_type=F32)
    nq = _rms(acc[:, 0:A_Q_RANK], aqg_ref[...]).astype(BF16)
    nkv = _rms(acc[:, A_Q_RANK:A_Q_RANK + A_KV_RANK], akvg_ref[...]).astype(BF16)
    kr = _rope(acc[:, 384:512], taba_ref[0], taba_ref[1]).astype(BF16)
    qa = jnp.dot(nq, wuq_ref[...], preferred_element_type=F32)
    qa = _rope(qa, tabq_ref[0], tabq_ref[1]) * ((A_NOPE + A_ROPE) ** -0.5 * LOG2E)
    qa_ref[...] = qa.astype(qa_ref.dtype)
    r = lax.broadcasted_iota(jnp.int32, (LANES, A_HEADS * LANES), 0)
    c = lax.broadcasted_iota(jnp.int32, (LANES, A_HEADS * LANES), 1)
    place = jnp.where((r < HALF) & ((c & (LANES - 1)) == r + HALF), 1.0, 0.0).astype(BF16)
    ka = (jnp.dot(nkv, wk_ref[...], preferred_element_type=F32)
          + jnp.dot(kr, place, preferred_element_type=F32))
    ka_ref[...] = ka.astype(ka_ref.dtype)
    nt_dims = (((1,), (1,)), ((), ()))
    vat_ref[...] = lax.dot_general(wv_ref[...], nkv, nt_dims, preferred_element_type=F32).astype(vat_ref.dtype)
    vbt_ref[...] = lax.dot_general(wvb_ref[...], h, nt_dims, preferred_element_type=F32).astype(vbt_ref.dtype)

    rr = lax.broadcasted_iota(jnp.int32, (LANES, LANES), 0)
    cc = lax.broadcasted_iota(jnp.int32, (LANES, LANES), 1)
    bd = jnp.where((rr >> 6) == (cc >> 6), 1.0 / HEAD_DIM, 0.0).astype(BF16)
    acc = jnp.dot(h, w_ref[:, A_SEG:A_SEG + B_SEG], preferred_element_type=F32)
    q = _rope(_head_norm(acc[:, 0:256], bqg_ref[...], bd), cx, sx) * (q_scale * LOG2E)
    k = _rope(_head_norm(acc[:, 256:512], bkg_ref[...], bd), cx, sx)
    z_ref[:, ZB:ZB + 256] = q.astype(z_ref.dtype)
    z_ref[:, ZB + 256:ZB + 512] = k.astype(z_ref.dtype)

    tm = h.shape[0]
    for widx, zoff, cls_ref, dil in ((0, ZC, None, 1), (1, 0, zc1_ref, C_PATTERNS[1][1]),
                                     (2, 0, zc2_ref, C_PATTERNS[2][1]), (3, ZD, None, 1)):
        base = A_SEG + B_SEG + widx * 768
        acc = jnp.dot(h, w_ref[:, base:base + 768], preferred_element_type=F32)
        q = _rope(acc[:, 0:256], cf, sf) * q_scale
        k = _rope(acc[:, 256:512], cf, sf)
        if cls_ref is None:
            z_ref[:, zoff:zoff + 256] = q.astype(z_ref.dtype)
            z_ref[:, zoff + 256:zoff + 512] = k.astype(z_ref.dtype)
            z_ref[:, zoff + 512:zoff + 768] = acc[:, 512:768].astype(z_ref.dtype)
        else:
            qkv = (q[:, 0:LANES], q[:, LANES:], k[:, 0:LANES], k[:, LANES:],
                   acc[:, 512:512 + LANES], acc[:, 512 + LANES:768])
            for ch, val in enumerate(qkv):
                cls_scr[ch] = val
            for c in range(dil):
                for ch in range(len(qkv)):
                    col = c * 768 + ch * LANES
                    cls_ref[:, col:col + LANES] = cls_scr[ch, pl.ds(c, tm // dil, stride=dil), :].astype(cls_ref.dtype)


def _in_proj(x2, gain, w, tabs, bqg, bkg, aqg, akvg, wuq, wk, wv, wvb, seq, tm):
    n, dm = x2.shape
    nt = seq // tm
    bsz = n // seq
    d1, d2 = C_PATTERNS[1][1], C_PATTERNS[2][1]
    tab_spec = pl.BlockSpec((2, tm, LANES), lambda i: (0, i % nt, 0))
    row = lambda cols: pl.BlockSpec((tm, cols), lambda i: (i, 0))
    vt_spec = pl.BlockSpec((None, 2 * LANES, tm), lambda i: (i // nt, 0, i % nt))
    cls_spec = lambda d: pl.BlockSpec((None, tm // d, d * 768), lambda i: (i // nt, i % nt, 0))
    return pl.pallas_call(
        _in_proj_kernel,
        grid=(n // tm,),
        in_specs=[row(dm), _resident((1, dm)), _resident(w.shape),
                  tab_spec, tab_spec, tab_spec, tab_spec,
                  _resident((1, LANES)), _resident((1, LANES)),
                  _resident((1, A_Q_RANK)), _resident((1, A_KV_RANK)),
                  _resident(wuq.shape), _resident(wk.shape), _resident(wv.shape), _resident(wvb.shape)],
        out_specs=[row(ZMAIN_COLS), row(A_HEADS * LANES), row(A_HEADS * LANES),
                   vt_spec, vt_spec, cls_spec(d1), cls_spec(d2)],
        out_shape=[jax.ShapeDtypeStruct((n, ZMAIN_COLS), BF16),
                   jax.ShapeDtypeStruct((n, A_HEADS * LANES), BF16),
                   jax.ShapeDtypeStruct((n, A_HEADS * LANES), BF16),
                   jax.ShapeDtypeStruct((bsz, 2 * LANES, seq), BF16),
                   jax.ShapeDtypeStruct((bsz, 2 * LANES, seq), BF16),
                   jax.ShapeDtypeStruct((bsz, seq // d1, d1 * 768), BF16),
                   jax.ShapeDtypeStruct((bsz, seq // d2, d2 * 768), BF16)],
        scratch_shapes=[pltpu.VMEM((768 // LANES, tm, LANES), F32)],
        compiler_params=_cparams(("parallel",), 48),
        name="in_proj",
    )(x2, gain, w, tabs["full"], tabs["axial"], tabs["a_k"], tabs["a_q"],
      bqg, bkg, aqg, akvg, wuq, wk, wv, wvb)


def _flash_kernel(q_ref, k_ref, vt_ref, o_ref, s_a, s_b, *, tk, packed):
    tq = q_ref.shape[0]
    nk = k_ref.shape[0] // tk
    lane = lax.broadcasted_iota(jnp.int32, (1, LANES), 1)
    if packed:
        qs = [jnp.where((lane < HALF) == (r == 0), q_ref[...], 0) for r in range(2)]
        kcs = [0, 0]
    else:
        qs = [q_ref[:, r * LANES:(r + 1) * LANES] for r in range(2)]
        kcs = [0, LANES]

    def scores(kk, dst):
        ks = pl.multiple_of(kk * tk, tk)
        for r in range(2):
            k = k_ref[pl.ds(ks, tk), kcs[r]:kcs[r] + LANES]
            dst[r] = lax.dot_general(k, qs[r], (((1,), (1,)), ((), ())), preferred_element_type=F32)

    def consume(kk, src, carry):
        ks = pl.multiple_of(kk * tk, tk)
        new = []
        for r in range(2):
            m, l, acc = carry[r]
            s = src[r]
            m_new = jnp.maximum(m, jnp.max(s, axis=0, keepdims=True))
            alpha = jnp.exp2(m - m_new)
            p = jnp.exp2(s - m_new)
            l = alpha * l + jnp.sum(p, axis=0, keepdims=True)
            vt = vt_ref[r * HALF:(r + 1) * HALF, pl.ds(ks, tk)]
            acc = alpha * acc + jnp.dot(vt, p.astype(BF16), preferred_element_type=F32)
            new.append((m_new, l, acc))
        return tuple(new)

    def pair(jj, carry):
        c0 = 2 * jj
        scores(c0 + 1, s_b)
        carry = consume(c0, s_a, carry)
        scores(c0 + 2, s_a)
        return consume(c0 + 1, s_b, carry)

    init = tuple((jnp.full((1, tq), NEG_INF, F32), jnp.zeros((1, tq), F32), jnp.zeros((HALF, tq), F32))
                 for _ in range(2))
    scores(0, s_a)
    carry = lax.fori_loop(0, nk // 2 - 1, pair, init)
    scores(nk - 1, s_b)
    carry = consume(nk - 2, s_a, carry)
    carry = consume(nk - 1, s_b, carry)
    halves = [acc / l for (_, l, acc) in carry]
    o_ref[...] = jnp.concatenate(halves, axis=0).T.astype(o_ref.dtype)


def _flash(q, k, vt, *, q_col0, k_col0, v_row0, seq, packed, tq, tk):
    n = q.shape[0]
    bsz = n // seq
    qw = LANES if packed else 2 * LANES
    nq = seq // tq
    k3 = k.reshape(bsz, seq, k.shape[1])
    qb, kb, vb = q_col0 // qw, k_col0 // qw, v_row0 // LANES
    return pl.pallas_call(
        functools.partial(_flash_kernel, tk=tk, packed=packed),
        grid=(bsz, 2, nq),
        in_specs=[pl.BlockSpec((tq, qw), lambda b, j, i: (b * nq + i, qb + j)),
                  pl.BlockSpec((None, seq, qw), lambda b, j, i: (b, 0, kb + j)),
                  pl.BlockSpec((None, LANES, seq), lambda b, j, i: (b, vb + j, 0))],
        out_specs=pl.BlockSpec((tq, LANES), lambda b, j, i: (b * nq + i, j)),
        out_shape=jax.ShapeDtypeStruct((n, 2 * LANES), BF16),
        scratch_shapes=[pltpu.VMEM((2, tk, tq), F32), pltpu.VMEM((2, tk, tq), F32)],
        compiler_params=_cparams(("parallel", "parallel", "parallel"), 48),
        name="flash_packed" if packed else "flash_slots",
    )(q, k3, vt)


def _banded_kernel(*refs, hw, has_sink, want_lse):
    if has_sink:
        sink_ref, q_ref, k_ref, v_ref = refs[:4]
        outs = refs[4:]
    else:
        q_ref, k_ref, v_ref = refs[:3]
        outs = refs[3:]
    o_ref = outs[0]
    tq = q_ref.shape[0]
    length = k_ref.shape[0]
    win = min(tq + 2 * hw, length)
    i = pl.program_id(1)
    ks = jnp.clip(i * tq - hw, 0, length - win)
    ks = pl.multiple_of(ks, HALF)
    qpos = i * tq + lax.broadcasted_iota(jnp.int32, (tq, win), 0)
    kpos = ks + lax.broadcasted_iota(jnp.int32, (tq, win), 1)
    valid = jnp.abs(qpos - kpos) <= hw
    lane = lax.broadcasted_iota(jnp.int32, (1, LANES), 1)
    first = lane < HALF
    for j in range(2):
        cols = slice(j * LANES, (j + 1) * LANES)
        kw = k_ref[pl.ds(ks, win), cols]
        vw = v_ref[pl.ds(ks, win), cols]
        o_halves, lse_halves = [], []
        for r in range(2):
            q = jnp.where(first == (r == 0), q_ref[:, cols], 0)
            s = lax.dot_general(q, kw, (((1,), (1,)), ((), ())), preferred_element_type=F32)
            s = jnp.where(valid, s, NEG_INF)
            m = jnp.max(s, axis=-1, keepdims=True)
            if has_sink:
                sink = sink_ref[2 * j + r]
                m = jnp.maximum(m, sink)
            e = jnp.exp(s - m)
            l = jnp.sum(e, axis=-1, keepdims=True)
            if has_sink:
                l = l + jnp.exp(sink - m)
            o = jnp.dot(e.astype(BF16), vw, preferred_element_type=F32) / l
            o_halves.append(o)
            if want_lse:
                lse_halves.append(jnp.broadcast_to(m + jnp.log(l), (tq, LANES)))
        o_ref[:, cols] = jnp.where(first, o_halves[0], o_halves[1]).astype(o_ref.dtype)
        if want_lse:
            outs[1][:, cols] = jnp.where(first, lse_halves[0], lse_halves[1])


def _banded(q, k, v, *, q_col0, k_col0, v_col0, row_cols, dil, seq, hw, tq, sink=None,
            want_lse=False, out_dtype=BF16):
    bsz, ls, _ = q.shape
    nq = ls // tq
    pw = 2 * LANES
    rb = row_cols // pw
    qb, kb, vb = q_col0 // pw, k_col0 // pw, v_col0 // pw
    in_specs = [pl.BlockSpec((None, tq, pw), lambda bc, i: (bc // dil, i, (bc % dil) * rb + qb)),
                pl.BlockSpec((None, ls, pw), lambda bc, i: (bc // dil, 0, (bc % dil) * rb + kb)),
                pl.BlockSpec((None, ls, pw), lambda bc, i: (bc // dil, 0, (bc % dil) * rb + vb))]
    args = [q, k, v]
    if sink is not None:
        in_specs = [pl.BlockSpec(memory_space=pltpu.SMEM)] + in_specs
        args = [sink] + args
    o_spec = pl.BlockSpec((None, tq, pw), lambda bc, i: (bc // dil, i, bc % dil))
    out_specs = [o_spec]
    out_shape = [jax.ShapeDtypeStruct((bsz, ls, dil * pw), out_dtype)]
    if want_lse:
        out_specs.append(o_spec)
        out_shape.append(jax.ShapeDtypeStruct((bsz, ls, dil * pw), F32))
    res = pl.pallas_call(
        functools.partial(_banded_kernel, hw=hw, has_sink=sink is not None, want_lse=want_lse),
        grid=(bsz * dil, nq),
        in_specs=in_specs,
        out_specs=out_specs,
        out_shape=out_shape,
        compiler_params=_cparams(("parallel", "parallel"), 48),
        name="banded_d%d" % dil,
    )(*args)
    return res


def _merge_kernel(x_ref, oa_ref, ob_ref, oc0_ref, oc1_ref, oc2_ref, l0_ref, l1_ref, l2_ref,
                  od_ref, gpre_ref, wg_ref, wb_ref, wo_ref, gpost_ref, out_ref, tok_scr):
    xf = x_ref[...]
    tm, dm = xf.shape
    h = _rms(xf, gpre_ref[...]).astype(BF16)
    pw = 2 * LANES
    toks = []
    for n, src in enumerate((oc1_ref, l1_ref, oc2_ref, l2_ref)):
        dil = src.shape[1] // pw
        for c in range(dil):
            for hp in range(2):
                col = c * pw + hp * LANES
                tok_scr[2 * n + hp, pl.ds(c, tm // dil, stride=dil), :] = src[:, col:col + LANES]
        toks.append(jnp.concatenate([tok_scr[2 * n], tok_scr[2 * n + 1]], axis=1))
    oc1, l1, oc2, l2 = toks
    l0 = l0_ref[...]
    mx = jnp.maximum(jnp.maximum(l0, l1), l2)
    w0, w1, w2 = jnp.exp(l0 - mx), jnp.exp(l1 - mx), jnp.exp(l2 - mx)
    oc = (w0 * oc0_ref[...] + w1 * oc1 + w2 * oc2) / (w0 + w1 + w2)
    branches = (oa_ref[...], ob_ref[...], oc.astype(BF16), od_ref[...])
    merged = None
    for n, o in enumerate(branches):
        gate = _sigmoid(jnp.dot(h, wg_ref[:, n * dm:(n + 1) * dm], preferred_element_type=F32))
        term = gate * jnp.dot(o, wb_ref[n], preferred_element_type=F32)
        merged = term if merged is None else merged + term
    y = jnp.dot(merged.astype(BF16), wo_ref[...], preferred_element_type=F32)
    out_ref[...] = xf + _rms(y, gpost_ref[...])


def _merge(x2, oa, ob, oc, lc, od, gpre, wg, wb, wo, gpost, seq, tm):
    n, dm = x2.shape
    nt = seq // tm
    row = lambda cols: pl.BlockSpec((tm, cols), lambda i: (i, 0))
    bw = 2 * LANES
    cls = lambda a: pl.BlockSpec((None, tm // (a.shape[2] // bw), a.shape[2]), lambda i: (i // nt, i % nt, 0))
    return pl.pallas_call(
        _merge_kernel,
        grid=(n // tm,),
        in_specs=[row(dm), row(bw), row(bw), row(bw), cls(oc[1]), cls(oc[2]), row(bw), cls(lc[1]), cls(lc[2]),
                  row(bw), _resident((1, dm)), _resident(wg.shape), _resident(wb.shape),
                  _resident(wo.shape), _resident((1, dm))],
        out_specs=row(dm),
        out_shape=jax.ShapeDtypeStruct((n, dm), F32),
        scratch_shapes=[pltpu.VMEM((8, tm, LANES), F32)],
        compiler_params=_cparams(("parallel",), 56),
        name="merge",
    )(x2, oa, ob, oc[0], oc[1], oc[2], lc[0], lc[1], lc[2], od, gpre, wg, wb, wo, gpost)


def _ffn_kernel(x_ref, g_ref, wg_ref, wu_ref, wd_ref, f_ref, h_scr, acc_scr):
    j = pl.program_id(1)

    @pl.when(j == 0)
    def _():
        h_scr[...] = _rms(x_ref[...], g_ref[...]).astype(BF16)
        acc_scr[...] = jnp.zeros_like(acc_scr)

    h = h_scr[...]
    a = jnp.dot(h, wg_ref[...], preferred_element_type=F32)
    u = jnp.dot(h, wu_ref[...], preferred_element_type=F32)
    act = (a * _sigmoid(a) * u).astype(BF16)
    acc_scr[...] += jnp.dot(act, wd_ref[...], preferred_element_type=F32)

    @pl.when(j == pl.num_programs(1) - 1)
    def _():
        f_ref[...] = acc_scr[...]


def _ffn(x2, gain, wg, wu, wd, tm, tf):
    n, dm = x2.shape
    dff = wg.shape[1]
    return pl.pallas_call(
        _ffn_kernel,
        grid=(n // tm, dff // tf),
        in_specs=[pl.BlockSpec((tm, dm), lambda i, j: (i, 0)),
                  pl.BlockSpec((1, dm), lambda i, j: (0, 0)),
                  pl.BlockSpec((dm, tf), lambda i, j: (0, j)),
                  pl.BlockSpec((dm, tf), lambda i, j: (0, j)),
                  pl.BlockSpec((tf, dm), lambda i, j: (j, 0))],
        out_specs=pl.BlockSpec((tm, dm), lambda i, j: (i, 0)),
        out_shape=jax.ShapeDtypeStruct((n, dm), F32),
        scratch_shapes=[pltpu.VMEM((tm, dm), BF16), pltpu.VMEM((tm, dm), F32)],
        compiler_params=_cparams(("parallel", "arbitrary"), 48),
        name="ffn",
    )(x2, gain, wg, wu, wd)


def _post_math(xf, f, p_ref, gf_ref, wpg_ref, wpp_ref, gp_ref, out_ref):
    x2 = xf + _rms(f, gf_ref[...])
    gate = _sigmoid(jnp.dot(x2.astype(BF16), wpg_ref[...], preferred_element_type=F32))
    e = jnp.dot(p_ref[...].astype(BF16), wpp_ref[...], preferred_element_type=F32) * gate
    out_ref[...] = x2 + _rms(e, gp_ref[...])


def _post_kernel(x_ref, f_ref, p_ref, gf_ref, wpg_ref, wpp_ref, gp_ref, out_ref):
    _post_math(x_ref[...], f_ref[...], p_ref, gf_ref, wpg_ref, wpp_ref, gp_ref, out_ref)


def _post(x2, f, p2, gf, wpg, wpp, gp, tm):
    n, dm = x2.shape
    row = lambda cols: pl.BlockSpec((tm, cols), lambda i: (i, 0))
    return pl.pallas_call(
        _post_kernel,
        grid=(n // tm,),
        in_specs=[row(dm), row(dm), row(p2.shape[1]), _resident((1, dm)), _resident(wpg.shape),
                  _resident(wpp.shape), _resident((1, dm))],
        out_specs=row(dm),
        out_shape=jax.ShapeDtypeStruct((n, dm), F32),
        compiler_params=_cparams(("parallel",), 48),
        name="post",
    )(x2, f, p2, gf, wpg, wpp, gp)


def _router_kernel(x_ref, g_ref, wr_ref, br_ref, hs_ref, info_ref):
    hf = _rms(x_ref[...], g_ref[...])
    hb = hf.astype(BF16)
    tm, dm = hf.shape
    sub = dm // LANES
    for s in range(sub):
        hs_ref[pl.ds(s, tm, stride=sub), :] = hf[:, s * LANES:(s + 1) * LANES]
    hl = (hf - hb.astype(F32)).astype(BF16)
    whi, wlo = wr_ref[0], wr_ref[1]
    logits = (jnp.dot(hb, whi, preferred_element_type=F32) + jnp.dot(hl, whi, preferred_element_type=F32)
              + jnp.dot(hb, wlo, preferred_element_type=F32)) + br_ref[...]
    lane = lax.broadcasted_iota(jnp.int32, logits.shape, 1).astype(F32)
    m1 = jnp.max(logits, axis=-1, keepdims=True)
    i1 = jnp.min(jnp.where(logits == m1, lane, float(LANES)), axis=-1, keepdims=True)
    rest = jnp.where(lane == i1, NEG_INF, logits)
    m2 = jnp.max(rest, axis=-1, keepdims=True)
    i2 = jnp.min(jnp.where(rest == m2, lane, float(LANES)), axis=-1, keepdims=True)
    e2 = jnp.exp(m2 - m1)
    g1 = 1.0 / (1.0 + e2)
    g2 = e2 / (1.0 + e2)
    info = jnp.where(lane == 0.0, i1, jnp.where(lane == 1.0, i2, jnp.where(lane == 2.0, g1, g2)))
    info_ref[...] = info


def _router(x2, gain, wr, br, tm):
    n, dm = x2.shape
    row = lambda cols: pl.BlockSpec((tm, cols), lambda i: (i, 0))
    return pl.pallas_call(
        _router_kernel,
        grid=(n // tm,),
        in_specs=[row(dm), _resident((1, dm)), _resident(wr.shape), _resident((1, LANES))],
        out_specs=[pl.BlockSpec((tm * (dm // LANES), LANES), lambda i: (i, 0)), row(LANES)],
        out_shape=[jax.ShapeDtypeStruct((n * (dm // LANES), LANES), F32),
                   jax.ShapeDtypeStruct((n, LANES), F32)],
        compiler_params=_cparams(("parallel",), 32),
        name="router",
    )(x2, gain, wr, br)


def _experts_kernel(be_ref, src0_ref, srcn_ref, dstp_ref, dstl_ref, hs_hbm, wg_ref, wu_ref, wd_ref,
                    yt_hbm, xg, xb, acc, ys, gsem, ssem, *, tm, sub):
    del be_ref
    i = pl.program_id(0)
    j = pl.program_id(1)
    nb = pl.num_programs(0)
    nf = pl.num_programs(1)
    slot = i % 2
    other = 1 - slot
    per_step = tm // nf

    def gather(idx_ref, r, dslot):
        src = pl.multiple_of(idx_ref[0, r] * sub, sub)
        dst = pl.multiple_of(r * sub, sub)
        return pltpu.make_async_copy(hs_hbm.at[pl.ds(src, sub)], xg.at[dslot, pl.ds(dst, sub)],
                                     gsem.at[dslot])

    def scatter(idx_ref, r, sslot):
        src = pl.multiple_of(r * sub, sub)
        dst = pl.multiple_of(idx_ref[0, r] * sub, sub)
        return pltpu.make_async_copy(ys.at[sslot, pl.ds(src, sub)], yt_hbm.at[pl.ds(dst, sub)],
                                     ssem.at[sslot])

    def wait_gather(dslot):
        pltpu.make_async_copy(hs_hbm.at[pl.ds(0, tm * sub)], xg.at[dslot], gsem.at[dslot]).wait()

    def wait_scatter(sslot):
        pltpu.make_async_copy(ys.at[sslot], yt_hbm.at[pl.ds(0, tm * sub)], ssem.at[sslot]).wait()

    @pl.when((i == 0) & (j == 0))
    def _():
        ys[1] = jnp.zeros(ys.shape[1:], ys.dtype)

        def start(r, c):
            gather(src0_ref, r, 0).start()
            return c

        lax.fori_loop(0, tm, start, 0)

    @pl.when(j == 0)
    def _():
        wait_gather(slot)
        for s in range(sub):
            xb[:, s * LANES:(s + 1) * LANES] = xg[slot, pl.ds(s, tm, stride=sub), :].astype(BF16)
        acc[...] = jnp.zeros_like(acc)

    x = xb[...]
    a = jnp.dot(x, wg_ref[...], preferred_element_type=F32)
    u = jnp.dot(x, wu_ref[...], preferred_element_type=F32)
    act = (a * _sigmoid(a) * u).astype(BF16)
    acc[...] += jnp.dot(act, wd_ref[...], preferred_element_type=F32)
    r0 = j * per_step
    for t in range(per_step):
        gather(srcn_ref, r0 + t, other).start()
    for t in range(per_step):
        scatter(dstp_ref, r0 + t, other).start()

    @pl.when(j == nf - 1)
    def _():
        @pl.when(i >= 1)
        def _():
            wait_scatter(slot)

        for s in range(sub):
            ys[slot, pl.ds(s, tm, stride=sub), :] = acc[:, s * LANES:(s + 1) * LANES]

        @pl.when(i == nb - 1)
        def _():
            def start(r, c):
                scatter(dstl_ref, r, slot).start()
                return c

            lax.fori_loop(0, tm, start, 0)
            wait_scatter(slot)
            wait_scatter(other)
            wait_gather(other)


def _experts(hs, blk_e, src_tok, dst_row, n_slabs, wg, wu, wd, tm, tf):
    nb = src_tok.shape[0]
    dm, dff = wg.shape[1], wg.shape[2]
    sub = dm // LANES
    nf = dff // tf
    smem = lambda imap: pl.BlockSpec((None, 1, tm), imap, memory_space=pltpu.SMEM)
    grid_spec = pltpu.PrefetchScalarGridSpec(
        num_scalar_prefetch=1,
        grid=(nb, nf),
        in_specs=[smem(lambda i, j, be: (0, 0, 0)),
                  smem(lambda i, j, be: (jnp.minimum(i + 1, nb - 1), 0, 0)),
                  smem(lambda i, j, be: (i, 0, 0)),
                  smem(lambda i, j, be: (nb, 0, 0)),
                  pl.BlockSpec(memory_space=pl.ANY),
                  pl.BlockSpec((None, dm, tf), lambda i, j, be: (be[i], 0, j)),
                  pl.BlockSpec((None, dm, tf), lambda i, j, be: (be[i], 0, j)),
                  pl.BlockSpec((None, tf, dm), lambda i, j, be: (be[i], j, 0))],
        out_specs=pl.BlockSpec(memory_space=pl.ANY),
        scratch_shapes=[pltpu.VMEM((2, tm * sub, LANES), F32), pltpu.VMEM((tm, dm), BF16),
                        pltpu.VMEM((tm, dm), F32), pltpu.VMEM((2, tm * sub, LANES), F32),
                        pltpu.SemaphoreType.DMA((2,)), pltpu.SemaphoreType.DMA((2,))],
    )
    return pl.pallas_call(
        functools.partial(_experts_kernel, tm=tm, sub=sub),
        grid_spec=grid_spec,
        out_shape=jax.ShapeDtypeStruct((n_slabs * sub, LANES), F32),
        compiler_params=_cparams(("arbitrary", "arbitrary"), 48),
        name="experts",
    )(blk_e, src_tok, src_tok, dst_row, dst_row, hs, wg, wu, wd)


def _combine_kernel(y_ref, x_ref, gt_ref, p_ref, gf_ref, wpg_ref, wpp_ref, gp_ref, out_ref):
    tt, dm = x_ref.shape
    sub = dm // LANES
    gt = gt_ref[...]

    def rows(slot):
        return jnp.concatenate([y_ref[pl.ds(slot * sub + s, tt, stride=TOP_K * sub), :] for s in range(sub)],
                               axis=1)

    f = gt[:, 2:3] * rows(0) + gt[:, 3:4] * rows(1)
    _post_math(x_ref[...], f, p_ref, gf_ref, wpg_ref, wpp_ref, gp_ref, out_ref)


def _combine(yt, info, x2, p2, gf, wpg, wpp, gp, tt):
    n, dm = x2.shape
    row = lambda cols: pl.BlockSpec((tt, cols), lambda i: (i, 0))
    return pl.pallas_call(
        _combine_kernel,
        grid=(n // tt,),
        in_specs=[pl.BlockSpec((tt * TOP_K * (dm // LANES), LANES), lambda i: (i, 0)),
                  row(dm), row(LANES), row(p2.shape[1]), _resident((1, dm)),
                  _resident(wpg.shape), _resident(wpp.shape), _resident((1, dm))],
        out_specs=row(dm),
        out_shape=jax.ShapeDtypeStruct((n, dm), F32),
        compiler_params=_cparams(("parallel",), 48),
        name="combine",
    )(yt, x2, info, p2, gf, wpg, wpp, gp)


def _moe(x2, p2, gain, wr, br, wg, wu, wd, gf, wpg, wpp, gp, *, tm_r, tt, tm_e, tf_e):
    n, _ = x2.shape
    hs, info = _router(x2, gain, wr, br, tm_r)
    n_asg = n * TOP_K
    top_e = info[:, 0:TOP_K].astype(jnp.int32)
    e_flat = top_e.reshape(n_asg)
    onehot = (e_flat[:, None] == jnp.arange(N_EXPERTS, dtype=jnp.int32)[None, :]).astype(jnp.int32)
    csum = jnp.cumsum(onehot, axis=0)
    rank = jnp.sum((csum - onehot) * onehot, axis=1)
    counts = csum[-1]
    padded = ((counts + tm_e - 1) // tm_e) * tm_e
    pend = jnp.cumsum(padded)
    pstart = pend - padded
    dest = (pstart[e_flat] + rank).astype(jnp.int32)
    n_blocks = -(-n_asg // tm_e) + N_EXPERTS
    n_rows = n_blocks * tm_e
    blk_e = jnp.clip(jnp.searchsorted(pend, jnp.arange(n_blocks, dtype=jnp.int32) * tm_e, side="right"),
                     0, N_EXPERTS - 1).astype(jnp.int32)
    asg = jnp.full((n_rows,), -1, jnp.int32).at[dest].set(jnp.arange(n_asg, dtype=jnp.int32),
                                                          unique_indices=True)
    is_pad = asg < 0
    pad_rank = jnp.cumsum(is_pad.astype(jnp.int32)) - 1
    src_tok = jnp.where(is_pad, 0, asg // TOP_K).reshape(n_blocks, 1, tm_e)
    dst_row = jnp.where(is_pad, n_asg + pad_rank, asg)
    spare = n_asg + (n_rows - n_asg) + jnp.arange(tm_e, dtype=jnp.int32)
    dst_row = jnp.concatenate([spare, dst_row]).reshape(n_blocks + 1, 1, tm_e)
    n_slabs = n_asg + (n_rows - n_asg) + tm_e
    yt = _experts(hs, blk_e, src_tok, dst_row, n_slabs, wg, wu, wd, tm_e, tf_e)
    return _combine(yt, info, x2, p2, gf, wpg, wpp, gp, tt)


def _tables(seq):
    pos = jnp.arange(seq, dtype=jnp.int32)

    def cs(p, half):
        inv = jnp.power(ROPE_THETA, -jnp.arange(half, dtype=F32) / half)
        ang = p.astype(F32)[:, None] * inv[None, :]
        return jnp.cos(ang), jnp.sin(ang)

    ones = lambda w: jnp.ones((seq, w), F32)
    zeros = lambda w: jnp.zeros((seq, w), F32)
    c, s = cs(pos, HEAD_DIM // 2)
    full = jnp.stack([jnp.tile(jnp.concatenate([c, c], 1), (1, 2)),
                      jnp.tile(jnp.concatenate([-s, s], 1), (1, 2))])
    cr, sr = cs(pos // GRID_W, HEAD_DIM // 4)
    cc, sc = cs(pos % GRID_W, HEAD_DIM // 4)
    axial = jnp.stack([jnp.tile(jnp.concatenate([cr, cc, cr, cc], 1), (1, 2)),
                       jnp.tile(jnp.concatenate([-sr, -sc, sr, sc], 1), (1, 2))])
    ca, sa = cs(pos, A_ROPE // 2)
    slot_c = jnp.concatenate([ca, ones(16), ca, ones(16)], 1)
    slot_s = jnp.concatenate([-sa, zeros(16), sa, zeros(16)], 1)
    a_k = jnp.stack([jnp.concatenate([slot_c, ones(HALF)], 1), jnp.concatenate([slot_s, zeros(HALF)], 1)])
    a_q = jnp.stack([jnp.concatenate([ones(HALF), slot_c], 1), jnp.concatenate([zeros(HALF), slot_s], 1)])
    return {"full": full, "axial": axial, "a_k": a_k, "a_q": a_q}


_AXIAL_PERM = tuple(list(range(0, 16)) + list(range(32, 48)) + list(range(16, 32)) + list(range(48, 64)))


def _dup_heads(w, n_heads, perm=None):
    rows = w.shape[0]
    w = w.reshape(rows, n_heads, HEAD_DIM)
    if perm is not None:
        w = w[:, :, perm]
    return jnp.stack([w, w], axis=2).reshape(rows, n_heads * 2 * HEAD_DIM)


def _assemble_w_in(w):
    dm = w.shape[0]
    perm = jnp.array(_AXIAL_PERM, jnp.int32)
    a, b, c, d = w[:, 0:416], w[:, 416:928], w[:, 928:3232], w[:, 3232:3744]
    z = lambda n: jnp.zeros((dm, n), w.dtype)
    kr = a[:, 384:416]
    a_seg = jnp.concatenate([a[:, 0:384], kr[:, 0:16], z(16), kr[:, 16:32], z(16), z(HALF)], axis=1)
    bq = b[:, 0:256].reshape(dm, 4, HEAD_DIM)[:, :, perm].reshape(dm, 256)
    b_seg = jnp.concatenate([bq, _dup_heads(b[:, 256:384], 2, perm)], axis=1)
    d_seg = jnp.concatenate([d[:, 0:256], _dup_heads(d[:, 256:384], 2), _dup_heads(d[:, 384:512], 2)], axis=1)
    w_vbt = _dup_heads(b[:, 384:512], 2).T.astype(BF16)
    return jnp.concatenate([a_seg, b_seg, c, d_seg], axis=1).astype(BF16), w_vbt


def _assemble_a(w_uq, w_ukv):
    zq = lambda n: jnp.zeros((w_uq.shape[0], n), w_uq.dtype)
    zk = lambda n: jnp.zeros((w_ukv.shape[0], n), w_ukv.dtype)
    dq = A_NOPE + A_ROPE
    q_cols, k_cols, v_cols = [], [], []
    for hh in range(A_HEADS):
        q = w_uq[:, hh * dq:(hh + 1) * dq]
        q_cols += [q[:, 0:A_NOPE], q[:, A_NOPE:A_NOPE + 16], zq(16), q[:, A_NOPE + 16:dq], zq(16)]
        kv = w_ukv[:, hh * (A_NOPE + A_V):(hh + 1) * (A_NOPE + A_V)]
        k_cols += [kv[:, 0:A_NOPE], zk(HALF)]
        v_cols += [kv[:, A_NOPE:A_NOPE + A_V]]
    cat = lambda cols: jnp.concatenate(cols, axis=1).astype(BF16)
    return cat(q_cols), cat(k_cols), cat(v_cols).T


def _gain_pair(g, perm=None):
    if perm is not None:
        g = g[jnp.array(perm, jnp.int32)]
    return jnp.tile(g, 2).reshape(1, LANES).astype(F32)


def kernel(x, p, w_in, a_qa_g, a_kva_g, a_w_uq, a_w_ukv, b_q_g, b_k_g, d_sink, w_branch, w_out,
           mix_pre_g, mix_post_g, ffn_pre_g, ffn_post_g, ffn_w_gate, ffn_w_up, ffn_w_down,
           router_w, router_b, moe_w_gate, moe_w_up, moe_w_down, ple_w_proj, ple_w_gate, ple_post_g):
    bsz, seq, dm = x.shape
    depth = w_in.shape[0]
    n = bsz * seq
    tm = min(512, seq)
    tq_flash = min(512, seq)
    tk_flash = min(512, seq)
    tabs = _tables(seq)
    row = lambda g: g.reshape(1, -1).astype(F32)
    x2 = x.reshape(n, dm)

    for i in range(depth):
        w_all, wvb = _assemble_w_in(w_in[i])
        wuq, wk, wv = _assemble_a(a_w_uq[i], a_w_ukv[i])
        z, qa, ka, vat, vbt, zc1, zc2 = _in_proj(
            x2, row(mix_pre_g[i]), w_all, tabs,
            _gain_pair(b_q_g[i], _AXIAL_PERM), _gain_pair(b_k_g[i], _AXIAL_PERM),
            row(a_qa_g[i]), row(a_kva_g[i]), wuq, wk, wv, wvb, seq, tm)
        z3 = z.reshape(bsz, seq, ZMAIN_COLS)
        o_a = _flash(qa, ka, vat, q_col0=0, k_col0=0, v_row0=0, seq=seq, packed=False,
                     tq=tq_flash, tk=tk_flash)
        o_b = _flash(z, z, vbt, q_col0=ZB, k_col0=ZB + 256, v_row0=0, seq=seq, packed=True,
                     tq=tq_flash, tk=tk_flash)
        o_c, l_c = [], []
        for (win, dil), (src, col0, rc) in zip(C_PATTERNS, ((z3, ZC, ZMAIN_COLS), (zc1, 0, 768), (zc2, 0, 768))):
            og, lg = _banded(src, src, src, q_col0=col0, k_col0=col0 + 256, v_col0=col0 + 512,
                             row_cols=rc, dil=dil, seq=seq, hw=win // (2 * dil), tq=min(256, seq // dil),
                             want_lse=True, out_dtype=F32)
            o_c.append(og)
            l_c.append(lg)
        o_c[0] = o_c[0].reshape(n, 2 * LANES)
        l_c[0] = l_c[0].reshape(n, 2 * LANES)
        (o_d,) = _banded(z3, z3, z3, q_col0=ZD, k_col0=ZD + 256, v_col0=ZD + 512,
                         row_cols=ZMAIN_COLS, dil=1, seq=seq, hw=D_HALF_WINDOW, tq=min(256, seq),
                         sink=d_sink[i].astype(F32))
        o_d = o_d.reshape(n, 2 * LANES)
        wg_gate = w_in[i][:, 3744:].astype(BF16)
        x2 = _merge(x2, o_a, o_b, o_c, l_c, o_d, row(mix_pre_g[i]), wg_gate,
                    w_branch[i].astype(BF16), w_out[i].astype(BF16), row(mix_post_g[i]), seq, tm)

        p2 = p[i].reshape(n, -1)
        wpg = ple_w_gate[i].astype(BF16)
        wpp = ple_w_proj[i].astype(BF16)
        j = i // 2
        if i % 2 == 0:
            f = _ffn(x2, row(ffn_pre_g[i]), ffn_w_gate[j].astype(BF16), ffn_w_up[j].astype(BF16),
                     ffn_w_down[j].astype(BF16), min(1024, n), 512)
            x2 = _post(x2, f, p2, row(ffn_post_g[i]), wpg, wpp, row(ple_post_g[i]), tm)
        else:
            wr32 = jnp.zeros((dm, LANES), F32).at[:, :N_EXPERTS].set(router_w[j].astype(F32))
            wr_hi = wr32.astype(BF16)
            wr = jnp.stack([wr_hi, (wr32 - wr_hi.astype(F32)).astype(BF16)])
            br = jnp.full((1, LANES), NEG_INF, F32).at[0, :N_EXPERTS].set(router_b[j].astype(F32))
            x2 = _moe(x2, p2, row(ffn_pre_g[i]), wr, br, moe_w_gate[j].astype(BF16),
                      moe_w_up[j].astype(BF16), moe_w_down[j].astype(BF16), row(ffn_post_g[i]),
                      wpg, wpp, row(ple_post_g[i]), tm_r=tm, tt=min(256, n), tm_e=512, tf_e=896)
    return x2.reshape(bsz, seq, dm)
```

```python
import functools

import jax
import jax.numpy as jnp
from jax import lax
from jax.experimental import pallas as pl
from jax.experimental.pallas import tpu as pltpu

F32 = jnp.float32
BF16 = jnp.bfloat16

GRID_W = 64
HEAD_DIM = 64
ROPE_THETA = 10000.0
NORM_EPS = 1e-6
NEG_INF = -1e30
A_HEADS = 4
A_Q_RANK = 256
A_KV_RANK = 128
A_NOPE = 64
A_ROPE = 32
A_V = 64
C_PATTERNS = ((128, 1), (512, 4), (2048, 16))
D_HALF_WINDOW = 128
N_BRANCHES = 4
N_EXPERTS = 8
TOP_K = 2

LANES = 128
HALF = 64
VMEM_MB = 1024 * 1024
LOG2E = 1.4426950408889634
BAND_SUB = 128

ZB = 0
ZC = 512
ZD = 512 + 768
ZMAIN_COLS = ZD + 768
A_SEG = 512
B_SEG = 512


def _cparams(sem, vmem_mb):
    return pltpu.CompilerParams(dimension_semantics=sem, vmem_limit_bytes=vmem_mb * VMEM_MB)


def _resident(shape):
    nd = len(shape)
    return pl.BlockSpec(shape, lambda *_: (0,) * nd, pipeline_mode=pl.Buffered(1))


def _rms(xf, g):
    return xf * lax.rsqrt(jnp.mean(xf * xf, axis=-1, keepdims=True) + NORM_EPS) * g


def _sigmoid(x):
    return 1.0 / (1.0 + jnp.exp(-x))


def _swap32(a):
    lane = lax.broadcasted_iota(jnp.int32, a.shape, 1)
    fwd = pltpu.roll(a, LANES - 32, 1)
    bwd = pltpu.roll(a, 32, 1)
    return jnp.where((lane & 32) == 0, fwd, bwd)


def _rope(a, cos, sin):
    outs = []
    for c in range(a.shape[1] // LANES):
        ch = a[:, c * LANES:(c + 1) * LANES]
        outs.append(ch * cos + _swap32(ch) * sin)
    return outs[0] if len(outs) == 1 else jnp.concatenate(outs, axis=1)


def _head_norm(a, g, bd):
    outs = []
    for c in range(a.shape[1] // LANES):
        ch = a[:, c * LANES:(c + 1) * LANES]
        sq = ch * ch
        hi = sq.astype(BF16)
        lo = (sq - hi.astype(F32)).astype(BF16)
        ms = (jnp.dot(hi, bd, preferred_element_type=F32)
              + jnp.dot(lo, bd, preferred_element_type=F32))
        outs.append(ch * lax.rsqrt(ms + NORM_EPS) * g)
    return outs[0] if len(outs) == 1 else jnp.concatenate(outs, axis=1)


def _in_proj_kernel(x_ref, g_ref, w_ref, tabf_ref, tabx_ref, taba_ref, tabq_ref,
                    bqg_ref, bkg_ref, aqg_ref, akvg_ref, wuq_ref, wk_ref, wv_ref, wvb_ref,
                    z_ref, qa_ref, ka_ref, vat_ref, vbt_ref, zc1_ref, zc2_ref, cls_scr):
    h = _rms(x_ref[...], g_ref[...]).astype(BF16)
    cf, sf = tabf_ref[0], tabf_ref[1]
    cx, sx = tabx_ref[0], tabx_ref[1]
    q_scale = HEAD_DIM ** -0.5

    acc = jnp.dot(h, w_ref[:, 0:A_SEG], preferred_element_type=F32)
    nq = _rms(acc[:, 0:A_Q_RANK], aqg_ref[...]).astype(BF16)
    nkv = _rms(acc[:, A_Q_RANK:A_Q_RANK + A_KV_RANK], akvg_ref[...]).astype(BF16)
    kr = _rope(acc[:, 384:512], taba_ref[0], taba_ref[1]).astype(BF16)
    qa = jnp.dot(nq, wuq_ref[...], preferred_element_type=F32)
    qa = _rope(qa, tabq_ref[0], tabq_ref[1]) * ((A_NOPE + A_ROPE) ** -0.5 * LOG2E)
    qa_ref[...] = qa.astype(qa_ref.dtype)
    r = lax.broadcasted_iota(jnp.int32, (LANES, A_HEADS * LANES), 0)
    c = lax.broadcasted_iota(jnp.int32, (LANES, A_HEADS * LANES), 1)
    place = jnp.where((r < HALF) & ((c & (LANES - 1)) == r + HALF), 1.0, 0.0).astype(BF16)
    ka = (jnp.dot(nkv, wk_ref[...], preferred_element_type=F32)
          + jnp.dot(kr, place, preferred_element_type=F32))
    ka_ref[...] = ka.astype(ka_ref.dtype)
    nt_dims = (((1,), (1,)), ((), ()))
    vat_ref[...] = lax.dot_general(wv_ref[...], nkv, nt_dims, preferred_element_type=F32).astype(vat_ref.dtype)
    vbt_ref[...] = lax.dot_general(wvb_ref[...], h, nt_dims, preferred_element_type=F32).astype(vbt_ref.dtype)

    rr = lax.broadcasted_iota(jnp.int32, (LANES, LANES), 0)
    cc = lax.broadcasted_iota(jnp.int32, (LANES, LANES), 1)
    bd = jnp.where((rr >> 6) == (cc >> 6), 1.0 / HEAD_DIM, 0.0).astype(BF16)
    acc = jnp.dot(h, w_ref[:, A_SEG:A_SEG + B_SEG], preferred_element_type=F32)
    q = _rope(_head_norm(acc[:, 0:256], bqg_ref[...], bd), cx, sx) * (q_scale * LOG2E)
    k = _rope(_head_norm(acc[:, 256:512], bkg_ref[...], bd), cx, sx)
    z_ref[:, ZB:ZB + 256] = q.astype(z_ref.dtype)
    z_ref[:, ZB + 256:ZB + 512] = k.astype(z_ref.dtype)

    tm = h.shape[0]
    for widx, zoff, cls_ref, dil in ((0, ZC, None, 1), (1, 0, zc1_ref, C_PATTERNS[1][1]),
                                     (2, 0, zc2_ref, C_PATTERNS[2][1]), (3, ZD, None, 1)):
        base = A_SEG + B_SEG + widx * 768
        acc = jnp.dot(h, w_ref[:, base:base + 768], preferred_element_type=F32)
        q = _rope(acc[:, 0:256], cf, sf) * q_scale
        k = _rope(acc[:, 256:512], cf, sf)
        if cls_ref is None:
            z_ref[:, zoff:zoff + 256] = q.astype(z_ref.dtype)
            z_ref[:, zoff + 256:zoff + 512] = k.astype(z_ref.dtype)
            z_ref[:, zoff + 512:zoff + 768] = acc[:, 512:768].astype(z_ref.dtype)
        else:
            qkv = (q[:, 0:LANES], q[:, LANES:], k[:, 0:LANES], k[:, LANES:],
                   acc[:, 512:512 + LANES], acc[:, 512 + LANES:768])
            for ch, val in enumerate(qkv):
                cls_scr[ch] = val
            for c in range(dil):
                for ch in range(len(qkv)):
                    col = c * 768 + ch * LANES
                    cls_ref[:, col:col + LANES] = cls_scr[ch, pl.ds(c, tm // dil, stride=dil), :].astype(cls_ref.dtype)


def _in_proj(x2, gain, w, tabs, bqg, bkg, aqg, akvg, wuq, wk, wv, wvb, seq, tm):
    n, dm = x2.shape
    nt = seq // tm
    bsz = n // seq
    d1, d2 = C_PATTERNS[1][1], C_PATTERNS[2][1]
    tab_spec = pl.BlockSpec((2, tm, LANES), lambda i: (0, i % nt, 0))
    row = lambda cols: pl.BlockSpec((tm, cols), lambda i: (i, 0))
    vt_spec = pl.BlockSpec((None, 2 * LANES, tm), lambda i: (i // nt, 0, i % nt))
    cls_spec = lambda d: pl.BlockSpec((None, tm // d, d * 768), lambda i: (i // nt, i % nt, 0))
    return pl.pallas_call(
        _in_proj_kernel,
        grid=(n // tm,),
        in_specs=[row(dm), _resident((1, dm)), _resident(w.shape),
                  tab_spec, tab_spec, tab_spec, tab_spec,
                  _resident((1, LANES)), _resident((1, LANES)),
                  _resident((1, A_Q_RANK)), _resident((1, A_KV_RANK)),
                  _resident(wuq.shape), _resident(wk.shape), _resident(wv.shape), _resident(wvb.shape)],
        out_specs=[row(ZMAIN_COLS), row(A_HEADS * LANES), row(A_HEADS * LANES),
                   vt_spec, vt_spec, cls_spec(d1), cls_spec(d2)],
        out_shape=[jax.ShapeDtypeStruct((n, ZMAIN_COLS), BF16),
                   jax.ShapeDtypeStruct((n, A_HEADS * LANES), BF16),
                   jax.ShapeDtypeStruct((n, A_HEADS * LANES), BF16),
                   jax.ShapeDtypeStruct((bsz, 2 * LANES, seq), BF16),
                   jax.ShapeDtypeStruct((bsz, 2 * LANES, seq), BF16),
                   jax.ShapeDtypeStruct((bsz, seq // d1, d1 * 768), BF16),
                   jax.ShapeDtypeStruct((bsz, seq // d2, d2 * 768), BF16)],
        scratch_shapes=[pltpu.VMEM((768 // LANES, tm, LANES), F32)],
        compiler_params=_cparams(("parallel",), 48),
        name="in_proj",
    )(x2, gain, w, tabs["full"], tabs["axial"], tabs["a_k"], tabs["a_q"],
      bqg, bkg, aqg, akvg, wuq, wk, wv, wvb)


def _flash_kernel(q_ref, k_ref, vt_ref, o_ref, s_a, s_b, *, tk, packed):
    tq = q_ref.shape[0]
    nk = k_ref.shape[0] // tk
    lane = lax.broadcasted_iota(jnp.int32, (1, LANES), 1)
    if packed:
        qs = [jnp.where((lane < HALF) == (r == 0), q_ref[...], 0) for r in range(2)]
        kcs = [0, 0]
    else:
        qs = [q_ref[:, r * LANES:(r + 1) * LANES] for r in range(2)]
        kcs = [0, LANES]

    def scores(kk, dst):
        ks = pl.multiple_of(kk * tk, tk)
        for r in range(2):
            k = k_ref[pl.ds(ks, tk), kcs[r]:kcs[r] + LANES]
            dst[r] = lax.dot_general(k, qs[r], (((1,), (1,)), ((), ())), preferred_element_type=F32)

    def consume(kk, src, carry):
        ks = pl.multiple_of(kk * tk, tk)
        new = []
        for r in range(2):
            m, l, acc = carry[r]
            s = src[r]
            m_new = jnp.maximum(m, jnp.max(s, axis=0, keepdims=True))
            alpha = jnp.exp2(m - m_new)
            p = jnp.exp2(s - m_new)
            l = alpha * l + jnp.sum(p, axis=0, keepdims=True)
            vt = vt_ref[r * HALF:(r + 1) * HALF, pl.ds(ks, tk)]
            acc = alpha * acc + jnp.dot(vt, p.astype(BF16), preferred_element_type=F32)
            new.append((m_new, l, acc))
        return tuple(new)

    def pair(jj, carry):
        c0 = 2 * jj
        scores(c0 + 1, s_b)
        carry = consume(c0, s_a, carry)
        scores(c0 + 2, s_a)
        return consume(c0 + 1, s_b, carry)

    init = tuple((jnp.full((1, tq), NEG_INF, F32), jnp.zeros((1, tq), F32), jnp.zeros((HALF, tq), F32))
                 for _ in range(2))
    scores(0, s_a)
    carry = lax.fori_loop(0, nk // 2 - 1, pair, init)
    scores(nk - 1, s_b)
    carry = consume(nk - 2, s_a, carry)
    carry = consume(nk - 1, s_b, carry)
    halves = [acc / l for (_, l, acc) in carry]
    o_ref[...] = jnp.concatenate(halves, axis=0).T.astype(o_ref.dtype)


def _flash(q, k, vt, *, q_col0, k_col0, v_row0, seq, packed, tq, tk):
    n = q.shape[0]
    bsz = n // seq
    qw = LANES if packed else 2 * LANES
    nq = seq // tq
    k3 = k.reshape(bsz, seq, k.shape[1])
    qb, kb, vb = q_col0 // qw, k_col0 // qw, v_row0 // LANES
    return pl.pallas_call(
        functools.partial(_flash_kernel, tk=tk, packed=packed),
        grid=(bsz, 2, nq),
        in_specs=[pl.BlockSpec((tq, qw), lambda b, j, i: (b * nq + i, qb + j)),
                  pl.BlockSpec((None, seq, qw), lambda b, j, i: (b, 0, kb + j)),
                  pl.BlockSpec((None, LANES, seq), lambda b, j, i: (b, vb + j, 0))],
        out_specs=pl.BlockSpec((tq, LANES), lambda b, j, i: (b * nq + i, j)),
        out_shape=jax.ShapeDtypeStruct((n, 2 * LANES), BF16),
        scratch_shapes=[pltpu.VMEM((2, tk, tq), F32), pltpu.VMEM((2, tk, tq), F32)],
        compiler_params=_cparams(("parallel", "parallel", "parallel"), 48),
        name="flash_packed" if packed else "flash_slots",
    )(q, k3, vt)


def _banded_kernel(*refs, hw, has_sink, want_lse):
    if has_sink:
        sink_ref, q_ref, k_ref, v_ref = refs[:4]
        outs = refs[4:]
    else:
        q_ref, k_ref, v_ref = refs[:3]
        outs = refs[3:]
    o_ref = outs[0]
    tq = q_ref.shape[0]
    length = k_ref.shape[0]
    sb = min(BAND_SUB, tq)
    win = min(sb + 2 * hw, length)
    i = pl.program_id(1)
    lane = lax.broadcasted_iota(jnp.int32, (1, LANES), 1)
    first = lane < HALF
    cols = [slice(j * LANES, (j + 1) * LANES) for j in range(2)]
    subs = list(range(tq // sb))
    kss, valids = [], []
    for u in subs:
        q0 = i * tq + u * sb
        ks = pl.multiple_of(jnp.clip(q0 - hw, 0, length - win), HALF)
        qpos = q0 + lax.broadcasted_iota(jnp.int32, (sb, win), 0)
        kpos = ks + lax.broadcasted_iota(jnp.int32, (sb, win), 1)
        kss.append(ks)
        valids.append(jnp.abs(qpos - kpos) <= hw)
    chains = [(u, j, r) for u in subs for j in range(2) for r in range(2)]
    kws = {(u, j): k_ref[pl.ds(kss[u], win), cols[j]] for u in subs for j in range(2)}
    vws = {(u, j): v_ref[pl.ds(kss[u], win), cols[j]] for u in subs for j in range(2)}
    ss = []
    for u, j, r in chains:
        q = jnp.where(first == (r == 0), q_ref[u * sb:(u + 1) * sb, cols[j]], 0)
        ss.append(lax.dot_general(q, kws[u, j], (((1,), (1,)), ((), ())), preferred_element_type=F32))
    ss = [jnp.where(valids[u], s, NEG_INF) for s, (u, j, r) in zip(ss, chains)]
    ms = [jnp.max(s, axis=-1, keepdims=True) for s in ss]
    if has_sink:
        sinks = [sink_ref[2 * j + r] for u, j, r in chains]
        ms = [jnp.maximum(m, sk) for m, sk in zip(ms, sinks)]
    es = [jnp.exp(s - m) for s, m in zip(ss, ms)]
    ls = [jnp.sum(e, axis=-1, keepdims=True) for e in es]
    if has_sink:
        ls = [l + jnp.exp(sk - m) for l, sk, m in zip(ls, sinks, ms)]
    os_ = [jnp.dot(e.astype(BF16), vws[u, j], preferred_element_type=F32) / l
           for e, l, (u, j, r) in zip(es, ls, chains)]
    for n in range(0, len(chains), 2):
        u, j, _ = chains[n]
        rows = slice(u * sb, (u + 1) * sb)
        o_ref[rows, cols[j]] = jnp.where(first, os_[n], os_[n + 1]).astype(o_ref.dtype)
        if want_lse:
            lses = [jnp.broadcast_to(ms[n + r] + jnp.log(ls[n + r]), (sb, LANES)) for r in range(2)]
            outs[1][rows, cols[j]] = jnp.where(first, lses[0], lses[1])


def _banded(q, k, v, *, q_col0, k_col0, v_col0, row_cols, dil, seq, hw, tq, sink=None,
            want_lse=False, out_dtype=BF16):
    bsz, ls, _ = q.shape
    nq = ls // tq
    pw = 2 * LANES
    rb = row_cols // pw
    qb, kb, vb = q_col0 // pw, k_col0 // pw, v_col0 // pw
    in_specs = [pl.BlockSpec((None, tq, pw), lambda bc, i: (bc // dil, i, (bc % dil) * rb + qb)),
                pl.BlockSpec((None, ls, pw), lambda bc, i: (bc // dil, 0, (bc % dil) * rb + kb)),
                pl.BlockSpec((None, ls, pw), lambda bc, i: (bc // dil, 0, (bc % dil) * rb + vb))]
    args = [q, k, v]
    if sink is not None:
        in_specs = [pl.BlockSpec(memory_space=pltpu.SMEM)] + in_specs
        args = [sink] + args
    o_spec = pl.BlockSpec((None, tq, pw), lambda bc, i: (bc // dil, i, bc % dil))
    out_specs = [o_spec]
    out_shape = [jax.ShapeDtypeStruct((bsz, ls, dil * pw), out_dtype)]
    if want_lse:
        out_specs.append(o_spec)
        out_shape.append(jax.ShapeDtypeStruct((bsz, ls, dil * pw), F32))
    res = pl.pallas_call(
        functools.partial(_banded_kernel, hw=hw, has_sink=sink is not None, want_lse=want_lse),
        grid=(bsz * dil, nq),
        in_specs=in_specs,
        out_specs=out_specs,
        out_shape=out_shape,
        compiler_params=_cparams(("parallel", "parallel"), 48),
        name="banded_d%d" % dil,
    )(*args)
    return res


def _merge_kernel(x_ref, oa_ref, ob_ref, oc0_ref, oc1_ref, oc2_ref, l0_ref, l1_ref, l2_ref,
                  od_ref, gpre_ref, wg_ref, wb_ref, wo_ref, gpost_ref, out_ref, tok_scr):
    xf = x_ref[...]
    tm, dm = xf.shape
    h = _rms(xf, gpre_ref[...]).astype(BF16)
    pw = 2 * LANES
    toks = []
    for n, src in enumerate((oc1_ref, l1_ref, oc2_ref, l2_ref)):
        dil = src.shape[1] // pw
        for c in range(dil):
            for hp in range(2):
                col = c * pw + hp * LANES
                tok_scr[2 * n + hp, pl.ds(c, tm // dil, stride=dil), :] = src[:, col:col + LANES]
        toks.append(jnp.concatenate([tok_scr[2 * n], tok_scr[2 * n + 1]], axis=1))
    oc1, l1, oc2, l2 = toks
    l0 = l0_ref[...]
    mx = jnp.maximum(jnp.maximum(l0, l1), l2)
    w0, w1, w2 = jnp.exp(l0 - mx), jnp.exp(l1 - mx), jnp.exp(l2 - mx)
    oc = (w0 * oc0_ref[...] + w1 * oc1 + w2 * oc2) / (w0 + w1 + w2)
    branches = (oa_ref[...], ob_ref[...], oc.astype(BF16), od_ref[...])
    merged = None
    for n, o in enumerate(branches):
        gate = _sigmoid(jnp.dot(h, wg_ref[:, n * dm:(n + 1) * dm], preferred_element_type=F32))
        term = gate * jnp.dot(o, wb_ref[n], preferred_element_type=F32)
        merged = term if merged is None else merged + term
    y = jnp.dot(merged.astype(BF16), wo_ref[...], preferred_element_type=F32)
    out_ref[...] = xf + _rms(y, gpost_ref[...])


def _merge(x2, oa, ob, oc, lc, od, gpre, wg, wb, wo, gpost, seq, tm):
    n, dm = x2.shape
    nt = seq // tm
    row = lambda cols: pl.BlockSpec((tm, cols), lambda i: (i, 0))
    bw = 2 * LANES
    cls = lambda a: pl.BlockSpec((None, tm // (a.shape[2] // bw), a.shape[2]), lambda i: (i // nt, i % nt, 0))
    return pl.pallas_call(
        _merge_kernel,
        grid=(n // tm,),
        in_specs=[row(dm), row(bw), row(bw), row(bw), cls(oc[1]), cls(oc[2]), row(bw), cls(lc[1]), cls(lc[2]),
                  row(bw), _resident((1, dm)), _resident(wg.shape), _resident(wb.shape),
                  _resident(wo.shape), _resident((1, dm))],
        out_specs=row(dm),
        out_shape=jax.ShapeDtypeStruct((n, dm), F32),
        scratch_shapes=[pltpu.VMEM((8, tm, LANES), F32)],
        compiler_params=_cparams(("parallel",), 56),
        name="merge",
    )(x2, oa, ob, oc[0], oc[1], oc[2], lc[0], lc[1], lc[2], od, gpre, wg, wb, wo, gpost)


def _ffn_kernel(x_ref, g_ref, wg_ref, wu_ref, wd_ref, f_ref, h_scr, acc_scr):
    j = pl.program_id(1)

    @pl.when(j == 0)
    def _():
        h_scr[...] = _rms(x_ref[...], g_ref[...]).astype(BF16)
        acc_scr[...] = jnp.zeros_like(acc_scr)

    h = h_scr[...]
    a = jnp.dot(h, wg_ref[...], preferred_element_type=F32)
    u = jnp.dot(h, wu_ref[...], preferred_element_type=F32)
    act = (a * _sigmoid(a) * u).astype(BF16)
    acc_scr[...] += jnp.dot(act, wd_ref[...], preferred_element_type=F32)

    @pl.when(j == pl.num_programs(1) - 1)
    def _():
        f_ref[...] = acc_scr[...]


def _ffn(x2, gain, wg, wu, wd, tm, tf):
    n, dm = x2.shape
    dff = wg.shape[1]
    return pl.pallas_call(
        _ffn_kernel,
        grid=(n // tm, dff // tf),
        in_specs=[pl.BlockSpec((tm, dm), lambda i, j: (i, 0)),
                  pl.BlockSpec((1, dm), lambda i, j: (0, 0)),
                  pl.BlockSpec((dm, tf), lambda i, j: (0, j)),
                  pl.BlockSpec((dm, tf), lambda i, j: (0, j)),
                  pl.BlockSpec((tf, dm), lambda i, j: (j, 0))],
        out_specs=pl.BlockSpec((tm, dm), lambda i, j: (i, 0)),
        out_shape=jax.ShapeDtypeStruct((n, dm), F32),
        scratch_shapes=[pltpu.VMEM((tm, dm), BF16), pltpu.VMEM((tm, dm), F32)],
        compiler_params=_cparams(("parallel", "arbitrary"), 48),
        name="ffn",
    )(x2, gain, wg, wu, wd)


def _post_math(xf, f, p_ref, gf_ref, wpg_ref, wpp_ref, gp_ref, out_ref):
    x2 = xf + _rms(f, gf_ref[...])
    gate = _sigmoid(jnp.dot(x2.astype(BF16), wpg_ref[...], preferred_element_type=F32))
    e = jnp.dot(p_ref[...].astype(BF16), wpp_ref[...], preferred_element_type=F32) * gate
    out_ref[...] = x2 + _rms(e, gp_ref[...])


def _post_kernel(x_ref, f_ref, p_ref, gf_ref, wpg_ref, wpp_ref, gp_ref, out_ref):
    _post_math(x_ref[...], f_ref[...], p_ref, gf_ref, wpg_ref, wpp_ref, gp_ref, out_ref)


def _post(x2, f, p2, gf, wpg, wpp, gp, tm):
    n, dm = x2.shape
    row = lambda cols: pl.BlockSpec((tm, cols), lambda i: (i, 0))
    return pl.pallas_call(
        _post_kernel,
        grid=(n // tm,),
        in_specs=[row(dm), row(dm), row(p2.shape[1]), _resident((1, dm)), _resident(wpg.shape),
                  _resident(wpp.shape), _resident((1, dm))],
        out_specs=row(dm),
        out_shape=jax.ShapeDtypeStruct((n, dm), F32),
        compiler_params=_cparams(("parallel",), 48),
        name="post",
    )(x2, f, p2, gf, wpg, wpp, gp)


def _router_kernel(x_ref, g_ref, wr_ref, br_ref, hs_ref, info_ref):
    hf = _rms(x_ref[...], g_ref[...])
    hb = hf.astype(BF16)
    tm, dm = hf.shape
    sub = dm // LANES
    for s in range(sub):
        hs_ref[pl.ds(s, tm, stride=sub), :] = hf[:, s * LANES:(s + 1) * LANES]
    hl = (hf - hb.astype(F32)).astype(BF16)
    whi, wlo = wr_ref[0], wr_ref[1]
    logits = (jnp.dot(hb, whi, preferred_element_type=F32) + jnp.dot(hl, whi, preferred_element_type=F32)
              + jnp.dot(hb, wlo, preferred_element_type=F32)) + br_ref[...]
    lane = lax.broadcasted_iota(jnp.int32, logits.shape, 1).astype(F32)
    m1 = jnp.max(logits, axis=-1, keepdims=True)
    i1 = jnp.min(jnp.where(logits == m1, lane, float(LANES)), axis=-1, keepdims=True)
    rest = jnp.where(lane == i1, NEG_INF, logits)
    m2 = jnp.max(rest, axis=-1, keepdims=True)
    i2 = jnp.min(jnp.where(rest == m2, lane, float(LANES)), axis=-1, keepdims=True)
    e2 = jnp.exp(m2 - m1)
    g1 = 1.0 / (1.0 + e2)
    g2 = e2 / (1.0 + e2)
    info = jnp.where(lane == 0.0, i1, jnp.where(lane == 1.0, i2, jnp.where(lane == 2.0, g1, g2)))
    info_ref[...] = info


def _router(x2, gain, wr, br, tm):
    n, dm = x2.shape
    row = lambda cols: pl.BlockSpec((tm, cols), lambda i: (i, 0))
    return pl.pallas_call(
        _router_kernel,
        grid=(n // tm,),
        in_specs=[row(dm), _resident((1, dm)), _resident(wr.shape), _resident((1, LANES))],
        out_specs=[pl.BlockSpec((tm * (dm // LANES), LANES), lambda i: (i, 0)), row(LANES)],
        out_shape=[jax.ShapeDtypeStruct((n * (dm // LANES), LANES), F32),
                   jax.ShapeDtypeStruct((n, LANES), F32)],
        compiler_params=_cparams(("parallel",), 32),
        name="router",
    )(x2, gain, wr, br)


def _experts_kernel(be_ref, src0_ref, srcn_ref, dstp_ref, dstl_ref, hs_hbm, wg_ref, wu_ref, wd_ref,
                    yt_hbm, xg, xb, acc, ys, gsem, ssem, *, tm, sub):
    del be_ref
    i = pl.program_id(0)
    j = pl.program_id(1)
    nb = pl.num_programs(0)
    nf = pl.num_programs(1)
    slot = i % 2
    other = 1 - slot
    per_step = tm // nf

    def gather(idx_ref, r, dslot):
        src = pl.multiple_of(idx_ref[0, r] * sub, sub)
        dst = pl.multiple_of(r * sub, sub)
        return pltpu.make_async_copy(hs_hbm.at[pl.ds(src, sub)], xg.at[dslot, pl.ds(dst, sub)],
                                     gsem.at[dslot])

    def scatter(idx_ref, r, sslot):
        src = pl.multiple_of(r * sub, sub)
        dst = pl.multiple_of(idx_ref[0, r] * sub, sub)
        return pltpu.make_async_copy(ys.at[sslot, pl.ds(src, sub)], yt_hbm.at[pl.ds(dst, sub)],
                                     ssem.at[sslot])

    def wait_gather(dslot):
        pltpu.make_async_copy(hs_hbm.at[pl.ds(0, tm * sub)], xg.at[dslot], gsem.at[dslot]).wait()

    def wait_scatter(sslot):
        pltpu.make_async_copy(ys.at[sslot], yt_hbm.at[pl.ds(0, tm * sub)], ssem.at[sslot]).wait()

    @pl.when((i == 0) & (j == 0))
    def _():
        ys[1] = jnp.zeros(ys.shape[1:], ys.dtype)

        def start(r, c):
            gather(src0_ref, r, 0).start()
            return c

        lax.fori_loop(0, tm, start, 0)

    @pl.when(j == 0)
    def _():
        wait_gather(slot)
        for s in range(sub):
            xb[:, s * LANES:(s + 1) * LANES] = xg[slot, pl.ds(s, tm, stride=sub), :].astype(BF16)
        acc[...] = jnp.zeros_like(acc)

    x = xb[...]
    a = jnp.dot(x, wg_ref[...], preferred_element_type=F32)
    u = jnp.dot(x, wu_ref[...], preferred_element_type=F32)
    act = (a * _sigmoid(a) * u).astype(BF16)
    acc[...] += jnp.dot(act, wd_ref[...], preferred_element_type=F32)
    r0 = j * per_step
    for t in range(per_step):
        gather(srcn_ref, r0 + t, other).start()
    for t in range(per_step):
        scatter(dstp_ref, r0 + t, other).start()

    @pl.when(j == nf - 1)
    def _():
        @pl.when(i >= 1)
        def _():
            wait_scatter(slot)

        for s in range(sub):
            ys[slot, pl.ds(s, tm, stride=sub), :] = acc[:, s * LANES:(s + 1) * LANES]

        @pl.when(i == nb - 1)
        def _():
            def start(r, c):
                scatter(dstl_ref, r, slot).start()
                return c

            lax.fori_loop(0, tm, start, 0)
            wait_scatter(slot)
            wait_scatter(other)
            wait_gather(other)


def _experts(hs, blk_e, src_tok, dst_row, n_slabs, wg, wu, wd, tm, tf):
    nb = src_tok.shape[0]
    dm, dff = wg.shape[1], wg.shape[2]
    sub = dm // LANES
    nf = dff // tf
    smem = lambda imap: pl.BlockSpec((None, 1, tm), imap, memory_space=pltpu.SMEM)
    grid_spec = pltpu.PrefetchScalarGridSpec(
        num_scalar_prefetch=1,
        grid=(nb, nf),
        in_specs=[smem(lambda i, j, be: (0, 0, 0)),
                  smem(lambda i, j, be: (jnp.minimum(i + 1, nb - 1), 0, 0)),
                  smem(lambda i, j, be: (i, 0, 0)),
                  smem(lambda i, j, be: (nb, 0, 0)),
                  pl.BlockSpec(memory_space=pl.ANY),
                  pl.BlockSpec((None, dm, tf), lambda i, j, be: (be[i], 0, j)),
                  pl.BlockSpec((None, dm, tf), lambda i, j, be: (be[i], 0, j)),
                  pl.BlockSpec((None, tf, dm), lambda i, j, be: (be[i], j, 0))],
        out_specs=pl.BlockSpec(memory_space=pl.ANY),
        scratch_shapes=[pltpu.VMEM((2, tm * sub, LANES), F32), pltpu.VMEM((tm, dm), BF16),
                        pltpu.VMEM((tm, dm), F32), pltpu.VMEM((2, tm * sub, LANES), F32),
                        pltpu.SemaphoreType.DMA((2,)), pltpu.SemaphoreType.DMA((2,))],
    )
    return pl.pallas_call(
        functools.partial(_experts_kernel, tm=tm, sub=sub),
        grid_spec=grid_spec,
        out_shape=jax.ShapeDtypeStruct((n_slabs * sub, LANES), F32),
        compiler_params=_cparams(("arbitrary", "arbitrary"), 48),
        name="experts",
    )(blk_e, src_tok, src_tok, dst_row, dst_row, hs, wg, wu, wd)


def _combine_kernel(y_ref, x_ref, gt_ref, p_ref, gf_ref, wpg_ref, wpp_ref, gp_ref, out_ref):
    tt, dm = x_ref.shape
    sub = dm // LANES
    gt = gt_ref[...]

    def rows(slot):
        return jnp.concatenate([y_ref[pl.ds(slot * sub + s, tt, stride=TOP_K * sub), :] for s in range(sub)],
                               axis=1)

    f = gt[:, 2:3] * rows(0) + gt[:, 3:4] * rows(1)
    _post_math(x_ref[...], f, p_ref, gf_ref, wpg_ref, wpp_ref, gp_ref, out_ref)


def _combine(yt, info, x2, p2, gf, wpg, wpp, gp, tt):
    n, dm = x2.shape
    row = lambda cols: pl.BlockSpec((tt, cols), lambda i: (i, 0))
    return pl.pallas_call(
        _combine_kernel,
        grid=(n // tt,),
        in_specs=[pl.BlockSpec((tt * TOP_K * (dm // LANES), LANES), lambda i: (i, 0)),
                  row(dm), row(LANES), row(p2.shape[1]), _resident((1, dm)),
                  _resident(wpg.shape), _resident(wpp.shape), _resident((1, dm))],
        out_specs=row(dm),
        out_shape=jax.ShapeDtypeStruct((n, dm), F32),
        compiler_params=_cparams(("parallel",), 48),
        name="combine",
    )(yt, x2, info, p2, gf, wpg, wpp, gp)


def _moe(x2, p2, gain, wr, br, wg, wu, wd, gf, wpg, wpp, gp, *, tm_r, tt, tm_e, tf_e):
    n, _ = x2.shape
    hs, info = _router(x2, gain, wr, br, tm_r)
    n_asg = n * TOP_K
    top_e = info[:, 0:TOP_K].astype(jnp.int32)
    e_flat = top_e.reshape(n_asg)
    onehot = (e_flat[:, None] == jnp.arange(N_EXPERTS, dtype=jnp.int32)[None, :]).astype(jnp.int32)
    csum = jnp.cumsum(onehot, axis=0)
    rank = jnp.sum((csum - onehot) * onehot, axis=1)
    counts = csum[-1]
    padded = ((counts + tm_e - 1) // tm_e) * tm_e
    pend = jnp.cumsum(padded)
    pstart = pend - padded
    dest = (pstart[e_flat] + rank).astype(jnp.int32)
    n_blocks = -(-n_asg // tm_e) + N_EXPERTS
    n_rows = n_blocks * tm_e
    blk_e = jnp.clip(jnp.searchsorted(pend, jnp.arange(n_blocks, dtype=jnp.int32) * tm_e, side="right"),
                     0, N_EXPERTS - 1).astype(jnp.int32)
    asg = jnp.full((n_rows,), -1, jnp.int32).at[dest].set(jnp.arange(n_asg, dtype=jnp.int32),
                                                          unique_indices=True)
    is_pad = asg < 0
    pad_rank = jnp.cumsum(is_pad.astype(jnp.int32)) - 1
    src_tok = jnp.where(is_pad, 0, asg // TOP_K).reshape(n_blocks, 1, tm_e)
    dst_row = jnp.where(is_pad, n_asg + pad_rank, asg)
    spare = n_asg + (n_rows - n_asg) + jnp.arange(tm_e, dtype=jnp.int32)
    dst_row = jnp.concatenate([spare, dst_row]).reshape(n_blocks + 1, 1, tm_e)
    n_slabs = n_asg + (n_rows - n_asg) + tm_e
    yt = _experts(hs, blk_e, src_tok, dst_row, n_slabs, wg, wu, wd, tm_e, tf_e)
    return _combine(yt, info, x2, p2, gf, wpg, wpp, gp, tt)


def _tables(seq):
    pos = jnp.arange(seq, dtype=jnp.int32)

    def cs(p, half):
        inv = jnp.power(ROPE_THETA, -jnp.arange(half, dtype=F32) / half)
        ang = p.astype(F32)[:, None] * inv[None, :]
        return jnp.cos(ang), jnp.sin(ang)

    ones = lambda w: jnp.ones((seq, w), F32)
    zeros = lambda w: jnp.zeros((seq, w), F32)
    c, s = cs(pos, HEAD_DIM // 2)
    full = jnp.stack([jnp.tile(jnp.concatenate([c, c], 1), (1, 2)),
                      jnp.tile(jnp.concatenate([-s, s], 1), (1, 2))])
    cr, sr = cs(pos // GRID_W, HEAD_DIM // 4)
    cc, sc = cs(pos % GRID_W, HEAD_DIM // 4)
    axial = jnp.stack([jnp.tile(jnp.concatenate([cr, cc, cr, cc], 1), (1, 2)),
                       jnp.tile(jnp.concatenate([-sr, -sc, sr, sc], 1), (1, 2))])
    ca, sa = cs(pos, A_ROPE // 2)
    slot_c = jnp.concatenate([ca, ones(16), ca, ones(16)], 1)
    slot_s = jnp.concatenate([-sa, zeros(16), sa, zeros(16)], 1)
    a_k = jnp.stack([jnp.concatenate([slot_c, ones(HALF)], 1), jnp.concatenate([slot_s, zeros(HALF)], 1)])
    a_q = jnp.stack([jnp.concatenate([ones(HALF), slot_c], 1), jnp.concatenate([zeros(HALF), slot_s], 1)])
    return {"full": full, "axial": axial, "a_k": a_k, "a_q": a_q}


_AXIAL_PERM = tuple(list(range(0, 16)) + list(range(32, 48)) + list(range(16, 32)) + list(range(48, 64)))


def _dup_heads(w, n_heads, perm=None):
    rows = w.shape[0]
    w = w.reshape(rows, n_heads, HEAD_DIM)
    if perm is not None:
        w = w[:, :, perm]
    return jnp.stack([w, w], axis=2).reshape(rows, n_heads * 2 * HEAD_DIM)


def _assemble_w_in(w):
    dm = w.shape[0]
    perm = jnp.array(_AXIAL_PERM, jnp.int32)
    a, b, c, d = w[:, 0:416], w[:, 416:928], w[:, 928:3232], w[:, 3232:3744]
    z = lambda n: jnp.zeros((dm, n), w.dtype)
    kr = a[:, 384:416]
    a_seg = jnp.concatenate([a[:, 0:384], kr[:, 0:16], z(16), kr[:, 16:32], z(16), z(HALF)], axis=1)
    bq = b[:, 0:256].reshape(dm, 4, HEAD_DIM)[:, :, perm].reshape(dm, 256)
    b_seg = jnp.concatenate([bq, _dup_heads(b[:, 256:384], 2, perm)], axis=1)
    d_seg = jnp.concatenate([d[:, 0:256], _dup_heads(d[:, 256:384], 2), _dup_heads(d[:, 384:512], 2)], axis=1)
    w_vbt = _dup_heads(b[:, 384:512], 2).T.astype(BF16)
    return jnp.concatenate([a_seg, b_seg, c, d_seg], axis=1).astype(BF16), w_vbt


def _assemble_a(w_uq, w_ukv):
    zq = lambda n: jnp.zeros((w_uq.shape[0], n), w_uq.dtype)
    zk = lambda n: jnp.zeros((w_ukv.shape[0], n), w_ukv.dtype)
    dq = A_NOPE + A_ROPE
    q_cols, k_cols, v_cols = [], [], []
    for hh in range(A_HEADS):
        q = w_uq[:, hh * dq:(hh + 1) * dq]
        q_cols += [q[:, 0:A_NOPE], q[:, A_NOPE:A_NOPE + 16], zq(16), q[:, A_NOPE + 16:dq], zq(16)]
        kv = w_ukv[:, hh * (A_NOPE + A_V):(hh + 1) * (A_NOPE + A_V)]
        k_cols += [kv[:, 0:A_NOPE], zk(HALF)]
        v_cols += [kv[:, A_NOPE:A_NOPE + A_V]]
    cat = lambda cols: jnp.concatenate(cols, axis=1).astype(BF16)
    return cat(q_cols), cat(k_cols), cat(v_cols).T


def _gain_pair(g, perm=None):
    if perm is not None:
        g = g[jnp.array(perm, jnp.int32)]
    return jnp.tile(g, 2).reshape(1, LANES).astype(F32)


def kernel(x, p, w_in, a_qa_g, a_kva_g, a_w_uq, a_w_ukv, b_q_g, b_k_g, d_sink, w_branch, w_out,
           mix_pre_g, mix_post_g, ffn_pre_g, ffn_post_g, ffn_w_gate, ffn_w_up, ffn_w_down,
           router_w, router_b, moe_w_gate, moe_w_up, moe_w_down, ple_w_proj, ple_w_gate, ple_post_g):
    bsz, seq, dm = x.shape
    depth = w_in.shape[0]
    n = bsz * seq
    tm = min(512, seq)
    tq_flash = min(512, seq)
    tk_flash = min(512, seq)
    tabs = _tables(seq)
    row = lambda g: g.reshape(1, -1).astype(F32)
    x2 = x.reshape(n, dm)

    for i in range(depth):
        w_all, wvb = _assemble_w_in(w_in[i])
        wuq, wk, wv = _assemble_a(a_w_uq[i], a_w_ukv[i])
        z, qa, ka, vat, vbt, zc1, zc2 = _in_proj(
            x2, row(mix_pre_g[i]), w_all, tabs,
            _gain_pair(b_q_g[i], _AXIAL_PERM), _gain_pair(b_k_g[i], _AXIAL_PERM),
            row(a_qa_g[i]), row(a_kva_g[i]), wuq, wk, wv, wvb, seq, tm)
        z3 = z.reshape(bsz, seq, ZMAIN_COLS)
        o_a = _flash(qa, ka, vat, q_col0=0, k_col0=0, v_row0=0, seq=seq, packed=False,
                     tq=tq_flash, tk=tk_flash)
        o_b = _flash(z, z, vbt, q_col0=ZB, k_col0=ZB + 256, v_row0=0, seq=seq, packed=True,
                     tq=tq_flash, tk=tk_flash)
        o_c, l_c = [], []
        for (win, dil), (src, col0, rc) in zip(C_PATTERNS, ((z3, ZC, ZMAIN_COLS), (zc1, 0, 768), (zc2, 0, 768))):
            og, lg = _banded(src, src, src, q_col0=col0, k_col0=col0 + 256, v_col0=col0 + 512,
                             row_cols=rc, dil=dil, seq=seq, hw=win // (2 * dil), tq=min(512, seq // dil),
                             want_lse=True, out_dtype=F32)
            o_c.append(og)
            l_c.append(lg)
        o_c[0] = o_c[0].reshape(n, 2 * LANES)
        l_c[0] = l_c[0].reshape(n, 2 * LANES)
        (o_d,) = _banded(z3, z3, z3, q_col0=ZD, k_col0=ZD + 256, v_col0=ZD + 512,
                         row_cols=ZMAIN_COLS, dil=1, seq=seq, hw=D_HALF_WINDOW, tq=min(512, seq),
                         sink=d_sink[i].astype(F32))
        o_d = o_d.reshape(n, 2 * LANES)
        wg_gate = w_in[i][:, 3744:].astype(BF16)
        x2 = _merge(x2, o_a, o_b, o_c, l_c, o_d, row(mix_pre_g[i]), wg_gate,
                    w_branch[i].astype(BF16), w_out[i].astype(BF16), row(mix_post_g[i]), seq, tm)

        p2 = p[i].reshape(n, -1)
        wpg = ple_w_gate[i].astype(BF16)
        wpp = ple_w_proj[i].astype(BF16)
        j = i // 2
        if i % 2 == 0:
            f = _ffn(x2, row(ffn_pre_g[i]), ffn_w_gate[j].astype(BF16), ffn_w_up[j].astype(BF16),
                     ffn_w_down[j].astype(BF16), min(1024, n), 512)
            x2 = _post(x2, f, p2, row(ffn_post_g[i]), wpg, wpp, row(ple_post_g[i]), tm)
        else:
            wr32 = jnp.zeros((dm, LANES), F32).at[:, :N_EXPERTS].set(router_w[j].astype(F32))
            wr_hi = wr32.astype(BF16)
            wr = jnp.stack([wr_hi, (wr32 - wr_hi.astype(F32)).astype(BF16)])
            br = jnp.full((1, LANES), NEG_INF, F32).at[0, :N_EXPERTS].set(router_b[j].astype(F32))
            x2 = _moe(x2, p2, row(ffn_pre_g[i]), wr, br, moe_w_gate[j].astype(BF16),
                      moe_w_up[j].astype(BF16), moe_w_down[j].astype(BF16), row(ffn_post_g[i]),
                      wpg, wpp, row(ple_post_g[i]), tm_r=tm, tt=min(256, n), tm_e=512, tf_e=1792)
    return x2.reshape(bsz, seq, dm)
```

```python
import functools

import jax
import jax.numpy as jnp
from jax import lax
from jax.experimental import pallas as pl
from jax.experimental.pallas import tpu as pltpu

F32 = jnp.float32
BF16 = jnp.bfloat16

GRID_W = 64
HEAD_DIM = 64
ROPE_THETA = 10000.0
NORM_EPS = 1e-6
NEG_INF = -1e30
A_HEADS = 4
A_Q_RANK = 256
A_KV_RANK = 128
A_NOPE = 64
A_ROPE = 32
A_V = 64
C_PATTERNS = ((128, 1), (512, 4), (2048, 16))
D_HALF_WINDOW = 128
N_BRANCHES = 4
N_EXPERTS = 8
TOP_K = 2

LANES = 128
HALF = 64
VMEM_MB = 1024 * 1024
LOG2E = 1.4426950408889634
BAND_SUB = 128

ZB = 0
ZC = 512
ZD = 512 + 768
ZMAIN_COLS = ZD + 768
A_SEG = 512
B_SEG = 512


def _cparams(sem, vmem_mb):
    return pltpu.CompilerParams(dimension_semantics=sem, vmem_limit_bytes=vmem_mb * VMEM_MB)


def _resident(shape):
    nd = len(shape)
    return pl.BlockSpec(shape, lambda *_: (0,) * nd, pipeline_mode=pl.Buffered(1))


def _rms(xf, g):
    return xf * lax.rsqrt(jnp.mean(xf * xf, axis=-1, keepdims=True) + NORM_EPS) * g


def _sigmoid(x):
    return 1.0 / (1.0 + jnp.exp(-x))


def _swap32(a):
    lane = lax.broadcasted_iota(jnp.int32, a.shape, 1)
    fwd = pltpu.roll(a, LANES - 32, 1)
    bwd = pltpu.roll(a, 32, 1)
    return jnp.where((lane & 32) == 0, fwd, bwd)


def _rope(a, cos, sin):
    outs = []
    for c in range(a.shape[1] // LANES):
        ch = a[:, c * LANES:(c + 1) * LANES]
        outs.append(ch * cos + _swap32(ch) * sin)
    return outs[0] if len(outs) == 1 else jnp.concatenate(outs, axis=1)


def _head_norm(a, g, bd):
    outs = []
    for c in range(a.shape[1] // LANES):
        ch = a[:, c * LANES:(c + 1) * LANES]
        sq = ch * ch
        hi = sq.astype(BF16)
        lo = (sq - hi.astype(F32)).astype(BF16)
        ms = (jnp.dot(hi, bd, preferred_element_type=F32)
              + jnp.dot(lo, bd, preferred_element_type=F32))
        outs.append(ch * lax.rsqrt(ms + NORM_EPS) * g)
    return outs[0] if len(outs) == 1 else jnp.concatenate(outs, axis=1)


def _in_proj_kernel(x_ref, g_ref, w_ref, tabf_ref, tabx_ref, taba_ref, tabq_ref,
                    bqg_ref, bkg_ref, aqg_ref, akvg_ref, wuq_ref, wk_ref, wv_ref, wvb_ref,
                    z_ref, qa_ref, ka_ref, vat_ref, vbt_ref, zc1_ref, zc2_ref, cls_scr):
    h = _rms(x_ref[...], g_ref[...]).astype(BF16)
    cf, sf = tabf_ref[0], tabf_ref[1]
    cx, sx = tabx_ref[0], tabx_ref[1]
    q_scale = HEAD_DIM ** -0.5

    acc = jnp.dot(h, w_ref[:, 0:A_SEG], preferred_element_type=F32)
    nq = _rms(acc[:, 0:A_Q_RANK], aqg_ref[...]).astype(BF16)
    nkv = _rms(acc[:, A_Q_RANK:A_Q_RANK + A_KV_RANK], akvg_ref[...]).astype(BF16)
    kr = _rope(acc[:, 384:512], taba_ref[0], taba_ref[1]).astype(BF16)
    qa = jnp.dot(nq, wuq_ref[...], preferred_element_type=F32)
    qa = _rope(qa, tabq_ref[0], tabq_ref[1]) * ((A_NOPE + A_ROPE) ** -0.5 * LOG2E)
    qa_ref[...] = qa.astype(qa_ref.dtype)
    r = lax.broadcasted_iota(jnp.int32, (LANES, A_HEADS * LANES), 0)
    c = lax.broadcasted_iota(jnp.int32, (LANES, A_HEADS * LANES), 1)
    place = jnp.where((r < HALF) & ((c & (LANES - 1)) == r + HALF), 1.0, 0.0).astype(BF16)
    ka = (jnp.dot(nkv, wk_ref[...], preferred_element_type=F32)
          + jnp.dot(kr, place, preferred_element_type=F32))
    ka_ref[...] = ka.astype(ka_ref.dtype)
    nt_dims = (((1,), (1,)), ((), ()))
    vat_ref[...] = lax.dot_general(wv_ref[...], nkv, nt_dims, preferred_element_type=F32).astype(vat_ref.dtype)
    vbt_ref[...] = lax.dot_general(wvb_ref[...], h, nt_dims, preferred_element_type=F32).astype(vbt_ref.dtype)

    rr = lax.broadcasted_iota(jnp.int32, (LANES, LANES), 0)
    cc = lax.broadcasted_iota(jnp.int32, (LANES, LANES), 1)
    bd = jnp.where((rr >> 6) == (cc >> 6), 1.0 / HEAD_DIM, 0.0).astype(BF16)
    acc = jnp.dot(h, w_ref[:, A_SEG:A_SEG + B_SEG], preferred_element_type=F32)
    q = _rope(_head_norm(acc[:, 0:256], bqg_ref[...], bd), cx, sx) * (q_scale * LOG2E)
    k = _rope(_head_norm(acc[:, 256:512], bkg_ref[...], bd), cx, sx)
    z_ref[:, ZB:ZB + 256] = q.astype(z_ref.dtype)
    z_ref[:, ZB + 256:ZB + 512] = k.astype(z_ref.dtype)

    tm = h.shape[0]
    for widx, zoff, cls_ref, dil in ((0, ZC, None, 1), (1, 0, zc1_ref, C_PATTERNS[1][1]),
                                     (2, 0, zc2_ref, C_PATTERNS[2][1]), (3, ZD, None, 1)):
        base = A_SEG + B_SEG + widx * 768
        acc = jnp.dot(h, w_ref[:, base:base + 768], preferred_element_type=F32)
        q = _rope(acc[:, 0:256], cf, sf) * q_scale
        k = _rope(acc[:, 256:512], cf, sf)
        if cls_ref is None:
            z_ref[:, zoff:zoff + 256] = q.astype(z_ref.dtype)
            z_ref[:, zoff + 256:zoff + 512] = k.astype(z_ref.dtype)
            z_ref[:, zoff + 512:zoff + 768] = acc[:, 512:768].astype(z_ref.dtype)
        else:
            qkv = (q[:, 0:LANES], q[:, LANES:], k[:, 0:LANES], k[:, LANES:],
                   acc[:, 512:512 + LANES], acc[:, 512 + LANES:768])
            for ch, val in enumerate(qkv):
                cls_scr[ch] = val
            for c in range(dil):
                for ch in range(len(qkv)):
                    col = c * 768 + ch * LANES
                    cls_ref[:, col:col + LANES] = cls_scr[ch, pl.ds(c, tm // dil, stride=dil), :].astype(cls_ref.dtype)


def _in_proj(x2, gain, w, tabs, bqg, bkg, aqg, akvg, wuq, wk, wv, wvb, seq, tm):
    n, dm = x2.shape
    nt = seq // tm
    bsz = n // seq
    d1, d2 = C_PATTERNS[1][1], C_PATTERNS[2][1]
    tab_spec = pl.BlockSpec((2, tm, LANES), lambda i: (0, i % nt, 0))
    row = lambda cols: pl.BlockSpec((tm, cols), lambda i: (i, 0))
    vt_spec = pl.BlockSpec((None, 2 * LANES, tm), lambda i: (i // nt, 0, i % nt))
    cls_spec = lambda d: pl.BlockSpec((None, tm // d, d * 768), lambda i: (i // nt, i % nt, 0))
    return pl.pallas_call(
        _in_proj_kernel,
        grid=(n // tm,),
        in_specs=[row(dm), _resident((1, dm)), _resident(w.shape),
                  tab_spec, tab_spec, tab_spec, tab_spec,
                  _resident((1, LANES)), _resident((1, LANES)),
                  _resident((1, A_Q_RANK)), _resident((1, A_KV_RANK)),
                  _resident(wuq.shape), _resident(wk.shape), _resident(wv.shape), _resident(wvb.shape)],
        out_specs=[row(ZMAIN_COLS), row(A_HEADS * LANES), row(A_HEADS * LANES),
                   vt_spec, vt_spec, cls_spec(d1), cls_spec(d2)],
        out_shape=[jax.ShapeDtypeStruct((n, ZMAIN_COLS), BF16),
                   jax.ShapeDtypeStruct((n, A_HEADS * LANES), BF16),
                   jax.ShapeDtypeStruct((n, A_HEADS * LANES), BF16),
                   jax.ShapeDtypeStruct((bsz, 2 * LANES, seq), BF16),
                   jax.ShapeDtypeStruct((bsz, 2 * LANES, seq), BF16),
                   jax.ShapeDtypeStruct((bsz, seq // d1, d1 * 768), BF16),
                   jax.ShapeDtypeStruct((bsz, seq // d2, d2 * 768), BF16)],
        scratch_shapes=[pltpu.VMEM((768 // LANES, tm, LANES), F32)],
        compiler_params=_cparams(("parallel",), 48),
        name="in_proj",
    )(x2, gain, w, tabs["full"], tabs["axial"], tabs["a_k"], tabs["a_q"],
      bqg, bkg, aqg, akvg, wuq, wk, wv, wvb)


FLASH_SAFE_EXP = 64.0


def _head_queries(q_ref, packed):
    lane = lax.broadcasted_iota(jnp.int32, (1, LANES), 1)
    if packed:
        return [jnp.where((lane < HALF) == (r == 0), q_ref[...], 0) for r in range(2)], [0, 0]
    return [q_ref[:, r * LANES:(r + 1) * LANES] for r in range(2)], [0, LANES]


def _flash_single_pass(q_ref, k_ref, vt_ref, s_a, s_b, *, tk, packed):
    tq = q_ref.shape[0]
    nk = k_ref.shape[0] // tk
    qs, kcs = _head_queries(q_ref, packed)
    qts = [q.astype(F32).T.astype(BF16) for q in qs]
    row = lax.broadcasted_iota(jnp.int32, (LANES, tq), 0)
    klane = lax.broadcasted_iota(jnp.int32, (tk, LANES), 1)
    k_one = jnp.where(klane == 0, 1.0, 0.0).astype(BF16)
    v_one = jnp.ones((16, tk), BF16)

    def scores(c, refs, dst):
        ks = pl.multiple_of(c * tk, tk)
        for r in range(2):
            k = jnp.concatenate([k_ref[pl.ds(ks, tk), kcs[r]:kcs[r] + LANES], k_one], axis=1)
            bias = jnp.where(row == 0, -refs[r], 0.0).astype(BF16)
            qt = jnp.concatenate([qts[r], bias], axis=0)
            dst[r] = jnp.dot(k, qt, preferred_element_type=F32)

    def consume(c, src, refs, sts, first):
        ks = pl.multiple_of(c * tk, tk)
        new = []
        for r in range(2):
            big, base, l, acc, hi, lo = sts[r]
            s = src[r]
            cmax = jnp.max(s, axis=0, keepdims=True)
            p = jnp.exp2(s).astype(BF16)
            lhs = jnp.concatenate([vt_ref[r * HALF:(r + 1) * HALF, pl.ds(ks, tk)], v_one], axis=0)
            pv = jnp.dot(lhs, p, preferred_element_type=F32)
            alpha = jnp.exp2(base - refs[r])
            l = l * alpha + pv[HALF:HALF + 1]
            acc = acc * alpha + pv[0:HALF]
            big = jnp.maximum(big, refs[r] + cmax)
            hi = jnp.maximum(hi, cmax)
            if first:
                lo = jnp.minimum(lo, cmax)
            new.append((big, refs[r], l, acc, hi, lo))
        return tuple(new)

    def ref_of(sts):
        return [st[0].astype(BF16).astype(F32) for st in sts]

    zero = jnp.zeros((1, tq), F32)
    sts = tuple((jnp.full((1, tq), NEG_INF, F32), zero, zero, jnp.zeros((HALF, tq), F32),
                 jnp.full((1, tq), NEG_INF, F32), jnp.full((1, tq), -NEG_INF, F32)) for _ in range(2))
    zeros2 = [zero, zero]
    scores(0, zeros2, s_a)
    scores(1, zeros2, s_b)
    sts = consume(0, s_a, zeros2, sts, True)
    ra = ref_of(sts)
    scores(2, ra, s_a)
    sts = consume(1, s_b, zeros2, sts, True)

    def pair(jj, carry):
        sts, ra = carry
        c0 = 2 * jj
        rb = ref_of(sts)
        scores(c0 + 1, rb, s_b)
        sts = consume(c0, s_a, ra, sts, False)
        ra = ref_of(sts)
        scores(c0 + 2, ra, s_a)
        sts = consume(c0 + 1, s_b, rb, sts, False)
        return sts, ra

    sts, ra = lax.fori_loop(1, nk // 2 - 1, pair, (sts, ra))
    rb = ref_of(sts)
    scores(nk - 1, rb, s_b)
    sts = consume(nk - 2, s_a, ra, sts, False)
    sts = consume(nk - 1, s_b, rb, sts, False)
    outs, bad = [], None
    for r in range(2):
        _, _, l, acc, hi, lo = sts[r]
        outs.append(acc / l)
        b = (hi > FLASH_SAFE_EXP) | (lo < -FLASH_SAFE_EXP) | jnp.logical_not(l > 2.0 ** -FLASH_SAFE_EXP)
        bad = b if bad is None else (bad | b)
    return jnp.concatenate(outs, axis=0), bad


def _flash_kernel(q_ref, k_ref, vt_ref, o_ref, s_a, s_b, *, tk, packed):
    out, bad = _flash_single_pass(q_ref, k_ref, vt_ref, s_a, s_b, tk=tk, packed=packed)
    o_ref[...] = out.T.astype(o_ref.dtype)

    @pl.when(jnp.max(jnp.where(bad, 1.0, 0.0)) > 0.0)
    def _():
        o_ref[...] = _flash_two_pass(q_ref, k_ref, vt_ref, s_a, s_b, tk=tk, packed=packed).T.astype(o_ref.dtype)


def _flash_two_pass(q_ref, k_ref, vt_ref, s_a, s_b, *, tk, packed):
    tq = q_ref.shape[0]
    nk = k_ref.shape[0] // tk
    qs, kcs = _head_queries(q_ref, packed)

    def scores(kk, dst):
        ks = pl.multiple_of(kk * tk, tk)
        for r in range(2):
            k = k_ref[pl.ds(ks, tk), kcs[r]:kcs[r] + LANES]
            dst[r] = lax.dot_general(k, qs[r], (((1,), (1,)), ((), ())), preferred_element_type=F32)

    def consume(kk, src, carry):
        ks = pl.multiple_of(kk * tk, tk)
        new = []
        for r in range(2):
            m, l, acc = carry[r]
            s = src[r]
            m_new = jnp.maximum(m, jnp.max(s, axis=0, keepdims=True))
            alpha = jnp.exp2(m - m_new)
            p = jnp.exp2(s - m_new)
            l = alpha * l + jnp.sum(p, axis=0, keepdims=True)
            vt = vt_ref[r * HALF:(r + 1) * HALF, pl.ds(ks, tk)]
            acc = alpha * acc + jnp.dot(vt, p.astype(BF16), preferred_element_type=F32)
            new.append((m_new, l, acc))
        return tuple(new)

    def pair(jj, carry):
        c0 = 2 * jj
        scores(c0 + 1, s_b)
        carry = consume(c0, s_a, carry)
        scores(c0 + 2, s_a)
        return consume(c0 + 1, s_b, carry)

    init = tuple((jnp.full((1, tq), NEG_INF, F32), jnp.zeros((1, tq), F32), jnp.zeros((HALF, tq), F32))
                 for _ in range(2))
    scores(0, s_a)
    carry = lax.fori_loop(0, nk // 2 - 1, pair, init)
    scores(nk - 1, s_b)
    carry = consume(nk - 2, s_a, carry)
    carry = consume(nk - 1, s_b, carry)
    return jnp.concatenate([acc / l for (_, l, acc) in carry], axis=0)


def _flash(q, k, vt, *, q_col0, k_col0, v_row0, seq, packed, tq, tk):
    n = q.shape[0]
    bsz = n // seq
    qw = LANES if packed else 2 * LANES
    nq = seq // tq
    assert seq % (2 * tk) == 0 and seq // tk >= 4, "the chunk pipeline needs an even number (>= 4) of key chunks"
    k3 = k.reshape(bsz, seq, k.shape[1])
    qb, kb, vb = q_col0 // qw, k_col0 // qw, v_row0 // LANES
    return pl.pallas_call(
        functools.partial(_flash_kernel, tk=tk, packed=packed),
        grid=(bsz, 2, nq),
        in_specs=[pl.BlockSpec((tq, qw), lambda b, j, i: (b * nq + i, qb + j)),
                  pl.BlockSpec((None, seq, qw), lambda b, j, i: (b, 0, kb + j)),
                  pl.BlockSpec((None, LANES, seq), lambda b, j, i: (b, vb + j, 0))],
        out_specs=pl.BlockSpec((tq, LANES), lambda b, j, i: (b * nq + i, j)),
        out_shape=jax.ShapeDtypeStruct((n, 2 * LANES), BF16),
        scratch_shapes=[pltpu.VMEM((2, tk, tq), F32), pltpu.VMEM((2, tk, tq), F32)],
        compiler_params=_cparams(("parallel", "parallel", "parallel"), 48),
        name="flash_packed" if packed else "flash_slots",
    )(q, k3, vt)


def _banded_kernel(*refs, hw, has_sink, want_lse):
    if has_sink:
        sink_ref, q_ref, k_ref, v_ref = refs[:4]
        outs = refs[4:]
    else:
        q_ref, k_ref, v_ref = refs[:3]
        outs = refs[3:]
    o_ref = outs[0]
    tq = q_ref.shape[0]
    length = k_ref.shape[0]
    sb = min(BAND_SUB, tq)
    win = min(sb + 2 * hw, length)
    i = pl.program_id(1)
    lane = lax.broadcasted_iota(jnp.int32, (1, LANES), 1)
    first = lane < HALF
    cols = [slice(j * LANES, (j + 1) * LANES) for j in range(2)]
    subs = list(range(tq // sb))
    kss, valids = [], []
    for u in subs:
        q0 = i * tq + u * sb
        ks = pl.multiple_of(jnp.clip(q0 - hw, 0, length - win), HALF)
        qpos = q0 + lax.broadcasted_iota(jnp.int32, (sb, win), 0)
        kpos = ks + lax.broadcasted_iota(jnp.int32, (sb, win), 1)
        kss.append(ks)
        valids.append(jnp.abs(qpos - kpos) <= hw)
    chains = [(u, j, r) for u in subs for j in range(2) for r in range(2)]
    kws = {(u, j): k_ref[pl.ds(kss[u], win), cols[j]] for u in subs for j in range(2)}
    vws = {(u, j): v_ref[pl.ds(kss[u], win), cols[j]] for u in subs for j in range(2)}
    ss = []
    for u, j, r in chains:
        q = jnp.where(first == (r == 0), q_ref[u * sb:(u + 1) * sb, cols[j]], 0)
        ss.append(lax.dot_general(q, kws[u, j], (((1,), (1,)), ((), ())), preferred_element_type=F32))
    ss = [jnp.where(valids[u], s, NEG_INF) for s, (u, j, r) in zip(ss, chains)]
    ms = [jnp.max(s, axis=-1, keepdims=True) for s in ss]
    if has_sink:
        sinks = [sink_ref[2 * j + r] for u, j, r in chains]
        ms = [jnp.maximum(m, sk) for m, sk in zip(ms, sinks)]
    es = [jnp.exp(s - m) for s, m in zip(ss, ms)]
    ls = [jnp.sum(e, axis=-1, keepdims=True) for e in es]
    if has_sink:
        ls = [l + jnp.exp(sk - m) for l, sk, m in zip(ls, sinks, ms)]
    os_ = [jnp.dot(e.astype(BF16), vws[u, j], preferred_element_type=F32) / l
           for e, l, (u, j, r) in zip(es, ls, chains)]
    for n in range(0, len(chains), 2):
        u, j, _ = chains[n]
        rows = slice(u * sb, (u + 1) * sb)
        o_ref[rows, cols[j]] = jnp.where(first, os_[n], os_[n + 1]).astype(o_ref.dtype)
        if want_lse:
            lses = [jnp.broadcast_to(ms[n + r] + jnp.log(ls[n + r]), (sb, LANES)) for r in range(2)]
            outs[1][rows, cols[j]] = jnp.where(first, lses[0], lses[1])


def _banded(q, k, v, *, q_col0, k_col0, v_col0, row_cols, dil, seq, hw, tq, sink=None,
            want_lse=False, out_dtype=BF16):
    bsz, ls, _ = q.shape
    nq = ls // tq
    pw = 2 * LANES
    rb = row_cols // pw
    qb, kb, vb = q_col0 // pw, k_col0 // pw, v_col0 // pw
    in_specs = [pl.BlockSpec((None, tq, pw), lambda bc, i: (bc // dil, i, (bc % dil) * rb + qb)),
                pl.BlockSpec((None, ls, pw), lambda bc, i: (bc // dil, 0, (bc % dil) * rb + kb)),
                pl.BlockSpec((None, ls, pw), lambda bc, i: (bc // dil, 0, (bc % dil) * rb + vb))]
    args = [q, k, v]
    if sink is not None:
        in_specs = [pl.BlockSpec(memory_space=pltpu.SMEM)] + in_specs
        args = [sink] + args
    o_spec = pl.BlockSpec((None, tq, pw), lambda bc, i: (bc // dil, i, bc % dil))
    out_specs = [o_spec]
    out_shape = [jax.ShapeDtypeStruct((bsz, ls, dil * pw), out_dtype)]
    if want_lse:
        out_specs.append(o_spec)
        out_shape.append(jax.ShapeDtypeStruct((bsz, ls, dil * pw), F32))
    res = pl.pallas_call(
        functools.partial(_banded_kernel, hw=hw, has_sink=sink is not None, want_lse=want_lse),
        grid=(bsz * dil, nq),
        in_specs=in_specs,
        out_specs=out_specs,
        out_shape=out_shape,
        compiler_params=_cparams(("parallel", "parallel"), 48),
        name="banded_d%d" % dil,
    )(*args)
    return res


def _merge_kernel(x_ref, oa_ref, ob_ref, oc0_ref, oc1_ref, oc2_ref, l0_ref, l1_ref, l2_ref,
                  od_ref, gpre_ref, wg_ref, wb_ref, wo_ref, gpost_ref, out_ref, tok_scr):
    xf = x_ref[...]
    tm, dm = xf.shape
    h = _rms(xf, gpre_ref[...]).astype(BF16)
    pw = 2 * LANES
    toks = []
    for n, src in enumerate((oc1_ref, l1_ref, oc2_ref, l2_ref)):
        dil = src.shape[1] // pw
        for c in range(dil):
            for hp in range(2):
                col = c * pw + hp * LANES
                tok_scr[2 * n + hp, pl.ds(c, tm // dil, stride=dil), :] = src[:, col:col + LANES]
        toks.append(jnp.concatenate([tok_scr[2 * n], tok_scr[2 * n + 1]], axis=1))
    oc1, l1, oc2, l2 = toks
    l0 = l0_ref[...]
    mx = jnp.maximum(jnp.maximum(l0, l1), l2)
    w0, w1, w2 = jnp.exp(l0 - mx), jnp.exp(l1 - mx), jnp.exp(l2 - mx)
    oc = (w0 * oc0_ref[...] + w1 * oc1 + w2 * oc2) / (w0 + w1 + w2)
    branches = (oa_ref[...], ob_ref[...], oc.astype(BF16), od_ref[...])
    merged = None
    for n, o in enumerate(branches):
        gate = _sigmoid(jnp.dot(h, wg_ref[:, n * dm:(n + 1) * dm], preferred_element_type=F32))
        term = gate * jnp.dot(o, wb_ref[n], preferred_element_type=F32)
        merged = term if merged is None else merged + term
    y = jnp.dot(merged.astype(BF16), wo_ref[...], preferred_element_type=F32)
    out_ref[...] = xf + _rms(y, gpost_ref[...])


def _merge(x2, oa, ob, oc, lc, od, gpre, wg, wb, wo, gpost, seq, tm):
    n, dm = x2.shape
    nt = seq // tm
    row = lambda cols: pl.BlockSpec((tm, cols), lambda i: (i, 0))
    bw = 2 * LANES
    cls = lambda a: pl.BlockSpec((None, tm // (a.shape[2] // bw), a.shape[2]), lambda i: (i // nt, i % nt, 0))
    return pl.pallas_call(
        _merge_kernel,
        grid=(n // tm,),
        in_specs=[row(dm), row(bw), row(bw), row(bw), cls(oc[1]), cls(oc[2]), row(bw), cls(lc[1]), cls(lc[2]),
                  row(bw), _resident((1, dm)), _resident(wg.shape), _resident(wb.shape),
                  _resident(wo.shape), _resident((1, dm))],
        out_specs=row(dm),
        out_shape=jax.ShapeDtypeStruct((n, dm), F32),
        scratch_shapes=[pltpu.VMEM((8, tm, LANES), F32)],
        compiler_params=_cparams(("parallel",), 56),
        name="merge",
    )(x2, oa, ob, oc[0], oc[1], oc[2], lc[0], lc[1], lc[2], od, gpre, wg, wb, wo, gpost)


def _ffn_kernel(x_ref, g_ref, wg_ref, wu_ref, wd_ref, f_ref, h_scr, acc_scr):
    j = pl.program_id(1)

    @pl.when(j == 0)
    def _():
        h_scr[...] = _rms(x_ref[...], g_ref[...]).astype(BF16)
        acc_scr[...] = jnp.zeros_like(acc_scr)

    h = h_scr[...]
    a = jnp.dot(h, wg_ref[...], preferred_element_type=F32)
    u = jnp.dot(h, wu_ref[...], preferred_element_type=F32)
    act = (a * _sigmoid(a) * u).astype(BF16)
    acc_scr[...] += jnp.dot(act, wd_ref[...], preferred_element_type=F32)

    @pl.when(j == pl.num_programs(1) - 1)
    def _():
        f_ref[...] = acc_scr[...]


def _ffn(x2, gain, wg, wu, wd, tm, tf):
    n, dm = x2.shape
    dff = wg.shape[1]
    return pl.pallas_call(
        _ffn_kernel,
        grid=(n // tm, dff // tf),
        in_specs=[pl.BlockSpec((tm, dm), lambda i, j: (i, 0)),
                  pl.BlockSpec((1, dm), lambda i, j: (0, 0)),
                  pl.BlockSpec((dm, tf), lambda i, j: (0, j)),
                  pl.BlockSpec((dm, tf), lambda i, j: (0, j)),
                  pl.BlockSpec((tf, dm), lambda i, j: (j, 0))],
        out_specs=pl.BlockSpec((tm, dm), lambda i, j: (i, 0)),
        out_shape=jax.ShapeDtypeStruct((n, dm), F32),
        scratch_shapes=[pltpu.VMEM((tm, dm), BF16), pltpu.VMEM((tm, dm), F32)],
        compiler_params=_cparams(("parallel", "arbitrary"), 48),
        name="ffn",
    )(x2, gain, wg, wu, wd)


def _post_math(xf, f, p_ref, gf_ref, wpg_ref, wpp_ref, gp_ref, out_ref):
    x2 = xf + _rms(f, gf_ref[...])
    gate = _sigmoid(jnp.dot(x2.astype(BF16), wpg_ref[...], preferred_element_type=F32))
    e = jnp.dot(p_ref[...].astype(BF16), wpp_ref[...], preferred_element_type=F32) * gate
    out_ref[...] = x2 + _rms(e, gp_ref[...])


def _post_kernel(x_ref, f_ref, p_ref, gf_ref, wpg_ref, wpp_ref, gp_ref, out_ref):
    _post_math(x_ref[...], f_ref[...], p_ref, gf_ref, wpg_ref, wpp_ref, gp_ref, out_ref)


def _post(x2, f, p2, gf, wpg, wpp, gp, tm):
    n, dm = x2.shape
    row = lambda cols: pl.BlockSpec((tm, cols), lambda i: (i, 0))
    return pl.pallas_call(
        _post_kernel,
        grid=(n // tm,),
        in_specs=[row(dm), row(dm), row(p2.shape[1]), _resident((1, dm)), _resident(wpg.shape),
                  _resident(wpp.shape), _resident((1, dm))],
        out_specs=row(dm),
        out_shape=jax.ShapeDtypeStruct((n, dm), F32),
        compiler_params=_cparams(("parallel",), 48),
        name="post",
    )(x2, f, p2, gf, wpg, wpp, gp)


def _router_kernel(x_ref, g_ref, wr_ref, br_ref, hs_ref, info_ref):
    hf = _rms(x_ref[...], g_ref[...])
    hb = hf.astype(BF16)
    tm, dm = hf.shape
    sub = dm // LANES
    for s in range(sub):
        hs_ref[pl.ds(s, tm, stride=sub), :] = hf[:, s * LANES:(s + 1) * LANES]
    hl = (hf - hb.astype(F32)).astype(BF16)
    whi, wlo = wr_ref[0], wr_ref[1]
    logits = (jnp.dot(hb, whi, preferred_element_type=F32) + jnp.dot(hl, whi, preferred_element_type=F32)
              + jnp.dot(hb, wlo, preferred_element_type=F32)) + br_ref[...]
    lane = lax.broadcasted_iota(jnp.int32, logits.shape, 1).astype(F32)
    m1 = jnp.max(logits, axis=-1, keepdims=True)
    i1 = jnp.min(jnp.where(logits == m1, lane, float(LANES)), axis=-1, keepdims=True)
    rest = jnp.where(lane == i1, NEG_INF, logits)
    m2 = jnp.max(rest, axis=-1, keepdims=True)
    i2 = jnp.min(jnp.where(rest == m2, lane, float(LANES)), axis=-1, keepdims=True)
    e2 = jnp.exp(m2 - m1)
    g1 = 1.0 / (1.0 + e2)
    g2 = e2 / (1.0 + e2)
    info = jnp.where(lane == 0.0, i1, jnp.where(lane == 1.0, i2, jnp.where(lane == 2.0, g1, g2)))
    info_ref[...] = info


def _router(x2, gain, wr, br, tm):
    n, dm = x2.shape
    row = lambda cols: pl.BlockSpec((tm, cols), lambda i: (i, 0))
    return pl.pallas_call(
        _router_kernel,
        grid=(n // tm,),
        in_specs=[row(dm), _resident((1, dm)), _resident(wr.shape), _resident((1, LANES))],
        out_specs=[pl.BlockSpec((tm * (dm // LANES), LANES), lambda i: (i, 0)), row(LANES)],
        out_shape=[jax.ShapeDtypeStruct((n * (dm // LANES), LANES), F32),
                   jax.ShapeDtypeStruct((n, LANES), F32)],
        compiler_params=_cparams(("parallel",), 32),
        name="router",
    )(x2, gain, wr, br)


def _experts_kernel(be_ref, src0_ref, srcn_ref, dstp_ref, dstl_ref, hs_hbm, wg_ref, wu_ref, wd_ref,
                    yt_hbm, xg, xb, acc, ys, gsem, ssem, *, tm, sub):
    del be_ref
    i = pl.program_id(0)
    j = pl.program_id(1)
    nb = pl.num_programs(0)
    nf = pl.num_programs(1)
    slot = i % 2
    other = 1 - slot
    per_step = tm // nf

    def gather(idx_ref, r, dslot):
        src = pl.multiple_of(idx_ref[0, r] * sub, sub)
        dst = pl.multiple_of(r * sub, sub)
        return pltpu.make_async_copy(hs_hbm.at[pl.ds(src, sub)], xg.at[dslot, pl.ds(dst, sub)],
                                     gsem.at[dslot])

    def scatter(idx_ref, r, sslot):
        src = pl.multiple_of(r * sub, sub)
        dst = pl.multiple_of(idx_ref[0, r] * sub, sub)
        return pltpu.make_async_copy(ys.at[sslot, pl.ds(src, sub)], yt_hbm.at[pl.ds(dst, sub)],
                                     ssem.at[sslot])

    def wait_gather(dslot):
        pltpu.make_async_copy(hs_hbm.at[pl.ds(0, tm * sub)], xg.at[dslot], gsem.at[dslot]).wait()

    def wait_scatter(sslot):
        pltpu.make_async_copy(ys.at[sslot], yt_hbm.at[pl.ds(0, tm * sub)], ssem.at[sslot]).wait()

    @pl.when((i == 0) & (j == 0))
    def _():
        ys[1] = jnp.zeros(ys.shape[1:], ys.dtype)

        def start(r, c):
            gather(src0_ref, r, 0).start()
            return c

        lax.fori_loop(0, tm, start, 0)

    @pl.when(j == 0)
    def _():
        wait_gather(slot)
        for s in range(sub):
            xb[:, s * LANES:(s + 1) * LANES] = xg[slot, pl.ds(s, tm, stride=sub), :].astype(BF16)
        acc[...] = jnp.zeros_like(acc)

    x = xb[...]
    a = jnp.dot(x, wg_ref[...], preferred_element_type=F32)
    u = jnp.dot(x, wu_ref[...], preferred_element_type=F32)
    act = (a * _sigmoid(a) * u).astype(BF16)
    acc[...] += jnp.dot(act, wd_ref[...], preferred_element_type=F32)
    r0 = j * per_step
    for t in range(per_step):
        gather(srcn_ref, r0 + t, other).start()
    for t in range(per_step):
        scatter(dstp_ref, r0 + t, other).start()

    @pl.when(j == nf - 1)
    def _():
        @pl.when(i >= 1)
        def _():
            wait_scatter(slot)

        for s in range(sub):
            ys[slot, pl.ds(s, tm, stride=sub), :] = acc[:, s * LANES:(s + 1) * LANES]

        @pl.when(i == nb - 1)
        def _():
            def start(r, c):
                scatter(dstl_ref, r, slot).start()
                return c

            lax.fori_loop(0, tm, start, 0)
            wait_scatter(slot)
            wait_scatter(other)
            wait_gather(other)


def _experts(hs, blk_e, src_tok, dst_row, n_slabs, wg, wu, wd, tm, tf):
    nb = src_tok.shape[0]
    dm, dff = wg.shape[1], wg.shape[2]
    sub = dm // LANES
    nf = dff // tf
    smem = lambda imap: pl.BlockSpec((None, 1, tm), imap, memory_space=pltpu.SMEM)
    grid_spec = pltpu.PrefetchScalarGridSpec(
        num_scalar_prefetch=1,
        grid=(nb, nf),
        in_specs=[smem(lambda i, j, be: (0, 0, 0)),
                  smem(lambda i, j, be: (jnp.minimum(i + 1, nb - 1), 0, 0)),
                  smem(lambda i, j, be: (i, 0, 0)),
                  smem(lambda i, j, be: (nb, 0, 0)),
                  pl.BlockSpec(memory_space=pl.ANY),
                  pl.BlockSpec((None, dm, tf), lambda i, j, be: (be[i], 0, j)),
                  pl.BlockSpec((None, dm, tf), lambda i, j, be: (be[i], 0, j)),
                  pl.BlockSpec((None, tf, dm), lambda i, j, be: (be[i], j, 0))],
        out_specs=pl.BlockSpec(memory_space=pl.ANY),
        scratch_shapes=[pltpu.VMEM((2, tm * sub, LANES), F32), pltpu.VMEM((tm, dm), BF16),
                        pltpu.VMEM((tm, dm), F32), pltpu.VMEM((2, tm * sub, LANES), F32),
                        pltpu.SemaphoreType.DMA((2,)), pltpu.SemaphoreType.DMA((2,))],
    )
    return pl.pallas_call(
        functools.partial(_experts_kernel, tm=tm, sub=sub),
        grid_spec=grid_spec,
        out_shape=jax.ShapeDtypeStruct((n_slabs * sub, LANES), F32),
        compiler_params=_cparams(("arbitrary", "arbitrary"), 48),
        name="experts",
    )(blk_e, src_tok, src_tok, dst_row, dst_row, hs, wg, wu, wd)


def _combine_kernel(y_ref, x_ref, gt_ref, p_ref, gf_ref, wpg_ref, wpp_ref, gp_ref, out_ref):
    tt, dm = x_ref.shape
    sub = dm // LANES
    gt = gt_ref[...]

    def rows(slot):
        return jnp.concatenate([y_ref[pl.ds(slot * sub + s, tt, stride=TOP_K * sub), :] for s in range(sub)],
                               axis=1)

    f = gt[:, 2:3] * rows(0) + gt[:, 3:4] * rows(1)
    _post_math(x_ref[...], f, p_ref, gf_ref, wpg_ref, wpp_ref, gp_ref, out_ref)


def _combine(yt, info, x2, p2, gf, wpg, wpp, gp, tt):
    n, dm = x2.shape
    row = lambda cols: pl.BlockSpec((tt, cols), lambda i: (i, 0))
    return pl.pallas_call(
        _combine_kernel,
        grid=(n // tt,),
        in_specs=[pl.BlockSpec((tt * TOP_K * (dm // LANES), LANES), lambda i: (i, 0)),
                  row(dm), row(LANES), row(p2.shape[1]), _resident((1, dm)),
                  _resident(wpg.shape), _resident(wpp.shape), _resident((1, dm))],
        out_specs=row(dm),
        out_shape=jax.ShapeDtypeStruct((n, dm), F32),
        compiler_params=_cparams(("parallel",), 48),
        name="combine",
    )(yt, x2, info, p2, gf, wpg, wpp, gp)


def _moe(x2, p2, gain, wr, br, wg, wu, wd, gf, wpg, wpp, gp, *, tm_r, tt, tm_e, tf_e):
    n, _ = x2.shape
    hs, info = _router(x2, gain, wr, br, tm_r)
    n_asg = n * TOP_K
    top_e = info[:, 0:TOP_K].astype(jnp.int32)
    e_flat = top_e.reshape(n_asg)
    onehot = (e_flat[:, None] == jnp.arange(N_EXPERTS, dtype=jnp.int32)[None, :]).astype(jnp.int32)
    csum = jnp.cumsum(onehot, axis=0)
    rank = jnp.sum((csum - onehot) * onehot, axis=1)
    counts = csum[-1]
    padded = ((counts + tm_e - 1) // tm_e) * tm_e
    pend = jnp.cumsum(padded)
    pstart = pend - padded
    dest = (pstart[e_flat] + rank).astype(jnp.int32)
    n_blocks = -(-n_asg // tm_e) + N_EXPERTS
    n_rows = n_blocks * tm_e
    blk_e = jnp.clip(jnp.searchsorted(pend, jnp.arange(n_blocks, dtype=jnp.int32) * tm_e, side="right"),
                     0, N_EXPERTS - 1).astype(jnp.int32)
    asg = jnp.full((n_rows,), -1, jnp.int32).at[dest].set(jnp.arange(n_asg, dtype=jnp.int32),
                                                          unique_indices=True)
    is_pad = asg < 0
    pad_rank = jnp.cumsum(is_pad.astype(jnp.int32)) - 1
    src_tok = jnp.where(is_pad, 0, asg // TOP_K).reshape(n_blocks, 1, tm_e)
    dst_row = jnp.where(is_pad, n_asg + pad_rank, asg)
    spare = n_asg + (n_rows - n_asg) + jnp.arange(tm_e, dtype=jnp.int32)
    dst_row = jnp.concatenate([spare, dst_row]).reshape(n_blocks + 1, 1, tm_e)
    n_slabs = n_asg + (n_rows - n_asg) + tm_e
    yt = _experts(hs, blk_e, src_tok, dst_row, n_slabs, wg, wu, wd, tm_e, tf_e)
    return _combine(yt, info, x2, p2, gf, wpg, wpp, gp, tt)


def _tables(seq):
    pos = jnp.arange(seq, dtype=jnp.int32)

    def cs(p, half):
        inv = jnp.power(ROPE_THETA, -jnp.arange(half, dtype=F32) / half)
        ang = p.astype(F32)[:, None] * inv[None, :]
        return jnp.cos(ang), jnp.sin(ang)

    ones = lambda w: jnp.ones((seq, w), F32)
    zeros = lambda w: jnp.zeros((seq, w), F32)
    c, s = cs(pos, HEAD_DIM // 2)
    full = jnp.stack([jnp.tile(jnp.concatenate([c, c], 1), (1, 2)),
                      jnp.tile(jnp.concatenate([-s, s], 1), (1, 2))])
    cr, sr = cs(pos // GRID_W, HEAD_DIM // 4)
    cc, sc = cs(pos % GRID_W, HEAD_DIM // 4)
    axial = jnp.stack([jnp.tile(jnp.concatenate([cr, cc, cr, cc], 1), (1, 2)),
                       jnp.tile(jnp.concatenate([-sr, -sc, sr, sc], 1), (1, 2))])
    ca, sa = cs(pos, A_ROPE // 2)
    slot_c = jnp.concatenate([ca, ones(16), ca, ones(16)], 1)
    slot_s = jnp.concatenate([-sa, zeros(16), sa, zeros(16)], 1)
    a_k = jnp.stack([jnp.concatenate([slot_c, ones(HALF)], 1), jnp.concatenate([slot_s, zeros(HALF)], 1)])
    a_q = jnp.stack([jnp.concatenate([ones(HALF), slot_c], 1), jnp.concatenate([zeros(HALF), slot_s], 1)])
    return {"full": full, "axial": axial, "a_k": a_k, "a_q": a_q}


_AXIAL_PERM = tuple(list(range(0, 16)) + list(range(32, 48)) + list(range(16, 32)) + list(range(48, 64)))


def _dup_heads(w, n_heads, perm=None):
    rows = w.shape[0]
    w = w.reshape(rows, n_heads, HEAD_DIM)
    if perm is not None:
        w = w[:, :, perm]
    return jnp.stack([w, w], axis=2).reshape(rows, n_heads * 2 * HEAD_DIM)


def _assemble_w_in(w):
    dm = w.shape[0]
    perm = jnp.array(_AXIAL_PERM, jnp.int32)
    a, b, c, d = w[:, 0:416], w[:, 416:928], w[:, 928:3232], w[:, 3232:3744]
    z = lambda n: jnp.zeros((dm, n), w.dtype)
    kr = a[:, 384:416]
    a_seg = jnp.concatenate([a[:, 0:384], kr[:, 0:16], z(16), kr[:, 16:32], z(16), z(HALF)], axis=1)
    bq = b[:, 0:256].reshape(dm, 4, HEAD_DIM)[:, :, perm].reshape(dm, 256)
    b_seg = jnp.concatenate([bq, _dup_heads(b[:, 256:384], 2, perm)], axis=1)
    d_seg = jnp.concatenate([d[:, 0:256], _dup_heads(d[:, 256:384], 2), _dup_heads(d[:, 384:512], 2)], axis=1)
    w_vbt = _dup_heads(b[:, 384:512], 2).T.astype(BF16)
    return jnp.concatenate([a_seg, b_seg, c, d_seg], axis=1).astype(BF16), w_vbt


def _assemble_a(w_uq, w_ukv):
    zq = lambda n: jnp.zeros((w_uq.shape[0], n), w_uq.dtype)
    zk = lambda n: jnp.zeros((w_ukv.shape[0], n), w_ukv.dtype)
    dq = A_NOPE + A_ROPE
    q_cols, k_cols, v_cols = [], [], []
    for hh in range(A_HEADS):
        q = w_uq[:, hh * dq:(hh + 1) * dq]
        q_cols += [q[:, 0:A_NOPE], q[:, A_NOPE:A_NOPE + 16], zq(16), q[:, A_NOPE + 16:dq], zq(16)]
        kv = w_ukv[:, hh * (A_NOPE + A_V):(hh + 1) * (A_NOPE + A_V)]
        k_cols += [kv[:, 0:A_NOPE], zk(HALF)]
        v_cols += [kv[:, A_NOPE:A_NOPE + A_V]]
    cat = lambda cols: jnp.concatenate(cols, axis=1).astype(BF16)
    return cat(q_cols), cat(k_cols), cat(v_cols).T


def _gain_pair(g, perm=None):
    if perm is not None:
        g = g[jnp.array(perm, jnp.int32)]
    return jnp.tile(g, 2).reshape(1, LANES).astype(F32)


def kernel(x, p, w_in, a_qa_g, a_kva_g, a_w_uq, a_w_ukv, b_q_g, b_k_g, d_sink, w_branch, w_out,
           mix_pre_g, mix_post_g, ffn_pre_g, ffn_post_g, ffn_w_gate, ffn_w_up, ffn_w_down,
           router_w, router_b, moe_w_gate, moe_w_up, moe_w_down, ple_w_proj, ple_w_gate, ple_post_g):
    bsz, seq, dm = x.shape
    depth = w_in.shape[0]
    n = bsz * seq
    tm = min(512, seq)
    tq_flash = min(512, seq)
    tk_flash = min(512, seq // 4)
    tabs = _tables(seq)
    row = lambda g: g.reshape(1, -1).astype(F32)
    x2 = x.reshape(n, dm)

    for i in range(depth):
        w_all, wvb = _assemble_w_in(w_in[i])
        wuq, wk, wv = _assemble_a(a_w_uq[i], a_w_ukv[i])
        z, qa, ka, vat, vbt, zc1, zc2 = _in_proj(
            x2, row(mix_pre_g[i]), w_all, tabs,
            _gain_pair(b_q_g[i], _AXIAL_PERM), _gain_pair(b_k_g[i], _AXIAL_PERM),
            row(a_qa_g[i]), row(a_kva_g[i]), wuq, wk, wv, wvb, seq, tm)
        z3 = z.reshape(bsz, seq, ZMAIN_COLS)
        o_a = _flash(qa, ka, vat, q_col0=0, k_col0=0, v_row0=0, seq=seq, packed=False,
                     tq=tq_flash, tk=tk_flash)
        o_b = _flash(z, z, vbt, q_col0=ZB, k_col0=ZB + 256, v_row0=0, seq=seq, packed=True,
                     tq=tq_flash, tk=tk_flash)
        o_c, l_c = [], []
        for (win, dil), (src, col0, rc) in zip(C_PATTERNS, ((z3, ZC, ZMAIN_COLS), (zc1, 0, 768), (zc2, 0, 768))):
            og, lg = _banded(src, src, src, q_col0=col0, k_col0=col0 + 256, v_col0=col0 + 512,
                             row_cols=rc, dil=dil, seq=seq, hw=win // (2 * dil), tq=min(512, seq // dil),
                             want_lse=True, out_dtype=F32)
            o_c.append(og)
            l_c.append(lg)
        o_c[0] = o_c[0].reshape(n, 2 * LANES)
        l_c[0] = l_c[0].reshape(n, 2 * LANES)
        (o_d,) = _banded(z3, z3, z3, q_col0=ZD, k_col0=ZD + 256, v_col0=ZD + 512,
                         row_cols=ZMAIN_COLS, dil=1, seq=seq, hw=D_HALF_WINDOW, tq=min(512, seq),
                         sink=d_sink[i].astype(F32))
        o_d = o_d.reshape(n, 2 * LANES)
        wg_gate = w_in[i][:, 3744:].astype(BF16)
        x2 = _merge(x2, o_a, o_b, o_c, l_c, o_d, row(mix_pre_g[i]), wg_gate,
                    w_branch[i].astype(BF16), w_out[i].astype(BF16), row(mix_post_g[i]), seq, tm)

        p2 = p[i].reshape(n, -1)
        wpg = ple_w_gate[i].astype(BF16)
        wpp = ple_w_proj[i].astype(BF16)
        j = i // 2
        if i % 2 == 0:
            f = _ffn(x2, row(ffn_pre_g[i]), ffn_w_gate[j].astype(BF16), ffn_w_up[j].astype(BF16),
                     ffn_w_down[j].astype(BF16), min(1024, n), 512)
            x2 = _post(x2, f, p2, row(ffn_post_g[i]), wpg, wpp, row(ple_post_g[i]), tm)
        else:
            wr32 = jnp.zeros((dm, LANES), F32).at[:, :N_EXPERTS].set(router_w[j].astype(F32))
            wr_hi = wr32.astype(BF16)
            wr = jnp.stack([wr_hi, (wr32 - wr_hi.astype(F32)).astype(BF16)])
            br = jnp.full((1, LANES), NEG_INF, F32).at[0, :N_EXPERTS].set(router_b[j].astype(F32))
            x2 = _moe(x2, p2, row(ffn_pre_g[i]), wr, br, moe_w_gate[j].astype(BF16),
                      moe_w_up[j].astype(BF16), moe_w_down[j].astype(BF16), row(ffn_post_g[i]),
                      wpg, wpp, row(ple_post_g[i]), tm_r=tm, tt=min(256, n), tm_e=512, tf_e=1792)
    return x2.reshape(bsz, seq, dm)
```

```python
import functools

import jax
import jax.numpy as jnp
from jax import lax
from jax.experimental import pallas as pl
from jax.experimental.pallas import tpu as pltpu

F32 = jnp.float32
BF16 = jnp.bfloat16

GRID_W = 64
HEAD_DIM = 64
ROPE_THETA = 10000.0
NORM_EPS = 1e-6
NEG_INF = -1e30
A_HEADS = 4
A_Q_RANK = 256
A_KV_RANK = 128
A_NOPE = 64
A_ROPE = 32
A_V = 64
C_PATTERNS = ((128, 1), (512, 4), (2048, 16))
D_HALF_WINDOW = 128
N_BRANCHES = 4
N_EXPERTS = 8
TOP_K = 2

LANES = 128
HALF = 64
VMEM_MB = 1024 * 1024
LOG2E = 1.4426950408889634
BAND_SUB = 128

ZB = 0
ZC = 512
ZD = 512 + 768
ZMAIN_COLS = ZD + 768
A_SEG = 512
B_SEG = 512


def _cparams(sem, vmem_mb):
    return pltpu.CompilerParams(dimension_semantics=sem, vmem_limit_bytes=vmem_mb * VMEM_MB)


def _resident(shape):
    nd = len(shape)
    return pl.BlockSpec(shape, lambda *_: (0,) * nd, pipeline_mode=pl.Buffered(1))


def _rms(xf, g):
    return xf * lax.rsqrt(jnp.mean(xf * xf, axis=-1, keepdims=True) + NORM_EPS) * g


def _sigmoid(x):
    return 1.0 / (1.0 + jnp.exp(-x))


def _swap32(a):
    lane = lax.broadcasted_iota(jnp.int32, a.shape, 1)
    fwd = pltpu.roll(a, LANES - 32, 1)
    bwd = pltpu.roll(a, 32, 1)
    return jnp.where((lane & 32) == 0, fwd, bwd)


def _rope(a, cos, sin):
    outs = []
    for c in range(a.shape[1] // LANES):
        ch = a[:, c * LANES:(c + 1) * LANES]
        outs.append(ch * cos + _swap32(ch) * sin)
    return outs[0] if len(outs) == 1 else jnp.concatenate(outs, axis=1)


def _head_norm(a, g, bd):
    outs = []
    for c in range(a.shape[1] // LANES):
        ch = a[:, c * LANES:(c + 1) * LANES]
        sq = ch * ch
        hi = sq.astype(BF16)
        lo = (sq - hi.astype(F32)).astype(BF16)
        ms = (jnp.dot(hi, bd, preferred_element_type=F32)
              + jnp.dot(lo, bd, preferred_element_type=F32))
        outs.append(ch * lax.rsqrt(ms + NORM_EPS) * g)
    return outs[0] if len(outs) == 1 else jnp.concatenate(outs, axis=1)


def _in_proj_kernel(x_ref, g_ref, w_ref, tabf_ref, tabx_ref, taba_ref, tabq_ref,
                    bqg_ref, bkg_ref, aqg_ref, akvg_ref, wuq_ref, wk_ref, wv_ref, wvb_ref,
                    z_ref, qa_ref, ka_ref, vat_ref, vbt_ref, zc1_ref, zc2_ref, cls_scr):
    h = _rms(x_ref[...], g_ref[...]).astype(BF16)
    cf, sf = tabf_ref[0], tabf_ref[1]
    cx, sx = tabx_ref[0], tabx_ref[1]
    q_scale = HEAD_DIM ** -0.5

    acc = jnp.dot(h, w_ref[:, 0:A_SEG], preferred_element_type=F32)
    nq = _rms(acc[:, 0:A_Q_RANK], aqg_ref[...]).astype(BF16)
    nkv = _rms(acc[:, A_Q_RANK:A_Q_RANK + A_KV_RANK], akvg_ref[...]).astype(BF16)
    kr = _rope(acc[:, 384:512], taba_ref[0], taba_ref[1]).astype(BF16)
    qa = jnp.dot(nq, wuq_ref[...], preferred_element_type=F32)
    qa = _rope(qa, tabq_ref[0], tabq_ref[1]) * ((A_NOPE + A_ROPE) ** -0.5 * LOG2E)
    qa_ref[...] = qa.astype(qa_ref.dtype)
    r = lax.broadcasted_iota(jnp.int32, (LANES, A_HEADS * LANES), 0)
    c = lax.broadcasted_iota(jnp.int32, (LANES, A_HEADS * LANES), 1)
    place = jnp.where((r < HALF) & ((c & (LANES - 1)) == r + HALF), 1.0, 0.0).astype(BF16)
    ka = (jnp.dot(nkv, wk_ref[...], preferred_element_type=F32)
          + jnp.dot(kr, place, preferred_element_type=F32))
    ka_ref[...] = ka.astype(ka_ref.dtype)
    nt_dims = (((1,), (1,)), ((), ()))
    vat_ref[...] = lax.dot_general(wv_ref[...], nkv, nt_dims, preferred_element_type=F32).astype(vat_ref.dtype)
    vbt_ref[...] = lax.dot_general(wvb_ref[...], h, nt_dims, preferred_element_type=F32).astype(vbt_ref.dtype)

    rr = lax.broadcasted_iota(jnp.int32, (LANES, LANES), 0)
    cc = lax.broadcasted_iota(jnp.int32, (LANES, LANES), 1)
    bd = jnp.where((rr >> 6) == (cc >> 6), 1.0 / HEAD_DIM, 0.0).astype(BF16)
    acc = jnp.dot(h, w_ref[:, A_SEG:A_SEG + B_SEG], preferred_element_type=F32)
    q = _rope(_head_norm(acc[:, 0:256], bqg_ref[...], bd), cx, sx) * (q_scale * LOG2E)
    k = _rope(_head_norm(acc[:, 256:512], bkg_ref[...], bd), cx, sx)
    z_ref[:, ZB:ZB + 256] = q.astype(z_ref.dtype)
    z_ref[:, ZB + 256:ZB + 512] = k.astype(z_ref.dtype)

    tm = h.shape[0]
    for widx, zoff, cls_ref, dil in ((0, ZC, None, 1), (1, 0, zc1_ref, C_PATTERNS[1][1]),
                                     (2, 0, zc2_ref, C_PATTERNS[2][1]), (3, ZD, None, 1)):
        base = A_SEG + B_SEG + widx * 768
        acc = jnp.dot(h, w_ref[:, base:base + 768], preferred_element_type=F32)
        q = _rope(acc[:, 0:256], cf, sf) * q_scale
        k = _rope(acc[:, 256:512], cf, sf)
        if cls_ref is None:
            z_ref[:, zoff:zoff + 256] = q.astype(z_ref.dtype)
            z_ref[:, zoff + 256:zoff + 512] = k.astype(z_ref.dtype)
            z_ref[:, zoff + 512:zoff + 768] = acc[:, 512:768].astype(z_ref.dtype)
        else:
            qkv = (q[:, 0:LANES], q[:, LANES:], k[:, 0:LANES], k[:, LANES:],
                   acc[:, 512:512 + LANES], acc[:, 512 + LANES:768])
            for ch, val in enumerate(qkv):
                cls_scr[ch] = val
            for c in range(dil):
                for ch in range(len(qkv)):
                    col = c * 768 + ch * LANES
                    cls_ref[:, col:col + LANES] = cls_scr[ch, pl.ds(c, tm // dil, stride=dil), :].astype(cls_ref.dtype)


def _in_proj(x2, gain, w, tabs, bqg, bkg, aqg, akvg, wuq, wk, wv, wvb, seq, tm):
    n, dm = x2.shape
    nt = seq // tm
    bsz = n // seq
    d1, d2 = C_PATTERNS[1][1], C_PATTERNS[2][1]
    tab_spec = pl.BlockSpec((2, tm, LANES), lambda i: (0, i % nt, 0))
    row = lambda cols: pl.BlockSpec((tm, cols), lambda i: (i, 0))
    vt_spec = pl.BlockSpec((None, 2 * LANES, tm), lambda i: (i // nt, 0, i % nt))
    cls_spec = lambda d: pl.BlockSpec((None, tm // d, d * 768), lambda i: (i // nt, i % nt, 0))
    return pl.pallas_call(
        _in_proj_kernel,
        grid=(n // tm,),
        in_specs=[row(dm), _resident((1, dm)), _resident(w.shape),
                  tab_spec, tab_spec, tab_spec, tab_spec,
                  _resident((1, LANES)), _resident((1, LANES)),
                  _resident((1, A_Q_RANK)), _resident((1, A_KV_RANK)),
                  _resident(wuq.shape), _resident(wk.shape), _resident(wv.shape), _resident(wvb.shape)],
        out_specs=[row(ZMAIN_COLS), row(A_HEADS * LANES), row(A_HEADS * LANES),
                   vt_spec, vt_spec, cls_spec(d1), cls_spec(d2)],
        out_shape=[jax.ShapeDtypeStruct((n, ZMAIN_COLS), BF16),
                   jax.ShapeDtypeStruct((n, A_HEADS * LANES), BF16),
                   jax.ShapeDtypeStruct((n, A_HEADS * LANES), BF16),
                   jax.ShapeDtypeStruct((bsz, 2 * LANES, seq), BF16),
                   jax.ShapeDtypeStruct((bsz, 2 * LANES, seq), BF16),
                   jax.ShapeDtypeStruct((bsz, seq // d1, d1 * 768), BF16),
                   jax.ShapeDtypeStruct((bsz, seq // d2, d2 * 768), BF16)],
        scratch_shapes=[pltpu.VMEM((768 // LANES, tm, LANES), F32)],
        compiler_params=_cparams(("parallel",), 48),
        name="in_proj",
    )(x2, gain, w, tabs["full"], tabs["axial"], tabs["a_k"], tabs["a_q"],
      bqg, bkg, aqg, akvg, wuq, wk, wv, wvb)


FLASH_SAFE_EXP = 64.0


def _head_queries(q_ref, packed):
    lane = lax.broadcasted_iota(jnp.int32, (1, LANES), 1)
    if packed:
        return [jnp.where((lane < HALF) == (r == 0), q_ref[...], 0) for r in range(2)], [0, 0]
    return [q_ref[:, r * LANES:(r + 1) * LANES] for r in range(2)], [0, LANES]


def _flash_single_pass(q_ref, k_ref, vt_ref, s_a, s_b, *, tk, packed):
    tq = q_ref.shape[0]
    nk = k_ref.shape[0] // tk
    qs, kcs = _head_queries(q_ref, packed)
    qts = [q.astype(F32).T.astype(BF16) for q in qs]
    row = lax.broadcasted_iota(jnp.int32, (LANES, tq), 0)
    klane = lax.broadcasted_iota(jnp.int32, (tk, LANES), 1)
    k_one = jnp.where(klane == 0, 1.0, 0.0).astype(BF16)
    v_one = jnp.ones((16, tk), BF16)

    def scores(c, refs, dst):
        ks = pl.multiple_of(c * tk, tk)
        for r in range(2):
            k = jnp.concatenate([k_ref[pl.ds(ks, tk), kcs[r]:kcs[r] + LANES], k_one], axis=1)
            bias = jnp.where(row == 0, -refs[r], 0.0).astype(BF16)
            qt = jnp.concatenate([qts[r], bias], axis=0)
            dst[r] = jnp.dot(k, qt, preferred_element_type=F32)

    def consume(c, src, refs, sts, first):
        ks = pl.multiple_of(c * tk, tk)
        new = []
        for r in range(2):
            big, base, l, acc, hi, lo = sts[r]
            s = src[r]
            cmax = jnp.max(s, axis=0, keepdims=True)
            p = jnp.exp2(s).astype(BF16)
            lhs = jnp.concatenate([vt_ref[r * HALF:(r + 1) * HALF, pl.ds(ks, tk)], v_one], axis=0)
            pv = jnp.dot(lhs, p, preferred_element_type=F32)
            alpha = jnp.exp2(base - refs[r])
            l = l * alpha + pv[HALF:HALF + 1]
            acc = acc * alpha + pv[0:HALF]
            big = jnp.maximum(big, refs[r] + cmax)
            hi = jnp.maximum(hi, cmax)
            if first:
                lo = jnp.minimum(lo, cmax)
            new.append((big, refs[r], l, acc, hi, lo))
        return tuple(new)

    def ref_of(sts):
        return [st[0].astype(BF16).astype(F32) for st in sts]

    zero = jnp.zeros((1, tq), F32)
    sts = tuple((jnp.full((1, tq), NEG_INF, F32), zero, zero, jnp.zeros((HALF, tq), F32),
                 jnp.full((1, tq), NEG_INF, F32), jnp.full((1, tq), -NEG_INF, F32)) for _ in range(2))
    zeros2 = [zero, zero]
    scores(0, zeros2, s_a)
    scores(1, zeros2, s_b)
    sts = consume(0, s_a, zeros2, sts, True)
    ra = ref_of(sts)
    scores(2, ra, s_a)
    sts = consume(1, s_b, zeros2, sts, True)

    def pair(jj, carry):
        sts, ra = carry
        c0 = 2 * jj
        rb = ref_of(sts)
        scores(c0 + 1, rb, s_b)
        sts = consume(c0, s_a, ra, sts, False)
        ra = ref_of(sts)
        scores(c0 + 2, ra, s_a)
        sts = consume(c0 + 1, s_b, rb, sts, False)
        return sts, ra

    sts, ra = lax.fori_loop(1, nk // 2 - 1, pair, (sts, ra))
    rb = ref_of(sts)
    scores(nk - 1, rb, s_b)
    sts = consume(nk - 2, s_a, ra, sts, False)
    sts = consume(nk - 1, s_b, rb, sts, False)
    outs, bad = [], None
    for r in range(2):
        _, _, l, acc, hi, lo = sts[r]
        outs.append(acc / l)
        b = (hi > FLASH_SAFE_EXP) | (lo < -FLASH_SAFE_EXP) | jnp.logical_not(l > 2.0 ** -FLASH_SAFE_EXP)
        bad = b if bad is None else (bad | b)
    return jnp.concatenate(outs, axis=0), bad


def _flash_kernel(q_ref, k_ref, vt_ref, o_ref, s_a, s_b, *, tk, packed):
    out, bad = _flash_single_pass(q_ref, k_ref, vt_ref, s_a, s_b, tk=tk, packed=packed)
    o_ref[...] = out.T.astype(o_ref.dtype)

    @pl.when(jnp.max(jnp.where(bad, 1.0, 0.0)) > 0.0)
    def _():
        o_ref[...] = _flash_two_pass(q_ref, k_ref, vt_ref, s_a, s_b, tk=tk, packed=packed).T.astype(o_ref.dtype)


def _flash_two_pass(q_ref, k_ref, vt_ref, s_a, s_b, *, tk, packed):
    tq = q_ref.shape[0]
    nk = k_ref.shape[0] // tk
    qs, kcs = _head_queries(q_ref, packed)

    def scores(kk, dst):
        ks = pl.multiple_of(kk * tk, tk)
        for r in range(2):
            k = k_ref[pl.ds(ks, tk), kcs[r]:kcs[r] + LANES]
            dst[r] = lax.dot_general(k, qs[r], (((1,), (1,)), ((), ())), preferred_element_type=F32)

    def consume(kk, src, carry):
        ks = pl.multiple_of(kk * tk, tk)
        new = []
        for r in range(2):
            m, l, acc = carry[r]
            s = src[r]
            m_new = jnp.maximum(m, jnp.max(s, axis=0, keepdims=True))
            alpha = jnp.exp2(m - m_new)
            p = jnp.exp2(s - m_new)
            l = alpha * l + jnp.sum(p, axis=0, keepdims=True)
            vt = vt_ref[r * HALF:(r + 1) * HALF, pl.ds(ks, tk)]
            acc = alpha * acc + jnp.dot(vt, p.astype(BF16), preferred_element_type=F32)
            new.append((m_new, l, acc))
        return tuple(new)

    def pair(jj, carry):
        c0 = 2 * jj
        scores(c0 + 1, s_b)
        carry = consume(c0, s_a, carry)
        scores(c0 + 2, s_a)
        return consume(c0 + 1, s_b, carry)

    init = tuple((jnp.full((1, tq), NEG_INF, F32), jnp.zeros((1, tq), F32), jnp.zeros((HALF, tq), F32))
                 for _ in range(2))
    scores(0, s_a)
    carry = lax.fori_loop(0, nk // 2 - 1, pair, init)
    scores(nk - 1, s_b)
    carry = consume(nk - 2, s_a, carry)
    carry = consume(nk - 1, s_b, carry)
    return jnp.concatenate([acc / l for (_, l, acc) in carry], axis=0)


def _flash(q, k, vt, *, q_col0, k_col0, v_row0, seq, packed, tq, tk):
    n = q.shape[0]
    bsz = n // seq
    qw = LANES if packed else 2 * LANES
    nq = seq // tq
    assert seq % (2 * tk) == 0 and seq // tk >= 4, "the chunk pipeline needs an even number (>= 4) of key chunks"
    k3 = k.reshape(bsz, seq, k.shape[1])
    qb, kb, vb = q_col0 // qw, k_col0 // qw, v_row0 // LANES
    return pl.pallas_call(
        functools.partial(_flash_kernel, tk=tk, packed=packed),
        grid=(bsz, 2, nq),
        in_specs=[pl.BlockSpec((tq, qw), lambda b, j, i: (b * nq + i, qb + j)),
                  pl.BlockSpec((None, seq, qw), lambda b, j, i: (b, 0, kb + j)),
                  pl.BlockSpec((None, LANES, seq), lambda b, j, i: (b, vb + j, 0))],
        out_specs=pl.BlockSpec((tq, LANES), lambda b, j, i: (b * nq + i, j)),
        out_shape=jax.ShapeDtypeStruct((n, 2 * LANES), BF16),
        scratch_shapes=[pltpu.VMEM((2, tk, tq), F32), pltpu.VMEM((2, tk, tq), F32)],
        compiler_params=_cparams(("parallel", "parallel", "parallel"), 48),
        name="flash_packed" if packed else "flash_slots",
    )(q, k3, vt)


def _banded_kernel(*refs, hw, has_sink, want_lse):
    if has_sink:
        sink_ref, q_ref, k_ref, v_ref = refs[:4]
        outs = refs[4:]
    else:
        q_ref, k_ref, v_ref = refs[:3]
        outs = refs[3:]
    o_ref = outs[0]
    tq = q_ref.shape[0]
    length = k_ref.shape[0]
    sb = min(BAND_SUB, tq)
    win = min(sb + 2 * hw, length)
    i = pl.program_id(1)
    lane = lax.broadcasted_iota(jnp.int32, (1, LANES), 1)
    first = lane < HALF
    cols = [slice(j * LANES, (j + 1) * LANES) for j in range(2)]
    subs = list(range(tq // sb))
    kss, valids = [], []
    for u in subs:
        q0 = i * tq + u * sb
        ks = pl.multiple_of(jnp.clip(q0 - hw, 0, length - win), HALF)
        qpos = q0 + lax.broadcasted_iota(jnp.int32, (sb, win), 0)
        kpos = ks + lax.broadcasted_iota(jnp.int32, (sb, win), 1)
        kss.append(ks)
        valids.append(jnp.abs(qpos - kpos) <= hw)
    chains = [(u, j, r) for u in subs for j in range(2) for r in range(2)]
    kws = {(u, j): k_ref[pl.ds(kss[u], win), cols[j]] for u in subs for j in range(2)}
    vws = {(u, j): v_ref[pl.ds(kss[u], win), cols[j]] for u in subs for j in range(2)}
    ss = []
    for u, j, r in chains:
        q = jnp.where(first == (r == 0), q_ref[u * sb:(u + 1) * sb, cols[j]], 0)
        ss.append(lax.dot_general(q, kws[u, j], (((1,), (1,)), ((), ())), preferred_element_type=F32))
    ss = [jnp.where(valids[u], s, NEG_INF) for s, (u, j, r) in zip(ss, chains)]
    ms = [jnp.max(s, axis=-1, keepdims=True) for s in ss]
    if has_sink:
        sinks = [sink_ref[2 * j + r] for u, j, r in chains]
        ms = [jnp.maximum(m, sk) for m, sk in zip(ms, sinks)]
    es = [jnp.exp(s - m) for s, m in zip(ss, ms)]
    ls = [jnp.sum(e, axis=-1, keepdims=True) for e in es]
    if has_sink:
        ls = [l + jnp.exp(sk - m) for l, sk, m in zip(ls, sinks, ms)]
    os_ = [jnp.dot(e.astype(BF16), vws[u, j], preferred_element_type=F32) / l
           for e, l, (u, j, r) in zip(es, ls, chains)]
    for n in range(0, len(chains), 2):
        u, j, _ = chains[n]
        rows = slice(u * sb, (u + 1) * sb)
        o_ref[rows, cols[j]] = jnp.where(first, os_[n], os_[n + 1]).astype(o_ref.dtype)
        if want_lse:
            lses = [jnp.broadcast_to(ms[n + r] + jnp.log(ls[n + r]), (sb, LANES)) for r in range(2)]
            outs[1][rows, cols[j]] = jnp.where(first, lses[0], lses[1])


def _banded(q, k, v, *, q_col0, k_col0, v_col0, row_cols, dil, seq, hw, tq, sink=None,
            want_lse=False, out_dtype=BF16):
    bsz, ls, _ = q.shape
    nq = ls // tq
    pw = 2 * LANES
    rb = row_cols // pw
    qb, kb, vb = q_col0 // pw, k_col0 // pw, v_col0 // pw
    in_specs = [pl.BlockSpec((None, tq, pw), lambda bc, i: (bc // dil, i, (bc % dil) * rb + qb)),
                pl.BlockSpec((None, ls, pw), lambda bc, i: (bc // dil, 0, (bc % dil) * rb + kb)),
                pl.BlockSpec((None, ls, pw), lambda bc, i: (bc // dil, 0, (bc % dil) * rb + vb))]
    args = [q, k, v]
    if sink is not None:
        in_specs = [pl.BlockSpec(memory_space=pltpu.SMEM)] + in_specs
        args = [sink] + args
    o_spec = pl.BlockSpec((None, tq, pw), lambda bc, i: (bc // dil, i, bc % dil))
    out_specs = [o_spec]
    out_shape = [jax.ShapeDtypeStruct((bsz, ls, dil * pw), out_dtype)]
    if want_lse:
        out_specs.append(o_spec)
        out_shape.append(jax.ShapeDtypeStruct((bsz, ls, dil * pw), F32))
    res = pl.pallas_call(
        functools.partial(_banded_kernel, hw=hw, has_sink=sink is not None, want_lse=want_lse),
        grid=(bsz * dil, nq),
        in_specs=in_specs,
        out_specs=out_specs,
        out_shape=out_shape,
        compiler_params=_cparams(("parallel", "parallel"), 48),
        name="banded_d%d" % dil,
    )(*args)
    return res


def _merge_kernel(x_ref, oa_ref, ob_ref, oc0_ref, oc1_ref, oc2_ref, l0_ref, l1_ref, l2_ref,
                  od_ref, gpre_ref, wg_ref, wb_ref, wo_ref, gpost_ref, out_ref, tok_scr):
    xf = x_ref[...]
    tm, dm = xf.shape
    h = _rms(xf, gpre_ref[...]).astype(BF16)
    pw = 2 * LANES
    toks = []
    for n, src in enumerate((oc1_ref, l1_ref, oc2_ref, l2_ref)):
        dil = src.shape[1] // pw
        for c in range(dil):
            for hp in range(2):
                col = c * pw + hp * LANES
                tok_scr[2 * n + hp, pl.ds(c, tm // dil, stride=dil), :] = src[:, col:col + LANES]
        toks.append(jnp.concatenate([tok_scr[2 * n], tok_scr[2 * n + 1]], axis=1))
    oc1, l1, oc2, l2 = toks
    l0 = l0_ref[...]
    mx = jnp.maximum(jnp.maximum(l0, l1), l2)
    w0, w1, w2 = jnp.exp(l0 - mx), jnp.exp(l1 - mx), jnp.exp(l2 - mx)
    oc = (w0 * oc0_ref[...] + w1 * oc1 + w2 * oc2) / (w0 + w1 + w2)
    branches = (oa_ref[...], ob_ref[...], oc.astype(BF16), od_ref[...])
    merged = None
    for n, o in enumerate(branches):
        gate = _sigmoid(jnp.dot(h, wg_ref[:, n * dm:(n + 1) * dm], preferred_element_type=F32))
        term = gate * jnp.dot(o, wb_ref[n], preferred_element_type=F32)
        merged = term if merged is None else merged + term
    y = jnp.dot(merged.astype(BF16), wo_ref[...], preferred_element_type=F32)
    out_ref[...] = xf + _rms(y, gpost_ref[...])


def _merge(x2, oa, ob, oc, lc, od, gpre, wg, wb, wo, gpost, seq, tm):
    n, dm = x2.shape
    nt = seq // tm
    row = lambda cols: pl.BlockSpec((tm, cols), lambda i: (i, 0))
    bw = 2 * LANES
    cls = lambda a: pl.BlockSpec((None, tm // (a.shape[2] // bw), a.shape[2]), lambda i: (i // nt, i % nt, 0))
    return pl.pallas_call(
        _merge_kernel,
        grid=(n // tm,),
        in_specs=[row(dm), row(bw), row(bw), row(bw), cls(oc[1]), cls(oc[2]), row(bw), cls(lc[1]), cls(lc[2]),
                  row(bw), _resident((1, dm)), _resident(wg.shape), _resident(wb.shape),
                  _resident(wo.shape), _resident((1, dm))],
        out_specs=row(dm),
        out_shape=jax.ShapeDtypeStruct((n, dm), F32),
        scratch_shapes=[pltpu.VMEM((8, tm, LANES), F32)],
        compiler_params=_cparams(("parallel",), 56),
        name="merge",
    )(x2, oa, ob, oc[0], oc[1], oc[2], lc[0], lc[1], lc[2], od, gpre, wg, wb, wo, gpost)


def _ffn_kernel(x_ref, g_ref, wg_ref, wu_ref, wd_ref, p_ref, gf_ref, wpg_ref, wpp_ref, gp_ref,
                out_ref, h_scr, acc_scr):
    j = pl.program_id(1)

    @pl.when(j == 0)
    def _():
        h_scr[...] = _rms(x_ref[...], g_ref[...]).astype(BF16)
        acc_scr[...] = jnp.zeros_like(acc_scr)

    h = h_scr[...]
    a = jnp.dot(h, wg_ref[...], preferred_element_type=F32)
    u = jnp.dot(h, wu_ref[...], preferred_element_type=F32)
    act = (a * _sigmoid(a) * u).astype(BF16)
    acc_scr[...] += jnp.dot(act, wd_ref[...], preferred_element_type=F32)

    @pl.when(j == pl.num_programs(1) - 1)
    def _():
        _post_math(x_ref[...], acc_scr[...], p_ref, gf_ref, wpg_ref, wpp_ref, gp_ref, out_ref)


def _ffn(x2, gain, wg, wu, wd, p2, gf, wpg, wpp, gp, tm, tf):
    n, dm = x2.shape
    dff = wg.shape[1]
    return pl.pallas_call(
        _ffn_kernel,
        grid=(n // tm, dff // tf),
        in_specs=[pl.BlockSpec((tm, dm), lambda i, j: (i, 0)),
                  pl.BlockSpec((1, dm), lambda i, j: (0, 0)),
                  pl.BlockSpec((dm, tf), lambda i, j: (0, j)),
                  pl.BlockSpec((dm, tf), lambda i, j: (0, j)),
                  pl.BlockSpec((tf, dm), lambda i, j: (j, 0)),
                  pl.BlockSpec((tm, p2.shape[1]), lambda i, j: (i, 0)),
                  _resident((1, dm)), _resident(wpg.shape), _resident(wpp.shape), _resident((1, dm))],
        out_specs=pl.BlockSpec((tm, dm), lambda i, j: (i, 0)),
        out_shape=jax.ShapeDtypeStruct((n, dm), F32),
        scratch_shapes=[pltpu.VMEM((tm, dm), BF16), pltpu.VMEM((tm, dm), F32)],
        compiler_params=_cparams(("parallel", "arbitrary"), 56),
        name="ffn",
    )(x2, gain, wg, wu, wd, p2, gf, wpg, wpp, gp)


def _post_math(xf, f, p_ref, gf_ref, wpg_ref, wpp_ref, gp_ref, out_ref):
    x2 = xf + _rms(f, gf_ref[...])
    gate = _sigmoid(jnp.dot(x2.astype(BF16), wpg_ref[...], preferred_element_type=F32))
    e = jnp.dot(p_ref[...].astype(BF16), wpp_ref[...], preferred_element_type=F32) * gate
    out_ref[...] = x2 + _rms(e, gp_ref[...])


def _router_kernel(x_ref, g_ref, wr_ref, br_ref, hs_ref, info_ref):
    hf = _rms(x_ref[...], g_ref[...])
    hb = hf.astype(BF16)
    tm, dm = hf.shape
    sub = dm // LANES
    for s in range(sub):
        hs_ref[pl.ds(s, tm, stride=sub), :] = hf[:, s * LANES:(s + 1) * LANES]
    hl = (hf - hb.astype(F32)).astype(BF16)
    whi, wlo = wr_ref[0], wr_ref[1]
    logits = (jnp.dot(hb, whi, preferred_element_type=F32) + jnp.dot(hl, whi, preferred_element_type=F32)
              + jnp.dot(hb, wlo, preferred_element_type=F32)) + br_ref[...]
    lane = lax.broadcasted_iota(jnp.int32, logits.shape, 1).astype(F32)
    m1 = jnp.max(logits, axis=-1, keepdims=True)
    i1 = jnp.min(jnp.where(logits == m1, lane, float(LANES)), axis=-1, keepdims=True)
    rest = jnp.where(lane == i1, NEG_INF, logits)
    m2 = jnp.max(rest, axis=-1, keepdims=True)
    i2 = jnp.min(jnp.where(rest == m2, lane, float(LANES)), axis=-1, keepdims=True)
    e2 = jnp.exp(m2 - m1)
    g1 = 1.0 / (1.0 + e2)
    g2 = e2 / (1.0 + e2)
    info = jnp.where(lane == 0.0, i1, jnp.where(lane == 1.0, i2, jnp.where(lane == 2.0, g1, g2)))
    info_ref[...] = info


def _router(x2, gain, wr, br, tm):
    n, dm = x2.shape
    row = lambda cols: pl.BlockSpec((tm, cols), lambda i: (i, 0))
    return pl.pallas_call(
        _router_kernel,
        grid=(n // tm,),
        in_specs=[row(dm), _resident((1, dm)), _resident(wr.shape), _resident((1, LANES))],
        out_specs=[pl.BlockSpec((tm * (dm // LANES), LANES), lambda i: (i, 0)), row(LANES)],
        out_shape=[jax.ShapeDtypeStruct((n * (dm // LANES), LANES), F32),
                   jax.ShapeDtypeStruct((n, LANES), F32)],
        compiler_params=_cparams(("parallel",), 32),
        name="router",
    )(x2, gain, wr, br)


def _experts_kernel(be_ref, src0_ref, srcn_ref, dstp_ref, dstl_ref, hs_hbm, wg_ref, wu_ref, wd_ref,
                    yt_hbm, xg, xb, acc, ys, gsem, ssem, *, tm, sub):
    del be_ref
    i = pl.program_id(0)
    j = pl.program_id(1)
    nb = pl.num_programs(0)
    nf = pl.num_programs(1)
    slot = i % 2
    other = 1 - slot
    per_step = tm // nf

    def gather(idx_ref, r, dslot):
        src = pl.multiple_of(idx_ref[0, r] * sub, sub)
        dst = pl.multiple_of(r * sub, sub)
        return pltpu.make_async_copy(hs_hbm.at[pl.ds(src, sub)], xg.at[dslot, pl.ds(dst, sub)],
                                     gsem.at[dslot])

    def scatter(idx_ref, r, sslot):
        src = pl.multiple_of(r * sub, sub)
        dst = pl.multiple_of(idx_ref[0, r] * sub, sub)
        return pltpu.make_async_copy(ys.at[sslot, pl.ds(src, sub)], yt_hbm.at[pl.ds(dst, sub)],
                                     ssem.at[sslot])

    def wait_gather(dslot):
        pltpu.make_async_copy(hs_hbm.at[pl.ds(0, tm * sub)], xg.at[dslot], gsem.at[dslot]).wait()

    def wait_scatter(sslot):
        pltpu.make_async_copy(ys.at[sslot], yt_hbm.at[pl.ds(0, tm * sub)], ssem.at[sslot]).wait()

    @pl.when((i == 0) & (j == 0))
    def _():
        ys[1] = jnp.zeros(ys.shape[1:], ys.dtype)

        def start(r, c):
            gather(src0_ref, r, 0).start()
            return c

        lax.fori_loop(0, tm, start, 0)

    @pl.when(j == 0)
    def _():
        wait_gather(slot)
        for s in range(sub):
            xb[:, s * LANES:(s + 1) * LANES] = xg[slot, pl.ds(s, tm, stride=sub), :].astype(BF16)
        acc[...] = jnp.zeros_like(acc)

    x = xb[...]
    a = jnp.dot(x, wg_ref[...], preferred_element_type=F32)
    u = jnp.dot(x, wu_ref[...], preferred_element_type=F32)
    act = (a * _sigmoid(a) * u).astype(BF16)
    acc[...] += jnp.dot(act, wd_ref[...], preferred_element_type=F32)
    r0 = j * per_step
    for t in range(per_step):
        gather(srcn_ref, r0 + t, other).start()
    for t in range(per_step):
        scatter(dstp_ref, r0 + t, other).start()

    @pl.when(j == nf - 1)
    def _():
        @pl.when(i >= 1)
        def _():
            wait_scatter(slot)

        for s in range(sub):
            ys[slot, pl.ds(s, tm, stride=sub), :] = acc[:, s * LANES:(s + 1) * LANES]

        @pl.when(i == nb - 1)
        def _():
            def start(r, c):
                scatter(dstl_ref, r, slot).start()
                return c

            lax.fori_loop(0, tm, start, 0)
            wait_scatter(slot)
            wait_scatter(other)
            wait_gather(other)


def _experts(hs, blk_e, src_tok, dst_row, n_slabs, wg, wu, wd, tm, tf):
    nb = src_tok.shape[0]
    dm, dff = wg.shape[1], wg.shape[2]
    sub = dm // LANES
    nf = dff // tf
    smem = lambda imap: pl.BlockSpec((None, 1, tm), imap, memory_space=pltpu.SMEM)
    grid_spec = pltpu.PrefetchScalarGridSpec(
        num_scalar_prefetch=1,
        grid=(nb, nf),
        in_specs=[smem(lambda i, j, be: (0, 0, 0)),
                  smem(lambda i, j, be: (jnp.minimum(i + 1, nb - 1), 0, 0)),
                  smem(lambda i, j, be: (i, 0, 0)),
                  smem(lambda i, j, be: (nb, 0, 0)),
                  pl.BlockSpec(memory_space=pl.ANY),
                  pl.BlockSpec((None, dm, tf), lambda i, j, be: (be[i], 0, j)),
                  pl.BlockSpec((None, dm, tf), lambda i, j, be: (be[i], 0, j)),
                  pl.BlockSpec((None, tf, dm), lambda i, j, be: (be[i], j, 0))],
        out_specs=pl.BlockSpec(memory_space=pl.ANY),
        scratch_shapes=[pltpu.VMEM((2, tm * sub, LANES), F32), pltpu.VMEM((tm, dm), BF16),
                        pltpu.VMEM((tm, dm), F32), pltpu.VMEM((2, tm * sub, LANES), F32),
                        pltpu.SemaphoreType.DMA((2,)), pltpu.SemaphoreType.DMA((2,))],
    )
    return pl.pallas_call(
        functools.partial(_experts_kernel, tm=tm, sub=sub),
        grid_spec=grid_spec,
        out_shape=jax.ShapeDtypeStruct((n_slabs * sub, LANES), F32),
        compiler_params=_cparams(("arbitrary", "arbitrary"), 48),
        name="experts",
    )(blk_e, src_tok, src_tok, dst_row, dst_row, hs, wg, wu, wd)


def _combine_kernel(y_ref, x_ref, gt_ref, p_ref, gf_ref, wpg_ref, wpp_ref, gp_ref, out_ref):
    tt, dm = x_ref.shape
    sub = dm // LANES
    gt = gt_ref[...]

    def rows(slot):
        return jnp.concatenate([y_ref[pl.ds(slot * sub + s, tt, stride=TOP_K * sub), :] for s in range(sub)],
                               axis=1)

    f = gt[:, 2:3] * rows(0) + gt[:, 3:4] * rows(1)
    _post_math(x_ref[...], f, p_ref, gf_ref, wpg_ref, wpp_ref, gp_ref, out_ref)


def _combine(yt, info, x2, p2, gf, wpg, wpp, gp, tt):
    n, dm = x2.shape
    row = lambda cols: pl.BlockSpec((tt, cols), lambda i: (i, 0))
    return pl.pallas_call(
        _combine_kernel,
        grid=(n // tt,),
        in_specs=[pl.BlockSpec((tt * TOP_K * (dm // LANES), LANES), lambda i: (i, 0)),
                  row(dm), row(LANES), row(p2.shape[1]), _resident((1, dm)),
                  _resident(wpg.shape), _resident(wpp.shape), _resident((1, dm))],
        out_specs=row(dm),
        out_shape=jax.ShapeDtypeStruct((n, dm), F32),
        compiler_params=_cparams(("parallel",), 48),
        name="combine",
    )(yt, x2, info, p2, gf, wpg, wpp, gp)


def _moe(x2, p2, gain, wr, br, wg, wu, wd, gf, wpg, wpp, gp, *, tm_r, tt, tm_e, tf_e):
    n, _ = x2.shape
    hs, info = _router(x2, gain, wr, br, tm_r)
    n_asg = n * TOP_K
    top_e = info[:, 0:TOP_K].astype(jnp.int32)
    e_flat = top_e.reshape(n_asg)
    onehot = (e_flat[:, None] == jnp.arange(N_EXPERTS, dtype=jnp.int32)[None, :]).astype(jnp.int32)
    csum = jnp.cumsum(onehot, axis=0)
    rank = jnp.sum((csum - onehot) * onehot, axis=1)
    counts = csum[-1]
    padded = ((counts + tm_e - 1) // tm_e) * tm_e
    pend = jnp.cumsum(padded)
    pstart = pend - padded
    dest = (pstart[e_flat] + rank).astype(jnp.int32)
    n_blocks = -(-n_asg // tm_e) + N_EXPERTS
    n_rows = n_blocks * tm_e
    blk_e = jnp.clip(jnp.searchsorted(pend, jnp.arange(n_blocks, dtype=jnp.int32) * tm_e, side="right"),
                     0, N_EXPERTS - 1).astype(jnp.int32)
    asg = jnp.full((n_rows,), -1, jnp.int32).at[dest].set(jnp.arange(n_asg, dtype=jnp.int32),
                                                          unique_indices=True)
    is_pad = asg < 0
    pad_rank = jnp.cumsum(is_pad.astype(jnp.int32)) - 1
    src_tok = jnp.where(is_pad, 0, asg // TOP_K).reshape(n_blocks, 1, tm_e)
    dst_row = jnp.where(is_pad, n_asg + pad_rank, asg)
    spare = n_asg + (n_rows - n_asg) + jnp.arange(tm_e, dtype=jnp.int32)
    dst_row = jnp.concatenate([spare, dst_row]).reshape(n_blocks + 1, 1, tm_e)
    n_slabs = n_asg + (n_rows - n_asg) + tm_e
    yt = _experts(hs, blk_e, src_tok, dst_row, n_slabs, wg, wu, wd, tm_e, tf_e)
    return _combine(yt, info, x2, p2, gf, wpg, wpp, gp, tt)


def _tables(seq):
    pos = jnp.arange(seq, dtype=jnp.int32)

    def cs(p, half):
        inv = jnp.power(ROPE_THETA, -jnp.arange(half, dtype=F32) / half)
        ang = p.astype(F32)[:, None] * inv[None, :]
        return jnp.cos(ang), jnp.sin(ang)

    ones = lambda w: jnp.ones((seq, w), F32)
    zeros = lambda w: jnp.zeros((seq, w), F32)
    (c, s), (cr, sr), (cc, sc), (ca, sa) = lax.optimization_barrier(
        (cs(pos, HEAD_DIM // 2), cs(pos // GRID_W, HEAD_DIM // 4), cs(pos % GRID_W, HEAD_DIM // 4),
         cs(pos, A_ROPE // 2)))
    full = jnp.stack([jnp.tile(jnp.concatenate([c, c], 1), (1, 2)),
                      jnp.tile(jnp.concatenate([-s, s], 1), (1, 2))])
    axial = jnp.stack([jnp.tile(jnp.concatenate([cr, cc, cr, cc], 1), (1, 2)),
                       jnp.tile(jnp.concatenate([-sr, -sc, sr, sc], 1), (1, 2))])
    slot_c =jnp.concatenate([ca, ones(16), ca, ones(16)], 1)
    slot_s = jnp.concatenate([-sa, zeros(16), sa, zeros(16)], 1)
    a_k = jnp.stack([jnp.concatenate([slot_c, ones(HALF)], 1), jnp.concatenate([slot_s, zeros(HALF)], 1)])
    a_q = jnp.stack([jnp.concatenate([ones(HALF), slot_c], 1), jnp.concatenate([zeros(HALF), slot_s], 1)])
    return {"full": full, "axial": axial, "a_k": a_k, "a_q": a_q}


_AXIAL_PERM = tuple(list(range(0, 16)) + list(range(32, 48)) + list(range(16, 32)) + list(range(48, 64)))


def _dup_heads(w, n_heads, perm=None):
    rows = w.shape[0]
    w = w.reshape(rows, n_heads, HEAD_DIM)
    if perm is not None:
        w = w[:, :, perm]
    return jnp.stack([w, w], axis=2).reshape(rows, n_heads * 2 * HEAD_DIM)


def _assemble_w_in(w):
    dm = w.shape[0]
    perm = jnp.array(_AXIAL_PERM, jnp.int32)
    a, b, c, d = w[:, 0:416], w[:, 416:928], w[:, 928:3232], w[:, 3232:3744]
    z = lambda n: jnp.zeros((dm, n), w.dtype)
    kr = a[:, 384:416]
    a_seg = jnp.concatenate([a[:, 0:384], kr[:, 0:16], z(16), kr[:, 16:32], z(16), z(HALF)], axis=1)
    bq = b[:, 0:256].reshape(dm, 4, HEAD_DIM)[:, :, perm].reshape(dm, 256)
    b_seg = jnp.concatenate([bq, _dup_heads(b[:, 256:384], 2, perm)], axis=1)
    d_seg = jnp.concatenate([d[:, 0:256], _dup_heads(d[:, 256:384], 2), _dup_heads(d[:, 384:512], 2)], axis=1)
    w_vbt = _dup_heads(b[:, 384:512], 2).T.astype(BF16)
    return jnp.concatenate([a_seg, b_seg, c, d_seg], axis=1).astype(BF16), w_vbt


def _assemble_a(w_uq, w_ukv):
    zq = lambda n: jnp.zeros((w_uq.shape[0], n), w_uq.dtype)
    zk = lambda n: jnp.zeros((w_ukv.shape[0], n), w_ukv.dtype)
    dq = A_NOPE + A_ROPE
    q_cols, k_cols, v_cols = [], [], []
    for hh in range(A_HEADS):
        q = w_uq[:, hh * dq:(hh + 1) * dq]
        q_cols += [q[:, 0:A_NOPE], q[:, A_NOPE:A_NOPE + 16], zq(16), q[:, A_NOPE + 16:dq], zq(16)]
        kv = w_ukv[:, hh * (A_NOPE + A_V):(hh + 1) * (A_NOPE + A_V)]
        k_cols += [kv[:, 0:A_NOPE], zk(HALF)]
        v_cols += [kv[:, A_NOPE:A_NOPE + A_V]]
    cat = lambda cols: jnp.concatenate(cols, axis=1).astype(BF16)
    return cat(q_cols), cat(k_cols), cat(v_cols).T


def _gain_pair(g, perm=None):
    if perm is not None:
        g = g[jnp.array(perm, jnp.int32)]
    return jnp.tile(g, 2).reshape(1, LANES).astype(F32)


def kernel(x, p, w_in, a_qa_g, a_kva_g, a_w_uq, a_w_ukv, b_q_g, b_k_g, d_sink, w_branch, w_out,
           mix_pre_g, mix_post_g, ffn_pre_g, ffn_post_g, ffn_w_gate, ffn_w_up, ffn_w_down,
           router_w, router_b, moe_w_gate, moe_w_up, moe_w_down, ple_w_proj, ple_w_gate, ple_post_g):
    bsz, seq, dm = x.shape
    depth = w_in.shape[0]
    n = bsz * seq
    tm = min(512, seq)
    tq_flash = min(512, seq)
    tk_flash = min(1024, seq // 4)
    tabs = _tables(seq)
    row = lambda g: g.reshape(1, -1).astype(F32)
    x2 = x.reshape(n, dm)

    for i in range(depth):
        w_all, wvb = _assemble_w_in(w_in[i])
        wuq, wk, wv = _assemble_a(a_w_uq[i], a_w_ukv[i])
        z, qa, ka, vat, vbt, zc1, zc2 = _in_proj(
            x2, row(mix_pre_g[i]), w_all, tabs,
            _gain_pair(b_q_g[i], _AXIAL_PERM), _gain_pair(b_k_g[i], _AXIAL_PERM),
            row(a_qa_g[i]), row(a_kva_g[i]), wuq, wk, wv, wvb, seq, tm)
        z3 = z.reshape(bsz, seq, ZMAIN_COLS)
        o_a = _flash(qa, ka, vat, q_col0=0, k_col0=0, v_row0=0, seq=seq, packed=False,
                     tq=tq_flash, tk=tk_flash)
        o_b = _flash(z, z, vbt, q_col0=ZB, k_col0=ZB + 256, v_row0=0, seq=seq, packed=True,
                     tq=tq_flash, tk=tk_flash)
        o_c, l_c = [], []
        for (win, dil), (src, col0, rc) in zip(C_PATTERNS, ((z3, ZC, ZMAIN_COLS), (zc1, 0, 768), (zc2, 0, 768))):
            og, lg = _banded(src, src, src, q_col0=col0, k_col0=col0 + 256, v_col0=col0 + 512,
                             row_cols=rc, dil=dil, seq=seq, hw=win // (2 * dil), tq=min(512, seq // dil),
                             want_lse=True, out_dtype=F32)
            o_c.append(og)
            l_c.append(lg)
        o_c[0] = o_c[0].reshape(n, 2 * LANES)
        l_c[0] = l_c[0].reshape(n, 2 * LANES)
        (o_d,) = _banded(z3, z3, z3, q_col0=ZD, k_col0=ZD + 256, v_col0=ZD + 512,
                         row_cols=ZMAIN_COLS, dil=1, seq=seq, hw=D_HALF_WINDOW, tq=min(512, seq),
                         sink=d_sink[i].astype(F32))
        o_d = o_d.reshape(n, 2 * LANES)
        wg_gate = w_in[i][:, 3744:].astype(BF16)
        x2 = _merge(x2, o_a, o_b, o_c, l_c, o_d, row(mix_pre_g[i]), wg_gate,
                    w_branch[i].astype(BF16), w_out[i].astype(BF16), row(mix_post_g[i]), seq, tm)

        p2 = p[i].reshape(n, -1)
        wpg = ple_w_gate[i].astype(BF16)
        wpp = ple_w_proj[i].astype(BF16)
        j = i // 2
        if i % 2 == 0:
            x2 = _ffn(x2, row(ffn_pre_g[i]), ffn_w_gate[j].astype(BF16), ffn_w_up[j].astype(BF16),
                      ffn_w_down[j].astype(BF16), p2, row(ffn_post_g[i]), wpg, wpp, row(ple_post_g[i]),
                      min(1024, n), 512)
        else:
            wr32 = jnp.zeros((dm, LANES), F32).at[:, :N_EXPERTS].set(router_w[j].astype(F32))
            wr_hi = wr32.astype(BF16)
            wr = jnp.stack([wr_hi, (wr32 - wr_hi.astype(F32)).astype(BF16)])
            br = jnp.full((1, LANES), NEG_INF, F32).at[0, :N_EXPERTS].set(router_b[j].astype(F32))
            x2 = _moe(x2, p2, row(ffn_pre_g[i]), wr, br, moe_w_gate[j].astype(BF16),
                      moe_w_up[j].astype(BF16), moe_w_down[j].astype(BF16), row(ffn_post_g[i]),
                      wpg, wpp, row(ple_post_g[i]), tm_r=tm, tt=min(256, n), tm_e=512, tf_e=1792)
    return x2.reshape(bsz, seq, dm)
```

```python
import functools

import jax
import jax.numpy as jnp
from jax import lax
from jax.experimental import pallas as pl
from jax.experimental.pallas import tpu as pltpu

F32 = jnp.float32
BF16 = jnp.bfloat16

GRID_W = 64
HEAD_DIM = 64
ROPE_THETA = 10000.0
NORM_EPS = 1e-6
NEG_INF = -1e30
A_HEADS = 4
A_Q_RANK = 256
A_KV_RANK = 128
A_NOPE = 64
A_ROPE = 32
A_V = 64
C_PATTERNS = ((128, 1), (512, 4), (2048, 16))
D_HALF_WINDOW = 128
N_BRANCHES = 4
N_EXPERTS = 8
TOP_K = 2

LANES = 128
HALF = 64
VMEM_MB = 1024 * 1024
LOG2E = 1.4426950408889634
BAND_SUB = 128

ZB = 0
ZC = 512
ZD = 512 + 768
ZMAIN_COLS = ZD + 768
A_SEG = 512
B_SEG = 512


def _cparams(sem, vmem_mb):
    return pltpu.CompilerParams(dimension_semantics=sem, vmem_limit_bytes=vmem_mb * VMEM_MB)


def _resident(shape):
    nd = len(shape)
    return pl.BlockSpec(shape, lambda *_: (0,) * nd, pipeline_mode=pl.Buffered(1))


def _rms(xf, g):
    return xf * lax.rsqrt(jnp.mean(xf * xf, axis=-1, keepdims=True) + NORM_EPS) * g


def _sigmoid(x):
    return 1.0 / (1.0 + jnp.exp(-x))


def _swap32(a):
    lane = lax.broadcasted_iota(jnp.int32, a.shape, 1)
    fwd = pltpu.roll(a, LANES - 32, 1)
    bwd = pltpu.roll(a, 32, 1)
    return jnp.where((lane & 32) == 0, fwd, bwd)


def _rope(a, cos, sin):
    outs = []
    for c in range(a.shape[1] // LANES):
        ch = a[:, c * LANES:(c + 1) * LANES]
        outs.append(ch * cos + _swap32(ch) * sin)
    return outs[0] if len(outs) == 1 else jnp.concatenate(outs, axis=1)


def _head_norm(a, g, bd):
    outs = []
    for c in range(a.shape[1] // LANES):
        ch = a[:, c * LANES:(c + 1) * LANES]
        sq = ch * ch
        hi = sq.astype(BF16)
        lo = (sq - hi.astype(F32)).astype(BF16)
        ms = (jnp.dot(hi, bd, preferred_element_type=F32)
              + jnp.dot(lo, bd, preferred_element_type=F32))
        outs.append(ch * lax.rsqrt(ms + NORM_EPS) * g)
    return outs[0] if len(outs) == 1 else jnp.concatenate(outs, axis=1)


def _in_proj_kernel(x_ref, g_ref, w_ref, tabf_ref, tabx_ref, taba_ref, tabq_ref,
                    bqg_ref, bkg_ref, aqg_ref, akvg_ref, wuq_ref, wk_ref, wv_ref, wvb_ref,
                    z_ref, qa_ref, ka_ref, vat_ref, vbt_ref, zc1_ref, zc2_ref, cls_scr):
    h = _rms(x_ref[...], g_ref[...]).astype(BF16)
    cf, sf = tabf_ref[0], tabf_ref[1]
    cx, sx = tabx_ref[0], tabx_ref[1]
    q_scale = HEAD_DIM ** -0.5

    acc = jnp.dot(h, w_ref[:, 0:A_SEG], preferred_element_type=F32)
    nq = _rms(acc[:, 0:A_Q_RANK], aqg_ref[...]).astype(BF16)
    nkv = _rms(acc[:, A_Q_RANK:A_Q_RANK + A_KV_RANK], akvg_ref[...]).astype(BF16)
    kr = _rope(acc[:, 384:512], taba_ref[0], taba_ref[1]).astype(BF16)
    qa = jnp.dot(nq, wuq_ref[...], preferred_element_type=F32)
    qa = _rope(qa, tabq_ref[0], tabq_ref[1]) * ((A_NOPE + A_ROPE) ** -0.5 * LOG2E)
    qa_ref[...] = qa.astype(qa_ref.dtype)
    r = lax.broadcasted_iota(jnp.int32, (LANES, A_HEADS * LANES), 0)
    c = lax.broadcasted_iota(jnp.int32, (LANES, A_HEADS * LANES), 1)
    place = jnp.where((r < HALF) & ((c & (LANES - 1)) == r + HALF), 1.0, 0.0).astype(BF16)
    ka = (jnp.dot(nkv, wk_ref[...], preferred_element_type=F32)
          + jnp.dot(kr, place, preferred_element_type=F32))
    ka_ref[...] = ka.astype(ka_ref.dtype)
    nt_dims = (((1,), (1,)), ((), ()))
    vat_ref[...] = lax.dot_general(wv_ref[...], nkv, nt_dims, preferred_element_type=F32).astype(vat_ref.dtype)
    vbt_ref[...] = lax.dot_general(wvb_ref[...], h, nt_dims, preferred_element_type=F32).astype(vbt_ref.dtype)

    rr = lax.broadcasted_iota(jnp.int32, (LANES, LANES), 0)
    cc = lax.broadcasted_iota(jnp.int32, (LANES, LANES), 1)
    bd = jnp.where((rr >> 6) == (cc >> 6), 1.0 / HEAD_DIM, 0.0).astype(BF16)
    acc = jnp.dot(h, w_ref[:, A_SEG:A_SEG + B_SEG], preferred_element_type=F32)
    q = _rope(_head_norm(acc[:, 0:256], bqg_ref[...], bd), cx, sx) * (q_scale * LOG2E)
    k = _rope(_head_norm(acc[:, 256:512], bkg_ref[...], bd), cx, sx)
    z_ref[:, ZB:ZB + 256] = q.astype(z_ref.dtype)
    z_ref[:, ZB + 256:ZB + 512] = k.astype(z_ref.dtype)

    tm = h.shape[0]
    for widx, zoff, cls_ref, dil in ((0, ZC, None, 1), (1, 0, zc1_ref, C_PATTERNS[1][1]),
                                     (2, 0, zc2_ref, C_PATTERNS[2][1]), (3, ZD, None, 1)):
        base = A_SEG + B_SEG + widx * 768
        acc = jnp.dot(h, w_ref[:, base:base + 768], preferred_element_type=F32)
        q = _rope(acc[:, 0:256], cf, sf) * q_scale
        k = _rope(acc[:, 256:512], cf, sf)
        if cls_ref is None:
            z_ref[:, zoff:zoff + 256] = q.astype(z_ref.dtype)
            z_ref[:, zoff + 256:zoff + 512] = k.astype(z_ref.dtype)
            z_ref[:, zoff + 512:zoff + 768] = acc[:, 512:768].astype(z_ref.dtype)
        else:
            qkv = (q[:, 0:LANES], q[:, LANES:], k[:, 0:LANES], k[:, LANES:],
                   acc[:, 512:512 + LANES], acc[:, 512 + LANES:768])
            for ch, val in enumerate(qkv):
                cls_scr[ch] = val
            for c in range(dil):
                for ch in range(len(qkv)):
                    col = c * 768 + ch * LANES
                    cls_ref[:, col:col + LANES] = cls_scr[ch, pl.ds(c, tm // dil, stride=dil), :].astype(cls_ref.dtype)


def _in_proj(x2, gain, w, tabs, bqg, bkg, aqg, akvg, wuq, wk, wv, wvb, seq, tm):
    n, dm = x2.shape
    nt = seq // tm
    bsz = n // seq
    d1, d2 = C_PATTERNS[1][1], C_PATTERNS[2][1]
    tab_spec = pl.BlockSpec((2, tm, LANES), lambda i: (0, i % nt, 0))
    row = lambda cols: pl.BlockSpec((tm, cols), lambda i: (i, 0))
    vt_spec = pl.BlockSpec((None, 2 * LANES, tm), lambda i: (i // nt, 0, i % nt))
    cls_spec = lambda d: pl.BlockSpec((None, tm // d, d * 768), lambda i: (i // nt, i % nt, 0))
    return pl.pallas_call(
        _in_proj_kernel,
        grid=(n // tm,),
        in_specs=[row(dm), _resident((1, dm)), _resident(w.shape),
                  tab_spec, tab_spec, tab_spec, tab_spec,
                  _resident((1, LANES)), _resident((1, LANES)),
                  _resident((1, A_Q_RANK)), _resident((1, A_KV_RANK)),
                  _resident(wuq.shape), _resident(wk.shape), _resident(wv.shape), _resident(wvb.shape)],
        out_specs=[row(ZMAIN_COLS), row(A_HEADS * LANES), row(A_HEADS * LANES),
                   vt_spec, vt_spec, cls_spec(d1), cls_spec(d2)],
        out_shape=[jax.ShapeDtypeStruct((n, ZMAIN_COLS), BF16),
                   jax.ShapeDtypeStruct((n, A_HEADS * LANES), BF16),
                   jax.ShapeDtypeStruct((n, A_HEADS * LANES), BF16),
                   jax.ShapeDtypeStruct((bsz, 2 * LANES, seq), BF16),
                   jax.ShapeDtypeStruct((bsz, 2 * LANES, seq), BF16),
                   jax.ShapeDtypeStruct((bsz, seq // d1, d1 * 768), BF16),
                   jax.ShapeDtypeStruct((bsz, seq // d2, d2 * 768), BF16)],
        scratch_shapes=[pltpu.VMEM((768 // LANES, tm, LANES), F32)],
        compiler_params=_cparams(("parallel",), 48),
        name="in_proj",
    )(x2, gain, w, tabs["full"], tabs["axial"], tabs["a_k"], tabs["a_q"],
      bqg, bkg, aqg, akvg, wuq, wk, wv, wvb)


FLASH_SAFE_EXP = 64.0


def _head_queries(q_ref, packed):
    lane = lax.broadcasted_iota(jnp.int32, (1, LANES), 1)
    if packed:
        return [jnp.where((lane < HALF) == (r == 0), q_ref[...], 0) for r in range(2)], [0, 0]
    return [q_ref[:, r * LANES:(r + 1) * LANES] for r in range(2)], [0, LANES]


def _flash_single_pass(q_ref, k_ref, vt_ref, s_a, s_b, *, tk, packed):
    tq = q_ref.shape[0]
    nk = k_ref.shape[0] // tk
    qs, kcs = _head_queries(q_ref, packed)
    qts = [q.astype(F32).T.astype(BF16) for q in qs]
    row = lax.broadcasted_iota(jnp.int32, (LANES, tq), 0)
    klane = lax.broadcasted_iota(jnp.int32, (tk, LANES), 1)
    k_one = jnp.where(klane == 0, 1.0, 0.0).astype(BF16)
    v_one = jnp.ones((16, tk), BF16)

    def scores(c, refs, dst):
        ks = pl.multiple_of(c * tk, tk)
        for r in range(2):
            k = jnp.concatenate([k_ref[pl.ds(ks, tk), kcs[r]:kcs[r] + LANES], k_one], axis=1)
            bias = jnp.where(row == 0, -refs[r], 0.0).astype(BF16)
            qt = jnp.concatenate([qts[r], bias], axis=0)
            dst[r] = jnp.dot(k, qt, preferred_element_type=F32)

    def consume(c, src, refs, sts, first):
        ks = pl.multiple_of(c * tk, tk)
        new = []
        for r in range(2):
            big, base, l, acc, hi, lo = sts[r]
            s = src[r]
            cmax = jnp.max(s, axis=0, keepdims=True)
            p = jnp.exp2(s).astype(BF16)
            lhs = jnp.concatenate([vt_ref[r * HALF:(r + 1) * HALF, pl.ds(ks, tk)], v_one], axis=0)
            pv = jnp.dot(lhs, p, preferred_element_type=F32)
            alpha = jnp.exp2(base - refs[r])
            l = l * alpha + pv[HALF:HALF + 1]
            acc = acc * alpha + pv[0:HALF]
            big = jnp.maximum(big, refs[r] + cmax)
            hi = jnp.maximum(hi, cmax)
            if first:
                lo = jnp.minimum(lo, cmax)
            new.append((big, refs[r], l, acc, hi, lo))
        return tuple(new)

    def ref_of(sts):
        return [st[0].astype(BF16).astype(F32) for st in sts]

    zero = jnp.zeros((1, tq), F32)
    sts = tuple((jnp.full((1, tq), NEG_INF, F32), zero, zero, jnp.zeros((HALF, tq), F32),
                 jnp.full((1, tq), NEG_INF, F32), jnp.full((1, tq), -NEG_INF, F32)) for _ in range(2))
    zeros2 = [zero, zero]
    scores(0, zeros2, s_a)
    scores(1, zeros2, s_b)
    sts = consume(0, s_a, zeros2, sts, True)
    ra = ref_of(sts)
    scores(2, ra, s_a)
    sts = consume(1, s_b, zeros2, sts, True)

    def pair(jj, carry):
        sts, ra = carry
        c0 = 2 * jj
        rb = ref_of(sts)
        scores(c0 + 1, rb, s_b)
        sts = consume(c0, s_a, ra, sts, False)
        ra = ref_of(sts)
        scores(c0 + 2, ra, s_a)
        sts = consume(c0 + 1, s_b, rb, sts, False)
        return sts, ra

    sts, ra = lax.fori_loop(1, nk // 2 - 1, pair, (sts, ra))
    rb = ref_of(sts)
    scores(nk - 1, rb, s_b)
    sts = consume(nk - 2, s_a, ra, sts, False)
    sts = consume(nk - 1, s_b, rb, sts, False)
    outs, bad = [], None
    for r in range(2):
        _, _, l, acc, hi, lo = sts[r]
        outs.append(acc / l)
        b = (hi > FLASH_SAFE_EXP) | (lo < -FLASH_SAFE_EXP) | jnp.logical_not(l > 2.0 ** -FLASH_SAFE_EXP)
        bad = b if bad is None else (bad | b)
    return jnp.concatenate(outs, axis=0), bad


def _flash_kernel(q_ref, k_ref, vt_ref, o_ref, s_a, s_b, *, tk, packed):
    out, bad = _flash_single_pass(q_ref, k_ref, vt_ref, s_a, s_b, tk=tk, packed=packed)
    o_ref[...] = out.T.astype(o_ref.dtype)

    @pl.when(jnp.max(jnp.where(bad, 1.0, 0.0)) > 0.0)
    def _():
        o_ref[...] = _flash_two_pass(q_ref, k_ref, vt_ref, s_a, s_b, tk=tk, packed=packed).T.astype(o_ref.dtype)


def _flash_two_pass(q_ref, k_ref, vt_ref, s_a, s_b, *, tk, packed):
    tq = q_ref.shape[0]
    nk = k_ref.shape[0] // tk
    qs, kcs = _head_queries(q_ref, packed)

    def scores(kk, dst):
        ks = pl.multiple_of(kk * tk, tk)
        for r in range(2):
            k = k_ref[pl.ds(ks, tk), kcs[r]:kcs[r] + LANES]
            dst[r] = lax.dot_general(k, qs[r], (((1,), (1,)), ((), ())), preferred_element_type=F32)

    def consume(kk, src, carry):
        ks = pl.multiple_of(kk * tk, tk)
        new = []
        for r in range(2):
            m, l, acc = carry[r]
            s = src[r]
            m_new = jnp.maximum(m, jnp.max(s, axis=0, keepdims=True))
            alpha = jnp.exp2(m - m_new)
            p = jnp.exp2(s - m_new)
            l = alpha * l + jnp.sum(p, axis=0, keepdims=True)
            vt = vt_ref[r * HALF:(r + 1) * HALF, pl.ds(ks, tk)]
            acc = alpha * acc + jnp.dot(vt, p.astype(BF16), preferred_element_type=F32)
            new.append((m_new, l, acc))
        return tuple(new)

    def pair(jj, carry):
        c0 = 2 * jj
        scores(c0 + 1, s_b)
        carry = consume(c0, s_a, carry)
        scores(c0 + 2, s_a)
        return consume(c0 + 1, s_b, carry)

    init = tuple((jnp.full((1, tq), NEG_INF, F32), jnp.zeros((1, tq), F32), jnp.zeros((HALF, tq), F32))
                 for _ in range(2))
    scores(0, s_a)
    carry = lax.fori_loop(0, nk // 2 - 1, pair, init)
    scores(nk - 1, s_b)
    carry = consume(nk - 2, s_a, carry)
    carry = consume(nk - 1, s_b, carry)
    return jnp.concatenate([acc / l for (_, l, acc) in carry], axis=0)


def _flash(q, k, vt, *, q_col0, k_col0, v_row0, seq, packed, tq, tk):
    n = q.shape[0]
    bsz = n // seq
    qw = LANES if packed else 2 * LANES
    nq = seq // tq
    assert seq % (2 * tk) == 0 and seq // tk >= 4, "the chunk pipeline needs an even number (>= 4) of key chunks"
    k3 = k.reshape(bsz, seq, k.shape[1])
    qb, kb, vb = q_col0 // qw, k_col0 // qw, v_row0 // LANES
    return pl.pallas_call(
        functools.partial(_flash_kernel, tk=tk, packed=packed),
        grid=(bsz, 2, nq),
        in_specs=[pl.BlockSpec((tq, qw), lambda b, j, i: (b * nq + i, qb + j)),
                  pl.BlockSpec((None, seq, qw), lambda b, j, i: (b, 0, kb + j)),
                  pl.BlockSpec((None, LANES, seq), lambda b, j, i: (b, vb + j, 0))],
        out_specs=pl.BlockSpec((tq, LANES), lambda b, j, i: (b * nq + i, j)),
        out_shape=jax.ShapeDtypeStruct((n, 2 * LANES), BF16),
        scratch_shapes=[pltpu.VMEM((2, tk, tq), F32), pltpu.VMEM((2, tk, tq), F32)],
        compiler_params=_cparams(("parallel", "parallel", "parallel"), 48),
        name="flash_packed" if packed else "flash_slots",
    )(q, k3, vt)


def _banded_kernel(*refs, hw, has_sink, want_lse):
    if has_sink:
        sink_ref, q_ref, k_ref, v_ref = refs[:4]
        outs = refs[4:]
    else:
        q_ref, k_ref, v_ref = refs[:3]
        outs = refs[3:]
    o_ref = outs[0]
    tq = q_ref.shape[0]
    length = k_ref.shape[0]
    sb = min(BAND_SUB, tq)
    win = min(sb + 2 * hw, length)
    i = pl.program_id(1)
    lane = lax.broadcasted_iota(jnp.int32, (1, LANES), 1)
    first = lane < HALF
    cols = [slice(j * LANES, (j + 1) * LANES) for j in range(2)]
    subs = list(range(tq // sb))
    kss, valids = [], []
    for u in subs:
        q0 = i * tq + u * sb
        ks = pl.multiple_of(jnp.clip(q0 - hw, 0, length - win), HALF)
        qpos = q0 + lax.broadcasted_iota(jnp.int32, (sb, win), 0)
        kpos = ks + lax.broadcasted_iota(jnp.int32, (sb, win), 1)
        kss.append(ks)
        valids.append(jnp.abs(qpos - kpos) <= hw)
    chains = [(u, j, r) for u in subs for j in range(2) for r in range(2)]
    kws = {(u, j): k_ref[pl.ds(kss[u], win), cols[j]] for u in subs for j in range(2)}
    vws = {(u, j): v_ref[pl.ds(kss[u], win), cols[j]] for u in subs for j in range(2)}
    ss = []
    for u, j, r in chains:
        q = jnp.where(first == (r == 0), q_ref[u * sb:(u + 1) * sb, cols[j]], 0)
        ss.append(lax.dot_general(q, kws[u, j], (((1,), (1,)), ((), ())), preferred_element_type=F32))
    ss = [jnp.where(valids[u], s, NEG_INF) for s, (u, j, r) in zip(ss, chains)]
    ms = [jnp.max(s, axis=-1, keepdims=True) for s in ss]
    if has_sink:
        sinks = [sink_ref[2 * j + r] for u, j, r in chains]
        ms = [jnp.maximum(m, sk) for m, sk in zip(ms, sinks)]
    es = [jnp.exp(s - m) for s, m in zip(ss, ms)]
    ls = [jnp.sum(e, axis=-1, keepdims=True) for e in es]
    if has_sink:
        ls = [l + jnp.exp(sk - m) for l, sk, m in zip(ls, sinks, ms)]
    os_ = [jnp.dot(e.astype(BF16), vws[u, j], preferred_element_type=F32) / l
           for e, l, (u, j, r) in zip(es, ls, chains)]
    for n in range(0, len(chains), 2):
        u, j, _ = chains[n]
        rows = slice(u * sb, (u + 1) * sb)
        o_ref[rows, cols[j]] = jnp.where(first, os_[n], os_[n + 1]).astype(o_ref.dtype)
        if want_lse:
            lses = [jnp.broadcast_to(ms[n + r] + jnp.log(ls[n + r]), (sb, LANES)) for r in range(2)]
            outs[1][rows, cols[j]] = jnp.where(first, lses[0], lses[1])


def _banded(q, k, v, *, q_col0, k_col0, v_col0, row_cols, dil, seq, hw, tq, sink=None,
            want_lse=False, out_dtype=BF16):
    bsz, ls, _ = q.shape
    nq = ls // tq
    pw = 2 * LANES
    rb = row_cols // pw
    qb, kb, vb = q_col0 // pw, k_col0 // pw, v_col0 // pw
    in_specs = [pl.BlockSpec((None, tq, pw), lambda bc, i: (bc // dil, i, (bc % dil) * rb + qb)),
                pl.BlockSpec((None, ls, pw), lambda bc, i: (bc // dil, 0, (bc % dil) * rb + kb)),
                pl.BlockSpec((None, ls, pw), lambda bc, i: (bc // dil, 0, (bc % dil) * rb + vb))]
    args = [q, k, v]
    if sink is not None:
        in_specs = [pl.BlockSpec(memory_space=pltpu.SMEM)] + in_specs
        args = [sink] + args
    o_spec = pl.BlockSpec((None, tq, pw), lambda bc, i: (bc // dil, i, bc % dil))
    out_specs = [o_spec]
    out_shape = [jax.ShapeDtypeStruct((bsz, ls, dil * pw), out_dtype)]
    if want_lse:
        out_specs.append(o_spec)
        out_shape.append(jax.ShapeDtypeStruct((bsz, ls, dil * pw), F32))
    res = pl.pallas_call(
        functools.partial(_banded_kernel, hw=hw, has_sink=sink is not None, want_lse=want_lse),
        grid=(bsz * dil, nq),
        in_specs=in_specs,
        out_specs=out_specs,
        out_shape=out_shape,
        compiler_params=_cparams(("parallel", "parallel"), 48),
        name="banded_d%d" % dil,
    )(*args)
    return res


def _merge_kernel(x_ref, oa_ref, ob_ref, oc0_ref, oc1_ref, oc2_ref, l0_ref, l1_ref, l2_ref,
                  od_ref, gpre_ref, wg_ref, wb_ref, wo_ref, gpost_ref, out_ref, tok_scr):
    xf = x_ref[...]
    tm, dm = xf.shape
    h = _rms(xf, gpre_ref[...]).astype(BF16)
    pw = 2 * LANES
    toks = []
    for n, src in enumerate((oc1_ref, l1_ref, oc2_ref, l2_ref)):
        dil = src.shape[1] // pw
        for c in range(dil):
            for hp in range(2):
                col = c * pw + hp * LANES
                tok_scr[2 * n + hp, pl.ds(c, tm // dil, stride=dil), :] = src[:, col:col + LANES]
        toks.append(jnp.concatenate([tok_scr[2 * n], tok_scr[2 * n + 1]], axis=1))
    oc1, l1, oc2, l2 = toks
    l0 = l0_ref[...]
    mx = jnp.maximum(jnp.maximum(l0, l1), l2)
    w0, w1, w2 = jnp.exp(l0 - mx), jnp.exp(l1 - mx), jnp.exp(l2 - mx)
    oc = (w0 * oc0_ref[...] + w1 * oc1 + w2 * oc2) / (w0 + w1 + w2)
    branches = (oa_ref[...], ob_ref[...], oc.astype(BF16), od_ref[...])
    merged = None
    for n, o in enumerate(branches):
        gate = _sigmoid(jnp.dot(h, wg_ref[:, n * dm:(n + 1) * dm], preferred_element_type=F32))
        term = gate * jnp.dot(o, wb_ref[n], preferred_element_type=F32)
        merged = term if merged is None else merged + term
    y = jnp.dot(merged.astype(BF16), wo_ref[...], preferred_element_type=F32)
    out_ref[...] = xf + _rms(y, gpost_ref[...])


def _merge(x2, oa, ob, oc, lc, od, gpre, wg, wb, wo, gpost, seq, tm):
    n, dm = x2.shape
    nt = seq // tm
    row = lambda cols: pl.BlockSpec((tm, cols), lambda i: (i, 0))
    bw = 2 * LANES
    cls = lambda a: pl.BlockSpec((None, tm // (a.shape[2] // bw), a.shape[2]), lambda i: (i // nt, i % nt, 0))
    return pl.pallas_call(
        _merge_kernel,
        grid=(n // tm,),
        in_specs=[row(dm), row(bw), row(bw), row(bw), cls(oc[1]), cls(oc[2]), row(bw), cls(lc[1]), cls(lc[2]),
                  row(bw), _resident((1, dm)), _resident(wg.shape), _resident(wb.shape),
                  _resident(wo.shape), _resident((1, dm))],
        out_specs=row(dm),
        out_shape=jax.ShapeDtypeStruct((n, dm), F32),
        scratch_shapes=[pltpu.VMEM((8, tm, LANES), F32)],
        compiler_params=_cparams(("parallel",), 56),
        name="merge",
    )(x2, oa, ob, oc[0], oc[1], oc[2], lc[0], lc[1], lc[2], od, gpre, wg, wb, wo, gpost)


def _ffn_kernel(x_ref, g_ref, wg_ref, wu_ref, wd_ref, p_ref, gf_ref, wpg_ref, wpp_ref, gp_ref,
                out_ref, h_scr, acc_scr):
    j = pl.program_id(1)

    @pl.when(j == 0)
    def _():
        h_scr[...] = _rms(x_ref[...], g_ref[...]).astype(BF16)
        acc_scr[...] = jnp.zeros_like(acc_scr)

    h = h_scr[...]
    a = jnp.dot(h, wg_ref[...], preferred_element_type=F32)
    u = jnp.dot(h, wu_ref[...], preferred_element_type=F32)
    act = (a * _sigmoid(a) * u).astype(BF16)
    acc_scr[...] += jnp.dot(act, wd_ref[...], preferred_element_type=F32)

    @pl.when(j == pl.num_programs(1) - 1)
    def _():
        _post_math(x_ref[...], acc_scr[...], p_ref, gf_ref, wpg_ref, wpp_ref, gp_ref, out_ref)


def _ffn(x2, gain, wg, wu, wd, p2, gf, wpg, wpp, gp, tm, tf):
    n, dm = x2.shape
    dff = wg.shape[1]
    return pl.pallas_call(
        _ffn_kernel,
        grid=(n // tm, dff // tf),
        in_specs=[pl.BlockSpec((tm, dm), lambda i, j: (i, 0)),
                  pl.BlockSpec((1, dm), lambda i, j: (0, 0)),
                  pl.BlockSpec((dm, tf), lambda i, j: (0, j)),
                  pl.BlockSpec((dm, tf), lambda i, j: (0, j)),
                  pl.BlockSpec((tf, dm), lambda i, j: (j, 0)),
                  pl.BlockSpec((tm, p2.shape[1]), lambda i, j: (i, 0)),
                  _resident((1, dm)), _resident(wpg.shape), _resident(wpp.shape), _resident((1, dm))],
        out_specs=pl.BlockSpec((tm, dm), lambda i, j: (i, 0)),
        out_shape=jax.ShapeDtypeStruct((n, dm), F32),
        scratch_shapes=[pltpu.VMEM((tm, dm), BF16), pltpu.VMEM((tm, dm), F32)],
        compiler_params=_cparams(("parallel", "arbitrary"), 56),
        name="ffn",
    )(x2, gain, wg, wu, wd, p2, gf, wpg, wpp, gp)


def _post_math(xf, f, p_ref, gf_ref, wpg_ref, wpp_ref, gp_ref, out_ref):
    x2 = xf + _rms(f, gf_ref[...])
    gate = _sigmoid(jnp.dot(x2.astype(BF16), wpg_ref[...], preferred_element_type=F32))
    e = jnp.dot(p_ref[...].astype(BF16), wpp_ref[...], preferred_element_type=F32) * gate
    out_ref[...] = x2 + _rms(e, gp_ref[...])


def _router_kernel(x_ref, g_ref, wr_ref, br_ref, hs_ref, info_ref):
    hf = _rms(x_ref[...], g_ref[...])
    hb = hf.astype(BF16)
    tm, dm = hf.shape
    sub = dm // LANES
    for s in range(sub):
        hs_ref[pl.ds(s, tm, stride=sub), :] = hf[:, s * LANES:(s + 1) * LANES]
    hl = (hf - hb.astype(F32)).astype(BF16)
    whi, wlo = wr_ref[0], wr_ref[1]
    logits = (jnp.dot(hb, whi, preferred_element_type=F32) + jnp.dot(hl, whi, preferred_element_type=F32)
              + jnp.dot(hb, wlo, preferred_element_type=F32)) + br_ref[...]
    lane = lax.broadcasted_iota(jnp.int32, logits.shape, 1).astype(F32)
    m1 = jnp.max(logits, axis=-1, keepdims=True)
    i1 = jnp.min(jnp.where(logits == m1, lane, float(LANES)), axis=-1, keepdims=True)
    rest = jnp.where(lane == i1, NEG_INF, logits)
    m2 = jnp.max(rest, axis=-1, keepdims=True)
    i2 = jnp.min(jnp.where(rest == m2, lane, float(LANES)), axis=-1, keepdims=True)
    e2 = jnp.exp(m2 - m1)
    g1 = 1.0 / (1.0 + e2)
    g2 = e2 / (1.0 + e2)
    info = jnp.where(lane == 0.0, i1, jnp.where(lane == 1.0, i2, jnp.where(lane == 2.0, g1, g2)))
    info_ref[...] = info


def _router(x2, gain, wr, br, tm):
    n, dm = x2.shape
    row = lambda cols: pl.BlockSpec((tm, cols), lambda i: (i, 0))
    return pl.pallas_call(
        _router_kernel,
        grid=(n // tm,),
        in_specs=[row(dm), _resident((1, dm)), _resident(wr.shape), _resident((1, LANES))],
        out_specs=[pl.BlockSpec((tm * (dm // LANES), LANES), lambda i: (i, 0)), row(LANES)],
        out_shape=[jax.ShapeDtypeStruct((n * (dm // LANES), LANES), F32),
                   jax.ShapeDtypeStruct((n, LANES), F32)],
        compiler_params=_cparams(("parallel",), 32),
        name="router",
    )(x2, gain, wr, br)


def _experts_kernel(be_ref, src0_ref, srcn_ref, dstp_ref, dstl_ref, hs_hbm, wg_ref, wu_ref, wd_ref,
                    yt_hbm, xg, xb, acc, ys, gsem, ssem, *, tm, sub):
    i = pl.program_id(0)
    j = pl.program_id(1)
    nb = pl.num_programs(0)
    nf = pl.num_programs(1)
    slot = i % 2
    other = 1 - slot

    def gather(idx_ref, r, dslot):
        src = pl.multiple_of(idx_ref[0, r] * sub, sub)
        dst = pl.multiple_of(r * sub, sub)
        return pltpu.make_async_copy(hs_hbm.at[pl.ds(src, sub)], xg.at[dslot, pl.ds(dst, sub)],
                                     gsem.at[dslot])

    def scatter(idx_ref, r, sslot):
        src = pl.multiple_of(r * sub, sub)
        dst = pl.multiple_of(idx_ref[0, r] * sub, sub)
        return pltpu.make_async_copy(ys.at[sslot, pl.ds(src, sub)], yt_hbm.at[pl.ds(dst, sub)],
                                     ssem.at[sslot])

    def wait_gather(dslot):
        pltpu.make_async_copy(hs_hbm.at[pl.ds(0, tm * sub)], xg.at[dslot], gsem.at[dslot]).wait()

    def wait_scatter(sslot):
        pltpu.make_async_copy(ys.at[sslot], yt_hbm.at[pl.ds(0, tm * sub)], ssem.at[sslot]).wait()

    @pl.when((i == 0) & (j == 0))
    def _():
        ys[1] = jnp.zeros(ys.shape[1:], ys.dtype)

        def start(r, c):
            gather(src0_ref, r, 0).start()
            return c

        lax.fori_loop(0, tm, start, 0)

    @pl.when(j == 0)
    def _():
        wait_gather(slot)
        for s in range(sub):
            xb[:, s * LANES:(s + 1) * LANES] = xg[slot, pl.ds(s, tm, stride=sub), :].astype(BF16)
        acc[...] = jnp.zeros_like(acc)

    @pl.when(i < be_ref[nb])
    def _():
        x = xb[...]
        a = jnp.dot(x, wg_ref[...], preferred_element_type=F32)
        u = jnp.dot(x, wu_ref[...], preferred_element_type=F32)
        act = (a * _sigmoid(a) * u).astype(BF16)
        acc[...] += jnp.dot(act, wd_ref[...], preferred_element_type=F32)

    @pl.when(j == 0)
    def _():
        for t in range(tm):
            gather(srcn_ref, t, other).start()

    @pl.when(j == nf - 1)
    def _():
        for t in range(tm):
            scatter(dstp_ref, t, other).start()

    @pl.when(j == nf - 1)
    def _():
        @pl.when(i >= 1)
        def _():
            wait_scatter(slot)

        for s in range(sub):
            ys[slot, pl.ds(s, tm, stride=sub), :] = acc[:, s * LANES:(s + 1) * LANES]

        @pl.when(i == nb - 1)
        def _():
            def start(r, c):
                scatter(dstl_ref, r, slot).start()
                return c

            lax.fori_loop(0, tm, start, 0)
            wait_scatter(slot)
            wait_scatter(other)
            wait_gather(other)


def _experts(hs, blk_e, src_tok, dst_row, n_slabs, wg, wu, wd, tm, tf):
    nb = src_tok.shape[0]
    dm, dff = wg.shape[1], wg.shape[2]
    sub = dm // LANES
    nf = dff // tf
    smem = lambda imap: pl.BlockSpec((None, 1, tm), imap, memory_space=pltpu.SMEM)
    grid_spec = pltpu.PrefetchScalarGridSpec(
        num_scalar_prefetch=1,
        grid=(nb, nf),
        in_specs=[smem(lambda i, j, be: (0, 0, 0)),
                  smem(lambda i, j, be: (jnp.minimum(i + 1, nb - 1), 0, 0)),
                  smem(lambda i, j, be: (i, 0, 0)),
                  smem(lambda i, j, be: (nb, 0, 0)),
                  pl.BlockSpec(memory_space=pl.ANY),
                  pl.BlockSpec((None, dm, tf), lambda i, j, be: (be[i], 0, j)),
                  pl.BlockSpec((None, dm, tf), lambda i, j, be: (be[i], 0, j)),
                  pl.BlockSpec((None, tf, dm), lambda i, j, be: (be[i], j, 0))],
        out_specs=pl.BlockSpec(memory_space=pl.ANY),
        scratch_shapes=[pltpu.VMEM((2, tm * sub, LANES), F32), pltpu.VMEM((tm, dm), BF16),
                        pltpu.VMEM((tm, dm), F32), pltpu.VMEM((2, tm * sub, LANES), F32),
                        pltpu.SemaphoreType.DMA((2,)), pltpu.SemaphoreType.DMA((2,))],
    )
    return pl.pallas_call(
        functools.partial(_experts_kernel, tm=tm, sub=sub),
        grid_spec=grid_spec,
        out_shape=jax.ShapeDtypeStruct((n_slabs * sub, LANES), F32),
        compiler_params=_cparams(("arbitrary", "arbitrary"), 48),
        name="experts",
    )(blk_e, src_tok, src_tok, dst_row, dst_row, hs, wg, wu, wd)


def _combine_kernel(y_ref, x_ref, gt_ref, p_ref, gf_ref, wpg_ref, wpp_ref, gp_ref, out_ref):
    tt, dm = x_ref.shape
    sub = dm // LANES
    gt = gt_ref[...]

    def rows(slot):
        return jnp.concatenate([y_ref[pl.ds(slot * sub + s, tt, stride=TOP_K * sub), :] for s in range(sub)],
                               axis=1)

    f = gt[:, 2:3] * rows(0) + gt[:, 3:4] * rows(1)
    _post_math(x_ref[...], f, p_ref, gf_ref, wpg_ref, wpp_ref, gp_ref, out_ref)


def _combine(yt, info, x2, p2, gf, wpg, wpp, gp, tt):
    n, dm = x2.shape
    row = lambda cols: pl.BlockSpec((tt, cols), lambda i: (i, 0))
    return pl.pallas_call(
        _combine_kernel,
        grid=(n // tt,),
        in_specs=[pl.BlockSpec((tt * TOP_K * (dm // LANES), LANES), lambda i: (i, 0)),
                  row(dm), row(LANES), row(p2.shape[1]), _resident((1, dm)),
                  _resident(wpg.shape), _resident(wpp.shape), _resident((1, dm))],
        out_specs=row(dm),
        out_shape=jax.ShapeDtypeStruct((n, dm), F32),
        compiler_params=_cparams(("parallel",), 48),
        name="combine",
    )(yt, x2, info, p2, gf, wpg, wpp, gp)


def _moe(x2, p2, gain, wr, br, wg, wu, wd, gf, wpg, wpp, gp, *, tm_r, tt, tm_e, tf_e):
    n, _ = x2.shape
    hs, info = _router(x2, gain, wr, br, tm_r)
    n_asg = n * TOP_K
    top_e = info[:, 0:TOP_K].astype(jnp.int32)
    e_flat = top_e.reshape(n_asg)
    onehot = (e_flat[:, None] == jnp.arange(N_EXPERTS, dtype=jnp.int32)[None, :]).astype(jnp.int32)
    csum = jnp.cumsum(onehot, axis=0)
    rank = jnp.sum((csum - onehot) * onehot, axis=1)
    counts = csum[-1]
    padded = ((counts + tm_e - 1) // tm_e) * tm_e
    pend = jnp.cumsum(padded)
    pstart = pend - padded
    dest = (pstart[e_flat] + rank).astype(jnp.int32)
    n_blocks = -(-n_asg // tm_e) + N_EXPERTS
    n_rows = n_blocks * tm_e
    blk_e = jnp.clip(jnp.searchsorted(pend, jnp.arange(n_blocks, dtype=jnp.int32) * tm_e, side="right"),
                     0, N_EXPERTS - 1).astype(jnp.int32)
    blk_e = jnp.concatenate([blk_e, (pend[-1:] // tm_e).astype(jnp.int32)])
    asg = jnp.full((n_rows,), -1, jnp.int32).at[dest].set(jnp.arange(n_asg, dtype=jnp.int32),
                                                          unique_indices=True)
    is_pad = asg < 0
    pad_rank = jnp.cumsum(is_pad.astype(jnp.int32)) - 1
    src_tok = jnp.where(is_pad, 0, asg // TOP_K).reshape(n_blocks, 1, tm_e)
    dst_row = jnp.where(is_pad, n_asg + pad_rank, asg)
    spare = n_asg + (n_rows - n_asg) + jnp.arange(tm_e, dtype=jnp.int32)
    dst_row = jnp.concatenate([spare, dst_row]).reshape(n_blocks + 1, 1, tm_e)
    n_slabs = n_asg + (n_rows - n_asg) + tm_e
    yt = _experts(hs, blk_e, src_tok, dst_row, n_slabs, wg, wu, wd, tm_e, tf_e)
    return _combine(yt, info, x2, p2, gf, wpg, wpp, gp, tt)


def _tables(seq):
    pos = jnp.arange(seq, dtype=jnp.int32)

    def cs(p, half):
        inv = jnp.power(ROPE_THETA, -jnp.arange(half, dtype=F32) / half)
        ang = p.astype(F32)[:, None] * inv[None, :]
        return jnp.cos(ang), jnp.sin(ang)

    ones = lambda w: jnp.ones((seq, w), F32)
    zeros = lambda w: jnp.zeros((seq, w), F32)
    (c, s), (cr, sr), (cc, sc), (ca, sa) = lax.optimization_barrier(
        (cs(pos, HEAD_DIM // 2), cs(pos // GRID_W, HEAD_DIM // 4), cs(pos % GRID_W, HEAD_DIM // 4),
         cs(pos, A_ROPE // 2)))
    full = jnp.stack([jnp.tile(jnp.concatenate([c, c], 1), (1, 2)),
                      jnp.tile(jnp.concatenate([-s, s], 1), (1, 2))])
    axial = jnp.stack([jnp.tile(jnp.concatenate([cr, cc, cr, cc], 1), (1, 2)),
                       jnp.tile(jnp.concatenate([-sr, -sc, sr, sc], 1), (1, 2))])
    slot_c =jnp.concatenate([ca, ones(16), ca, ones(16)], 1)
    slot_s = jnp.concatenate([-sa, zeros(16), sa, zeros(16)], 1)
    a_k = jnp.stack([jnp.concatenate([slot_c, ones(HALF)], 1), jnp.concatenate([slot_s, zeros(HALF)], 1)])
    a_q = jnp.stack([jnp.concatenate([ones(HALF), slot_c], 1), jnp.concatenate([zeros(HALF), slot_s], 1)])
    return {"full": full, "axial": axial, "a_k": a_k, "a_q": a_q}


_AXIAL_PERM = tuple(list(range(0, 16)) + list(range(32, 48)) + list(range(16, 32)) + list(range(48, 64)))


def _dup_heads(w, n_heads, perm=None):
    rows = w.shape[0]
    w = w.reshape(rows, n_heads, HEAD_DIM)
    if perm is not None:
        w = w[:, :, perm]
    return jnp.stack([w, w], axis=2).reshape(rows, n_heads * 2 * HEAD_DIM)


def _assemble_w_in(w):
    dm = w.shape[0]
    perm = jnp.array(_AXIAL_PERM, jnp.int32)
    a, b, c, d = w[:, 0:416], w[:, 416:928], w[:, 928:3232], w[:, 3232:3744]
    z = lambda n: jnp.zeros((dm, n), w.dtype)
    kr = a[:, 384:416]
    a_seg = jnp.concatenate([a[:, 0:384], kr[:, 0:16], z(16), kr[:, 16:32], z(16), z(HALF)], axis=1)
    bq = b[:, 0:256].reshape(dm, 4, HEAD_DIM)[:, :, perm].reshape(dm, 256)
    b_seg = jnp.concatenate([bq, _dup_heads(b[:, 256:384], 2, perm)], axis=1)
    d_seg = jnp.concatenate([d[:, 0:256], _dup_heads(d[:, 256:384], 2), _dup_heads(d[:, 384:512], 2)], axis=1)
    w_vbt = _dup_heads(b[:, 384:512], 2).T.astype(BF16)
    return jnp.concatenate([a_seg, b_seg, c, d_seg], axis=1).astype(BF16), w_vbt


def _assemble_a(w_uq, w_ukv):
    zq = lambda n: jnp.zeros((w_uq.shape[0], n), w_uq.dtype)
    zk = lambda n: jnp.zeros((w_ukv.shape[0], n), w_ukv.dtype)
    dq = A_NOPE + A_ROPE
    q_cols, k_cols, v_cols = [], [], []
    for hh in range(A_HEADS):
        q = w_uq[:, hh * dq:(hh + 1) * dq]
        q_cols += [q[:, 0:A_NOPE], q[:, A_NOPE:A_NOPE + 16], zq(16), q[:, A_NOPE + 16:dq], zq(16)]
        kv = w_ukv[:, hh * (A_NOPE + A_V):(hh + 1) * (A_NOPE + A_V)]
        k_cols += [kv[:, 0:A_NOPE], zk(HALF)]
        v_cols += [kv[:, A_NOPE:A_NOPE + A_V]]
    cat = lambda cols: jnp.concatenate(cols, axis=1).astype(BF16)
    return cat(q_cols), cat(k_cols), cat(v_cols).T


def _gain_pair(g, perm=None):
    if perm is not None:
        g = g[jnp.array(perm, jnp.int32)]
    return jnp.tile(g, 2).reshape(1, LANES).astype(F32)


def kernel(x, p, w_in, a_qa_g, a_kva_g, a_w_uq, a_w_ukv, b_q_g, b_k_g, d_sink, w_branch, w_out,
           mix_pre_g, mix_post_g, ffn_pre_g, ffn_post_g, ffn_w_gate, ffn_w_up, ffn_w_down,
           router_w, router_b, moe_w_gate, moe_w_up, moe_w_down, ple_w_proj, ple_w_gate, ple_post_g):
    bsz, seq, dm = x.shape
    depth = w_in.shape[0]
    n = bsz * seq
    tm = min(512, seq)
    tq_flash = min(512, seq)
    tk_flash = min(1024, seq // 4)
    tabs = _tables(seq)
    row = lambda g: g.reshape(1, -1).astype(F32)
    x2 = x.reshape(n, dm)

    for i in range(depth):
        w_in16 = w_in[i].astype(BF16)
        w_all, wvb = _assemble_w_in(w_in16)
        wuq, wk, wv = _assemble_a(a_w_uq[i], a_w_ukv[i])
        z, qa, ka, vat, vbt, zc1, zc2 = _in_proj(
            x2, row(mix_pre_g[i]), w_all, tabs,
            _gain_pair(b_q_g[i], _AXIAL_PERM), _gain_pair(b_k_g[i], _AXIAL_PERM),
            row(a_qa_g[i]), row(a_kva_g[i]), wuq, wk, wv, wvb, seq, tm)
        z3 = z.reshape(bsz, seq, ZMAIN_COLS)
        o_a = _flash(qa, ka, vat, q_col0=0, k_col0=0, v_row0=0, seq=seq, packed=False,
                     tq=tq_flash, tk=tk_flash)
        o_b = _flash(z, z, vbt, q_col0=ZB, k_col0=ZB + 256, v_row0=0, seq=seq, packed=True,
                     tq=tq_flash, tk=tk_flash)
        o_c, l_c = [], []
        for (win, dil), (src, col0, rc) in zip(C_PATTERNS, ((z3, ZC, ZMAIN_COLS), (zc1, 0, 768), (zc2, 0, 768))):
            og, lg = _banded(src, src, src, q_col0=col0, k_col0=col0 + 256, v_col0=col0 + 512,
                             row_cols=rc, dil=dil, seq=seq, hw=win // (2 * dil), tq=min(512, seq // dil),
                             want_lse=True, out_dtype=F32)
            o_c.append(og)
            l_c.append(lg)
        o_c[0] = o_c[0].reshape(n, 2 * LANES)
        l_c[0] = l_c[0].reshape(n, 2 * LANES)
        (o_d,) = _banded(z3, z3, z3, q_col0=ZD, k_col0=ZD + 256, v_col0=ZD + 512,
                         row_cols=ZMAIN_COLS, dil=1, seq=seq, hw=D_HALF_WINDOW, tq=min(512, seq),
                         sink=d_sink[i].astype(F32))
        o_d = o_d.reshape(n, 2 * LANES)
        wg_gate = w_in16[:, 3744:]
        x2 = _merge(x2, o_a, o_b, o_c, l_c, o_d, row(mix_pre_g[i]), wg_gate,
                    w_branch[i].astype(BF16), w_out[i].astype(BF16), row(mix_post_g[i]), seq, tm)

        p2 = p[i].reshape(n, -1)
        wpg = ple_w_gate[i].astype(BF16)
        wpp = ple_w_proj[i].astype(BF16)
        j = i // 2
        if i % 2 == 0:
            x2 = _ffn(x2, row(ffn_pre_g[i]), ffn_w_gate[j].astype(BF16), ffn_w_up[j].astype(BF16),
                      ffn_w_down[j].astype(BF16), p2, row(ffn_post_g[i]), wpg, wpp, row(ple_post_g[i]),
                      min(1024, n), 512)
        else:
            wr32 = jnp.zeros((dm, LANES), F32).at[:, :N_EXPERTS].set(router_w[j].astype(F32))
            wr_hi = wr32.astype(BF16)
            wr = jnp.stack([wr_hi, (wr32 - wr_hi.astype(F32)).astype(BF16)])
            br = jnp.full((1, LANES), NEG_INF, F32).at[0, :N_EXPERTS].set(router_b[j].astype(F32))
            x2 = _moe(x2, p2, row(ffn_pre_g[i]), wr, br, moe_w_gate[j].astype(BF16),
                      moe_w_up[j].astype(BF16), moe_w_down[j].astype(BF16), row(ffn_post_g[i]),
                      wpg, wpp, row(ple_post_g[i]), tm_r=tm, tt=min(256, n), tm_e=512, tf_e=1792)
    return x2.reshape(bsz, seq, dm)
```

```python
import functools

import jax
import jax.numpy as jnp
from jax import lax
from jax.experimental import pallas as pl
from jax.experimental.pallas import tpu as pltpu

F32 = jnp.float32
BF16 = jnp.bfloat16

GRID_W = 64
HEAD_DIM = 64
ROPE_THETA = 10000.0
NORM_EPS = 1e-6
NEG_INF = -1e30
A_HEADS = 4
A_Q_RANK = 256
A_KV_RANK = 128
A_NOPE = 64
A_ROPE = 32
A_V = 64
C_PATTERNS = ((128, 1), (512, 4), (2048, 16))
D_HALF_WINDOW = 128
N_BRANCHES = 4
N_EXPERTS = 8
TOP_K = 2

LANES = 128
HALF = 64
VMEM_MB = 1024 * 1024
LOG2E = 1.4426950408889634
BAND_SUB = 128

ZB = 0
ZC = 512
ZD = 512 + 768
ZMAIN_COLS = ZD + 768
A_SEG = 512
B_SEG = 512


def _cparams(sem, vmem_mb):
    return pltpu.CompilerParams(dimension_semantics=sem, vmem_limit_bytes=vmem_mb * VMEM_MB)


def _resident(shape):
    nd = len(shape)
    return pl.BlockSpec(shape, lambda *_: (0,) * nd, pipeline_mode=pl.Buffered(1))


def _rms(xf, g):
    return xf * lax.rsqrt(jnp.mean(xf * xf, axis=-1, keepdims=True) + NORM_EPS) * g


def _sigmoid(x):
    return 1.0 / (1.0 + jnp.exp(-x))


def _swap32(a):
    lane = lax.broadcasted_iota(jnp.int32, a.shape, 1)
    fwd = pltpu.roll(a, LANES - 32, 1)
    bwd = pltpu.roll(a, 32, 1)
    return jnp.where((lane & 32) == 0, fwd, bwd)


def _rope(a, cos, sin):
    outs = []
    for c in range(a.shape[1] // LANES):
        ch = a[:, c * LANES:(c + 1) * LANES]
        outs.append(ch * cos + _swap32(ch) * sin)
    return outs[0] if len(outs) == 1 else jnp.concatenate(outs, axis=1)


def _head_norm(a, g, bd):
    outs = []
    for c in range(a.shape[1] // LANES):
        ch = a[:, c * LANES:(c + 1) * LANES]
        sq = ch * ch
        hi = sq.astype(BF16)
        lo = (sq - hi.astype(F32)).astype(BF16)
        ms = (jnp.dot(hi, bd, preferred_element_type=F32)
              + jnp.dot(lo, bd, preferred_element_type=F32))
        outs.append(ch * lax.rsqrt(ms + NORM_EPS) * g)
    return outs[0] if len(outs) == 1 else jnp.concatenate(outs, axis=1)


def _in_proj_kernel(x_ref, g_ref, w_ref, tabf_ref, tabx_ref, taba_ref, tabq_ref,
                    bqg_ref, bkg_ref, aqg_ref, akvg_ref, wuq_ref, wk_ref, wv_ref, wvb_ref,
                    z_ref, qa_ref, ka_ref, vat_ref, vbt_ref, zc1_ref, zc2_ref, cls_scr):
    h = _rms(x_ref[...], g_ref[...]).astype(BF16)
    cf, sf = tabf_ref[0], tabf_ref[1]
    cx, sx = tabx_ref[0], tabx_ref[1]
    q_scale = HEAD_DIM ** -0.5

    acc = jnp.dot(h, w_ref[:, 0:A_SEG], preferred_element_type=F32)
    nq = _rms(acc[:, 0:A_Q_RANK], aqg_ref[...]).astype(BF16)
    nkv = _rms(acc[:, A_Q_RANK:A_Q_RANK + A_KV_RANK], akvg_ref[...]).astype(BF16)
    kr = _rope(acc[:, 384:512], taba_ref[0], taba_ref[1]).astype(BF16)
    qa = jnp.dot(nq, wuq_ref[...], preferred_element_type=F32)
    qa = _rope(qa, tabq_ref[0], tabq_ref[1]) * ((A_NOPE + A_ROPE) ** -0.5 * LOG2E)
    qa_ref[...] = qa.astype(qa_ref.dtype)
    r = lax.broadcasted_iota(jnp.int32, (LANES, A_HEADS * LANES), 0)
    c = lax.broadcasted_iota(jnp.int32, (LANES, A_HEADS * LANES), 1)
    place = jnp.where((r < HALF) & ((c & (LANES - 1)) == r + HALF), 1.0, 0.0).astype(BF16)
    ka = (jnp.dot(nkv, wk_ref[...], preferred_element_type=F32)
          + jnp.dot(kr, place, preferred_element_type=F32))
    ka_ref[...] = ka.astype(ka_ref.dtype)
    nt_dims = (((1,), (1,)), ((), ()))
    vat_ref[...] = lax.dot_general(wv_ref[...], nkv, nt_dims, preferred_element_type=F32).astype(vat_ref.dtype)
    vbt_ref[...] = lax.dot_general(wvb_ref[...], h, nt_dims, preferred_element_type=F32).astype(vbt_ref.dtype)

    rr = lax.broadcasted_iota(jnp.int32, (LANES, LANES), 0)
    cc = lax.broadcasted_iota(jnp.int32, (LANES, LANES), 1)
    bd = jnp.where((rr >> 6) == (cc >> 6), 1.0 / HEAD_DIM, 0.0).astype(BF16)
    acc = jnp.dot(h, w_ref[:, A_SEG:A_SEG + B_SEG], preferred_element_type=F32)
    q = _rope(_head_norm(acc[:, 0:256], bqg_ref[...], bd), cx, sx) * (q_scale * LOG2E)
    k = _rope(_head_norm(acc[:, 256:512], bkg_ref[...], bd), cx, sx)
    z_ref[:, ZB:ZB + 256] = q.astype(z_ref.dtype)
    z_ref[:, ZB + 256:ZB + 512] = k.astype(z_ref.dtype)

    tm = h.shape[0]
    for widx, zoff, cls_ref, dil in ((0, ZC, None, 1), (1, 0, zc1_ref, C_PATTERNS[1][1]),
                                     (2, 0, zc2_ref, C_PATTERNS[2][1]), (3, ZD, None, 1)):
        base = A_SEG + B_SEG + widx * 768
        acc = jnp.dot(h, w_ref[:, base:base + 768], preferred_element_type=F32)
        q = _rope(acc[:, 0:256], cf, sf) * q_scale
        k = _rope(acc[:, 256:512], cf, sf)
        if cls_ref is None:
            z_ref[:, zoff:zoff + 256] = q.astype(z_ref.dtype)
            z_ref[:, zoff + 256:zoff + 512] = k.astype(z_ref.dtype)
            z_ref[:, zoff + 512:zoff + 768] = acc[:, 512:768].astype(z_ref.dtype)
        else:
            qkv = (q[:, 0:LANES], q[:, LANES:], k[:, 0:LANES], k[:, LANES:],
                   acc[:, 512:512 + LANES], acc[:, 512 + LANES:768])
            for ch, val in enumerate(qkv):
                cls_scr[ch] = val
            for c in range(dil):
                for ch in range(len(qkv)):
                    col = c * 768 + ch * LANES
                    cls_ref[:, col:col + LANES] = cls_scr[ch, pl.ds(c, tm // dil, stride=dil), :].astype(cls_ref.dtype)


def _in_proj(x2, gain, w, tabs, bqg, bkg, aqg, akvg, wuq, wk, wv, wvb, seq, tm):
    n, dm = x2.shape
    nt = seq // tm
    bsz = n // seq
    d1, d2 = C_PATTERNS[1][1], C_PATTERNS[2][1]
    tab_spec = pl.BlockSpec((2, tm, LANES), lambda i: (0, i % nt, 0))
    row = lambda cols: pl.BlockSpec((tm, cols), lambda i: (i, 0))
    vt_spec = pl.BlockSpec((None, 2 * LANES, tm), lambda i: (i // nt, 0, i % nt))
    cls_spec = lambda d: pl.BlockSpec((None, tm // d, d * 768), lambda i: (i // nt, i % nt, 0))
    return pl.pallas_call(
        _in_proj_kernel,
        grid=(n // tm,),
        in_specs=[row(dm), _resident((1, dm)), _resident(w.shape),
                  tab_spec, tab_spec, tab_spec, tab_spec,
                  _resident((1, LANES)), _resident((1, LANES)),
                  _resident((1, A_Q_RANK)), _resident((1, A_KV_RANK)),
                  _resident(wuq.shape), _resident(wk.shape), _resident(wv.shape), _resident(wvb.shape)],
        out_specs=[row(ZMAIN_COLS), row(A_HEADS * LANES), row(A_HEADS * LANES),
                   vt_spec, vt_spec, cls_spec(d1), cls_spec(d2)],
        out_shape=[jax.ShapeDtypeStruct((n, ZMAIN_COLS), BF16),
                   jax.ShapeDtypeStruct((n, A_HEADS * LANES), BF16),
                   jax.ShapeDtypeStruct((n, A_HEADS * LANES), BF16),
                   jax.ShapeDtypeStruct((bsz, 2 * LANES, seq), BF16),
                   jax.ShapeDtypeStruct((bsz, 2 * LANES, seq), BF16),
                   jax.ShapeDtypeStruct((bsz, seq // d1, d1 * 768), BF16),
                   jax.ShapeDtypeStruct((bsz, seq // d2, d2 * 768), BF16)],
        scratch_shapes=[pltpu.VMEM((768 // LANES, tm, LANES), F32)],
        compiler_params=_cparams(("parallel",), 48),
        name="in_proj",
    )(x2, gain, w, tabs["full"], tabs["axial"], tabs["a_k"], tabs["a_q"],
      bqg, bkg, aqg, akvg, wuq, wk, wv, wvb)


FLASH_SAFE_EXP = 64.0


def _head_queries(q_ref, packed):
    lane = lax.broadcasted_iota(jnp.int32, (1, LANES), 1)
    if packed:
        return [jnp.where((lane < HALF) == (r == 0), q_ref[...], 0) for r in range(2)], [0, 0]
    return [q_ref[:, r * LANES:(r + 1) * LANES] for r in range(2)], [0, LANES]


def _flash_single_pass(q_ref, k_ref, vt_ref, s_a, s_b, *, tk, packed):
    tq = q_ref.shape[0]
    nk = k_ref.shape[0] // tk
    qs, kcs = _head_queries(q_ref, packed)
    qts = [q.astype(F32).T.astype(BF16) for q in qs]
    row = lax.broadcasted_iota(jnp.int32, (LANES, tq), 0)
    klane = lax.broadcasted_iota(jnp.int32, (tk, LANES), 1)
    k_one = jnp.where(klane == 0, 1.0, 0.0).astype(BF16)

    def scores(c, refs, dst):
        ks = pl.multiple_of(c * tk, tk)
        for r in range(2):
            k = jnp.concatenate([k_ref[pl.ds(ks, tk), kcs[r]:kcs[r] + LANES], k_one], axis=1)
            bias = jnp.where(row == 0, -refs[r], 0.0).astype(BF16)
            qt = jnp.concatenate([qts[r], bias], axis=0)
            dst[r] = jnp.dot(k, qt, preferred_element_type=F32)

    def consume(c, src, refs, sts, first):
        ks = pl.multiple_of(c * tk, tk)
        new = []
        for r in range(2):
            big, base, l, acc, hi, lo = sts[r]
            s = src[r]
            cmax = jnp.max(s, axis=0, keepdims=True)
            p = jnp.exp2(s)
            psum = jnp.sum(p, axis=0, keepdims=True)
            pv = jnp.dot(vt_ref[r * HALF:(r + 1) * HALF, pl.ds(ks, tk)], p.astype(BF16),
                         preferred_element_type=F32)
            alpha = jnp.exp2(base - refs[r])
            l = l * alpha + psum
            acc = acc * alpha + pv
            big = jnp.maximum(big, refs[r] + cmax)
            hi = jnp.maximum(hi, cmax)
            if first:
                lo = jnp.minimum(lo, cmax)
            new.append((big, refs[r], l, acc, hi, lo))
        return tuple(new)

    def ref_of(sts):
        return [st[0].astype(BF16).astype(F32) for st in sts]

    zero = jnp.zeros((1, tq), F32)
    sts = tuple((jnp.full((1, tq), NEG_INF, F32), zero, zero, jnp.zeros((HALF, tq), F32),
                 jnp.full((1, tq), NEG_INF, F32), jnp.full((1, tq), -NEG_INF, F32)) for _ in range(2))
    zeros2 = [zero, zero]
    scores(0, zeros2, s_a)
    scores(1, zeros2, s_b)
    sts = consume(0, s_a, zeros2, sts, True)
    ra = ref_of(sts)
    scores(2, ra, s_a)
    sts = consume(1, s_b, zeros2, sts, True)

    def pair(jj, carry):
        sts, ra = carry
        c0 = 2 * jj
        rb = ref_of(sts)
        scores(c0 + 1, rb, s_b)
        sts = consume(c0, s_a, ra, sts, False)
        ra = ref_of(sts)
        scores(c0 + 2, ra, s_a)
        sts = consume(c0 + 1, s_b, rb, sts, False)
        return sts, ra

    sts, ra = lax.fori_loop(1, nk // 2 - 1, pair, (sts, ra))
    rb = ref_of(sts)
    scores(nk - 1, rb, s_b)
    sts = consume(nk - 2, s_a, ra, sts, False)
    sts = consume(nk - 1, s_b, rb, sts, False)
    outs, bad = [], None
    for r in range(2):
        _, _, l, acc, hi, lo = sts[r]
        outs.append(acc / l)
        b = (hi > FLASH_SAFE_EXP) | (lo < -FLASH_SAFE_EXP) | jnp.logical_not(l > 2.0 ** -FLASH_SAFE_EXP)
        bad = b if bad is None else (bad | b)
    return jnp.concatenate(outs, axis=0), bad


def _flash_kernel(q_ref, k_ref, vt_ref, o_ref, s_a, s_b, *, tk, packed):
    out, bad = _flash_single_pass(q_ref, k_ref, vt_ref, s_a, s_b, tk=tk, packed=packed)
    o_ref[...] = out.T.astype(o_ref.dtype)

    @pl.when(jnp.max(jnp.where(bad, 1.0, 0.0)) > 0.0)
    def _():
        o_ref[...] = _flash_two_pass(q_ref, k_ref, vt_ref, s_a, s_b, tk=tk, packed=packed).T.astype(o_ref.dtype)


def _flash_two_pass(q_ref, k_ref, vt_ref, s_a, s_b, *, tk, packed):
    tq = q_ref.shape[0]
    nk = k_ref.shape[0] // tk
    qs, kcs = _head_queries(q_ref, packed)

    def scores(kk, dst):
        ks = pl.multiple_of(kk * tk, tk)
        for r in range(2):
            k = k_ref[pl.ds(ks, tk), kcs[r]:kcs[r] + LANES]
            dst[r] = lax.dot_general(k, qs[r], (((1,), (1,)), ((), ())), preferred_element_type=F32)

    def consume(kk, src, carry):
        ks = pl.multiple_of(kk * tk, tk)
        new = []
        for r in range(2):
            m, l, acc = carry[r]
            s = src[r]
            m_new = jnp.maximum(m, jnp.max(s, axis=0, keepdims=True))
            alpha = jnp.exp2(m - m_new)
            p = jnp.exp2(s - m_new)
            l = alpha * l + jnp.sum(p, axis=0, keepdims=True)
            vt = vt_ref[r * HALF:(r + 1) * HALF, pl.ds(ks, tk)]
            acc = alpha * acc + jnp.dot(vt, p.astype(BF16), preferred_element_type=F32)
            new.append((m_new, l, acc))
        return tuple(new)

    def pair(jj, carry):
        c0 = 2 * jj
        scores(c0 + 1, s_b)
        carry = consume(c0, s_a, carry)
        scores(c0 + 2, s_a)
        return consume(c0 + 1, s_b, carry)

    init = tuple((jnp.full((1, tq), NEG_INF, F32), jnp.zeros((1, tq), F32), jnp.zeros((HALF, tq), F32))
                 for _ in range(2))
    scores(0, s_a)
    carry = lax.fori_loop(0, nk // 2 - 1, pair, init)
    scores(nk - 1, s_b)
    carry = consume(nk - 2, s_a, carry)
    carry = consume(nk - 1, s_b, carry)
    return jnp.concatenate([acc / l for (_, l, acc) in carry], axis=0)


def _flash(q, k, vt, *, q_col0, k_col0, v_row0, seq, packed, tq, tk):
    n = q.shape[0]
    bsz = n // seq
    qw = LANES if packed else 2 * LANES
    nq = seq // tq
    assert seq % (2 * tk) == 0 and seq // tk >= 4, "the chunk pipeline needs an even number (>= 4) of key chunks"
    k3 = k.reshape(bsz, seq, k.shape[1])
    qb, kb, vb = q_col0 // qw, k_col0 // qw, v_row0 // LANES
    return pl.pallas_call(
        functools.partial(_flash_kernel, tk=tk, packed=packed),
        grid=(bsz, 2, nq),
        in_specs=[pl.BlockSpec((tq, qw), lambda b, j, i: (b * nq + i, qb + j)),
                  pl.BlockSpec((None, seq, qw), lambda b, j, i: (b, 0, kb + j)),
                  pl.BlockSpec((None, LANES, seq), lambda b, j, i: (b, vb + j, 0))],
        out_specs=pl.BlockSpec((tq, LANES), lambda b, j, i: (b * nq + i, j)),
        out_shape=jax.ShapeDtypeStruct((n, 2 * LANES), BF16),
        scratch_shapes=[pltpu.VMEM((2, tk, tq), F32), pltpu.VMEM((2, tk, tq), F32)],
        compiler_params=_cparams(("parallel", "parallel", "parallel"), 48),
        name="flash_packed" if packed else "flash_slots",
    )(q, k3, vt)


def _banded_kernel(*refs, hw, has_sink, want_lse):
    if has_sink:
        sink_ref, q_ref, k_ref, v_ref = refs[:4]
        outs = refs[4:]
    else:
        q_ref, k_ref, v_ref = refs[:3]
        outs = refs[3:]
    o_ref = outs[0]
    tq = q_ref.shape[0]
    length = k_ref.shape[0]
    sb = min(BAND_SUB, tq)
    win = min(sb + 2 * hw, length)
    i = pl.program_id(1)
    lane = lax.broadcasted_iota(jnp.int32, (1, LANES), 1)
    first = lane < HALF
    cols = [slice(j * LANES, (j + 1) * LANES) for j in range(2)]
    subs = list(range(tq // sb))
    kss, valids = [], []
    for u in subs:
        q0 = i * tq + u * sb
        ks = pl.multiple_of(jnp.clip(q0 - hw, 0, length - win), HALF)
        qpos = q0 + lax.broadcasted_iota(jnp.int32, (sb, win), 0)
        kpos = ks + lax.broadcasted_iota(jnp.int32, (sb, win), 1)
        kss.append(ks)
        valids.append(jnp.abs(qpos - kpos) <= hw)
    chains = [(u, j, r) for u in subs for j in range(2) for r in range(2)]
    kws = {(u, j): k_ref[pl.ds(kss[u], win), cols[j]] for u in subs for j in range(2)}
    vws = {(u, j): v_ref[pl.ds(kss[u], win), cols[j]] for u in subs for j in range(2)}
    ss = []
    for u, j, r in chains:
        q = jnp.where(first == (r == 0), q_ref[u * sb:(u + 1) * sb, cols[j]], 0)
        ss.append(lax.dot_general(q, kws[u, j], (((1,), (1,)), ((), ())), preferred_element_type=F32))
    ss = [jnp.where(valids[u], s, NEG_INF) for s, (u, j, r) in zip(ss, chains)]
    ms = [jnp.max(s, axis=-1, keepdims=True) for s in ss]
    if has_sink:
        sinks = [sink_ref[2 * j + r] for u, j, r in chains]
        ms = [jnp.maximum(m, sk) for m, sk in zip(ms, sinks)]
    es = [jnp.exp(s - m) for s, m in zip(ss, ms)]
    ls = [jnp.sum(e, axis=-1, keepdims=True) for e in es]
    if has_sink:
        ls = [l + jnp.exp(sk - m) for l, sk, m in zip(ls, sinks, ms)]
    os_ = [jnp.dot(e.astype(BF16), vws[u, j], preferred_element_type=F32) / l
           for e, l, (u, j, r) in zip(es, ls, chains)]
    for n in range(0, len(chains), 2):
        u, j, _ = chains[n]
        rows = slice(u * sb, (u + 1) * sb)
        o_ref[rows, cols[j]] = jnp.where(first, os_[n], os_[n + 1]).astype(o_ref.dtype)
        if want_lse:
            lses = [jnp.broadcast_to(ms[n + r] + jnp.log(ls[n + r]), (sb, LANES)) for r in range(2)]
            outs[1][rows, cols[j]] = jnp.where(first, lses[0], lses[1])


def _banded(q, k, v, *, q_col0, k_col0, v_col0, row_cols, dil, seq, hw, tq, sink=None,
            want_lse=False, out_dtype=BF16):
    bsz, ls, _ = q.shape
    nq = ls // tq
    pw = 2 * LANES
    rb = row_cols // pw
    qb, kb, vb = q_col0 // pw, k_col0 // pw, v_col0 // pw
    in_specs = [pl.BlockSpec((None, tq, pw), lambda bc, i: (bc // dil, i, (bc % dil) * rb + qb)),
                pl.BlockSpec((None, ls, pw), lambda bc, i: (bc // dil, 0, (bc % dil) * rb + kb)),
                pl.BlockSpec((None, ls, pw), lambda bc, i: (bc // dil, 0, (bc % dil) * rb + vb))]
    args = [q, k, v]
    if sink is not None:
        in_specs = [pl.BlockSpec(memory_space=pltpu.SMEM)] + in_specs
        args = [sink] + args
    o_spec = pl.BlockSpec((None, tq, pw), lambda bc, i: (bc // dil, i, bc % dil))
    out_specs = [o_spec]
    out_shape = [jax.ShapeDtypeStruct((bsz, ls, dil * pw), out_dtype)]
    if want_lse:
        out_specs.append(o_spec)
        out_shape.append(jax.ShapeDtypeStruct((bsz, ls, dil * pw), F32))
    res = pl.pallas_call(
        functools.partial(_banded_kernel, hw=hw, has_sink=sink is not None, want_lse=want_lse),
        grid=(bsz * dil, nq),
        in_specs=in_specs,
        out_specs=out_specs,
        out_shape=out_shape,
        compiler_params=_cparams(("parallel", "parallel"), 48),
        name="banded_d%d" % dil,
    )(*args)
    return res


def _merge_kernel(x_ref, oa_ref, ob_ref, oc0_ref, oc1_ref, oc2_ref, l0_ref, l1_ref, l2_ref,
                  od_ref, gpre_ref, wg_ref, wb_ref, wo_ref, gpost_ref, out_ref, tok_scr):
    xf = x_ref[...]
    tm, dm = xf.shape
    h = _rms(xf, gpre_ref[...]).astype(BF16)
    pw = 2 * LANES
    toks = []
    for n, src in enumerate((oc1_ref, l1_ref, oc2_ref, l2_ref)):
        dil = src.shape[1] // pw
        for c in range(dil):
            for hp in range(2):
                col = c * pw + hp * LANES
                tok_scr[2 * n + hp, pl.ds(c, tm // dil, stride=dil), :] = src[:, col:col + LANES]
        toks.append(jnp.concatenate([tok_scr[2 * n], tok_scr[2 * n + 1]], axis=1))
    oc1, l1, oc2, l2 = toks
    l0 = l0_ref[...]
    mx = jnp.maximum(jnp.maximum(l0, l1), l2)
    w0, w1, w2 = jnp.exp(l0 - mx), jnp.exp(l1 - mx), jnp.exp(l2 - mx)
    oc = (w0 * oc0_ref[...] + w1 * oc1 + w2 * oc2) / (w0 + w1 + w2)
    branches = (oa_ref[...], ob_ref[...], oc.astype(BF16), od_ref[...])
    merged = None
    for n, o in enumerate(branches):
        gate = _sigmoid(jnp.dot(h, wg_ref[:, n * dm:(n + 1) * dm], preferred_element_type=F32))
        term = gate * jnp.dot(o, wb_ref[n], preferred_element_type=F32)
        merged = term if merged is None else merged + term
    y = jnp.dot(merged.astype(BF16), wo_ref[...], preferred_element_type=F32)
    out_ref[...] = xf + _rms(y, gpost_ref[...])


def _merge(x2, oa, ob, oc, lc, od, gpre, wg, wb, wo, gpost, seq, tm):
    n, dm = x2.shape
    nt = seq // tm
    row = lambda cols: pl.BlockSpec((tm, cols), lambda i: (i, 0))
    bw = 2 * LANES
    cls = lambda a: pl.BlockSpec((None, tm // (a.shape[2] // bw), a.shape[2]), lambda i: (i // nt, i % nt, 0))
    return pl.pallas_call(
        _merge_kernel,
        grid=(n // tm,),
        in_specs=[row(dm), row(bw), row(bw), row(bw), cls(oc[1]), cls(oc[2]), row(bw), cls(lc[1]), cls(lc[2]),
                  row(bw), _resident((1, dm)), _resident(wg.shape), _resident(wb.shape),
                  _resident(wo.shape), _resident((1, dm))],
        out_specs=row(dm),
        out_shape=jax.ShapeDtypeStruct((n, dm), F32),
        scratch_shapes=[pltpu.VMEM((8, tm, LANES), F32)],
        compiler_params=_cparams(("parallel",), 56),
        name="merge",
    )(x2, oa, ob, oc[0], oc[1], oc[2], lc[0], lc[1], lc[2], od, gpre, wg, wb, wo, gpost)


def _ffn_kernel(x_ref, g_ref, wg_ref, wu_ref, wd_ref, p_ref, gf_ref, wpg_ref, wpp_ref, gp_ref,
                out_ref, h_scr, acc_scr):
    j = pl.program_id(1)

    @pl.when(j == 0)
    def _():
        h_scr[...] = _rms(x_ref[...], g_ref[...]).astype(BF16)
        acc_scr[...] = jnp.zeros_like(acc_scr)

    h = h_scr[...]
    a = jnp.dot(h, wg_ref[...], preferred_element_type=F32)
    u = jnp.dot(h, wu_ref[...], preferred_element_type=F32)
    act = (a * _sigmoid(a) * u).astype(BF16)
    acc_scr[...] += jnp.dot(act, wd_ref[...], preferred_element_type=F32)

    @pl.when(j == pl.num_programs(1) - 1)
    def _():
        _post_math(x_ref[...], acc_scr[...], p_ref, gf_ref, wpg_ref, wpp_ref, gp_ref, out_ref)


def _ffn(x2, gain, wg, wu, wd, p2, gf, wpg, wpp, gp, tm, tf):
    n, dm = x2.shape
    dff = wg.shape[1]
    return pl.pallas_call(
        _ffn_kernel,
        grid=(n // tm, dff // tf),
        in_specs=[pl.BlockSpec((tm, dm), lambda i, j: (i, 0)),
                  pl.BlockSpec((1, dm), lambda i, j: (0, 0)),
                  pl.BlockSpec((dm, tf), lambda i, j: (0, j)),
                  pl.BlockSpec((dm, tf), lambda i, j: (0, j)),
                  pl.BlockSpec((tf, dm), lambda i, j: (j, 0)),
                  pl.BlockSpec((tm, p2.shape[1]), lambda i, j: (i, 0)),
                  _resident((1, dm)), _resident(wpg.shape), _resident(wpp.shape), _resident((1, dm))],
        out_specs=pl.BlockSpec((tm, dm), lambda i, j: (i, 0)),
        out_shape=jax.ShapeDtypeStruct((n, dm), F32),
        scratch_shapes=[pltpu.VMEM((tm, dm), BF16), pltpu.VMEM((tm, dm), F32)],
        compiler_params=_cparams(("parallel", "arbitrary"), 56),
        name="ffn",
    )(x2, gain, wg, wu, wd, p2, gf, wpg, wpp, gp)


def _post_math(xf, f, p_ref, gf_ref, wpg_ref, wpp_ref, gp_ref, out_ref):
    x2 = xf + _rms(f, gf_ref[...])
    gate = _sigmoid(jnp.dot(x2.astype(BF16), wpg_ref[...], preferred_element_type=F32))
    e = jnp.dot(p_ref[...].astype(BF16), wpp_ref[...], preferred_element_type=F32) * gate
    out_ref[...] = x2 + _rms(e, gp_ref[...])


def _router_kernel(x_ref, g_ref, wr_ref, br_ref, hs_ref, info_ref):
    hf = _rms(x_ref[...], g_ref[...])
    hb = hf.astype(BF16)
    tm, dm = hf.shape
    sub = dm // LANES
    for s in range(sub):
        hs_ref[pl.ds(s, tm, stride=sub), :] = hf[:, s * LANES:(s + 1) * LANES]
    hl = (hf - hb.astype(F32)).astype(BF16)
    whi, wlo = wr_ref[0], wr_ref[1]
    logits = (jnp.dot(hb, whi, preferred_element_type=F32) + jnp.dot(hl, whi, preferred_element_type=F32)
              + jnp.dot(hb, wlo, preferred_element_type=F32)) + br_ref[...]
    lane = lax.broadcasted_iota(jnp.int32, logits.shape, 1).astype(F32)
    m1 = jnp.max(logits, axis=-1, keepdims=True)
    i1 = jnp.min(jnp.where(logits == m1, lane, float(LANES)), axis=-1, keepdims=True)
    rest = jnp.where(lane == i1, NEG_INF, logits)
    m2 = jnp.max(rest, axis=-1, keepdims=True)
    i2 = jnp.min(jnp.where(rest == m2, lane, float(LANES)), axis=-1, keepdims=True)
    e2 = jnp.exp(m2 - m1)
    g1 = 1.0 / (1.0 + e2)
    g2 = e2 / (1.0 + e2)
    info = jnp.where(lane == 0.0, i1, jnp.where(lane == 1.0, i2, jnp.where(lane == 2.0, g1, g2)))
    info_ref[...] = info


def _router(x2, gain, wr, br, tm):
    n, dm = x2.shape
    row = lambda cols: pl.BlockSpec((tm, cols), lambda i: (i, 0))
    return pl.pallas_call(
        _router_kernel,
        grid=(n // tm,),
        in_specs=[row(dm), _resident((1, dm)), _resident(wr.shape), _resident((1, LANES))],
        out_specs=[pl.BlockSpec((tm * (dm // LANES), LANES), lambda i: (i, 0)), row(LANES)],
        out_shape=[jax.ShapeDtypeStruct((n * (dm // LANES), LANES), F32),
                   jax.ShapeDtypeStruct((n, LANES), F32)],
        compiler_params=_cparams(("parallel",), 32),
        name="router",
    )(x2, gain, wr, br)


def _experts_kernel(be_ref, src0_ref, srcn_ref, dstp_ref, dstl_ref, hs_hbm, wg_ref, wu_ref, wd_ref,
                    yt_hbm, xg, xb, acc, ys, gsem, ssem, *, tm, sub):
    i = pl.program_id(0)
    j = pl.program_id(1)
    nb = pl.num_programs(0)
    nf = pl.num_programs(1)
    slot = i % 2
    other = 1 - slot

    def gather(idx_ref, r, dslot):
        src = pl.multiple_of(idx_ref[0, r] * sub, sub)
        dst = pl.multiple_of(r * sub, sub)
        return pltpu.make_async_copy(hs_hbm.at[pl.ds(src, sub)], xg.at[dslot, pl.ds(dst, sub)],
                                     gsem.at[dslot])

    def scatter(idx_ref, r, sslot):
        src = pl.multiple_of(r * sub, sub)
        dst = pl.multiple_of(idx_ref[0, r] * sub, sub)
        return pltpu.make_async_copy(ys.at[sslot, pl.ds(src, sub)], yt_hbm.at[pl.ds(dst, sub)],
                                     ssem.at[sslot])

    def wait_gather(dslot):
        pltpu.make_async_copy(hs_hbm.at[pl.ds(0, tm * sub)], xg.at[dslot], gsem.at[dslot]).wait()

    def wait_scatter(sslot):
        pltpu.make_async_copy(ys.at[sslot], yt_hbm.at[pl.ds(0, tm * sub)], ssem.at[sslot]).wait()

    @pl.when((i == 0) & (j == 0))
    def _():
        ys[1] = jnp.zeros(ys.shape[1:], ys.dtype)

        def start(r, c):
            gather(src0_ref, r, 0).start()
            return c

        lax.fori_loop(0, tm, start, 0)

    @pl.when(j == 0)
    def _():
        wait_gather(slot)
        for s in range(sub):
            xb[:, s * LANES:(s + 1) * LANES] = xg[slot, pl.ds(s, tm, stride=sub), :].astype(BF16)
        acc[...] = jnp.zeros_like(acc)

    @pl.when(i < be_ref[nb])
    def _():
        x = xb[...]
        a = jnp.dot(x, wg_ref[...], preferred_element_type=F32)
        u = jnp.dot(x, wu_ref[...], preferred_element_type=F32)
        act = (a * _sigmoid(a) * u).astype(BF16)
        acc[...] += jnp.dot(act, wd_ref[...], preferred_element_type=F32)

    @pl.when(j == 0)
    def _():
        for t in range(tm):
            gather(srcn_ref, t, other).start()

    @pl.when(j == nf - 1)
    def _():
        for t in range(tm):
            scatter(dstp_ref, t, other).start()

    @pl.when(j == nf - 1)
    def _():
        @pl.when(i >= 1)
        def _():
            wait_scatter(slot)

        for s in range(sub):
            ys[slot, pl.ds(s, tm, stride=sub), :] = acc[:, s * LANES:(s + 1) * LANES]

        @pl.when(i == nb - 1)
        def _():
            def start(r, c):
                scatter(dstl_ref, r, slot).start()
                return c

            lax.fori_loop(0, tm, start, 0)
            wait_scatter(slot)
            wait_scatter(other)
            wait_gather(other)


def _experts(hs, blk_e, src_tok, dst_row, n_slabs, wg, wu, wd, tm, tf):
    nb = src_tok.shape[0]
    dm, dff = wg.shape[1], wg.shape[2]
    sub = dm // LANES
    nf = dff // tf
    smem = lambda imap: pl.BlockSpec((None, 1, tm), imap, memory_space=pltpu.SMEM)
    grid_spec = pltpu.PrefetchScalarGridSpec(
        num_scalar_prefetch=1,
        grid=(nb, nf),
        in_specs=[smem(lambda i, j, be: (0, 0, 0)),
                  smem(lambda i, j, be: (jnp.minimum(i + 1, nb - 1), 0, 0)),
                  smem(lambda i, j, be: (i, 0, 0)),
                  smem(lambda i, j, be: (nb, 0, 0)),
                  pl.BlockSpec(memory_space=pl.ANY),
                  pl.BlockSpec((None, dm, tf), lambda i, j, be: (be[i], 0, j)),
                  pl.BlockSpec((None, dm, tf), lambda i, j, be: (be[i], 0, j)),
                  pl.BlockSpec((None, tf, dm), lambda i, j, be: (be[i], j, 0))],
        out_specs=pl.BlockSpec(memory_space=pl.ANY),
        scratch_shapes=[pltpu.VMEM((2, tm * sub, LANES), F32), pltpu.VMEM((tm, dm), BF16),
                        pltpu.VMEM((tm, dm), F32), pltpu.VMEM((2, tm * sub, LANES), F32),
                        pltpu.SemaphoreType.DMA((2,)), pltpu.SemaphoreType.DMA((2,))],
    )
    return pl.pallas_call(
        functools.partial(_experts_kernel, tm=tm, sub=sub),
        grid_spec=grid_spec,
        out_shape=jax.ShapeDtypeStruct((n_slabs * sub, LANES), F32),
        compiler_params=_cparams(("arbitrary", "arbitrary"), 48),
        name="experts",
    )(blk_e, src_tok, src_tok, dst_row, dst_row, hs, wg, wu, wd)


def _combine_kernel(y_ref, x_ref, gt_ref, p_ref, gf_ref, wpg_ref, wpp_ref, gp_ref, out_ref):
    tt, dm = x_ref.shape
    sub = dm // LANES
    gt = gt_ref[...]

    def rows(slot):
        return jnp.concatenate([y_ref[pl.ds(slot * sub + s, tt, stride=TOP_K * sub), :] for s in range(sub)],
                               axis=1)

    f = gt[:, 2:3] * rows(0) + gt[:, 3:4] * rows(1)
    _post_math(x_ref[...], f, p_ref, gf_ref, wpg_ref, wpp_ref, gp_ref, out_ref)


def _combine(yt, info, x2, p2, gf, wpg, wpp, gp, tt):
    n, dm = x2.shape
    row = lambda cols: pl.BlockSpec((tt, cols), lambda i: (i, 0))
    return pl.pallas_call(
        _combine_kernel,
        grid=(n // tt,),
        in_specs=[pl.BlockSpec((tt * TOP_K * (dm // LANES), LANES), lambda i: (i, 0)),
                  row(dm), row(LANES), row(p2.shape[1]), _resident((1, dm)),
                  _resident(wpg.shape), _resident(wpp.shape), _resident((1, dm))],
        out_specs=row(dm),
        out_shape=jax.ShapeDtypeStruct((n, dm), F32),
        compiler_params=_cparams(("parallel",), 48),
        name="combine",
    )(yt, x2, info, p2, gf, wpg, wpp, gp)


def _moe(x2, p2, gain, wr, br, wg, wu, wd, gf, wpg, wpp, gp, *, tm_r, tt, tm_e, tf_e):
    n, _ = x2.shape
    hs, info = _router(x2, gain, wr, br, tm_r)
    n_asg = n * TOP_K
    top_e = info[:, 0:TOP_K].astype(jnp.int32)
    e_flat = top_e.reshape(n_asg)
    onehot = (e_flat[:, None] == jnp.arange(N_EXPERTS, dtype=jnp.int32)[None, :]).astype(jnp.int32)
    csum = jnp.cumsum(onehot, axis=0)
    rank = jnp.sum((csum - onehot) * onehot, axis=1)
    counts = csum[-1]
    padded = ((counts + tm_e - 1) // tm_e) * tm_e
    pend = jnp.cumsum(padded)
    pstart = pend - padded
    dest = (pstart[e_flat] + rank).astype(jnp.int32)
    n_blocks = -(-n_asg // tm_e) + N_EXPERTS
    n_rows = n_blocks * tm_e
    blk_e = jnp.clip(jnp.searchsorted(pend, jnp.arange(n_blocks, dtype=jnp.int32) * tm_e, side="right"),
                     0, N_EXPERTS - 1).astype(jnp.int32)
    blk_e = jnp.concatenate([blk_e, (pend[-1:] // tm_e).astype(jnp.int32)])
    asg = jnp.full((n_rows,), -1, jnp.int32).at[dest].set(jnp.arange(n_asg, dtype=jnp.int32),
                                                          unique_indices=True)
    is_pad = asg < 0
    pad_rank = jnp.cumsum(is_pad.astype(jnp.int32)) - 1
    src_tok = jnp.where(is_pad, 0, asg // TOP_K).reshape(n_blocks, 1, tm_e)
    dst_row = jnp.where(is_pad, n_asg + pad_rank, asg)
    spare = n_asg + (n_rows - n_asg) + jnp.arange(tm_e, dtype=jnp.int32)
    dst_row = jnp.concatenate([spare, dst_row]).reshape(n_blocks + 1, 1, tm_e)
    n_slabs = n_asg + (n_rows - n_asg) + tm_e
    yt = _experts(hs, blk_e, src_tok, dst_row, n_slabs, wg, wu, wd, tm_e, tf_e)
    return _combine(yt, info, x2, p2, gf, wpg, wpp, gp, tt)


def _tables(seq):
    pos = jnp.arange(seq, dtype=jnp.int32)

    def cs(p, half):
        inv = jnp.power(ROPE_THETA, -jnp.arange(half, dtype=F32) / half)
        ang = p.astype(F32)[:, None] * inv[None, :]
        return jnp.cos(ang), jnp.sin(ang)

    ones = lambda w: jnp.ones((seq, w), F32)
    zeros = lambda w: jnp.zeros((seq, w), F32)
    (c, s), (cr, sr), (cc, sc), (ca, sa) = lax.optimization_barrier(
        (cs(pos, HEAD_DIM // 2), cs(pos // GRID_W, HEAD_DIM // 4), cs(pos % GRID_W, HEAD_DIM // 4),
         cs(pos, A_ROPE // 2)))
    full = jnp.stack([jnp.tile(jnp.concatenate([c, c], 1), (1, 2)),
                      jnp.tile(jnp.concatenate([-s, s], 1), (1, 2))])
    axial = jnp.stack([jnp.tile(jnp.concatenate([cr, cc, cr, cc], 1), (1, 2)),
                       jnp.tile(jnp.concatenate([-sr, -sc, sr, sc], 1), (1, 2))])
    slot_c =jnp.concatenate([ca, ones(16), ca, ones(16)], 1)
    slot_s = jnp.concatenate([-sa, zeros(16), sa, zeros(16)], 1)
    a_k = jnp.stack([jnp.concatenate([slot_c, ones(HALF)], 1), jnp.concatenate([slot_s, zeros(HALF)], 1)])
    a_q = jnp.stack([jnp.concatenate([ones(HALF), slot_c], 1), jnp.concatenate([zeros(HALF), slot_s], 1)])
    return {"full": full, "axial": axial, "a_k": a_k, "a_q": a_q}


_AXIAL_PERM = tuple(list(range(0, 16)) + list(range(32, 48)) + list(range(16, 32)) + list(range(48, 64)))


def _dup_heads(w, n_heads, perm=None):
    rows = w.shape[0]
    w = w.reshape(rows, n_heads, HEAD_DIM)
    if perm is not None:
        w = w[:, :, perm]
    return jnp.stack([w, w], axis=2).reshape(rows, n_heads * 2 * HEAD_DIM)


def _assemble_w_in(w):
    dm = w.shape[0]
    perm = jnp.array(_AXIAL_PERM, jnp.int32)
    a, b, c, d = w[:, 0:416], w[:, 416:928], w[:, 928:3232], w[:, 3232:3744]
    z = lambda n: jnp.zeros((dm, n), w.dtype)
    kr = a[:, 384:416]
    a_seg = jnp.concatenate([a[:, 0:384], kr[:, 0:16], z(16), kr[:, 16:32], z(16), z(HALF)], axis=1)
    bq = b[:, 0:256].reshape(dm, 4, HEAD_DIM)[:, :, perm].reshape(dm, 256)
    b_seg = jnp.concatenate([bq, _dup_heads(b[:, 256:384], 2, perm)], axis=1)
    d_seg = jnp.concatenate([d[:, 0:256], _dup_heads(d[:, 256:384], 2), _dup_heads(d[:, 384:512], 2)], axis=1)
    w_vbt = _dup_heads(b[:, 384:512], 2).T.astype(BF16)
    return jnp.concatenate([a_seg, b_seg, c, d_seg], axis=1).astype(BF16), w_vbt


def _assemble_a(w_uq, w_ukv):
    zq = lambda n: jnp.zeros((w_uq.shape[0], n), w_uq.dtype)
    zk = lambda n: jnp.zeros((w_ukv.shape[0], n), w_ukv.dtype)
    dq = A_NOPE + A_ROPE
    q_cols, k_cols, v_cols = [], [], []
    for hh in range(A_HEADS):
        q = w_uq[:, hh * dq:(hh + 1) * dq]
        q_cols += [q[:, 0:A_NOPE], q[:, A_NOPE:A_NOPE + 16], zq(16), q[:, A_NOPE + 16:dq], zq(16)]
        kv = w_ukv[:, hh * (A_NOPE + A_V):(hh + 1) * (A_NOPE + A_V)]
        k_cols += [kv[:, 0:A_NOPE], zk(HALF)]
        v_cols += [kv[:, A_NOPE:A_NOPE + A_V]]
    cat = lambda cols: jnp.concatenate(cols, axis=1).astype(BF16)
    return cat(q_cols), cat(k_cols), cat(v_cols).T


def _gain_pair(g, perm=None):
    if perm is not None:
        g = g[jnp.array(perm, jnp.int32)]
    return jnp.tile(g, 2).reshape(1, LANES).astype(F32)


def kernel(x, p, w_in, a_qa_g, a_kva_g, a_w_uq, a_w_ukv, b_q_g, b_k_g, d_sink, w_branch, w_out,
           mix_pre_g, mix_post_g, ffn_pre_g, ffn_post_g, ffn_w_gate, ffn_w_up, ffn_w_down,
           router_w, router_b, moe_w_gate, moe_w_up, moe_w_down, ple_w_proj, ple_w_gate, ple_post_g):
    bsz, seq, dm = x.shape
    depth = w_in.shape[0]
    n = bsz * seq
    tm = min(512, seq)
    tq_flash = min(512, seq)
    tk_flash = min(1024, seq // 4)
    tabs = _tables(seq)
    row = lambda g: g.reshape(1, -1).astype(F32)
    x2 = x.reshape(n, dm)

    for i in range(depth):
        w_in16 = w_in[i].astype(BF16)
        w_all, wvb = _assemble_w_in(w_in16)
        wuq, wk, wv = _assemble_a(a_w_uq[i], a_w_ukv[i])
        z, qa, ka, vat, vbt, zc1, zc2 = _in_proj(
            x2, row(mix_pre_g[i]), w_all, tabs,
            _gain_pair(b_q_g[i], _AXIAL_PERM), _gain_pair(b_k_g[i], _AXIAL_PERM),
            row(a_qa_g[i]), row(a_kva_g[i]), wuq, wk, wv, wvb, seq, tm)
        z3 = z.reshape(bsz, seq, ZMAIN_COLS)
        o_a = _flash(qa, ka, vat, q_col0=0, k_col0=0, v_row0=0, seq=seq, packed=False,
                     tq=tq_flash, tk=tk_flash)
        o_b = _flash(z, z, vbt, q_col0=ZB, k_col0=ZB + 256, v_row0=0, seq=seq, packed=True,
                     tq=tq_flash, tk=tk_flash)
        o_c, l_c = [], []
        for (win, dil), (src, col0, rc) in zip(C_PATTERNS, ((z3, ZC, ZMAIN_COLS), (zc1, 0, 768), (zc2, 0, 768))):
            og, lg = _banded(src, src, src, q_col0=col0, k_col0=col0 + 256, v_col0=col0 + 512,
                             row_cols=rc, dil=dil, seq=seq, hw=win // (2 * dil), tq=min(512, seq // dil),
                             want_lse=True, out_dtype=F32)
            o_c.append(og)
            l_c.append(lg)
        o_c[0] = o_c[0].reshape(n, 2 * LANES)
        l_c[0] = l_c[0].reshape(n, 2 * LANES)
        (o_d,) = _banded(z3, z3, z3, q_col0=ZD, k_col0=ZD + 256, v_col0=ZD + 512,
                         row_cols=ZMAIN_COLS, dil=1, seq=seq, hw=D_HALF_WINDOW, tq=min(512, seq),
                         sink=d_sink[i].astype(F32))
        o_d = o_d.reshape(n, 2 * LANES)
        wg_gate = w_in16[:, 3744:]
        x2 = _merge(x2, o_a, o_b, o_c, l_c, o_d, row(mix_pre_g[i]), wg_gate,
                    w_branch[i].astype(BF16), w_out[i].astype(BF16), row(mix_post_g[i]), seq, tm)

        p2 = p[i].reshape(n, -1)
        wpg = ple_w_gate[i].astype(BF16)
        wpp = ple_w_proj[i].astype(BF16)
        j = i // 2
        if i % 2 == 0:
            x2 = _ffn(x2, row(ffn_pre_g[i]), ffn_w_gate[j].astype(BF16), ffn_w_up[j].astype(BF16),
                      ffn_w_down[j].astype(BF16), p2, row(ffn_post_g[i]), wpg, wpp, row(ple_post_g[i]),
                      min(1024, n), 512)
        else:
            wr32 = jnp.zeros((dm, LANES), F32).at[:, :N_EXPERTS].set(router_w[j].astype(F32))
            wr_hi = wr32.astype(BF16)
            wr = jnp.stack([wr_hi, (wr32 - wr_hi.astype(F32)).astype(BF16)])
            br = jnp.full((1, LANES), NEG_INF, F32).at[0, :N_EXPERTS].set(router_b[j].astype(F32))
            x2 = _moe(x2, p2, row(ffn_pre_g[i]), wr, br, moe_w_gate[j].astype(BF16),
                      moe_w_up[j].astype(BF16), moe_w_down[j].astype(BF16), row(ffn_post_g[i]),
                      wpg, wpp, row(ple_post_g[i]), tm_r=tm, tt=min(256, n), tm_e=512, tf_e=1792)
    return x2.reshape(bsz, seq, dm)
```

```python
import functools

import jax
import jax.numpy as jnp
import numpy as np
from jax import lax
from jax.experimental import pallas as pl
from jax.experimental.pallas import tpu as pltpu

F32 = jnp.float32
BF16 = jnp.bfloat16

GRID_W = 64
HEAD_DIM = 64
ROPE_THETA = 10000.0
NORM_EPS = 1e-6
NEG_INF = -1e30
A_HEADS = 4
A_Q_RANK = 256
A_KV_RANK = 128
A_NOPE = 64
A_ROPE = 32
A_V = 64
C_PATTERNS = ((128, 1), (512, 4), (2048, 16))
D_HALF_WINDOW = 128
N_BRANCHES = 4
N_EXPERTS = 8
TOP_K = 2

LANES = 128
HALF = 64
VMEM_MB = 1024 * 1024
LOG2E = 1.4426950408889634
BAND_SUB = 128

ZB = 0
ZC = 512
ZD = 512 + 768
ZMAIN_COLS = ZD + 768
A_SEG = 512
B_SEG = 512


def _cparams(sem, vmem_mb):
    return pltpu.CompilerParams(dimension_semantics=sem, vmem_limit_bytes=vmem_mb * VMEM_MB)


def _resident(shape):
    nd = len(shape)
    return pl.BlockSpec(shape, lambda *_: (0,) * nd, pipeline_mode=pl.Buffered(1))


def _rms(xf, g):
    return xf * lax.rsqrt(jnp.mean(xf * xf, axis=-1, keepdims=True) + NORM_EPS) * g


def _sigmoid(x):
    return 1.0 / (1.0 + jnp.exp(-x))


def _swap32(a):
    lane = lax.broadcasted_iota(jnp.int32, a.shape, 1)
    fwd = pltpu.roll(a, LANES - 32, 1)
    bwd = pltpu.roll(a, 32, 1)
    return jnp.where((lane & 32) == 0, fwd, bwd)


def _rope(a, cos, sin):
    outs = []
    for c in range(a.shape[1] // LANES):
        ch = a[:, c * LANES:(c + 1) * LANES]
        outs.append(ch * cos + _swap32(ch) * sin)
    return outs[0] if len(outs) == 1 else jnp.concatenate(outs, axis=1)


def _head_norm(a, g, bd):
    outs = []
    for c in range(a.shape[1] // LANES):
        ch = a[:, c * LANES:(c + 1) * LANES]
        sq = ch * ch
        hi = sq.astype(BF16)
        lo = (sq - hi.astype(F32)).astype(BF16)
        ms = (jnp.dot(hi, bd, preferred_element_type=F32)
              + jnp.dot(lo, bd, preferred_element_type=F32))
        outs.append(ch * lax.rsqrt(ms + NORM_EPS) * g)
    return outs[0] if len(outs) == 1 else jnp.concatenate(outs, axis=1)


def _in_proj_kernel(x_ref, g_ref, w_ref, tabf_ref, tabx_ref, taba_ref, tabq_ref,
                    bqg_ref, bkg_ref, aqg_ref, akvg_ref, wuq_ref, wk_ref, wv_ref, wvb_ref,
                    z_ref, qa_ref, ka_ref, vat_ref, vbt_ref, zc1_ref, zc2_ref, cls_scr):
    h = _rms(x_ref[...], g_ref[...]).astype(BF16)
    cf, sf = tabf_ref[0], tabf_ref[1]
    cx, sx = tabx_ref[0], tabx_ref[1]
    q_scale = HEAD_DIM ** -0.5

    acc = jnp.dot(h, w_ref[:, 0:A_SEG], preferred_element_type=F32)
    nq = _rms(acc[:, 0:A_Q_RANK], aqg_ref[...]).astype(BF16)
    nkv = _rms(acc[:, A_Q_RANK:A_Q_RANK + A_KV_RANK], akvg_ref[...]).astype(BF16)
    kr = _rope(acc[:, 384:512], taba_ref[0], taba_ref[1]).astype(BF16)
    qa = jnp.dot(nq, wuq_ref[...], preferred_element_type=F32)
    qa = _rope(qa, tabq_ref[0], tabq_ref[1]) * ((A_NOPE + A_ROPE) ** -0.5 * LOG2E)
    qa_ref[...] = qa.astype(qa_ref.dtype)
    r = lax.broadcasted_iota(jnp.int32, (LANES, A_HEADS * LANES), 0)
    c = lax.broadcasted_iota(jnp.int32, (LANES, A_HEADS * LANES), 1)
    place = jnp.where((r < HALF) & ((c & (LANES - 1)) == r + HALF), 1.0, 0.0).astype(BF16)
    ka = (jnp.dot(nkv, wk_ref[...], preferred_element_type=F32)
          + jnp.dot(kr, place, preferred_element_type=F32))
    ka_ref[...] = ka.astype(ka_ref.dtype)
    nt_dims = (((1,), (1,)), ((), ()))
    vat_ref[...] = lax.dot_general(wv_ref[...], nkv, nt_dims, preferred_element_type=F32).astype(vat_ref.dtype)
    vbt_ref[...] = lax.dot_general(wvb_ref[...], h, nt_dims, preferred_element_type=F32).astype(vbt_ref.dtype)

    rr = lax.broadcasted_iota(jnp.int32, (LANES, LANES), 0)
    cc = lax.broadcasted_iota(jnp.int32, (LANES, LANES), 1)
    bd = jnp.where((rr >> 6) == (cc >> 6), 1.0 / HEAD_DIM, 0.0).astype(BF16)
    acc = jnp.dot(h, w_ref[:, A_SEG:A_SEG + B_SEG], preferred_element_type=F32)
    q = _rope(_head_norm(acc[:, 0:256], bqg_ref[...], bd), cx, sx) * (q_scale * LOG2E)
    k = _rope(_head_norm(acc[:, 256:512], bkg_ref[...], bd), cx, sx)
    z_ref[:, ZB:ZB + 256] = q.astype(z_ref.dtype)
    z_ref[:, ZB + 256:ZB + 512] = k.astype(z_ref.dtype)

    tm = h.shape[0]
    for widx, zoff, cls_ref, dil in ((0, ZC, None, 1), (1, 0, zc1_ref, C_PATTERNS[1][1]),
                                     (2, 0, zc2_ref, C_PATTERNS[2][1]), (3, ZD, None, 1)):
        base = A_SEG + B_SEG + widx * 768
        acc = jnp.dot(h, w_ref[:, base:base + 768], preferred_element_type=F32)
        q = _rope(acc[:, 0:256], cf, sf) * q_scale
        k = _rope(acc[:, 256:512], cf, sf)
        if cls_ref is None:
            z_ref[:, zoff:zoff + 256] = q.astype(z_ref.dtype)
            z_ref[:, zoff + 256:zoff + 512] = k.astype(z_ref.dtype)
            z_ref[:, zoff + 512:zoff + 768] = acc[:, 512:768].astype(z_ref.dtype)
        else:
            qkv = (q[:, 0:LANES], q[:, LANES:], k[:, 0:LANES], k[:, LANES:],
                   acc[:, 512:512 + LANES], acc[:, 512 + LANES:768])
            for ch, val in enumerate(qkv):
                cls_scr[ch] = val
            for c in range(dil):
                for ch in range(len(qkv)):
                    col = c * 768 + ch * LANES
                    cls_ref[:, col:col + LANES] = cls_scr[ch, pl.ds(c, tm // dil, stride=dil), :].astype(cls_ref.dtype)


def _in_proj(x2, gain, w, tabs, bqg, bkg, aqg, akvg, wuq, wk, wv, wvb, seq, tm):
    n, dm = x2.shape
    nt = seq // tm
    bsz = n // seq
    d1, d2 = C_PATTERNS[1][1], C_PATTERNS[2][1]
    tab_spec = pl.BlockSpec((2, tm, LANES), lambda i: (0, i % nt, 0))
    row = lambda cols: pl.BlockSpec((tm, cols), lambda i: (i, 0))
    vt_spec = pl.BlockSpec((None, 2 * LANES, tm), lambda i: (i // nt, 0, i % nt))
    cls_spec = lambda d: pl.BlockSpec((None, tm // d, d * 768), lambda i: (i // nt, i % nt, 0))
    return pl.pallas_call(
        _in_proj_kernel,
        grid=(n // tm,),
        in_specs=[row(dm), _resident((1, dm)), _resident(w.shape),
                  tab_spec, tab_spec, tab_spec, tab_spec,
                  _resident((1, LANES)), _resident((1, LANES)),
                  _resident((1, A_Q_RANK)), _resident((1, A_KV_RANK)),
                  _resident(wuq.shape), _resident(wk.shape), _resident(wv.shape), _resident(wvb.shape)],
        out_specs=[row(ZMAIN_COLS), row(A_HEADS * LANES), row(A_HEADS * LANES),
                   vt_spec, vt_spec, cls_spec(d1), cls_spec(d2)],
        out_shape=[jax.ShapeDtypeStruct((n, ZMAIN_COLS), BF16),
                   jax.ShapeDtypeStruct((n, A_HEADS * LANES), BF16),
                   jax.ShapeDtypeStruct((n, A_HEADS * LANES), BF16),
                   jax.ShapeDtypeStruct((bsz, 2 * LANES, seq), BF16),
                   jax.ShapeDtypeStruct((bsz, 2 * LANES, seq), BF16),
                   jax.ShapeDtypeStruct((bsz, seq // d1, d1 * 768), BF16),
                   jax.ShapeDtypeStruct((bsz, seq // d2, d2 * 768), BF16)],
        scratch_shapes=[pltpu.VMEM((768 // LANES, tm, LANES), F32)],
        compiler_params=_cparams(("parallel",), 48),
        name="in_proj",
    )(x2, gain, w, tabs["full"], tabs["axial"], tabs["a_k"], tabs["a_q"],
      bqg, bkg, aqg, akvg, wuq, wk, wv, wvb)


FLASH_SAFE_EXP = 64.0


def _head_queries(q_ref, packed):
    lane = lax.broadcasted_iota(jnp.int32, (1, LANES), 1)
    if packed:
        return [jnp.where((lane < HALF) == (r == 0), q_ref[...], 0) for r in range(2)], [0, 0]
    return [q_ref[:, r * LANES:(r + 1) * LANES] for r in range(2)], [0, LANES]


def _flash_single_pass(q_ref, k_ref, vt_ref, s_a, s_b, *, tk, packed):
    tq = q_ref.shape[0]
    nk = k_ref.shape[0] // tk
    qs, kcs = _head_queries(q_ref, packed)
    qts = [q.astype(F32).T.astype(BF16) for q in qs]
    row = lax.broadcasted_iota(jnp.int32, (LANES, tq), 0)
    klane = lax.broadcasted_iota(jnp.int32, (tk, LANES), 1)
    k_one = jnp.where(klane == 0, 1.0, 0.0).astype(BF16)
    v_one = jnp.ones((16, tk), BF16)

    def scores(c, refs, dst):
        ks = pl.multiple_of(c * tk, tk)
        for r in range(2):
            k = jnp.concatenate([k_ref[pl.ds(ks, tk), kcs[r]:kcs[r] + LANES], k_one], axis=1)
            bias = jnp.where(row == 0, -refs[r], 0.0).astype(BF16)
            qt = jnp.concatenate([qts[r], bias], axis=0)
            dst[r] = jnp.dot(k, qt, preferred_element_type=F32)

    def consume(c, src, refs, sts, first):
        ks = pl.multiple_of(c * tk, tk)
        new = []
        for r in range(2):
            big, base, l, acc, hi, lo = sts[r]
            s = src[r]
            cmax = jnp.max(s, axis=0, keepdims=True)
            p = jnp.exp2(s).astype(BF16)
            lhs = jnp.concatenate([vt_ref[r * HALF:(r + 1) * HALF, pl.ds(ks, tk)], v_one], axis=0)
            pv = jnp.dot(lhs, p, preferred_element_type=F32)
            alpha = jnp.exp2(base - refs[r])
            l = l * alpha + pv[HALF:HALF + 1]
            acc = acc * alpha + pv[0:HALF]
            big = jnp.maximum(big, refs[r] + cmax)
            hi = jnp.maximum(hi, cmax)
            if first:
                lo = jnp.minimum(lo, cmax)
            new.append((big, refs[r], l, acc, hi, lo))
        return tuple(new)

    def ref_of(sts):
        return [st[0].astype(BF16).astype(F32) for st in sts]

    zero = jnp.zeros((1, tq), F32)
    sts = tuple((jnp.full((1, tq), NEG_INF, F32), zero, zero, jnp.zeros((HALF, tq), F32),
                 jnp.full((1, tq), NEG_INF, F32), jnp.full((1, tq), -NEG_INF, F32)) for _ in range(2))
    zeros2 = [zero, zero]
    scores(0, zeros2, s_a)
    scores(1, zeros2, s_b)
    sts = consume(0, s_a, zeros2, sts, True)
    ra = ref_of(sts)
    scores(2, ra, s_a)
    sts = consume(1, s_b, zeros2, sts, True)

    def pair(jj, carry):
        sts, ra = carry
        c0 = 2 * jj
        rb = ref_of(sts)
        scores(c0 + 1, rb, s_b)
        sts = consume(c0, s_a, ra, sts, False)
        ra = ref_of(sts)
        scores(c0 + 2, ra, s_a)
        sts = consume(c0 + 1, s_b, rb, sts, False)
        return sts, ra

    sts, ra = lax.fori_loop(1, nk // 2 - 1, pair, (sts, ra))
    rb = ref_of(sts)
    scores(nk - 1, rb, s_b)
    sts = consume(nk - 2, s_a, ra, sts, False)
    sts = consume(nk - 1, s_b, rb, sts, False)
    outs, bad = [], None
    for r in range(2):
        _, _, l, acc, hi, lo = sts[r]
        outs.append(acc / l)
        b = (hi > FLASH_SAFE_EXP) | (lo < -FLASH_SAFE_EXP) | jnp.logical_not(l > 2.0 ** -FLASH_SAFE_EXP)
        bad = b if bad is None else (bad | b)
    return jnp.concatenate(outs, axis=0), bad


def _flash_kernel(q_ref, k_ref, vt_ref, o_ref, s_a, s_b, *, tk, packed):
    out, bad = _flash_single_pass(q_ref, k_ref, vt_ref, s_a, s_b, tk=tk, packed=packed)
    o_ref[...] = out.T.astype(o_ref.dtype)

    @pl.when(jnp.max(jnp.where(bad, 1.0, 0.0)) > 0.0)
    def _():
        o_ref[...] = _flash_two_pass(q_ref, k_ref, vt_ref, s_a, s_b, tk=tk, packed=packed).T.astype(o_ref.dtype)


def _flash_two_pass(q_ref, k_ref, vt_ref, s_a, s_b, *, tk, packed):
    tq = q_ref.shape[0]
    nk = k_ref.shape[0] // tk
    qs, kcs = _head_queries(q_ref, packed)

    def scores(kk, dst):
        ks = pl.multiple_of(kk * tk, tk)
        for r in range(2):
            k = k_ref[pl.ds(ks, tk), kcs[r]:kcs[r] + LANES]
            dst[r] = lax.dot_general(k, qs[r], (((1,), (1,)), ((), ())), preferred_element_type=F32)

    def consume(kk, src, carry):
        ks = pl.multiple_of(kk * tk, tk)
        new = []
        for r in range(2):
            m, l, acc = carry[r]
            s = src[r]
            m_new = jnp.maximum(m, jnp.max(s, axis=0, keepdims=True))
            alpha = jnp.exp2(m - m_new)
            p = jnp.exp2(s - m_new)
            l = alpha * l + jnp.sum(p, axis=0, keepdims=True)
            vt = vt_ref[r * HALF:(r + 1) * HALF, pl.ds(ks, tk)]
            acc = alpha * acc + jnp.dot(vt, p.astype(BF16), preferred_element_type=F32)
            new.append((m_new, l, acc))
        return tuple(new)

    def pair(jj, carry):
        c0 = 2 * jj
        scores(c0 + 1, s_b)
        carry = consume(c0, s_a, carry)
        scores(c0 + 2, s_a)
        return consume(c0 + 1, s_b, carry)

    init = tuple((jnp.full((1, tq), NEG_INF, F32), jnp.zeros((1, tq), F32), jnp.zeros((HALF, tq), F32))
                 for _ in range(2))
    scores(0, s_a)
    carry = lax.fori_loop(0, nk // 2 - 1, pair, init)
    scores(nk - 1, s_b)
    carry = consume(nk - 2, s_a, carry)
    carry = consume(nk - 1, s_b, carry)
    return jnp.concatenate([acc / l for (_, l, acc) in carry], axis=0)


def _flash(q, k, vt, *, q_col0, k_col0, v_row0, seq, packed, tq, tk):
    n = q.shape[0]
    bsz = n // seq
    qw = LANES if packed else 2 * LANES
    nq = seq // tq
    assert seq % (2 * tk) == 0 and seq // tk >= 4, "the chunk pipeline needs an even number (>= 4) of key chunks"
    k3 = k.reshape(bsz, seq, k.shape[1])
    qb, kb, vb = q_col0 // qw, k_col0 // qw, v_row0 // LANES
    return pl.pallas_call(
        functools.partial(_flash_kernel, tk=tk, packed=packed),
        grid=(bsz, 2, nq),
        in_specs=[pl.BlockSpec((tq, qw), lambda b, j, i: (b * nq + i, qb + j)),
                  pl.BlockSpec((None, seq, qw), lambda b, j, i: (b, 0, kb + j)),
                  pl.BlockSpec((None, LANES, seq), lambda b, j, i: (b, vb + j, 0))],
        out_specs=pl.BlockSpec((tq, LANES), lambda b, j, i: (b * nq + i, j)),
        out_shape=jax.ShapeDtypeStruct((n, 2 * LANES), BF16),
        scratch_shapes=[pltpu.VMEM((2, tk, tq), F32), pltpu.VMEM((2, tk, tq), F32)],
        compiler_params=_cparams(("parallel", "parallel", "parallel"), 48),
        name="flash_packed" if packed else "flash_slots",
    )(q, k3, vt)


def _banded_kernel(*refs, hw, has_sink, want_lse):
    if has_sink:
        sink_ref, q_ref, k_ref, v_ref = refs[:4]
        outs = refs[4:]
    else:
        q_ref, k_ref, v_ref = refs[:3]
        outs = refs[3:]
    o_ref = outs[0]
    tq = q_ref.shape[0]
    length = k_ref.shape[0]
    sb = min(BAND_SUB, tq)
    win = min(sb + 2 * hw, length)
    i = pl.program_id(1)
    lane = lax.broadcasted_iota(jnp.int32, (1, LANES), 1)
    first = lane < HALF
    cols = [slice(j * LANES, (j + 1) * LANES) for j in range(2)]
    subs = list(range(tq // sb))
    kss, valids = [], []
    for u in subs:
        q0 = i * tq + u * sb
        ks = pl.multiple_of(jnp.clip(q0 - hw, 0, length - win), HALF)
        qpos = q0 + lax.broadcasted_iota(jnp.int32, (sb, win), 0)
        kpos = ks + lax.broadcasted_iota(jnp.int32, (sb, win), 1)
        kss.append(ks)
        valids.append(jnp.abs(qpos - kpos) <= hw)
    chains = [(u, j, r) for u in subs for j in range(2) for r in range(2)]
    kws = {(u, j): k_ref[pl.ds(kss[u], win), cols[j]] for u in subs for j in range(2)}
    vws = {(u, j): v_ref[pl.ds(kss[u], win), cols[j]] for u in subs for j in range(2)}
    ss = []
    for u, j, r in chains:
        q = jnp.where(first == (r == 0), q_ref[u * sb:(u + 1) * sb, cols[j]], 0)
        ss.append(lax.dot_general(q, kws[u, j], (((1,), (1,)), ((), ())), preferred_element_type=F32))
    ss = [jnp.where(valids[u], s, NEG_INF) for s, (u, j, r) in zip(ss, chains)]
    ms = [jnp.max(s, axis=-1, keepdims=True) for s in ss]
    if has_sink:
        sinks = [sink_ref[2 * j + r] for u, j, r in chains]
        ms = [jnp.maximum(m, sk) for m, sk in zip(ms, sinks)]
    es = [jnp.exp(s - m) for s, m in zip(ss, ms)]
    ls = [jnp.sum(e, axis=-1, keepdims=True) for e in es]
    if has_sink:
        ls = [l + jnp.exp(sk - m) for l, sk, m in zip(ls, sinks, ms)]
    os_ = [jnp.dot(e.astype(BF16), vws[u, j], preferred_element_type=F32) / l
           for e, l, (u, j, r) in zip(es, ls, chains)]
    for n in range(0, len(chains), 2):
        u, j, _ = chains[n]
        rows = slice(u * sb, (u + 1) * sb)
        o_ref[rows, cols[j]] = jnp.where(first, os_[n], os_[n + 1]).astype(o_ref.dtype)
        if want_lse:
            lses = [jnp.broadcast_to(ms[n + r] + jnp.log(ls[n + r]), (sb, LANES)) for r in range(2)]
            outs[1][rows, cols[j]] = jnp.where(first, lses[0], lses[1])


def _banded(q, k, v, *, q_col0, k_col0, v_col0, row_cols, dil, seq, hw, tq, sink=None,
            want_lse=False, out_dtype=BF16):
    bsz, ls, _ = q.shape
    nq = ls // tq
    pw = 2 * LANES
    rb = row_cols // pw
    qb, kb, vb = q_col0 // pw, k_col0 // pw, v_col0 // pw
    in_specs = [pl.BlockSpec((None, tq, pw), lambda bc, i: (bc // dil, i, (bc % dil) * rb + qb)),
                pl.BlockSpec((None, ls, pw), lambda bc, i: (bc // dil, 0, (bc % dil) * rb + kb)),
                pl.BlockSpec((None, ls, pw), lambda bc, i: (bc // dil, 0, (bc % dil) * rb + vb))]
    args = [q, k, v]
    if sink is not None:
        in_specs = [pl.BlockSpec(memory_space=pltpu.SMEM)] + in_specs
        args = [sink] + args
    o_spec = pl.BlockSpec((None, tq, pw), lambda bc, i: (bc // dil, i, bc % dil))
    out_specs = [o_spec]
    out_shape = [jax.ShapeDtypeStruct((bsz, ls, dil * pw), out_dtype)]
    if want_lse:
        out_specs.append(o_spec)
        out_shape.append(jax.ShapeDtypeStruct((bsz, ls, dil * pw), F32))
    res = pl.pallas_call(
        functools.partial(_banded_kernel, hw=hw, has_sink=sink is not None, want_lse=want_lse),
        grid=(bsz * dil, nq),
        in_specs=in_specs,
        out_specs=out_specs,
        out_shape=out_shape,
        compiler_params=_cparams(("parallel", "parallel"), 48),
        name="banded_d%d" % dil,
    )(*args)
    return res


def _merge_kernel(x_ref, oa_ref, ob_ref, oc0_ref, oc1_ref, oc2_ref, l0_ref, l1_ref, l2_ref,
                  od_ref, gpre_ref, wg_ref, wb_ref, wo_ref, gpost_ref, out_ref, tok_scr):
    xf = x_ref[...]
    tm, dm = xf.shape
    h = _rms(xf, gpre_ref[...]).astype(BF16)
    pw = 2 * LANES
    toks = []
    for n, src in enumerate((oc1_ref, l1_ref, oc2_ref, l2_ref)):
        dil = src.shape[1] // pw
        for c in range(dil):
            for hp in range(2):
                col = c * pw + hp * LANES
                tok_scr[2 * n + hp, pl.ds(c, tm // dil, stride=dil), :] = src[:, col:col + LANES]
        toks.append(jnp.concatenate([tok_scr[2 * n], tok_scr[2 * n + 1]], axis=1))
    oc1, l1, oc2, l2 = toks
    l0 = l0_ref[...]
    mx = jnp.maximum(jnp.maximum(l0, l1), l2)
    w0, w1, w2 = jnp.exp(l0 - mx), jnp.exp(l1 - mx), jnp.exp(l2 - mx)
    oc = (w0 * oc0_ref[...] + w1 * oc1 + w2 * oc2) / (w0 + w1 + w2)
    branches = (oa_ref[...], ob_ref[...], oc.astype(BF16), od_ref[...])
    merged = None
    for n, o in enumerate(branches):
        gate = _sigmoid(jnp.dot(h, wg_ref[:, n * dm:(n + 1) * dm], preferred_element_type=F32))
        term = gate * jnp.dot(o, wb_ref[n], preferred_element_type=F32)
        merged = term if merged is None else merged + term
    y = jnp.dot(merged.astype(BF16), wo_ref[...], preferred_element_type=F32)
    out_ref[...] = xf + _rms(y, gpost_ref[...])


def _merge(x2, oa, ob, oc, lc, od, gpre, wg, wb, wo, gpost, seq, tm):
    n, dm = x2.shape
    nt = seq // tm
    row = lambda cols: pl.BlockSpec((tm, cols), lambda i: (i, 0))
    bw = 2 * LANES
    cls = lambda a: pl.BlockSpec((None, tm // (a.shape[2] // bw), a.shape[2]), lambda i: (i // nt, i % nt, 0))
    return pl.pallas_call(
        _merge_kernel,
        grid=(n // tm,),
        in_specs=[row(dm), row(bw), row(bw), row(bw), cls(oc[1]), cls(oc[2]), row(bw), cls(lc[1]), cls(lc[2]),
                  row(bw), _resident((1, dm)), _resident(wg.shape), _resident(wb.shape),
                  _resident(wo.shape), _resident((1, dm))],
        out_specs=row(dm),
        out_shape=jax.ShapeDtypeStruct((n, dm), F32),
        scratch_shapes=[pltpu.VMEM((8, tm, LANES), F32)],
        compiler_params=_cparams(("parallel",), 56),
        name="merge",
    )(x2, oa, ob, oc[0], oc[1], oc[2], lc[0], lc[1], lc[2], od, gpre, wg, wb, wo, gpost)


def _ffn_kernel(x_ref, g_ref, wg_ref, wu_ref, wd_ref, p_ref, gf_ref, wpg_ref, wpp_ref, gp_ref,
                out_ref, h_scr, acc_scr):
    j = pl.program_id(1)

    @pl.when(j == 0)
    def _():
        h_scr[...] = _rms(x_ref[...], g_ref[...]).astype(BF16)
        acc_scr[...] = jnp.zeros_like(acc_scr)

    h = h_scr[...]
    a = jnp.dot(h, wg_ref[...], preferred_element_type=F32)
    u = jnp.dot(h, wu_ref[...], preferred_element_type=F32)
    act = (a * _sigmoid(a) * u).astype(BF16)
    acc_scr[...] += jnp.dot(act, wd_ref[...], preferred_element_type=F32)

    @pl.when(j == pl.num_programs(1) - 1)
    def _():
        _post_math(x_ref[...], acc_scr[...], p_ref, gf_ref, wpg_ref, wpp_ref, gp_ref, out_ref)


def _ffn(x2, gain, wg, wu, wd, p2, gf, wpg, wpp, gp, tm, tf):
    n, dm = x2.shape
    dff = wg.shape[1]
    return pl.pallas_call(
        _ffn_kernel,
        grid=(n // tm, dff // tf),
        in_specs=[pl.BlockSpec((tm, dm), lambda i, j: (i, 0)),
                  pl.BlockSpec((1, dm), lambda i, j: (0, 0)),
                  pl.BlockSpec((dm, tf), lambda i, j: (0, j)),
                  pl.BlockSpec((dm, tf), lambda i, j: (0, j)),
                  pl.BlockSpec((tf, dm), lambda i, j: (j, 0)),
                  pl.BlockSpec((tm, p2.shape[1]), lambda i, j: (i, 0)),
                  _resident((1, dm)), _resident(wpg.shape), _resident(wpp.shape), _resident((1, dm))],
        out_specs=pl.BlockSpec((tm, dm), lambda i, j: (i, 0)),
        out_shape=jax.ShapeDtypeStruct((n, dm), F32),
        scratch_shapes=[pltpu.VMEM((tm, dm), BF16), pltpu.VMEM((tm, dm), F32)],
        compiler_params=_cparams(("parallel", "arbitrary"), 56),
        name="ffn",
    )(x2, gain, wg, wu, wd, p2, gf, wpg, wpp, gp)


def _post_math(xf, f, p_ref, gf_ref, wpg_ref, wpp_ref, gp_ref, out_ref):
    x2 = xf + _rms(f, gf_ref[...])
    gate = _sigmoid(jnp.dot(x2.astype(BF16), wpg_ref[...], preferred_element_type=F32))
    e = jnp.dot(p_ref[...].astype(BF16), wpp_ref[...], preferred_element_type=F32) * gate
    out_ref[...] = x2 + _rms(e, gp_ref[...])


def _router_kernel(x_ref, g_ref, wr_ref, br_ref, hs_ref, info_ref):
    hf = _rms(x_ref[...], g_ref[...])
    hb = hf.astype(BF16)
    tm, dm = hf.shape
    sub = dm // LANES
    for s in range(sub):
        hs_ref[pl.ds(s, tm, stride=sub), :] = hf[:, s * LANES:(s + 1) * LANES]
    hl = (hf - hb.astype(F32)).astype(BF16)
    whi, wlo = wr_ref[0], wr_ref[1]
    logits = (jnp.dot(hb, whi, preferred_element_type=F32) + jnp.dot(hl, whi, preferred_element_type=F32)
              + jnp.dot(hb, wlo, preferred_element_type=F32)) + br_ref[...]
    lane = lax.broadcasted_iota(jnp.int32, logits.shape, 1).astype(F32)
    m1 = jnp.max(logits, axis=-1, keepdims=True)
    i1 = jnp.min(jnp.where(logits == m1, lane, float(LANES)), axis=-1, keepdims=True)
    rest = jnp.where(lane == i1, NEG_INF, logits)
    m2 = jnp.max(rest, axis=-1, keepdims=True)
    i2 = jnp.min(jnp.where(rest == m2, lane, float(LANES)), axis=-1, keepdims=True)
    e2 = jnp.exp(m2 - m1)
    g1 = 1.0 / (1.0 + e2)
    g2 = e2 / (1.0 + e2)
    info = jnp.where(lane == 0.0, i1, jnp.where(lane == 1.0, i2, jnp.where(lane == 2.0, g1, g2)))
    info_ref[...] = info


def _router(x2, gain, wr, br, tm):
    n, dm = x2.shape
    row = lambda cols: pl.BlockSpec((tm, cols), lambda i: (i, 0))
    return pl.pallas_call(
        _router_kernel,
        grid=(n // tm,),
        in_specs=[row(dm), _resident((1, dm)), _resident(wr.shape), _resident((1, LANES))],
        out_specs=[pl.BlockSpec((tm * (dm // LANES), LANES), lambda i: (i, 0)), row(LANES)],
        out_shape=[jax.ShapeDtypeStruct((n * (dm // LANES), LANES), F32),
                   jax.ShapeDtypeStruct((n, LANES), F32)],
        compiler_params=_cparams(("parallel",), 32),
        name="router",
    )(x2, gain, wr, br)


def _experts_kernel(be_ref, src0_ref, srcn_ref, dstp_ref, dstl_ref, hs_hbm, wg_ref, wu_ref, wd_ref,
                    yt_hbm, xg, xb, acc, ys, gsem, ssem, *, tm, sub):
    i = pl.program_id(0)
    j = pl.program_id(1)
    nb = pl.num_programs(0)
    nf = pl.num_programs(1)
    slot = i % 2
    other = 1 - slot

    def gather(idx_ref, r, dslot):
        src = pl.multiple_of(idx_ref[0, r] * sub, sub)
        dst = pl.multiple_of(r * sub, sub)
        return pltpu.make_async_copy(hs_hbm.at[pl.ds(src, sub)], xg.at[dslot, pl.ds(dst, sub)],
                                     gsem.at[dslot])

    def scatter(idx_ref, r, sslot):
        src = pl.multiple_of(r * sub, sub)
        dst = pl.multiple_of(idx_ref[0, r] * sub, sub)
        return pltpu.make_async_copy(ys.at[sslot, pl.ds(src, sub)], yt_hbm.at[pl.ds(dst, sub)],
                                     ssem.at[sslot])

    def wait_gather(dslot):
        pltpu.make_async_copy(hs_hbm.at[pl.ds(0, tm * sub)], xg.at[dslot], gsem.at[dslot]).wait()

    def wait_scatter(sslot):
        pltpu.make_async_copy(ys.at[sslot], yt_hbm.at[pl.ds(0, tm * sub)], ssem.at[sslot]).wait()

    @pl.when((i == 0) & (j == 0))
    def _():
        ys[1] = jnp.zeros(ys.shape[1:], ys.dtype)

        def start(r, c):
            gather(src0_ref, r, 0).start()
            return c

        lax.fori_loop(0, tm, start, 0)

    @pl.when(j == 0)
    def _():
        wait_gather(slot)
        for s in range(sub):
            xb[:, s * LANES:(s + 1) * LANES] = xg[slot, pl.ds(s, tm, stride=sub), :].astype(BF16)
        acc[...] = jnp.zeros_like(acc)

    @pl.when(i < be_ref[nb])
    def _():
        x = xb[...]
        a = jnp.dot(x, wg_ref[...], preferred_element_type=F32)
        u = jnp.dot(x, wu_ref[...], preferred_element_type=F32)
        act = (a * _sigmoid(a) * u).astype(BF16)
        acc[...] += jnp.dot(act, wd_ref[...], preferred_element_type=F32)

    @pl.when(j == 0)
    def _():
        for t in range(tm):
            gather(srcn_ref, t, other).start()

    @pl.when(j == nf - 1)
    def _():
        for t in range(tm):
            scatter(dstp_ref, t, other).start()

    @pl.when(j == nf - 1)
    def _():
        @pl.when(i >= 1)
        def _():
            wait_scatter(slot)

        for s in range(sub):
            ys[slot, pl.ds(s, tm, stride=sub), :] = acc[:, s * LANES:(s + 1) * LANES]

        @pl.when(i == nb - 1)
        def _():
            def start(r, c):
                scatter(dstl_ref, r, slot).start()
                return c

            lax.fori_loop(0, tm, start, 0)
            wait_scatter(slot)
            wait_scatter(other)
            wait_gather(other)


def _experts(hs, blk_e, src_tok, dst_row, n_slabs, wg, wu, wd, tm, tf):
    nb = src_tok.shape[0]
    dm, dff = wg.shape[1], wg.shape[2]
    sub = dm // LANES
    nf = dff // tf
    smem = lambda imap: pl.BlockSpec((None, 1, tm), imap, memory_space=pltpu.SMEM)
    grid_spec = pltpu.PrefetchScalarGridSpec(
        num_scalar_prefetch=1,
        grid=(nb, nf),
        in_specs=[smem(lambda i, j, be: (0, 0, 0)),
                  smem(lambda i, j, be: (jnp.minimum(i + 1, nb - 1), 0, 0)),
                  smem(lambda i, j, be: (i, 0, 0)),
                  smem(lambda i, j, be: (nb, 0, 0)),
                  pl.BlockSpec(memory_space=pl.ANY),
                  pl.BlockSpec((None, dm, tf), lambda i, j, be: (be[i], 0, j)),
                  pl.BlockSpec((None, dm, tf), lambda i, j, be: (be[i], 0, j)),
                  pl.BlockSpec((None, tf, dm), lambda i, j, be: (be[i], j, 0))],
        out_specs=pl.BlockSpec(memory_space=pl.ANY),
        scratch_shapes=[pltpu.VMEM((2, tm * sub, LANES), F32), pltpu.VMEM((tm, dm), BF16),
                        pltpu.VMEM((tm, dm), F32), pltpu.VMEM((2, tm * sub, LANES), F32),
                        pltpu.SemaphoreType.DMA((2,)), pltpu.SemaphoreType.DMA((2,))],
    )
    return pl.pallas_call(
        functools.partial(_experts_kernel, tm=tm, sub=sub),
        grid_spec=grid_spec,
        out_shape=jax.ShapeDtypeStruct((n_slabs * sub, LANES), F32),
        compiler_params=_cparams(("arbitrary", "arbitrary"), 48),
        name="experts",
    )(blk_e, src_tok, src_tok, dst_row, dst_row, hs, wg, wu, wd)


def _combine_kernel(y_ref, x_ref, gt_ref, p_ref, gf_ref, wpg_ref, wpp_ref, gp_ref, out_ref):
    tt, dm = x_ref.shape
    sub = dm // LANES
    gt = gt_ref[...]

    def rows(slot):
        return jnp.concatenate([y_ref[pl.ds(slot * sub + s, tt, stride=TOP_K * sub), :] for s in range(sub)],
                               axis=1)

    f = gt[:, 2:3] * rows(0) + gt[:, 3:4] * rows(1)
    _post_math(x_ref[...], f, p_ref, gf_ref, wpg_ref, wpp_ref, gp_ref, out_ref)


def _combine(yt, info, x2, p2, gf, wpg, wpp, gp, tt):
    n, dm = x2.shape
    row = lambda cols: pl.BlockSpec((tt, cols), lambda i: (i, 0))
    return pl.pallas_call(
        _combine_kernel,
        grid=(n // tt,),
        in_specs=[pl.BlockSpec((tt * TOP_K * (dm // LANES), LANES), lambda i: (i, 0)),
                  row(dm), row(LANES), row(p2.shape[1]), _resident((1, dm)),
                  _resident(wpg.shape), _resident(wpp.shape), _resident((1, dm))],
        out_specs=row(dm),
        out_shape=jax.ShapeDtypeStruct((n, dm), F32),
        compiler_params=_cparams(("parallel",), 48),
        name="combine",
    )(yt, x2, info, p2, gf, wpg, wpp, gp)


def _moe(x2, p2, gain, wr, br, wg, wu, wd, gf, wpg, wpp, gp, *, tm_r, tt, tm_e, tf_e):
    n, _ = x2.shape
    hs, info = _router(x2, gain, wr, br, tm_r)
    n_asg = n * TOP_K
    top_e = info[:, 0:TOP_K].astype(jnp.int32)
    e_flat = top_e.reshape(n_asg)
    onehot = (e_flat[:, None] == jnp.arange(N_EXPERTS, dtype=jnp.int32)[None, :]).astype(jnp.int32)
    csum = jnp.cumsum(onehot, axis=0)
    rank = jnp.sum((csum - onehot) * onehot, axis=1)
    counts = csum[-1]
    padded = ((counts + tm_e - 1) // tm_e) * tm_e
    pend = jnp.cumsum(padded)
    pstart = pend - padded
    dest = (pstart[e_flat] + rank).astype(jnp.int32)
    n_blocks = -(-n_asg // tm_e) + N_EXPERTS
    n_rows = n_blocks * tm_e
    blk_e = jnp.clip(jnp.searchsorted(pend, jnp.arange(n_blocks, dtype=jnp.int32) * tm_e, side="right"),
                     0, N_EXPERTS - 1).astype(jnp.int32)
    blk_e = jnp.concatenate([blk_e, (pend[-1:] // tm_e).astype(jnp.int32)])
    asg = jnp.full((n_rows,), -1, jnp.int32).at[dest].set(jnp.arange(n_asg, dtype=jnp.int32),
                                                          unique_indices=True)
    is_pad = asg < 0
    pad_rank = jnp.cumsum(is_pad.astype(jnp.int32)) - 1
    src_tok = jnp.where(is_pad, 0, asg // TOP_K).reshape(n_blocks, 1, tm_e)
    dst_row = jnp.where(is_pad, n_asg + pad_rank, asg)
    spare = n_asg + (n_rows - n_asg) + jnp.arange(tm_e, dtype=jnp.int32)
    dst_row = jnp.concatenate([spare, dst_row]).reshape(n_blocks + 1, 1, tm_e)
    n_slabs = n_asg + (n_rows - n_asg) + tm_e
    yt = _experts(hs, blk_e, src_tok, dst_row, n_slabs, wg, wu, wd, tm_e, tf_e)
    return _combine(yt, info, x2, p2, gf, wpg, wpp, gp, tt)


def _tables(seq):
    pos = np.arange(seq, dtype=np.int32)

    def cs(p, half):
        inv = np.power(np.float32(ROPE_THETA), -np.arange(half, dtype=np.float32) / np.float32(half))
        ang = p.astype(np.float32)[:, None] * inv[None, :].astype(np.float32)
        return jnp.asarray(np.cos(ang), F32), jnp.asarray(np.sin(ang), F32)

    ones = lambda w: jnp.ones((seq, w), F32)
    zeros = lambda w: jnp.zeros((seq, w), F32)
    (c, s), (cr, sr), (cc, sc), (ca, sa) = (
        cs(pos, HEAD_DIM // 2), cs(pos // GRID_W, HEAD_DIM // 4), cs(pos % GRID_W, HEAD_DIM // 4),
        cs(pos, A_ROPE // 2))
    full = jnp.stack([jnp.tile(jnp.concatenate([c, c], 1), (1, 2)),
                      jnp.tile(jnp.concatenate([-s, s], 1), (1, 2))])
    axial = jnp.stack([jnp.tile(jnp.concatenate([cr, cc, cr, cc], 1), (1, 2)),
                       jnp.tile(jnp.concatenate([-sr, -sc, sr, sc], 1), (1, 2))])
    slot_c =jnp.concatenate([ca, ones(16), ca, ones(16)], 1)
    slot_s = jnp.concatenate([-sa, zeros(16), sa, zeros(16)], 1)
    a_k = jnp.stack([jnp.concatenate([slot_c, ones(HALF)], 1), jnp.concatenate([slot_s, zeros(HALF)], 1)])
    a_q = jnp.stack([jnp.concatenate([ones(HALF), slot_c], 1), jnp.concatenate([zeros(HALF), slot_s], 1)])
    return {"full": full, "axial": axial, "a_k": a_k, "a_q": a_q}


_AXIAL_PERM = tuple(list(range(0, 16)) + list(range(32, 48)) + list(range(16, 32)) + list(range(48, 64)))


def _dup_heads(w, n_heads, perm=None):
    rows = w.shape[0]
    w = w.reshape(rows, n_heads, HEAD_DIM)
    if perm is not None:
        w = w[:, :, perm]
    return jnp.stack([w, w], axis=2).reshape(rows, n_heads * 2 * HEAD_DIM)


def _assemble_w_in(w):
    dm = w.shape[0]
    perm = jnp.array(_AXIAL_PERM, jnp.int32)
    a, b, c, d = w[:, 0:416], w[:, 416:928], w[:, 928:3232], w[:, 3232:3744]
    z = lambda n: jnp.zeros((dm, n), w.dtype)
    kr = a[:, 384:416]
    a_seg = jnp.concatenate([a[:, 0:384], kr[:, 0:16], z(16), kr[:, 16:32], z(16), z(HALF)], axis=1)
    bq = b[:, 0:256].reshape(dm, 4, HEAD_DIM)[:, :, perm].reshape(dm, 256)
    b_seg = jnp.concatenate([bq, _dup_heads(b[:, 256:384], 2, perm)], axis=1)
    d_seg = jnp.concatenate([d[:, 0:256], _dup_heads(d[:, 256:384], 2), _dup_heads(d[:, 384:512], 2)], axis=1)
    w_vbt = _dup_heads(b[:, 384:512], 2).T.astype(BF16)
    return jnp.concatenate([a_seg, b_seg, c, d_seg], axis=1).astype(BF16), w_vbt


def _assemble_a(w_uq, w_ukv):
    zq = lambda n: jnp.zeros((w_uq.shape[0], n), w_uq.dtype)
    zk = lambda n: jnp.zeros((w_ukv.shape[0], n), w_ukv.dtype)
    dq = A_NOPE + A_ROPE
    q_cols, k_cols, v_cols = [], [], []
    for hh in range(A_HEADS):
        q = w_uq[:, hh * dq:(hh + 1) * dq]
        q_cols += [q[:, 0:A_NOPE], q[:, A_NOPE:A_NOPE + 16], zq(16), q[:, A_NOPE + 16:dq], zq(16)]
        kv = w_ukv[:, hh * (A_NOPE + A_V):(hh + 1) * (A_NOPE + A_V)]
        k_cols += [kv[:, 0:A_NOPE], zk(HALF)]
        v_cols += [kv[:, A_NOPE:A_NOPE + A_V]]
    cat = lambda cols: jnp.concatenate(cols, axis=1).astype(BF16)
    return cat(q_cols), cat(k_cols), cat(v_cols).T


def _gain_pair(g, perm=None):
    if perm is not None:
        g = g[jnp.array(perm, jnp.int32)]
    return jnp.tile(g, 2).reshape(1, LANES).astype(F32)


def kernel(x, p, w_in, a_qa_g, a_kva_g, a_w_uq, a_w_ukv, b_q_g, b_k_g, d_sink, w_branch, w_out,
           mix_pre_g, mix_post_g, ffn_pre_g, ffn_post_g, ffn_w_gate, ffn_w_up, ffn_w_down,
           router_w, router_b, moe_w_gate, moe_w_up, moe_w_down, ple_w_proj, ple_w_gate, ple_post_g):
    bsz, seq, dm = x.shape
    depth = w_in.shape[0]
    n = bsz * seq
    tm = min(512, seq)
    tq_flash = min(512, seq)
    tk_flash = min(1024, seq // 4)
    tabs = _tables(seq)
    row = lambda g: g.reshape(1, -1).astype(F32)
    x2 = x.reshape(n, dm)

    for i in range(depth):
        w_in16 = w_in[i].astype(BF16)
        w_all, wvb = _assemble_w_in(w_in16)
        wuq, wk, wv = _assemble_a(a_w_uq[i], a_w_ukv[i])
        z, qa, ka, vat, vbt, zc1, zc2 = _in_proj(
            x2, row(mix_pre_g[i]), w_all, tabs,
            _gain_pair(b_q_g[i], _AXIAL_PERM), _gain_pair(b_k_g[i], _AXIAL_PERM),
            row(a_qa_g[i]), row(a_kva_g[i]), wuq, wk, wv, wvb, seq, tm)
        z3 = z.reshape(bsz, seq, ZMAIN_COLS)
        o_a = _flash(qa, ka, vat, q_col0=0, k_col0=0, v_row0=0, seq=seq, packed=False,
                     tq=tq_flash, tk=tk_flash)
        o_b = _flash(z, z, vbt, q_col0=ZB, k_col0=ZB + 256, v_row0=0, seq=seq, packed=True,
                     tq=tq_flash, tk=tk_flash)
        o_c, l_c = [], []
        for (win, dil), (src, col0, rc) in zip(C_PATTERNS, ((z3, ZC, ZMAIN_COLS), (zc1, 0, 768), (zc2, 0, 768))):
            og, lg = _banded(src, src, src, q_col0=col0, k_col0=col0 + 256, v_col0=col0 + 512,
                             row_cols=rc, dil=dil, seq=seq, hw=win // (2 * dil), tq=min(512, seq // dil),
                             want_lse=True, out_dtype=F32)
            o_c.append(og)
            l_c.append(lg)
        o_c[0] = o_c[0].reshape(n, 2 * LANES)
        l_c[0] = l_c[0].reshape(n, 2 * LANES)
        (o_d,) = _banded(z3, z3, z3, q_col0=ZD, k_col0=ZD + 256, v_col0=ZD + 512,
                         row_cols=ZMAIN_COLS, dil=1, seq=seq, hw=D_HALF_WINDOW, tq=min(512, seq),
                         sink=d_sink[i].astype(F32))
        o_d = o_d.reshape(n, 2 * LANES)
        wg_gate = w_in16[:, 3744:]
        x2 = _merge(x2, o_a, o_b, o_c, l_c, o_d, row(mix_pre_g[i]), wg_gate,
                    w_branch[i].astype(BF16), w_out[i].astype(BF16), row(mix_post_g[i]), seq, tm)

        p2 = p[i].reshape(n, -1)
        wpg = ple_w_gate[i].astype(BF16)
        wpp = ple_w_proj[i].astype(BF16)
        j = i // 2
        if i % 2 == 0:
            x2 = _ffn(x2, row(ffn_pre_g[i]), ffn_w_gate[j].astype(BF16), ffn_w_up[j].astype(BF16),
                      ffn_w_down[j].astype(BF16), p2, row(ffn_post_g[i]), wpg, wpp, row(ple_post_g[i]),
                      min(1024, n), 512)
        else:
            wr32 = jnp.zeros((dm, LANES), F32).at[:, :N_EXPERTS].set(router_w[j].astype(F32))
            wr_hi = wr32.astype(BF16)
            wr = jnp.stack([wr_hi, (wr32 - wr_hi.astype(F32)).astype(BF16)])
            br = jnp.full((1, LANES), NEG_INF, F32).at[0, :N_EXPERTS].set(router_b[j].astype(F32))
            x2 = _moe(x2, p2, row(ffn_pre_g[i]), wr, br, moe_w_gate[j].astype(BF16),
                      moe_w_up[j].astype(BF16), moe_w_down[j].astype(BF16), row(ffn_post_g[i]),
                      wpg, wpp, row(ple_post_g[i]), tm_r=tm, tt=min(256, n), tm_e=512, tf_e=1792)
    return x2.reshape(bsz, seq, dm)
```

```python
import functools

import jax
import jax.numpy as jnp
import numpy as np
from jax import lax
from jax.experimental import pallas as pl
from jax.experimental.pallas import tpu as pltpu

F32 = jnp.float32
BF16 = jnp.bfloat16

GRID_W = 64
HEAD_DIM = 64
ROPE_THETA = 10000.0
NORM_EPS = 1e-6
NEG_INF = -1e30
A_HEADS = 4
A_Q_RANK = 256
A_KV_RANK = 128
A_NOPE = 64
A_ROPE = 32
A_V = 64
C_PATTERNS = ((128, 1), (512, 4), (2048, 16))
D_HALF_WINDOW = 128
N_BRANCHES = 4
N_EXPERTS = 8
TOP_K = 2

LANES = 128
HALF = 64
VMEM_MB = 1024 * 1024
LOG2E = 1.4426950408889634
BAND_SUB = 128

PAIR = 2 * LANES
QKV_SEG = 3 * PAIR
SRC_A = A_Q_RANK + A_KV_RANK + A_ROPE
SRC_B = SRC_A + PAIR + 2 * LANES
SRC_C = SRC_B + len(C_PATTERNS) * QKV_SEG
SRC_D = SRC_C + PAIR + 2 * LANES

ZB = 0
ZC = 512
ZD = 512 + QKV_SEG
ZMAIN_COLS = ZD + QKV_SEG
A_SEG = 512
B_SEG = 512


def _cparams(sem, vmem_mb):
    return pltpu.CompilerParams(dimension_semantics=sem, vmem_limit_bytes=vmem_mb * VMEM_MB)


def _resident(shape):
    nd = len(shape)
    return pl.BlockSpec(shape, lambda *_: (0,) * nd, pipeline_mode=pl.Buffered(1))


def _rms(xf, g):
    return xf * lax.rsqrt(jnp.mean(xf * xf, axis=-1, keepdims=True) + NORM_EPS) * g


def _sigmoid(x):
    return 1.0 / (1.0 + jnp.exp(-x))


def _swap32(a):
    lane = lax.broadcasted_iota(jnp.int32, a.shape, 1)
    fwd = pltpu.roll(a, LANES - 32, 1)
    bwd = pltpu.roll(a, 32, 1)
    return jnp.where((lane & 32) == 0, fwd, bwd)


def _rope(a, cos, sin):
    outs = []
    for c in range(a.shape[1] // LANES):
        ch = a[:, c * LANES:(c + 1) * LANES]
        outs.append(ch * cos + _swap32(ch) * sin)
    return outs[0] if len(outs) == 1 else jnp.concatenate(outs, axis=1)


def _head_norm(a, g, bd):
    outs = []
    for c in range(a.shape[1] // LANES):
        ch = a[:, c * LANES:(c + 1) * LANES]
        sq = ch * ch
        hi = sq.astype(BF16)
        lo = (sq - hi.astype(F32)).astype(BF16)
        ms = (jnp.dot(hi, bd, preferred_element_type=F32)
              + jnp.dot(lo, bd, preferred_element_type=F32))
        outs.append(ch * lax.rsqrt(ms + NORM_EPS) * g)
    return outs[0] if len(outs) == 1 else jnp.concatenate(outs, axis=1)


def _in_proj_kernel(x_ref, g_ref, w_ref, tabf_ref, tabx_ref, taba_ref, tabq_ref,
                    bqg_ref, bkg_ref, aqg_ref, akvg_ref, wuq_ref, wk_ref, wv_ref, wvb_ref,
                    z_ref, qa_ref, ka_ref, vat_ref, vbt_ref, zc1_ref, zc2_ref, cls_scr):
    h = _rms(x_ref[...], g_ref[...]).astype(BF16)
    cf, sf = tabf_ref[0], tabf_ref[1]
    cx, sx = tabx_ref[0], tabx_ref[1]
    q_scale = HEAD_DIM ** -0.5

    acc = jnp.dot(h, w_ref[:, 0:A_SEG], preferred_element_type=F32)
    nq = _rms(acc[:, 0:A_Q_RANK], aqg_ref[...]).astype(BF16)
    nkv = _rms(acc[:, A_Q_RANK:A_Q_RANK + A_KV_RANK], akvg_ref[...]).astype(BF16)
    kr = _rope(acc[:, 384:512], taba_ref[0], taba_ref[1]).astype(BF16)
    qa = jnp.dot(nq, wuq_ref[...], preferred_element_type=F32)
    qa = _rope(qa, tabq_ref[0], tabq_ref[1]) * ((A_NOPE + A_ROPE) ** -0.5 * LOG2E)
    qa_ref[...] = qa.astype(qa_ref.dtype)
    r = lax.broadcasted_iota(jnp.int32, (LANES, A_HEADS * LANES), 0)
    c = lax.broadcasted_iota(jnp.int32, (LANES, A_HEADS * LANES), 1)
    place = jnp.where((r < HALF) & ((c & (LANES - 1)) == r + HALF), 1.0, 0.0).astype(BF16)
    ka = (jnp.dot(nkv, wk_ref[...], preferred_element_type=F32)
          + jnp.dot(kr, place, preferred_element_type=F32))
    ka_ref[...] = ka.astype(ka_ref.dtype)
    nt_dims = (((1,), (1,)), ((), ()))
    vat_ref[...] = lax.dot_general(wv_ref[...], nkv, nt_dims, preferred_element_type=F32).astype(vat_ref.dtype)
    vbt_ref[...] = lax.dot_general(wvb_ref[...], h, nt_dims, preferred_element_type=F32).astype(vbt_ref.dtype)

    rr = lax.broadcasted_iota(jnp.int32, (LANES, LANES), 0)
    cc = lax.broadcasted_iota(jnp.int32, (LANES, LANES), 1)
    bd = jnp.where((rr >> 6) == (cc >> 6), 1.0 / HEAD_DIM, 0.0).astype(BF16)
    acc = jnp.dot(h, w_ref[:, A_SEG:A_SEG + B_SEG], preferred_element_type=F32)
    q = _rope(_head_norm(acc[:, 0:256], bqg_ref[...], bd), cx, sx) * (q_scale * LOG2E)
    k = _rope(_head_norm(acc[:, 256:512], bkg_ref[...], bd), cx, sx)
    z_ref[:, ZB:ZB + 256] = q.astype(z_ref.dtype)
    z_ref[:, ZB + 256:ZB + 512] = k.astype(z_ref.dtype)

    tm = h.shape[0]
    for widx, zoff, cls_ref, dil in ((0, ZC, None, 1), (1, 0, zc1_ref, C_PATTERNS[1][1]),
                                     (2, 0, zc2_ref, C_PATTERNS[2][1]), (3, ZD, None, 1)):
        base = A_SEG + B_SEG + widx * QKV_SEG
        acc = jnp.dot(h, w_ref[:, base:base + QKV_SEG], preferred_element_type=F32)
        q = _rope(acc[:, 0:256], cf, sf) * q_scale
        k = _rope(acc[:, 256:512], cf, sf)
        if cls_ref is None:
            z_ref[:, zoff:zoff + 256] = q.astype(z_ref.dtype)
            z_ref[:, zoff + 256:zoff + 512] = k.astype(z_ref.dtype)
            z_ref[:, zoff + 512:zoff + QKV_SEG] = acc[:, 512:QKV_SEG].astype(z_ref.dtype)
        else:
            qkv = (q[:, 0:LANES], q[:, LANES:], k[:, 0:LANES], k[:, LANES:],
                   acc[:, 512:512 + LANES], acc[:, 512 + LANES:QKV_SEG])
            for ch, val in enumerate(qkv):
                cls_scr[ch] = val
            for c in range(dil):
                for ch in range(len(qkv)):
                    col = c * QKV_SEG + ch * LANES
                    cls_ref[:, col:col + LANES] = cls_scr[ch, pl.ds(c, tm // dil, stride=dil), :].astype(cls_ref.dtype)


def _in_proj(x2, gain, w, tabs, bqg, bkg, aqg, akvg, wuq, wk, wv, wvb, seq, tm):
    n, dm = x2.shape
    nt = seq // tm
    bsz = n // seq
    d1, d2 = C_PATTERNS[1][1], C_PATTERNS[2][1]
    tab_spec = pl.BlockSpec((2, tm, LANES), lambda i: (0, i % nt, 0))
    row = lambda cols: pl.BlockSpec((tm, cols), lambda i: (i, 0))
    vt_spec = pl.BlockSpec((None, 2 * LANES, tm), lambda i: (i // nt, 0, i % nt))
    cls_spec = lambda d: pl.BlockSpec((None, tm // d, d * QKV_SEG), lambda i: (i // nt, i % nt, 0))
    return pl.pallas_call(
        _in_proj_kernel,
        grid=(n // tm,),
        in_specs=[row(dm), _resident((1, dm)), _resident(w.shape),
                  tab_spec, tab_spec, tab_spec, tab_spec,
                  _resident((1, LANES)), _resident((1, LANES)),
                  _resident((1, A_Q_RANK)), _resident((1, A_KV_RANK)),
                  _resident(wuq.shape), _resident(wk.shape), _resident(wv.shape), _resident(wvb.shape)],
        out_specs=[row(ZMAIN_COLS), row(A_HEADS * LANES), row(A_HEADS * LANES),
                   vt_spec, vt_spec, cls_spec(d1), cls_spec(d2)],
        out_shape=[jax.ShapeDtypeStruct((n, ZMAIN_COLS), BF16),
                   jax.ShapeDtypeStruct((n, A_HEADS * LANES), BF16),
                   jax.ShapeDtypeStruct((n, A_HEADS * LANES), BF16),
                   jax.ShapeDtypeStruct((bsz, 2 * LANES, seq), BF16),
                   jax.ShapeDtypeStruct((bsz, 2 * LANES, seq), BF16),
                   jax.ShapeDtypeStruct((bsz, seq // d1, d1 * QKV_SEG), BF16),
                   jax.ShapeDtypeStruct((bsz, seq // d2, d2 * QKV_SEG), BF16)],
        scratch_shapes=[pltpu.VMEM((QKV_SEG // LANES, tm, LANES), F32)],
        compiler_params=_cparams(("parallel",), 48),
        name="in_proj",
    )(x2, gain, w, tabs["full"], tabs["axial"], tabs["a_k"], tabs["a_q"],
      bqg, bkg, aqg, akvg, wuq, wk, wv, wvb)


FLASH_SAFE_EXP = 64.0


def _head_queries(q_ref, packed):
    lane = lax.broadcasted_iota(jnp.int32, (1, LANES), 1)
    if packed:
        return [jnp.where((lane < HALF) == (r == 0), q_ref[...], 0) for r in range(2)], [0, 0]
    return [q_ref[:, r * LANES:(r + 1) * LANES] for r in range(2)], [0, LANES]


def _flash_single_pass(q_ref, k_ref, vt_ref, s_a, s_b, *, tk, packed):
    tq = q_ref.shape[0]
    nk = k_ref.shape[0] // tk
    qs, kcs = _head_queries(q_ref, packed)
    qts = [q.astype(F32).T.astype(BF16) for q in qs]
    row = lax.broadcasted_iota(jnp.int32, (LANES, tq), 0)
    klane = lax.broadcasted_iota(jnp.int32, (tk, LANES), 1)
    k_one = jnp.where(klane == 0, 1.0, 0.0).astype(BF16)
    v_one = jnp.ones((16, tk), BF16)

    def scores(c, refs, dst):
        ks = pl.multiple_of(c * tk, tk)
        for r in range(2):
            k = jnp.concatenate([k_ref[pl.ds(ks, tk), kcs[r]:kcs[r] + LANES], k_one], axis=1)
            bias = jnp.where(row == 0, -refs[r], 0.0).astype(BF16)
            qt = jnp.concatenate([qts[r], bias], axis=0)
            dst[r] = jnp.dot(k, qt, preferred_element_type=F32)

    def consume(c, src, refs, sts, first):
        ks = pl.multiple_of(c * tk, tk)
        new = []
        for r in range(2):
            big, base, l, acc, hi, lo = sts[r]
            s = src[r]
            cmax = jnp.max(s, axis=0, keepdims=True)
            p = jnp.exp2(s).astype(BF16)
            lhs = jnp.concatenate([vt_ref[r * HALF:(r + 1) * HALF, pl.ds(ks, tk)], v_one], axis=0)
            pv = jnp.dot(lhs, p, preferred_element_type=F32)
            alpha = jnp.exp2(base - refs[r])
            l = l * alpha + pv[HALF:HALF + 1]
            acc = acc * alpha + pv[0:HALF]
            big = jnp.maximum(big, refs[r] + cmax)
            hi = jnp.maximum(hi, cmax)
            if first:
                lo = jnp.minimum(lo, cmax)
            new.append((big, refs[r], l, acc, hi, lo))
        return tuple(new)

    def ref_of(sts):
        return [st[0].astype(BF16).astype(F32) for st in sts]

    zero = jnp.zeros((1, tq), F32)
    sts = tuple((jnp.full((1, tq), NEG_INF, F32), zero, zero, jnp.zeros((HALF, tq), F32),
                 jnp.full((1, tq), NEG_INF, F32), jnp.full((1, tq), -NEG_INF, F32)) for _ in range(2))
    zeros2 = [zero, zero]
    scores(0, zeros2, s_a)
    scores(1, zeros2, s_b)
    sts = consume(0, s_a, zeros2, sts, True)
    ra = ref_of(sts)
    scores(2, ra, s_a)
    sts = consume(1, s_b, zeros2, sts, True)

    def pair(jj, carry):
        sts, ra = carry
        c0 = 2 * jj
        rb = ref_of(sts)
        scores(c0 + 1, rb, s_b)
        sts = consume(c0, s_a, ra, sts, False)
        ra = ref_of(sts)
        scores(c0 + 2, ra, s_a)
        sts = consume(c0 + 1, s_b, rb, sts, False)
        return sts, ra

    sts, ra = lax.fori_loop(1, nk // 2 - 1, pair, (sts, ra))
    rb = ref_of(sts)
    scores(nk - 1, rb, s_b)
    sts = consume(nk - 2, s_a, ra, sts, False)
    sts = consume(nk - 1, s_b, rb, sts, False)
    outs, bad = [], None
    for r in range(2):
        _, _, l, acc, hi, lo = sts[r]
        outs.append(acc / l)
        b = (hi > FLASH_SAFE_EXP) | (lo < -FLASH_SAFE_EXP) | jnp.logical_not(l > 2.0 ** -FLASH_SAFE_EXP)
        bad = b if bad is None else (bad | b)
    return jnp.concatenate(outs, axis=0), bad


def _flash_kernel(q_ref, k_ref, vt_ref, o_ref, s_a, s_b, *, tk, packed):
    out, bad = _flash_single_pass(q_ref, k_ref, vt_ref, s_a, s_b, tk=tk, packed=packed)
    o_ref[...] = out.T.astype(o_ref.dtype)

    @pl.when(jnp.max(jnp.where(bad, 1.0, 0.0)) > 0.0)
    def _():
        o_ref[...] = _flash_two_pass(q_ref, k_ref, vt_ref, s_a, s_b, tk=tk, packed=packed).T.astype(o_ref.dtype)


def _flash_two_pass(q_ref, k_ref, vt_ref, s_a, s_b, *, tk, packed):
    tq = q_ref.shape[0]
    nk = k_ref.shape[0] // tk
    qs, kcs = _head_queries(q_ref, packed)

    def scores(kk, dst):
        ks = pl.multiple_of(kk * tk, tk)
        for r in range(2):
            k = k_ref[pl.ds(ks, tk), kcs[r]:kcs[r] + LANES]
            dst[r] = lax.dot_general(k, qs[r], (((1,), (1,)), ((), ())), preferred_element_type=F32)

    def consume(kk, src, carry):
        ks = pl.multiple_of(kk * tk, tk)
        new = []
        for r in range(2):
            m, l, acc = carry[r]
            s = src[r]
            m_new = jnp.maximum(m, jnp.max(s, axis=0, keepdims=True))
            alpha = jnp.exp2(m - m_new)
            p = jnp.exp2(s - m_new)
            l = alpha * l + jnp.sum(p, axis=0, keepdims=True)
            vt = vt_ref[r * HALF:(r + 1) * HALF, pl.ds(ks, tk)]
            acc = alpha * acc + jnp.dot(vt, p.astype(BF16), preferred_element_type=F32)
            new.append((m_new, l, acc))
        return tuple(new)

    def pair(jj, carry):
        c0 = 2 * jj
        scores(c0 + 1, s_b)
        carry = consume(c0, s_a, carry)
        scores(c0 + 2, s_a)
        return consume(c0 + 1, s_b, carry)

    init = tuple((jnp.full((1, tq), NEG_INF, F32), jnp.zeros((1, tq), F32), jnp.zeros((HALF, tq), F32))
                 for _ in range(2))
    scores(0, s_a)
    carry = lax.fori_loop(0, nk // 2 - 1, pair, init)
    scores(nk - 1, s_b)
    carry = consume(nk - 2, s_a, carry)
    carry = consume(nk - 1, s_b, carry)
    return jnp.concatenate([acc / l for (_, l, acc) in carry], axis=0)


def _flash(q, k, vt, *, q_col0, k_col0, v_row0, seq, packed, tq, tk):
    n = q.shape[0]
    bsz = n // seq
    qw = LANES if packed else 2 * LANES
    nq = seq // tq
    assert seq % (2 * tk) == 0 and seq // tk >= 4, "the chunk pipeline needs an even number (>= 4) of key chunks"
    k3 = k.reshape(bsz, seq, k.shape[1])
    qb, kb, vb = q_col0 // qw, k_col0 // qw, v_row0 // LANES
    return pl.pallas_call(
        functools.partial(_flash_kernel, tk=tk, packed=packed),
        grid=(bsz, 2, nq),
        in_specs=[pl.BlockSpec((tq, qw), lambda b, j, i: (b * nq + i, qb + j)),
                  pl.BlockSpec((None, seq, qw), lambda b, j, i: (b, 0, kb + j)),
                  pl.BlockSpec((None, LANES, seq), lambda b, j, i: (b, vb + j, 0))],
        out_specs=pl.BlockSpec((tq, LANES), lambda b, j, i: (b * nq + i, j)),
        out_shape=jax.ShapeDtypeStruct((n, 2 * LANES), BF16),
        scratch_shapes=[pltpu.VMEM((2, tk, tq), F32), pltpu.VMEM((2, tk, tq), F32)],
        compiler_params=_cparams(("parallel", "parallel", "parallel"), 48),
        name="flash_packed" if packed else "flash_slots",
    )(q, k3, vt)


def _banded_kernel(*refs, hw, has_sink, want_lse):
    if has_sink:
        sink_ref, q_ref, k_ref, v_ref = refs[:4]
        outs = refs[4:]
    else:
        q_ref, k_ref, v_ref = refs[:3]
        outs = refs[3:]
    o_ref = outs[0]
    tq = q_ref.shape[0]
    length = k_ref.shape[0]
    sb = min(BAND_SUB, tq)
    win = min(sb + 2 * hw, length)
    i = pl.program_id(1)
    lane = lax.broadcasted_iota(jnp.int32, (1, LANES), 1)
    first = lane < HALF
    cols = [slice(j * LANES, (j + 1) * LANES) for j in range(2)]
    subs = list(range(tq // sb))
    kss, valids = [], []
    for u in subs:
        q0 = i * tq + u * sb
        ks = pl.multiple_of(jnp.clip(q0 - hw, 0, length - win), HALF)
        qpos = q0 + lax.broadcasted_iota(jnp.int32, (sb, win), 0)
        kpos = ks + lax.broadcasted_iota(jnp.int32, (sb, win), 1)
        kss.append(ks)
        valids.append(jnp.abs(qpos - kpos) <= hw)
    chains = [(u, j, r) for u in subs for j in range(2) for r in range(2)]
    kws = {(u, j): k_ref[pl.ds(kss[u], win), cols[j]] for u in subs for j in range(2)}
    vws = {(u, j): v_ref[pl.ds(kss[u], win), cols[j]] for u in subs for j in range(2)}
    ss = []
    for u, j, r in chains:
        q = jnp.where(first == (r == 0), q_ref[u * sb:(u + 1) * sb, cols[j]], 0)
        ss.append(lax.dot_general(q, kws[u, j], (((1,), (1,)), ((), ())), preferred_element_type=F32))
    ss = [jnp.where(valids[u], s, NEG_INF) for s, (u, j, r) in zip(ss, chains)]
    ms = [jnp.max(s, axis=-1, keepdims=True) for s in ss]
    if has_sink:
        sinks = [sink_ref[2 * j + r] for u, j, r in chains]
        ms = [jnp.maximum(m, sk) for m, sk in zip(ms, sinks)]
    es = [jnp.exp(s - m) for s, m in zip(ss, ms)]
    ls = [jnp.sum(e, axis=-1, keepdims=True) for e in es]
    if has_sink:
        ls = [l + jnp.exp(sk - m) for l, sk, m in zip(ls, sinks, ms)]
    os_ = [jnp.dot(e.astype(BF16), vws[u, j], preferred_element_type=F32) / l
           for e, l, (u, j, r) in zip(es, ls, chains)]
    for n in range(0, len(chains), 2):
        u, j, _ = chains[n]
        rows = slice(u * sb, (u + 1) * sb)
        o_ref[rows, cols[j]] = jnp.where(first, os_[n], os_[n + 1]).astype(o_ref.dtype)
        if want_lse:
            lses = [jnp.broadcast_to(ms[n + r] + jnp.log(ls[n + r]), (sb, LANES)) for r in range(2)]
            outs[1][rows, cols[j]] = jnp.where(first, lses[0], lses[1])


def _banded(q, k, v, *, q_col0, k_col0, v_col0, row_cols, dil, seq, hw, tq, sink=None,
            want_lse=False, out_dtype=BF16):
    bsz, ls, _ = q.shape
    nq = ls // tq
    pw = 2 * LANES
    rb = row_cols // pw
    qb, kb, vb = q_col0 // pw, k_col0 // pw, v_col0 // pw
    in_specs = [pl.BlockSpec((None, tq, pw), lambda bc, i: (bc // dil, i, (bc % dil) * rb + qb)),
                pl.BlockSpec((None, ls, pw), lambda bc, i: (bc // dil, 0, (bc % dil) * rb + kb)),
                pl.BlockSpec((None, ls, pw), lambda bc, i: (bc // dil, 0, (bc % dil) * rb + vb))]
    args = [q, k, v]
    if sink is not None:
        in_specs = [pl.BlockSpec(memory_space=pltpu.SMEM)] + in_specs
        args = [sink] + args
    o_spec = pl.BlockSpec((None, tq, pw), lambda bc, i: (bc // dil, i, bc % dil))
    out_specs = [o_spec]
    out_shape = [jax.ShapeDtypeStruct((bsz, ls, dil * pw), out_dtype)]
    if want_lse:
        out_specs.append(o_spec)
        out_shape.append(jax.ShapeDtypeStruct((bsz, ls, dil * pw), F32))
    res = pl.pallas_call(
        functools.partial(_banded_kernel, hw=hw, has_sink=sink is not None, want_lse=want_lse),
        grid=(bsz * dil, nq),
        in_specs=in_specs,
        out_specs=out_specs,
        out_shape=out_shape,
        compiler_params=_cparams(("parallel", "parallel"), 48),
        name="banded_d%d" % dil,
    )(*args)
    return res


def _merge_kernel(x_ref, oa_ref, ob_ref, oc0_ref, oc1_ref, oc2_ref, l0_ref, l1_ref, l2_ref,
                  od_ref, gpre_ref, wg_ref, wb_ref, wo_ref, gpost_ref, out_ref, tok_scr):
    xf = x_ref[...]
    tm, dm = xf.shape
    h = _rms(xf, gpre_ref[...]).astype(BF16)
    pw = 2 * LANES
    toks = []
    for n, src in enumerate((oc1_ref, l1_ref, oc2_ref, l2_ref)):
        dil = src.shape[1] // pw
        for c in range(dil):
            for hp in range(2):
                col = c * pw + hp * LANES
                tok_scr[2 * n + hp, pl.ds(c, tm // dil, stride=dil), :] = src[:, col:col + LANES]
        toks.append(jnp.concatenate([tok_scr[2 * n], tok_scr[2 * n + 1]], axis=1))
    oc1, l1, oc2, l2 = toks
    l0 = l0_ref[...]
    mx = jnp.maximum(jnp.maximum(l0, l1), l2)
    w0, w1, w2 = jnp.exp(l0 - mx), jnp.exp(l1 - mx), jnp.exp(l2 - mx)
    oc = (w0 * oc0_ref[...] + w1 * oc1 + w2 * oc2) / (w0 + w1 + w2)
    branches = (oa_ref[...], ob_ref[...], oc.astype(BF16), od_ref[...])
    projected = [jnp.dot(o, wb_ref[n], preferred_element_type=F32) for n, o in enumerate(branches)]
    merged = None
    for n in range(N_BRANCHES):
        z = jnp.dot(h, wg_ref[:, n * dm:(n + 1) * dm], preferred_element_type=F32)
        term = projected[n] / (1.0 + jnp.exp(-z))
        merged = term if merged is None else merged + term
    y = jnp.dot(merged.astype(BF16), wo_ref[...], preferred_element_type=F32)
    out_ref[...] = xf + _rms(y, gpost_ref[...])


def _merge(x2, oa, ob, oc, lc, od, gpre, wg, wb, wo, gpost, seq, tm):
    n, dm = x2.shape
    nt = seq // tm
    row = lambda cols: pl.BlockSpec((tm, cols), lambda i: (i, 0))
    bw = 2 * LANES
    cls = lambda a: pl.BlockSpec((None, tm // (a.shape[2] // bw), a.shape[2]), lambda i: (i // nt, i % nt, 0))
    return pl.pallas_call(
        _merge_kernel,
        grid=(n // tm,),
        in_specs=[row(dm), row(bw), row(bw), row(bw), cls(oc[1]), cls(oc[2]), row(bw), cls(lc[1]), cls(lc[2]),
                  row(bw), _resident((1, dm)), _resident(wg.shape), _resident(wb.shape),
                  _resident(wo.shape), _resident((1, dm))],
        out_specs=row(dm),
        out_shape=jax.ShapeDtypeStruct((n, dm), F32),
        scratch_shapes=[pltpu.VMEM((8, tm, LANES), F32)],
        compiler_params=_cparams(("parallel",), 56),
        name="merge",
    )(x2, oa, ob, oc[0], oc[1], oc[2], lc[0], lc[1], lc[2], od, gpre, wg, wb, wo, gpost)


def _ffn_kernel(x_ref, g_ref, wg_ref, wu_ref, wd_ref, p_ref, gf_ref, wpg_ref, wpp_ref, gp_ref,
                out_ref, h_scr, acc_scr):
    j = pl.program_id(1)

    @pl.when(j == 0)
    def _():
        h_scr[...] = _rms(x_ref[...], g_ref[...]).astype(BF16)
        acc_scr[...] = jnp.zeros_like(acc_scr)

    h = h_scr[...]
    a = jnp.dot(h, wg_ref[...], preferred_element_type=F32)
    u = jnp.dot(h, wu_ref[...], preferred_element_type=F32)
    act = (a * _sigmoid(a) * u).astype(BF16)
    acc_scr[...] += jnp.dot(act, wd_ref[...], preferred_element_type=F32)

    @pl.when(j == pl.num_programs(1) - 1)
    def _():
        _post_math(x_ref[...], acc_scr[...], p_ref, gf_ref, wpg_ref, wpp_ref, gp_ref, out_ref)


def _ffn(x2, gain, wg, wu, wd, p2, gf, wpg, wpp, gp, tm, tf):
    n, dm = x2.shape
    dff = wg.shape[1]
    return pl.pallas_call(
        _ffn_kernel,
        grid=(n // tm, dff // tf),
        in_specs=[pl.BlockSpec((tm, dm), lambda i, j: (i, 0)),
                  pl.BlockSpec((1, dm), lambda i, j: (0, 0)),
                  pl.BlockSpec((dm, tf), lambda i, j: (0, j)),
                  pl.BlockSpec((dm, tf), lambda i, j: (0, j)),
                  pl.BlockSpec((tf, dm), lambda i, j: (j, 0)),
                  pl.BlockSpec((tm, p2.shape[1]), lambda i, j: (i, 0)),
                  _resident((1, dm)), _resident(wpg.shape), _resident(wpp.shape), _resident((1, dm))],
        out_specs=pl.BlockSpec((tm, dm), lambda i, j: (i, 0)),
        out_shape=jax.ShapeDtypeStruct((n, dm), F32),
        scratch_shapes=[pltpu.VMEM((tm, dm), BF16), pltpu.VMEM((tm, dm), F32)],
        compiler_params=_cparams(("parallel", "arbitrary"), 56),
        name="ffn",
    )(x2, gain, wg, wu, wd, p2, gf, wpg, wpp, gp)


def _post_math(xf, f, p_ref, gf_ref, wpg_ref, wpp_ref, gp_ref, out_ref):
    x2 = xf + _rms(f, gf_ref[...])
    gate = _sigmoid(jnp.dot(x2.astype(BF16), wpg_ref[...], preferred_element_type=F32))
    e = jnp.dot(p_ref[...].astype(BF16), wpp_ref[...], preferred_element_type=F32) * gate
    out_ref[...] = x2 + _rms(e, gp_ref[...])


def _router_kernel(x_ref, g_ref, wr_ref, br_ref, hs_ref, info_ref):
    hf = _rms(x_ref[...], g_ref[...])
    hb = hf.astype(BF16)
    tm, dm = hf.shape
    sub = dm // LANES
    for s in range(sub):
        hs_ref[pl.ds(s, tm, stride=sub), :] = hf[:, s * LANES:(s + 1) * LANES]
    hl = (hf - hb.astype(F32)).astype(BF16)
    whi, wlo = wr_ref[0], wr_ref[1]
    logits = (jnp.dot(hb, whi, preferred_element_type=F32) + jnp.dot(hl, whi, preferred_element_type=F32)
              + jnp.dot(hb, wlo, preferred_element_type=F32)) + br_ref[...]
    lane = lax.broadcasted_iota(jnp.int32, logits.shape, 1).astype(F32)
    m1 = jnp.max(logits, axis=-1, keepdims=True)
    i1 = jnp.min(jnp.where(logits == m1, lane, float(LANES)), axis=-1, keepdims=True)
    rest = jnp.where(lane == i1, NEG_INF, logits)
    m2 = jnp.max(rest, axis=-1, keepdims=True)
    i2 = jnp.min(jnp.where(rest == m2, lane, float(LANES)), axis=-1, keepdims=True)
    e2 = jnp.exp(m2 - m1)
    g1 = 1.0 / (1.0 + e2)
    g2 = e2 / (1.0 + e2)
    info = jnp.where(lane == 0.0, i1, jnp.where(lane == 1.0, i2, jnp.where(lane == 2.0, g1, g2)))
    info_ref[...] = info


def _router(x2, gain, wr, br, tm):
    n, dm = x2.shape
    row = lambda cols: pl.BlockSpec((tm, cols), lambda i: (i, 0))
    return pl.pallas_call(
        _router_kernel,
        grid=(n // tm,),
        in_specs=[row(dm), _resident((1, dm)), _resident(wr.shape), _resident((1, LANES))],
        out_specs=[pl.BlockSpec((tm * (dm // LANES), LANES), lambda i: (i, 0)), row(LANES)],
        out_shape=[jax.ShapeDtypeStruct((n * (dm // LANES), LANES), F32),
                   jax.ShapeDtypeStruct((n, LANES), F32)],
        compiler_params=_cparams(("parallel",), 32),
        name="router",
    )(x2, gain, wr, br)


def _experts_kernel(be_ref, src0_ref, srcn_ref, dstp_ref, dstl_ref, hs_hbm, wg_ref, wu_ref, wd_ref,
                    yt_hbm, xg, xb, acc, ys, gsem, ssem, *, tm, sub):
    i = pl.program_id(0)
    j = pl.program_id(1)
    nb = pl.num_programs(0)
    nf = pl.num_programs(1)
    slot = i % 2
    other = 1 - slot

    def gather(idx_ref, r, dslot):
        src = pl.multiple_of(idx_ref[0, r] * sub, sub)
        dst = pl.multiple_of(r * sub, sub)
        return pltpu.make_async_copy(hs_hbm.at[pl.ds(src, sub)], xg.at[dslot, pl.ds(dst, sub)],
                                     gsem.at[dslot])

    def scatter(idx_ref, r, sslot):
        src = pl.multiple_of(r * sub, sub)
        dst = pl.multiple_of(idx_ref[0, r] * sub, sub)
        return pltpu.make_async_copy(ys.at[sslot, pl.ds(src, sub)], yt_hbm.at[pl.ds(dst, sub)],
                                     ssem.at[sslot])

    def wait_gather(dslot):
        pltpu.make_async_copy(hs_hbm.at[pl.ds(0, tm * sub)], xg.at[dslot], gsem.at[dslot]).wait()

    def wait_scatter(sslot):
        pltpu.make_async_copy(ys.at[sslot], yt_hbm.at[pl.ds(0, tm * sub)], ssem.at[sslot]).wait()

    @pl.when((i == 0) & (j == 0))
    def _():
        ys[1] = jnp.zeros(ys.shape[1:], ys.dtype)

        def start(r, c):
            gather(src0_ref, r, 0).start()
            return c

        lax.fori_loop(0, tm, start, 0)

    @pl.when(j == 0)
    def _():
        wait_gather(slot)
        for s in range(sub):
            xb[:, s * LANES:(s + 1) * LANES] = xg[slot, pl.ds(s, tm, stride=sub), :].astype(BF16)
        acc[...] = jnp.zeros_like(acc)

    @pl.when(i < be_ref[nb])
    def _():
        x = xb[...]
        a = jnp.dot(x, wg_ref[...], preferred_element_type=F32)
        u = jnp.dot(x, wu_ref[...], preferred_element_type=F32)
        act = (a * _sigmoid(a) * u).astype(BF16)
        acc[...] += jnp.dot(act, wd_ref[...], preferred_element_type=F32)

    @pl.when(j == 0)
    def _():
        for t in range(tm):
            gather(srcn_ref, t, other).start()

    @pl.when(j == nf - 1)
    def _():
        for t in range(tm):
            scatter(dstp_ref, t, other).start()

    @pl.when(j == nf - 1)
    def _():
        @pl.when(i >= 1)
        def _():
            wait_scatter(slot)

        for s in range(sub):
            ys[slot, pl.ds(s, tm, stride=sub), :] = acc[:, s * LANES:(s + 1) * LANES]

        @pl.when(i == nb - 1)
        def _():
            def start(r, c):
                scatter(dstl_ref, r, slot).start()
                return c

            lax.fori_loop(0, tm, start, 0)
            wait_scatter(slot)
            wait_scatter(other)
            wait_gather(other)


def _experts(hs, blk_e, src_tok, dst_row, n_slabs, wg, wu, wd, tm, tf):
    nb = src_tok.shape[0]
    dm, dff = wg.shape[1], wg.shape[2]
    sub = dm // LANES
    nf = dff // tf
    smem = lambda imap: pl.BlockSpec((None, 1, tm), imap, memory_space=pltpu.SMEM)
    grid_spec = pltpu.PrefetchScalarGridSpec(
        num_scalar_prefetch=1,
        grid=(nb, nf),
        in_specs=[smem(lambda i, j, be: (0, 0, 0)),
                  smem(lambda i, j, be: (jnp.minimum(i + 1, nb - 1), 0, 0)),
                  smem(lambda i, j, be: (i, 0, 0)),
                  smem(lambda i, j, be: (nb, 0, 0)),
                  pl.BlockSpec(memory_space=pl.ANY),
                  pl.BlockSpec((None, dm, tf), lambda i, j, be: (be[i], 0, j)),
                  pl.BlockSpec((None, dm, tf), lambda i, j, be: (be[i], 0, j)),
                  pl.BlockSpec((None, tf, dm), lambda i, j, be: (be[i], j, 0))],
        out_specs=pl.BlockSpec(memory_space=pl.ANY),
        scratch_shapes=[pltpu.VMEM((2, tm * sub, LANES), F32), pltpu.VMEM((tm, dm), BF16),
                        pltpu.VMEM((tm, dm), F32), pltpu.VMEM((2, tm * sub, LANES), F32),
                        pltpu.SemaphoreType.DMA((2,)), pltpu.SemaphoreType.DMA((2,))],
    )
    return pl.pallas_call(
        functools.partial(_experts_kernel, tm=tm, sub=sub),
        grid_spec=grid_spec,
        out_shape=jax.ShapeDtypeStruct((n_slabs * sub, LANES), F32),
        compiler_params=_cparams(("arbitrary", "arbitrary"), 48),
        name="experts",
    )(blk_e, src_tok, src_tok, dst_row, dst_row, hs, wg, wu, wd)


def _combine_kernel(y_ref, x_ref, gt_ref, p_ref, gf_ref, wpg_ref, wpp_ref, gp_ref, out_ref):
    tt, dm = x_ref.shape
    sub = dm // LANES
    gt = gt_ref[...]

    def rows(slot):
        return jnp.concatenate([y_ref[pl.ds(slot * sub + s, tt, stride=TOP_K * sub), :] for s in range(sub)],
                               axis=1)

    f = gt[:, 2:3] * rows(0) + gt[:, 3:4] * rows(1)
    _post_math(x_ref[...], f, p_ref, gf_ref, wpg_ref, wpp_ref, gp_ref, out_ref)


def _combine(yt, info, x2, p2, gf, wpg, wpp, gp, tt):
    n, dm = x2.shape
    row = lambda cols: pl.BlockSpec((tt, cols), lambda i: (i, 0))
    return pl.pallas_call(
        _combine_kernel,
        grid=(n // tt,),
        in_specs=[pl.BlockSpec((tt * TOP_K * (dm // LANES), LANES), lambda i: (i, 0)),
                  row(dm), row(LANES), row(p2.shape[1]), _resident((1, dm)),
                  _resident(wpg.shape), _resident(wpp.shape), _resident((1, dm))],
        out_specs=row(dm),
        out_shape=jax.ShapeDtypeStruct((n, dm), F32),
        compiler_params=_cparams(("parallel",), 48),
        name="combine",
    )(yt, x2, info, p2, gf, wpg, wpp, gp)


def _moe(x2, p2, gain, wr, br, wg, wu, wd, gf, wpg, wpp, gp, *, tm_r, tt, tm_e, tf_e):
    n, _ = x2.shape
    hs, info = _router(x2, gain, wr, br, tm_r)
    n_asg = n * TOP_K
    top_e = info[:, 0:TOP_K].astype(jnp.int32)
    e_flat = top_e.reshape(n_asg)
    onehot = (e_flat[:, None] == jnp.arange(N_EXPERTS, dtype=jnp.int32)[None, :]).astype(jnp.int32)
    csum = jnp.cumsum(onehot, axis=0)
    rank = jnp.sum((csum - onehot) * onehot, axis=1)
    counts = csum[-1]
    padded = ((counts + tm_e - 1) // tm_e) * tm_e
    pend = jnp.cumsum(padded)
    pstart = pend - padded
    dest = (pstart[e_flat] + rank).astype(jnp.int32)
    n_blocks = -(-n_asg // tm_e) + N_EXPERTS
    n_rows = n_blocks * tm_e
    blk_e = jnp.clip(jnp.searchsorted(pend, jnp.arange(n_blocks, dtype=jnp.int32) * tm_e, side="right"),
                     0, N_EXPERTS - 1).astype(jnp.int32)
    blk_e = jnp.concatenate([blk_e, (pend[-1:] // tm_e).astype(jnp.int32)])
    asg = jnp.full((n_rows,), -1, jnp.int32).at[dest].set(jnp.arange(n_asg, dtype=jnp.int32),
                                                          unique_indices=True)
    is_pad = asg < 0
    pad_rank = jnp.cumsum(is_pad.astype(jnp.int32)) - 1
    src_tok = jnp.where(is_pad, 0, asg // TOP_K).reshape(n_blocks, 1, tm_e)
    dst_row = jnp.where(is_pad, n_asg + pad_rank, asg)
    spare = n_asg + (n_rows - n_asg) + jnp.arange(tm_e, dtype=jnp.int32)
    dst_row = jnp.concatenate([spare, dst_row]).reshape(n_blocks + 1, 1, tm_e)
    n_slabs = n_asg + (n_rows - n_asg) + tm_e
    yt = _experts(hs, blk_e, src_tok, dst_row, n_slabs, wg, wu, wd, tm_e, tf_e)
    return _combine(yt, info, x2, p2, gf, wpg, wpp, gp, tt)


def _tables(seq):
    pos = np.arange(seq, dtype=np.int32)

    def cs(p, half):
        inv = np.power(np.float32(ROPE_THETA), -np.arange(half, dtype=np.float32) / np.float32(half))
        ang = p.astype(np.float32)[:, None] * inv[None, :].astype(np.float32)
        return jnp.asarray(np.cos(ang), F32), jnp.asarray(np.sin(ang), F32)

    ones = lambda w: jnp.ones((seq, w), F32)
    zeros = lambda w: jnp.zeros((seq, w), F32)
    (c, s), (cr, sr), (cc, sc), (ca, sa) = (
        cs(pos, HEAD_DIM // 2), cs(pos // GRID_W, HEAD_DIM // 4), cs(pos % GRID_W, HEAD_DIM // 4),
        cs(pos, A_ROPE // 2))
    full = jnp.stack([jnp.tile(jnp.concatenate([c, c], 1), (1, 2)),
                      jnp.tile(jnp.concatenate([-s, s], 1), (1, 2))])
    axial = jnp.stack([jnp.tile(jnp.concatenate([cr, cc, cr, cc], 1), (1, 2)),
                       jnp.tile(jnp.concatenate([-sr, -sc, sr, sc], 1), (1, 2))])
    slot_c =jnp.concatenate([ca, ones(16), ca, ones(16)], 1)
    slot_s = jnp.concatenate([-sa, zeros(16), sa, zeros(16)], 1)
    a_k = jnp.stack([jnp.concatenate([slot_c, ones(HALF)], 1), jnp.concatenate([slot_s, zeros(HALF)], 1)])
    a_q = jnp.stack([jnp.concatenate([ones(HALF), slot_c], 1), jnp.concatenate([zeros(HALF), slot_s], 1)])
    return {"full": full, "axial": axial, "a_k": a_k, "a_q": a_q}


_AXIAL_PERM = tuple(list(range(0, 16)) + list(range(32, 48)) + list(range(16, 32)) + list(range(48, 64)))


def _dup_heads(w, n_heads, perm=None):
    rows = w.shape[0]
    w = w.reshape(rows, n_heads, HEAD_DIM)
    if perm is not None:
        w = w[:, :, perm]
    return jnp.stack([w, w], axis=2).reshape(rows, n_heads * 2 * HEAD_DIM)


def _assemble_w_in(w):
    dm = w.shape[0]
    perm = jnp.array(_AXIAL_PERM, jnp.int32)
    a, b, c, d = w[:, 0:SRC_A], w[:, SRC_A:SRC_B], w[:, SRC_B:SRC_C], w[:, SRC_C:SRC_D]
    z = lambda n: jnp.zeros((dm, n), w.dtype)
    kr = a[:, 384:416]
    a_seg = jnp.concatenate([a[:, 0:384], kr[:, 0:16], z(16), kr[:, 16:32], z(16), z(HALF)], axis=1)
    bq = b[:, 0:256].reshape(dm, 4, HEAD_DIM)[:, :, perm].reshape(dm, 256)
    b_seg = jnp.concatenate([bq, _dup_heads(b[:, 256:384], 2, perm)], axis=1)
    d_seg = jnp.concatenate([d[:, 0:256], _dup_heads(d[:, 256:384], 2), _dup_heads(d[:, 384:512], 2)], axis=1)
    w_vbt = _dup_heads(b[:, 384:512], 2).T.astype(BF16)
    return jnp.concatenate([a_seg, b_seg, c, d_seg], axis=1).astype(BF16), w_vbt


def _assemble_a(w_uq, w_ukv):
    zq = lambda n: jnp.zeros((w_uq.shape[0], n), w_uq.dtype)
    zk = lambda n: jnp.zeros((w_ukv.shape[0], n), w_ukv.dtype)
    dq = A_NOPE + A_ROPE
    q_cols, k_cols, v_cols = [], [], []
    for hh in range(A_HEADS):
        q = w_uq[:, hh * dq:(hh + 1) * dq]
        q_cols += [q[:, 0:A_NOPE], q[:, A_NOPE:A_NOPE + 16], zq(16), q[:, A_NOPE + 16:dq], zq(16)]
        kv = w_ukv[:, hh * (A_NOPE + A_V):(hh + 1) * (A_NOPE + A_V)]
        k_cols += [kv[:, 0:A_NOPE], zk(HALF)]
        v_cols += [kv[:, A_NOPE:A_NOPE + A_V]]
    cat = lambda cols: jnp.concatenate(cols, axis=1).astype(BF16)
    return cat(q_cols), cat(k_cols), cat(v_cols).T


def _gain_pair(g, perm=None):
    if perm is not None:
        g = g[jnp.array(perm, jnp.int32)]
    return jnp.tile(g, 2).reshape(1, LANES).astype(F32)


def kernel(x, p, w_in, a_qa_g, a_kva_g, a_w_uq, a_w_ukv, b_q_g, b_k_g, d_sink, w_branch, w_out,
           mix_pre_g, mix_post_g, ffn_pre_g, ffn_post_g, ffn_w_gate, ffn_w_up, ffn_w_down,
           router_w, router_b, moe_w_gate, moe_w_up, moe_w_down, ple_w_proj, ple_w_gate, ple_post_g):
    bsz, seq, dm = x.shape
    depth = w_in.shape[0]
    n = bsz * seq
    tm = min(512, seq)
    tq_flash = min(512, seq)
    tk_flash = min(1024, seq // 4)
    tabs = _tables(seq)
    row = lambda g: g.reshape(1, -1).astype(F32)
    x2 = x.reshape(n, dm)

    for i in range(depth):
        w_in16 = w_in[i].astype(BF16)
        w_all, wvb = _assemble_w_in(w_in16)
        wuq, wk, wv = _assemble_a(a_w_uq[i], a_w_ukv[i])
        z, qa, ka, vat, vbt, zc1, zc2 = _in_proj(
            x2, row(mix_pre_g[i]), w_all, tabs,
            _gain_pair(b_q_g[i], _AXIAL_PERM), _gain_pair(b_k_g[i], _AXIAL_PERM),
            row(a_qa_g[i]), row(a_kva_g[i]), wuq, wk, wv, wvb, seq, tm)
        z3 = z.reshape(bsz, seq, ZMAIN_COLS)
        o_a = _flash(qa, ka, vat, q_col0=0, k_col0=0, v_row0=0, seq=seq, packed=False,
                     tq=tq_flash, tk=tk_flash)
        o_b = _flash(z, z, vbt, q_col0=ZB, k_col0=ZB + 256, v_row0=0, seq=seq, packed=True,
                     tq=tq_flash, tk=tk_flash)
        o_c, l_c = [], []
        for (win, dil), (src, col0, rc) in zip(C_PATTERNS, ((z3, ZC, ZMAIN_COLS), (zc1, 0, QKV_SEG), (zc2, 0, QKV_SEG))):
            og, lg = _banded(src, src, src, q_col0=col0, k_col0=col0 + 256, v_col0=col0 + 512,
                             row_cols=rc, dil=dil, seq=seq, hw=win // (2 * dil), tq=min(512, seq // dil),
                             want_lse=True, out_dtype=F32)
            o_c.append(og)
            l_c.append(lg)
        o_c[0] = o_c[0].reshape(n, 2 * LANES)
        l_c[0] = l_c[0].reshape(n, 2 * LANES)
        (o_d,) = _banded(z3, z3, z3, q_col0=ZD, k_col0=ZD + 256, v_col0=ZD + 512,
                         row_cols=ZMAIN_COLS, dil=1, seq=seq, hw=D_HALF_WINDOW, tq=min(512, seq),
                         sink=d_sink[i].astype(F32))
        o_d = o_d.reshape(n, 2 * LANES)
        wg_gate = w_in16[:, SRC_D:]
        x2 = _merge(x2, o_a, o_b, o_c, l_c, o_d, row(mix_pre_g[i]), wg_gate,
                    w_branch[i].astype(BF16), w_out[i].astype(BF16), row(mix_post_g[i]), seq, tm)

        p2 = p[i].reshape(n, -1)
        wpg = ple_w_gate[i].astype(BF16)
        wpp = ple_w_proj[i].astype(BF16)
        j = i // 2
        if i % 2 == 0:
            x2 = _ffn(x2, row(ffn_pre_g[i]), ffn_w_gate[j].astype(BF16), ffn_w_up[j].astype(BF16),
                      ffn_w_down[j].astype(BF16), p2, row(ffn_post_g[i]), wpg, wpp, row(ple_post_g[i]),
                      min(1024, n), 512)
        else:
            wr32 = jnp.zeros((dm, LANES), F32).at[:, :N_EXPERTS].set(router_w[j].astype(F32))
            wr_hi = wr32.astype(BF16)
            wr = jnp.stack([wr_hi, (wr32 - wr_hi.astype(F32)).astype(BF16)])
            br = jnp.full((1, LANES), NEG_INF, F32).at[0, :N_EXPERTS].set(router_b[j].astype(F32))
            x2 = _moe(x2, p2, row(ffn_pre_g[i]), wr, br, moe_w_gate[j].astype(BF16),
                      moe_w_up[j].astype(BF16), moe_w_down[j].astype(BF16), row(ffn_post_g[i]),
                      wpg, wpp, row(ple_post_g[i]), tm_r=tm, tt=min(256, n), tm_e=512, tf_e=1792)
    return x2.reshape(bsz, seq, dm)
```

```python
import functools

import jax
import jax.numpy as jnp
import numpy as np
from jax import lax
from jax.experimental import pallas as pl
from jax.experimental.pallas import tpu as pltpu

F32 = jnp.float32
BF16 = jnp.bfloat16

GRID_W = 64
HEAD_DIM = 64
ROPE_THETA = 10000.0
NORM_EPS = 1e-6
NEG_INF = -1e30
A_HEADS = 4
A_Q_RANK = 256
A_KV_RANK = 128
A_NOPE = 64
A_ROPE = 32
A_V = 64
C_PATTERNS = ((128, 1), (512, 4), (2048, 16))
D_HALF_WINDOW = 128
N_EXPERTS = 8
TOP_K = 2

LANES = 128
HALF = 64
VMEM_MB = 1024 * 1024
LOG2E = 1.4426950408889634
BAND_SUB = 128

PAIR = 2 * LANES
QKV_SEG = 3 * PAIR
SRC_A = A_Q_RANK + A_KV_RANK + A_ROPE
SRC_B = SRC_A + PAIR + 2 * LANES
SRC_C = SRC_B + len(C_PATTERNS) * QKV_SEG
SRC_D = SRC_C + PAIR + 2 * LANES

ZB = 0
ZC = 512
ZD = 512 + QKV_SEG
ZMAIN_COLS = ZD + QKV_SEG
A_SEG = 512
B_SEG = 512


def _cparams(sem, vmem_mb):
    return pltpu.CompilerParams(dimension_semantics=sem, vmem_limit_bytes=vmem_mb * VMEM_MB)


def _resident(shape):
    nd = len(shape)
    return pl.BlockSpec(shape, lambda *_: (0,) * nd, pipeline_mode=pl.Buffered(1))


def _rms(xf, g):
    return xf * lax.rsqrt(jnp.mean(xf * xf, axis=-1, keepdims=True) + NORM_EPS) * g


def _sigmoid(x):
    return 1.0 / (1.0 + jnp.exp(-x))


def _swap32(a):
    lane = lax.broadcasted_iota(jnp.int32, a.shape, 1)
    fwd = pltpu.roll(a, LANES - 32, 1)
    bwd = pltpu.roll(a, 32, 1)
    return jnp.where((lane & 32) == 0, fwd, bwd)


def _rope(a, cos, sin):
    outs = []
    for c in range(a.shape[1] // LANES):
        ch = a[:, c * LANES:(c + 1) * LANES]
        outs.append(ch * cos + _swap32(ch) * sin)
    return outs[0] if len(outs) == 1 else jnp.concatenate(outs, axis=1)


def _head_norm(a, g, bd):
    outs = []
    for c in range(a.shape[1] // LANES):
        ch = a[:, c * LANES:(c + 1) * LANES]
        sq = ch * ch
        hi = sq.astype(BF16)
        lo = (sq - hi.astype(F32)).astype(BF16)
        ms = (jnp.dot(hi, bd, preferred_element_type=F32)
              + jnp.dot(lo, bd, preferred_element_type=F32))
        outs.append(ch * lax.rsqrt(ms + NORM_EPS) * g)
    return outs[0] if len(outs) == 1 else jnp.concatenate(outs, axis=1)


def _in_proj_kernel(x_ref, g_ref, w_ref, tabf_ref, tabx_ref, taba_ref, tabq_ref,
                    bqg_ref, bkg_ref, aqg_ref, akvg_ref, wuq_ref, wk_ref, wv_ref, wvb_ref,
                    z_ref, qa_ref, ka_ref, vat_ref, vbt_ref, zc1_ref, zc2_ref, cls_scr):
    h = _rms(x_ref[...], g_ref[...]).astype(BF16)
    cf, sf = tabf_ref[0], tabf_ref[1]
    cx, sx = tabx_ref[0], tabx_ref[1]
    q_scale = HEAD_DIM ** -0.5

    acc = jnp.dot(h, w_ref[:, 0:A_SEG], preferred_element_type=F32)
    nq = _rms(acc[:, 0:A_Q_RANK], aqg_ref[...]).astype(BF16)
    nkv = _rms(acc[:, A_Q_RANK:A_Q_RANK + A_KV_RANK], akvg_ref[...]).astype(BF16)
    kr = _rope(acc[:, 384:512], taba_ref[0], taba_ref[1]).astype(BF16)
    qa = jnp.dot(nq, wuq_ref[...], preferred_element_type=F32)
    qa = _rope(qa, tabq_ref[0], tabq_ref[1]) * ((A_NOPE + A_ROPE) ** -0.5 * LOG2E)
    qa_ref[...] = qa.astype(qa_ref.dtype)
    r = lax.broadcasted_iota(jnp.int32, (LANES, A_HEADS * LANES), 0)
    c = lax.broadcasted_iota(jnp.int32, (LANES, A_HEADS * LANES), 1)
    place = jnp.where((r < HALF) & ((c & (LANES - 1)) == r + HALF), 1.0, 0.0).astype(BF16)
    ka = (jnp.dot(nkv, wk_ref[...], preferred_element_type=F32)
          + jnp.dot(kr, place, preferred_element_type=F32))
    ka_ref[...] = ka.astype(ka_ref.dtype)
    nt_dims = (((1,), (1,)), ((), ()))
    vat_ref[...] = lax.dot_general(wv_ref[...], nkv, nt_dims, preferred_element_type=F32).astype(vat_ref.dtype)
    vbt_ref[...] = lax.dot_general(wvb_ref[...], h, nt_dims, preferred_element_type=F32).astype(vbt_ref.dtype)

    rr = lax.broadcasted_iota(jnp.int32, (LANES, LANES), 0)
    cc = lax.broadcasted_iota(jnp.int32, (LANES, LANES), 1)
    bd = jnp.where((rr >> 6) == (cc >> 6), 1.0 / HEAD_DIM, 0.0).astype(BF16)
    acc = jnp.dot(h, w_ref[:, A_SEG:A_SEG + B_SEG], preferred_element_type=F32)
    q = _rope(_head_norm(acc[:, 0:256], bqg_ref[...], bd), cx, sx) * (q_scale * LOG2E)
    k = _rope(_head_norm(acc[:, 256:512], bkg_ref[...], bd), cx, sx)
    z_ref[:, ZB:ZB + 256] = q.astype(z_ref.dtype)
    z_ref[:, ZB + 256:ZB + 512] = k.astype(z_ref.dtype)

    tm = h.shape[0]
    for widx, zoff, cls_ref, dil in ((0, ZC, None, 1), (1, 0, zc1_ref, C_PATTERNS[1][1]),
                                     (2, 0, zc2_ref, C_PATTERNS[2][1]), (3, ZD, None, 1)):
        base = A_SEG + B_SEG + widx * QKV_SEG
        acc = jnp.dot(h, w_ref[:, base:base + QKV_SEG], preferred_element_type=F32)
        q = _rope(acc[:, 0:256], cf, sf) * q_scale
        k = _rope(acc[:, 256:512], cf, sf)
        if cls_ref is None:
            z_ref[:, zoff:zoff + 256] = q.astype(z_ref.dtype)
            z_ref[:, zoff + 256:zoff + 512] = k.astype(z_ref.dtype)
            z_ref[:, zoff + 512:zoff + QKV_SEG] = acc[:, 512:QKV_SEG].astype(z_ref.dtype)
        else:
            qkv = (q[:, 0:LANES], q[:, LANES:], k[:, 0:LANES], k[:, LANES:],
                   acc[:, 512:512 + LANES], acc[:, 512 + LANES:QKV_SEG])
            for ch, val in enumerate(qkv):
                cls_scr[ch] = val
            for c in range(dil):
                for ch in range(len(qkv)):
                    col = c * QKV_SEG + ch * LANES
                    cls_ref[:, col:col + LANES] = cls_scr[ch, pl.ds(c, tm // dil, stride=dil), :].astype(cls_ref.dtype)


def _in_proj(x2, gain, w, tabs, bqg, bkg, aqg, akvg, wuq, wk, wv, wvb, seq, tm):
    n, dm = x2.shape
    nt = seq // tm
    bsz = n // seq
    d1, d2 = C_PATTERNS[1][1], C_PATTERNS[2][1]
    tab_spec = pl.BlockSpec((2, tm, LANES), lambda i: (0, i % nt, 0))
    row = lambda cols: pl.BlockSpec((tm, cols), lambda i: (i, 0))
    vt_spec = pl.BlockSpec((None, 2 * LANES, tm), lambda i: (i // nt, 0, i % nt))
    cls_spec = lambda d: pl.BlockSpec((None, tm // d, d * QKV_SEG), lambda i: (i // nt, i % nt, 0))
    return pl.pallas_call(
        _in_proj_kernel,
        grid=(n // tm,),
        in_specs=[row(dm), _resident((1, dm)), _resident(w.shape),
                  tab_spec, tab_spec, tab_spec, tab_spec,
                  _resident((1, LANES)), _resident((1, LANES)),
                  _resident((1, A_Q_RANK)), _resident((1, A_KV_RANK)),
                  _resident(wuq.shape), _resident(wk.shape), _resident(wv.shape), _resident(wvb.shape)],
        out_specs=[row(ZMAIN_COLS), row(A_HEADS * LANES), row(A_HEADS * LANES),
                   vt_spec, vt_spec, cls_spec(d1), cls_spec(d2)],
        out_shape=[jax.ShapeDtypeStruct((n, ZMAIN_COLS), BF16),
                   jax.ShapeDtypeStruct((n, A_HEADS * LANES), BF16),
                   jax.ShapeDtypeStruct((n, A_HEADS * LANES), BF16),
                   jax.ShapeDtypeStruct((bsz, 2 * LANES, seq), BF16),
                   jax.ShapeDtypeStruct((bsz, 2 * LANES, seq), BF16),
                   jax.ShapeDtypeStruct((bsz, seq // d1, d1 * QKV_SEG), BF16),
                   jax.ShapeDtypeStruct((bsz, seq // d2, d2 * QKV_SEG), BF16)],
        scratch_shapes=[pltpu.VMEM((QKV_SEG // LANES, tm, LANES), F32)],
        compiler_params=_cparams(("parallel",), 48),
        name="in_proj",
    )(x2, gain, w, tabs["full"], tabs["axial"], tabs["a_k"], tabs["a_q"],
      bqg, bkg, aqg, akvg, wuq, wk, wv, wvb)


FLASH_SAFE_EXP = 64.0


def _head_queries(q_ref, packed):
    lane = lax.broadcasted_iota(jnp.int32, (1, LANES), 1)
    if packed:
        return [jnp.where((lane < HALF) == (r == 0), q_ref[...], 0) for r in range(2)], [0, 0]
    return [q_ref[:, r * LANES:(r + 1) * LANES] for r in range(2)], [0, LANES]


def _flash_single_pass(q_ref, k_ref, vt_ref, s_a, s_b, *, tk, packed):
    tq = q_ref.shape[0]
    nk = k_ref.shape[0] // tk
    qs, kcs = _head_queries(q_ref, packed)
    qts = [q.astype(F32).T.astype(BF16) for q in qs]
    row = lax.broadcasted_iota(jnp.int32, (LANES, tq), 0)
    klane = lax.broadcasted_iota(jnp.int32, (tk, LANES), 1)
    k_one = jnp.where(klane == 0, 1.0, 0.0).astype(BF16)
    v_one = jnp.ones((16, tk), BF16)

    def scores(c, refs, dst):
        ks = pl.multiple_of(c * tk, tk)
        for r in range(2):
            k = jnp.concatenate([k_ref[pl.ds(ks, tk), kcs[r]:kcs[r] + LANES], k_one], axis=1)
            bias = jnp.where(row == 0, -refs[r], 0.0).astype(BF16)
            qt = jnp.concatenate([qts[r], bias], axis=0)
            dst[r] = jnp.dot(k, qt, preferred_element_type=F32)

    def consume(c, src, refs, sts, first):
        ks = pl.multiple_of(c * tk, tk)
        new = []
        for r in range(2):
            big, base, l, acc, hi, lo = sts[r]
            s = src[r]
            cmax = jnp.max(s, axis=0, keepdims=True)
            p = jnp.exp2(s).astype(BF16)
            lhs = jnp.concatenate([vt_ref[r * HALF:(r + 1) * HALF, pl.ds(ks, tk)], v_one], axis=0)
            pv = jnp.dot(lhs, p, preferred_element_type=F32)
            alpha = jnp.exp2(base - refs[r])
            l = l * alpha + pv[HALF:HALF + 1]
            acc = acc * alpha + pv[0:HALF]
            big = jnp.maximum(big, refs[r] + cmax)
            hi = jnp.maximum(hi, cmax)
            if first:
                lo = jnp.minimum(lo, cmax)
            new.append((big, refs[r], l, acc, hi, lo))
        return tuple(new)

    def ref_of(sts):
        return [st[0].astype(BF16).astype(F32) for st in sts]

    zero = jnp.zeros((1, tq), F32)
    sts = tuple((jnp.full((1, tq), NEG_INF, F32), zero, zero, jnp.zeros((HALF, tq), F32),
                 jnp.full((1, tq), NEG_INF, F32), jnp.full((1, tq), -NEG_INF, F32)) for _ in range(2))
    zeros2 = [zero, zero]
    scores(0, zeros2, s_a)
    scores(1, zeros2, s_b)
    sts = consume(0, s_a, zeros2, sts, True)
    ra = ref_of(sts)
    scores(2, ra, s_a)
    sts = consume(1, s_b, zeros2, sts, True)

    def pair(jj, carry):
        sts, ra = carry
        c0 = 2 * jj
        rb = ref_of(sts)
        scores(c0 + 1, rb, s_b)
        sts = consume(c0, s_a, ra, sts, False)
        ra = ref_of(sts)
        scores(c0 + 2, ra, s_a)
        sts = consume(c0 + 1, s_b, rb, sts, False)
        return sts, ra

    sts, ra = lax.fori_loop(1, nk // 2 - 1, pair, (sts, ra))
    rb = ref_of(sts)
    scores(nk - 1, rb, s_b)
    sts = consume(nk - 2, s_a, ra, sts, False)
    sts = consume(nk - 1, s_b, rb, sts, False)
    outs, bad = [], None
    for r in range(2):
        _, _, l, acc, hi, lo = sts[r]
        outs.append(acc / l)
        b = (hi > FLASH_SAFE_EXP) | (lo < -FLASH_SAFE_EXP) | jnp.logical_not(l > 2.0 ** -FLASH_SAFE_EXP)
        bad = b if bad is None else (bad | b)
    return jnp.concatenate(outs, axis=0), bad


def _flash_kernel(q_ref, k_ref, vt_ref, o_ref, s_a, s_b, *, tk, packed):
    out, bad = _flash_single_pass(q_ref, k_ref, vt_ref, s_a, s_b, tk=tk, packed=packed)
    o_ref[...] = out.T.astype(o_ref.dtype)

    @pl.when(jnp.max(jnp.where(bad, 1.0, 0.0)) > 0.0)
    def _():
        o_ref[...] = _flash_two_pass(q_ref, k_ref, vt_ref, s_a, s_b, tk=tk, packed=packed).T.astype(o_ref.dtype)


def _flash_two_pass(q_ref, k_ref, vt_ref, s_a, s_b, *, tk, packed):
    tq = q_ref.shape[0]
    nk = k_ref.shape[0] // tk
    qs, kcs = _head_queries(q_ref, packed)

    def scores(kk, dst):
        ks = pl.multiple_of(kk * tk, tk)
        for r in range(2):
            k = k_ref[pl.ds(ks, tk), kcs[r]:kcs[r] + LANES]
            dst[r] = lax.dot_general(k, qs[r], (((1,), (1,)), ((), ())), preferred_element_type=F32)

    def consume(kk, src, carry):
        ks = pl.multiple_of(kk * tk, tk)
        new = []
        for r in range(2):
            m, l, acc = carry[r]
            s = src[r]
            m_new = jnp.maximum(m, jnp.max(s, axis=0, keepdims=True))
            alpha = jnp.exp2(m - m_new)
            p = jnp.exp2(s - m_new)
            l = alpha * l + jnp.sum(p, axis=0, keepdims=True)
            vt = vt_ref[r * HALF:(r + 1) * HALF, pl.ds(ks, tk)]
            acc = alpha * acc + jnp.dot(vt, p.astype(BF16), preferred_element_type=F32)
            new.append((m_new, l, acc))
        return tuple(new)

    def pair(jj, carry):
        c0 = 2 * jj
        scores(c0 + 1, s_b)
        carry = consume(c0, s_a, carry)
        scores(c0 + 2, s_a)
        return consume(c0 + 1, s_b, carry)

    init = tuple((jnp.full((1, tq), NEG_INF, F32), jnp.zeros((1, tq), F32), jnp.zeros((HALF, tq), F32))
                 for _ in range(2))
    scores(0, s_a)
    carry = lax.fori_loop(0, nk // 2 - 1, pair, init)
    scores(nk - 1, s_b)
    carry = consume(nk - 2, s_a, carry)
    carry = consume(nk - 1, s_b, carry)
    return jnp.concatenate([acc / l for (_, l, acc) in carry], axis=0)


def _flash(q, k, vt, *, q_col0, k_col0, v_row0, seq, packed, tq, tk):
    n = q.shape[0]
    bsz = n // seq
    qw = LANES if packed else 2 * LANES
    nq = seq // tq
    assert seq % (2 * tk) == 0 and seq // tk >= 4, "the chunk pipeline needs an even number (>= 4) of key chunks"
    k3 = k.reshape(bsz, seq, k.shape[1])
    qb, kb, vb = q_col0 // qw, k_col0 // qw, v_row0 // LANES
    return pl.pallas_call(
        functools.partial(_flash_kernel, tk=tk, packed=packed),
        grid=(bsz, 2, nq),
        in_specs=[pl.BlockSpec((tq, qw), lambda b, j, i: (b * nq + i, qb + j)),
                  pl.BlockSpec((None, seq, qw), lambda b, j, i: (b, 0, kb + j)),
                  pl.BlockSpec((None, LANES, seq), lambda b, j, i: (b, vb + j, 0))],
        out_specs=pl.BlockSpec((tq, LANES), lambda b, j, i: (b * nq + i, j)),
        out_shape=jax.ShapeDtypeStruct((n, 2 * LANES), BF16),
        scratch_shapes=[pltpu.VMEM((2, tk, tq), F32), pltpu.VMEM((2, tk, tq), F32)],
        compiler_params=_cparams(("parallel", "parallel", "parallel"), 48),
        name="flash_packed" if packed else "flash_slots",
    )(q, k3, vt)


def _banded_kernel(*refs, hw, has_sink, want_lse):
    if has_sink:
        sink_ref, q_ref, k_ref, v_ref = refs[:4]
        outs = refs[4:]
    else:
        q_ref, k_ref, v_ref = refs[:3]
        outs = refs[3:]
    o_ref = outs[0]
    tq = q_ref.shape[0]
    length = k_ref.shape[0]
    sb = min(BAND_SUB, tq)
    win = min(sb + 2 * hw, length)
    i = pl.program_id(1)
    lane = lax.broadcasted_iota(jnp.int32, (1, LANES), 1)
    first = lane < HALF
    cols = [slice(j * LANES, (j + 1) * LANES) for j in range(2)]
    subs = list(range(tq // sb))
    kss, valids = [], []
    for u in subs:
        q0 = i * tq + u * sb
        ks = pl.multiple_of(jnp.clip(q0 - hw, 0, length - win), HALF)
        qpos = q0 + lax.broadcasted_iota(jnp.int32, (sb, win), 0)
        kpos = ks + lax.broadcasted_iota(jnp.int32, (sb, win), 1)
        kss.append(ks)
        valids.append(jnp.abs(qpos - kpos) <= hw)
    chains = [(u, j, r) for u in subs for j in range(2) for r in range(2)]
    kws = {(u, j): k_ref[pl.ds(kss[u], win), cols[j]] for u in subs for j in range(2)}
    vws = {(u, j): v_ref[pl.ds(kss[u], win), cols[j]] for u in subs for j in range(2)}
    ss = []
    for u, j, r in chains:
        q = jnp.where(first == (r == 0), q_ref[u * sb:(u + 1) * sb, cols[j]], 0)
        ss.append(lax.dot_general(q, kws[u, j], (((1,), (1,)), ((), ())), preferred_element_type=F32))
    ss = [jnp.where(valids[u], s, NEG_INF) for s, (u, j, r) in zip(ss, chains)]
    ms = [jnp.max(s, axis=-1, keepdims=True) for s in ss]
    if has_sink:
        sinks = [sink_ref[2 * j + r] for u, j, r in chains]
        ms = [jnp.maximum(m, sk) for m, sk in zip(ms, sinks)]
    es = [jnp.exp(s - m) for s, m in zip(ss, ms)]
    ls = [jnp.sum(e, axis=-1, keepdims=True) for e in es]
    if has_sink:
        ls = [l + jnp.exp(sk - m) for l, sk, m in zip(ls, sinks, ms)]
    os_ = [jnp.dot(e.astype(BF16), vws[u, j], preferred_element_type=F32) / l
           for e, l, (u, j, r) in zip(es, ls, chains)]
    for n in range(0, len(chains), 2):
        u, j, _ = chains[n]
        rows = slice(u * sb, (u + 1) * sb)
        o_ref[rows, cols[j]] = jnp.where(first, os_[n], os_[n + 1]).astype(o_ref.dtype)
        if want_lse:
            lses = [jnp.broadcast_to(ms[n + r] + jnp.log(ls[n + r]), (sb, LANES)) for r in range(2)]
            outs[1][rows, cols[j]] = jnp.where(first, lses[0], lses[1])


def _banded(q, k, v, *, q_col0, k_col0, v_col0, row_cols, dil, seq, hw, tq, sink=None,
            want_lse=False, out_dtype=BF16):
    bsz, ls, _ = q.shape
    nq = ls // tq
    pw = 2 * LANES
    rb = row_cols // pw
    qb, kb, vb = q_col0 // pw, k_col0 // pw, v_col0 // pw
    in_specs = [pl.BlockSpec((None, tq, pw), lambda bc, i: (bc // dil, i, (bc % dil) * rb + qb)),
                pl.BlockSpec((None, ls, pw), lambda bc, i: (bc // dil, 0, (bc % dil) * rb + kb)),
                pl.BlockSpec((None, ls, pw), lambda bc, i: (bc // dil, 0, (bc % dil) * rb + vb))]
    args = [q, k, v]
    if sink is not None:
        in_specs = [pl.BlockSpec(memory_space=pltpu.SMEM)] + in_specs
        args = [sink] + args
    o_spec = pl.BlockSpec((None, tq, pw), lambda bc, i: (bc // dil, i, bc % dil))
    out_specs = [o_spec]
    out_shape = [jax.ShapeDtypeStruct((bsz, ls, dil * pw), out_dtype)]
    if want_lse:
        out_specs.append(o_spec)
        out_shape.append(jax.ShapeDtypeStruct((bsz, ls, dil * pw), F32))
    res = pl.pallas_call(
        functools.partial(_banded_kernel, hw=hw, has_sink=sink is not None, want_lse=want_lse),
        grid=(bsz * dil, nq),
        in_specs=in_specs,
        out_specs=out_specs,
        out_shape=out_shape,
        compiler_params=_cparams(("parallel", "parallel"), 48),
        name="banded_d%d" % dil,
    )(*args)
    return res


def _merge_kernel(x_ref, oa_ref, ob_ref, oc0_ref, oc1_ref, oc2_ref, l0_ref, l1_ref, l2_ref,
                  od_ref, gpre_ref, wg_ref, wb_ref, wo_ref, gpost_ref, out_ref, tok_scr):
    xf = x_ref[...]
    tm, dm = xf.shape
    h = _rms(xf, gpre_ref[...]).astype(BF16)
    pw = 2 * LANES
    toks = []
    for n, src in enumerate((oc1_ref, l1_ref, oc2_ref, l2_ref)):
        dil = src.shape[1] // pw
        for c in range(dil):
            for hp in range(2):
                col = c * pw + hp * LANES
                tok_scr[2 * n + hp, pl.ds(c, tm // dil, stride=dil), :] = src[:, col:col + LANES]
        toks.append(jnp.concatenate([tok_scr[2 * n], tok_scr[2 * n + 1]], axis=1))
    oc1, l1, oc2, l2 = toks
    l0 = l0_ref[...]
    mx = jnp.maximum(jnp.maximum(l0, l1), l2)
    w0, w1, w2 = jnp.exp(l0 - mx), jnp.exp(l1 - mx), jnp.exp(l2 - mx)
    oc = (w0 * oc0_ref[...] + w1 * oc1 + w2 * oc2) / (w0 + w1 + w2)
    branches = (oa_ref[...], ob_ref[...], oc.astype(BF16), od_ref[...])
    merged = None
    for n, o in enumerate(branches):
        gate = _sigmoid(jnp.dot(h, wg_ref[:, n * dm:(n + 1) * dm], preferred_element_type=F32))
        term = gate * jnp.dot(o, wb_ref[n], preferred_element_type=F32)
        merged = term if merged is None else merged + term
    y = jnp.dot(merged.astype(BF16), wo_ref[...], preferred_element_type=F32)
    out_ref[...] = xf + _rms(y, gpost_ref[...])


def _merge(x2, oa, ob, oc, lc, od, gpre, wg, wb, wo, gpost, seq, tm):
    n, dm = x2.shape
    nt = seq // tm
    row = lambda cols: pl.BlockSpec((tm, cols), lambda i: (i, 0))
    bw = 2 * LANES
    cls = lambda a: pl.BlockSpec((None, tm // (a.shape[2] // bw), a.shape[2]), lambda i: (i // nt, i % nt, 0))
    return pl.pallas_call(
        _merge_kernel,
        grid=(n // tm,),
        in_specs=[row(dm), row(bw), row(bw), row(bw), cls(oc[1]), cls(oc[2]), row(bw), cls(lc[1]), cls(lc[2]),
                  row(bw), _resident((1, dm)), _resident(wg.shape), _resident(wb.shape),
                  _resident(wo.shape), _resident((1, dm))],
        out_specs=row(dm),
        out_shape=jax.ShapeDtypeStruct((n, dm), F32),
        scratch_shapes=[pltpu.VMEM((8, tm, LANES), F32)],
        compiler_params=_cparams(("parallel",), 56),
        name="merge",
    )(x2, oa, ob, oc[0], oc[1], oc[2], lc[0], lc[1], lc[2], od, gpre, wg, wb, wo, gpost)


def _ffn_kernel(x_ref, g_ref, wg_ref, wu_ref, wd_ref, p_ref, gf_ref, wpg_ref, wpp_ref, gp_ref,
                out_ref, h_scr, acc_scr):
    j = pl.program_id(1)

    @pl.when(j == 0)
    def _():
        h_scr[...] = _rms(x_ref[...], g_ref[...]).astype(BF16)
        acc_scr[...] = jnp.zeros_like(acc_scr)

    h = h_scr[...]
    a = jnp.dot(h, wg_ref[...], preferred_element_type=F32)
    u = jnp.dot(h, wu_ref[...], preferred_element_type=F32)
    act = (a * _sigmoid(a) * u).astype(BF16)
    acc_scr[...] += jnp.dot(act, wd_ref[...], preferred_element_type=F32)

    @pl.when(j == pl.num_programs(1) - 1)
    def _():
        _post_math(x_ref[...], acc_scr[...], p_ref, gf_ref, wpg_ref, wpp_ref, gp_ref, out_ref)


def _ffn(x2, gain, wg, wu, wd, p2, gf, wpg, wpp, gp, tm, tf):
    n, dm = x2.shape
    dff = wg.shape[1]
    return pl.pallas_call(
        _ffn_kernel,
        grid=(n // tm, dff // tf),
        in_specs=[pl.BlockSpec((tm, dm), lambda i, j: (i, 0)),
                  pl.BlockSpec((1, dm), lambda i, j: (0, 0)),
                  pl.BlockSpec((dm, tf), lambda i, j: (0, j)),
                  pl.BlockSpec((dm, tf), lambda i, j: (0, j)),
                  pl.BlockSpec((tf, dm), lambda i, j: (j, 0)),
                  pl.BlockSpec((tm, p2.shape[1]), lambda i, j: (i, 0)),
                  _resident((1, dm)), _resident(wpg.shape), _resident(wpp.shape), _resident((1, dm))],
        out_specs=pl.BlockSpec((tm, dm), lambda i, j: (i, 0)),
        out_shape=jax.ShapeDtypeStruct((n, dm), F32),
        scratch_shapes=[pltpu.VMEM((tm, dm), BF16), pltpu.VMEM((tm, dm), F32)],
        compiler_params=_cparams(("parallel", "arbitrary"), 56),
        name="ffn",
    )(x2, gain, wg, wu, wd, p2, gf, wpg, wpp, gp)


def _post_math(xf, f, p_ref, gf_ref, wpg_ref, wpp_ref, gp_ref, out_ref):
    x2 = xf + _rms(f, gf_ref[...])
    gate = _sigmoid(jnp.dot(x2.astype(BF16), wpg_ref[...], preferred_element_type=F32))
    e = jnp.dot(p_ref[...].astype(BF16), wpp_ref[...], preferred_element_type=F32) * gate
    out_ref[...] = x2 + _rms(e, gp_ref[...])


def _router_kernel(x_ref, g_ref, wr_ref, br_ref, hs_ref, info_ref):
    hf = _rms(x_ref[...], g_ref[...])
    hb = hf.astype(BF16)
    tm, dm = hf.shape
    sub = dm // LANES
    for s in range(sub):
        hs_ref[pl.ds(s, tm, stride=sub), :] = hf[:, s * LANES:(s + 1) * LANES]
    hl = (hf - hb.astype(F32)).astype(BF16)
    whi, wlo = wr_ref[0], wr_ref[1]
    logits = (jnp.dot(hb, whi, preferred_element_type=F32) + jnp.dot(hl, whi, preferred_element_type=F32)
              + jnp.dot(hb, wlo, preferred_element_type=F32)) + br_ref[...]
    lane = lax.broadcasted_iota(jnp.int32, logits.shape, 1).astype(F32)
    m1 = jnp.max(logits, axis=-1, keepdims=True)
    i1 = jnp.min(jnp.where(logits == m1, lane, float(LANES)), axis=-1, keepdims=True)
    rest = jnp.where(lane == i1, NEG_INF, logits)
    m2 = jnp.max(rest, axis=-1, keepdims=True)
    i2 = jnp.min(jnp.where(rest == m2, lane, float(LANES)), axis=-1, keepdims=True)
    e2 = jnp.exp(m2 - m1)
    g1 = 1.0 / (1.0 + e2)
    g2 = e2 / (1.0 + e2)
    info = jnp.where(lane == 0.0, i1, jnp.where(lane == 1.0, i2, jnp.where(lane == 2.0, g1, g2)))
    info_ref[...] = info


def _router(x2, gain, wr, br, tm):
    n, dm = x2.shape
    row = lambda cols: pl.BlockSpec((tm, cols), lambda i: (i, 0))
    return pl.pallas_call(
        _router_kernel,
        grid=(n // tm,),
        in_specs=[row(dm), _resident((1, dm)), _resident(wr.shape), _resident((1, LANES))],
        out_specs=[pl.BlockSpec((tm * (dm // LANES), LANES), lambda i: (i, 0)), row(LANES)],
        out_shape=[jax.ShapeDtypeStruct((n * (dm // LANES), LANES), F32),
                   jax.ShapeDtypeStruct((n, LANES), F32)],
        compiler_params=_cparams(("parallel",), 32),
        name="router",
    )(x2, gain, wr, br)


def _experts_kernel(be_ref, src0_ref, srcn_ref, dstp_ref, dstl_ref, hs_hbm, wg_ref, wu_ref, wd_ref,
                    yt_hbm, xg, xb, acc, ys, gsem, ssem, *, tm, sub):
    i = pl.program_id(0)
    j = pl.program_id(1)
    nb = pl.num_programs(0)
    nf = pl.num_programs(1)
    slot = i % 2
    other = 1 - slot

    def gather(idx_ref, r, dslot):
        src = pl.multiple_of(idx_ref[0, r] * sub, sub)
        dst = pl.multiple_of(r * sub, sub)
        return pltpu.make_async_copy(hs_hbm.at[pl.ds(src, sub)], xg.at[dslot, pl.ds(dst, sub)],
                                     gsem.at[dslot])

    def scatter(idx_ref, r, sslot):
        src = pl.multiple_of(r * sub, sub)
        dst = pl.multiple_of(idx_ref[0, r] * sub, sub)
        return pltpu.make_async_copy(ys.at[sslot, pl.ds(src, sub)], yt_hbm.at[pl.ds(dst, sub)],
                                     ssem.at[sslot])

    def wait_gather(dslot):
        pltpu.make_async_copy(hs_hbm.at[pl.ds(0, tm * sub)], xg.at[dslot], gsem.at[dslot]).wait()

    def wait_scatter(sslot):
        pltpu.make_async_copy(ys.at[sslot], yt_hbm.at[pl.ds(0, tm * sub)], ssem.at[sslot]).wait()

    @pl.when((i == 0) & (j == 0))
    def _():
        ys[1] = jnp.zeros(ys.shape[1:], ys.dtype)

        def start(r, c):
            gather(src0_ref, r, 0).start()
            return c

        lax.fori_loop(0, tm, start, 0)

    @pl.when(j == 0)
    def _():
        wait_gather(slot)
        for s in range(sub):
            xb[:, s * LANES:(s + 1) * LANES] = xg[slot, pl.ds(s, tm, stride=sub), :].astype(BF16)
        acc[...] = jnp.zeros_like(acc)

    @pl.when(i < be_ref[nb])
    def _():
        x = xb[...]
        a = jnp.dot(x, wg_ref[...], preferred_element_type=F32)
        u = jnp.dot(x, wu_ref[...], preferred_element_type=F32)
        act = (a * _sigmoid(a) * u).astype(BF16)
        acc[...] += jnp.dot(act, wd_ref[...], preferred_element_type=F32)

    @pl.when(j == 0)
    def _():
        for t in range(tm):
            gather(srcn_ref, t, other).start()

    @pl.when(j == nf - 1)
    def _():
        for t in range(tm):
            scatter(dstp_ref, t, other).start()

    @pl.when(j == nf - 1)
    def _():
        @pl.when(i >= 1)
        def _():
            wait_scatter(slot)

        for s in range(sub):
            ys[slot, pl.ds(s, tm, stride=sub), :] = acc[:, s * LANES:(s + 1) * LANES]

        @pl.when(i == nb - 1)
        def _():
            def start(r, c):
                scatter(dstl_ref, r, slot).start()
                return c

            lax.fori_loop(0, tm, start, 0)
            wait_scatter(slot)
            wait_scatter(other)
            wait_gather(other)


def _experts(hs, blk_e, src_tok, dst_row, n_slabs, wg, wu, wd, tm, tf):
    nb = src_tok.shape[0]
    dm, dff = wg.shape[1], wg.shape[2]
    sub = dm // LANES
    nf = dff // tf
    smem = lambda imap: pl.BlockSpec((None, 1, tm), imap, memory_space=pltpu.SMEM)
    grid_spec = pltpu.PrefetchScalarGridSpec(
        num_scalar_prefetch=1,
        grid=(nb, nf),
        in_specs=[smem(lambda i, j, be: (0, 0, 0)),
                  smem(lambda i, j, be: (jnp.minimum(i + 1, nb - 1), 0, 0)),
                  smem(lambda i, j, be: (i, 0, 0)),
                  smem(lambda i, j, be: (nb, 0, 0)),
                  pl.BlockSpec(memory_space=pl.ANY),
                  pl.BlockSpec((None, dm, tf), lambda i, j, be: (be[i], 0, j)),
                  pl.BlockSpec((None, dm, tf), lambda i, j, be: (be[i], 0, j)),
                  pl.BlockSpec((None, tf, dm), lambda i, j, be: (be[i], j, 0))],
        out_specs=pl.BlockSpec(memory_space=pl.ANY),
        scratch_shapes=[pltpu.VMEM((2, tm * sub, LANES), F32), pltpu.VMEM((tm, dm), BF16),
                        pltpu.VMEM((tm, dm), F32), pltpu.VMEM((2, tm * sub, LANES), F32),
                        pltpu.SemaphoreType.DMA((2,)), pltpu.SemaphoreType.DMA((2,))],
    )
    return pl.pallas_call(
        functools.partial(_experts_kernel, tm=tm, sub=sub),
        grid_spec=grid_spec,
        out_shape=jax.ShapeDtypeStruct((n_slabs * sub, LANES), F32),
        compiler_params=_cparams(("arbitrary", "arbitrary"), 48),
        name="experts",
    )(blk_e, src_tok, src_tok, dst_row, dst_row, hs, wg, wu, wd)


def _combine_kernel(y_ref, x_ref, gt_ref, p_ref, gf_ref, wpg_ref, wpp_ref, gp_ref, out_ref):
    tt, dm = x_ref.shape
    sub = dm // LANES
    gt = gt_ref[...]

    def rows(slot):
        return jnp.concatenate([y_ref[pl.ds(slot * sub + s, tt, stride=TOP_K * sub), :] for s in range(sub)],
                               axis=1)

    f = gt[:, 2:3] * rows(0) + gt[:, 3:4] * rows(1)
    _post_math(x_ref[...], f, p_ref, gf_ref, wpg_ref, wpp_ref, gp_ref, out_ref)


def _combine(yt, info, x2, p2, gf, wpg, wpp, gp, tt):
    n, dm = x2.shape
    row = lambda cols: pl.BlockSpec((tt, cols), lambda i: (i, 0))
    return pl.pallas_call(
        _combine_kernel,
        grid=(n // tt,),
        in_specs=[pl.BlockSpec((tt * TOP_K * (dm // LANES), LANES), lambda i: (i, 0)),
                  row(dm), row(LANES), row(p2.shape[1]), _resident((1, dm)),
                  _resident(wpg.shape), _resident(wpp.shape), _resident((1, dm))],
        out_specs=row(dm),
        out_shape=jax.ShapeDtypeStruct((n, dm), F32),
        compiler_params=_cparams(("parallel",), 48),
        name="combine",
    )(yt, x2, info, p2, gf, wpg, wpp, gp)


def _moe(x2, p2, gain, wr, br, wg, wu, wd, gf, wpg, wpp, gp, *, tm_r, tt, tm_e, tf_e):
    n, _ = x2.shape
    hs, info = _router(x2, gain, wr, br, tm_r)
    n_asg = n * TOP_K
    top_e = info[:, 0:TOP_K].astype(jnp.int32)
    e_flat = top_e.reshape(n_asg)
    onehot = (e_flat[:, None] == jnp.arange(N_EXPERTS, dtype=jnp.int32)[None, :]).astype(jnp.int32)
    csum = jnp.cumsum(onehot, axis=0)
    rank = jnp.sum((csum - onehot) * onehot, axis=1)
    counts = csum[-1]
    padded = ((counts + tm_e - 1) // tm_e) * tm_e
    pend = jnp.cumsum(padded)
    pstart = pend - padded
    dest = (pstart[e_flat] + rank).astype(jnp.int32)
    n_blocks = -(-n_asg // tm_e) + N_EXPERTS
    n_rows = n_blocks * tm_e
    blk_e = jnp.clip(jnp.searchsorted(pend, jnp.arange(n_blocks, dtype=jnp.int32) * tm_e, side="right"),
                     0, N_EXPERTS - 1).astype(jnp.int32)
    blk_e = jnp.concatenate([blk_e, (pend[-1:] // tm_e).astype(jnp.int32)])
    asg = jnp.full((n_rows,), -1, jnp.int32).at[dest].set(jnp.arange(n_asg, dtype=jnp.int32),
                                                          unique_indices=True)
    is_pad = asg < 0
    pad_rank = jnp.cumsum(is_pad.astype(jnp.int32)) - 1
    src_tok = jnp.where(is_pad, 0, asg // TOP_K).reshape(n_blocks, 1, tm_e)
    dst_row = jnp.where(is_pad, n_asg + pad_rank, asg)
    spare = n_asg + (n_rows - n_asg) + jnp.arange(tm_e, dtype=jnp.int32)
    dst_row = jnp.concatenate([spare, dst_row]).reshape(n_blocks + 1, 1, tm_e)
    n_slabs = n_asg + (n_rows - n_asg) + tm_e
    yt = _experts(hs, blk_e, src_tok, dst_row, n_slabs, wg, wu, wd, tm_e, tf_e)
    return _combine(yt, info, x2, p2, gf, wpg, wpp, gp, tt)


def _tables(seq):
    pos = np.arange(seq, dtype=np.int32)

    def cs(p, half):
        inv = np.power(np.float32(ROPE_THETA), -np.arange(half, dtype=np.float32) / np.float32(half))
        ang = p.astype(np.float32)[:, None] * inv[None, :].astype(np.float32)
        return jnp.asarray(np.cos(ang), F32), jnp.asarray(np.sin(ang), F32)

    ones = lambda w: jnp.ones((seq, w), F32)
    zeros = lambda w: jnp.zeros((seq, w), F32)
    (c, s), (cr, sr), (cc, sc), (ca, sa) = (
        cs(pos, HEAD_DIM // 2), cs(pos // GRID_W, HEAD_DIM // 4), cs(pos % GRID_W, HEAD_DIM // 4),
        cs(pos, A_ROPE // 2))
    full = jnp.stack([jnp.tile(jnp.concatenate([c, c], 1), (1, 2)),
                      jnp.tile(jnp.concatenate([-s, s], 1), (1, 2))])
    axial = jnp.stack([jnp.tile(jnp.concatenate([cr, cc, cr, cc], 1), (1, 2)),
                       jnp.tile(jnp.concatenate([-sr, -sc, sr, sc], 1), (1, 2))])
    slot_c = jnp.concatenate([ca, ones(16), ca, ones(16)], 1)
    slot_s = jnp.concatenate([-sa, zeros(16), sa, zeros(16)], 1)
    a_k = jnp.stack([jnp.concatenate([slot_c, ones(HALF)], 1), jnp.concatenate([slot_s, zeros(HALF)], 1)])
    a_q = jnp.stack([jnp.concatenate([ones(HALF), slot_c], 1), jnp.concatenate([zeros(HALF), slot_s], 1)])
    return {"full": full, "axial": axial, "a_k": a_k, "a_q": a_q}


_AXIAL_PERM = tuple(list(range(0, 16)) + list(range(32, 48)) + list(range(16, 32)) + list(range(48, 64)))


def _dup_heads(w, n_heads, perm=None):
    rows = w.shape[0]
    w = w.reshape(rows, n_heads, HEAD_DIM)
    if perm is not None:
        w = w[:, :, perm]
    return jnp.stack([w, w], axis=2).reshape(rows, n_heads * 2 * HEAD_DIM)


def _assemble_w_in(w):
    dm = w.shape[0]
    perm = jnp.array(_AXIAL_PERM, jnp.int32)
    a, b, c, d = w[:, 0:SRC_A], w[:, SRC_A:SRC_B], w[:, SRC_B:SRC_C], w[:, SRC_C:SRC_D]
    z = lambda n: jnp.zeros((dm, n), w.dtype)
    kr = a[:, 384:416]
    a_seg = jnp.concatenate([a[:, 0:384], kr[:, 0:16], z(16), kr[:, 16:32], z(16), z(HALF)], axis=1)
    bq = b[:, 0:256].reshape(dm, 4, HEAD_DIM)[:, :, perm].reshape(dm, 256)
    b_seg = jnp.concatenate([bq, _dup_heads(b[:, 256:384], 2, perm)], axis=1)
    d_seg = jnp.concatenate([d[:, 0:256], _dup_heads(d[:, 256:384], 2), _dup_heads(d[:, 384:512], 2)], axis=1)
    w_vbt = _dup_heads(b[:, 384:512], 2).T.astype(BF16)
    return jnp.concatenate([a_seg, b_seg, c, d_seg], axis=1).astype(BF16), w_vbt


def _assemble_a(w_uq, w_ukv):
    zq = lambda n: jnp.zeros((w_uq.shape[0], n), w_uq.dtype)
    zk = lambda n: jnp.zeros((w_ukv.shape[0], n), w_ukv.dtype)
    dq = A_NOPE + A_ROPE
    q_cols, k_cols, v_cols = [], [], []
    for hh in range(A_HEADS):
        q = w_uq[:, hh * dq:(hh + 1) * dq]
        q_cols += [q[:, 0:A_NOPE], q[:, A_NOPE:A_NOPE + 16], zq(16), q[:, A_NOPE + 16:dq], zq(16)]
        kv = w_ukv[:, hh * (A_NOPE + A_V):(hh + 1) * (A_NOPE + A_V)]
        k_cols += [kv[:, 0:A_NOPE], zk(HALF)]
        v_cols += [kv[:, A_NOPE:A_NOPE + A_V]]
    cat = lambda cols: jnp.concatenate(cols, axis=1).astype(BF16)
    return cat(q_cols), cat(k_cols), cat(v_cols).T


def _gain_pair(g, perm=None):
    if perm is not None:
        g = g[jnp.array(perm, jnp.int32)]
    return jnp.tile(g, 2).reshape(1, LANES).astype(F32)


def kernel(x, p, w_in, a_qa_g, a_kva_g, a_w_uq, a_w_ukv, b_q_g, b_k_g, d_sink, w_branch, w_out,
           mix_pre_g, mix_post_g, ffn_pre_g, ffn_post_g, ffn_w_gate, ffn_w_up, ffn_w_down,
           router_w, router_b, moe_w_gate, moe_w_up, moe_w_down, ple_w_proj, ple_w_gate, ple_post_g):
    bsz, seq, dm = x.shape
    depth = w_in.shape[0]
    n = bsz * seq
    tm = min(512, seq)
    tq_flash = min(512, seq)
    tk_flash = min(1024, seq // 4)
    tabs = _tables(seq)
    row = lambda g: g.reshape(1, -1).astype(F32)
    x2 = x.reshape(n, dm)

    for i in range(depth):
        w_in16 = w_in[i].astype(BF16)
        w_all, wvb = _assemble_w_in(w_in16)
        wuq, wk, wv = _assemble_a(a_w_uq[i], a_w_ukv[i])
        z, qa, ka, vat, vbt, zc1, zc2 = _in_proj(
            x2, row(mix_pre_g[i]), w_all, tabs,
            _gain_pair(b_q_g[i], _AXIAL_PERM), _gain_pair(b_k_g[i], _AXIAL_PERM),
            row(a_qa_g[i]), row(a_kva_g[i]), wuq, wk, wv, wvb, seq, tm)
        z3 = z.reshape(bsz, seq, ZMAIN_COLS)
        o_a = _flash(qa, ka, vat, q_col0=0, k_col0=0, v_row0=0, seq=seq, packed=False,
                     tq=tq_flash, tk=tk_flash)
        o_b = _flash(z, z, vbt, q_col0=ZB, k_col0=ZB + 256, v_row0=0, seq=seq, packed=True,
                     tq=tq_flash, tk=tk_flash)
        o_c, l_c = [], []
        for (win, dil), (src, col0, rc) in zip(C_PATTERNS, ((z3, ZC, ZMAIN_COLS), (zc1, 0, QKV_SEG), (zc2, 0, QKV_SEG))):
            og, lg = _banded(src, src, src, q_col0=col0, k_col0=col0 + 256, v_col0=col0 + 512,
                             row_cols=rc, dil=dil, seq=seq, hw=win // (2 * dil), tq=min(512, seq // dil),
                             want_lse=True, out_dtype=F32)
            o_c.append(og)
            l_c.append(lg)
        o_c[0] = o_c[0].reshape(n, 2 * LANES)
        l_c[0] = l_c[0].reshape(n, 2 * LANES)
        (o_d,) = _banded(z3, z3, z3, q_col0=ZD, k_col0=ZD + 256, v_col0=ZD + 512,
                         row_cols=ZMAIN_COLS, dil=1, seq=seq, hw=D_HALF_WINDOW, tq=min(512, seq),
                         sink=d_sink[i].astype(F32))
        o_d = o_d.reshape(n, 2 * LANES)
        wg_gate = w_in16[:, SRC_D:]
        x2 = _merge(x2, o_a, o_b, o_c, l_c, o_d, row(mix_pre_g[i]), wg_gate,
                    w_branch[i].astype(BF16), w_out[i].astype(BF16), row(mix_post_g[i]), seq, tm)

        p2 = p[i].reshape(n, -1)
        wpg = ple_w_gate[i].astype(BF16)
        wpp = ple_w_proj[i].astype(BF16)
        j = i // 2
        if i % 2 == 0:
            x2 = _ffn(x2, row(ffn_pre_g[i]), ffn_w_gate[j].astype(BF16), ffn_w_up[j].astype(BF16),
                      ffn_w_down[j].astype(BF16), p2, row(ffn_post_g[i]), wpg, wpp, row(ple_post_g[i]),
                      min(1024, n), 512)
        else:
            wr32 = jnp.zeros((dm, LANES), F32).at[:, :N_EXPERTS].set(router_w[j].astype(F32))
            wr_hi = wr32.astype(BF16)
            wr = jnp.stack([wr_hi, (wr32 - wr_hi.astype(F32)).astype(BF16)])
            br = jnp.full((1, LANES), NEG_INF, F32).at[0, :N_EXPERTS].set(router_b[j].astype(F32))
            x2 = _moe(x2, p2, row(ffn_pre_g[i]), wr, br, moe_w_gate[j].astype(BF16),
                      moe_w_up[j].astype(BF16), moe_w_down[j].astype(BF16), row(ffn_post_g[i]),
                      wpg, wpp, row(ple_post_g[i]), tm_r=tm, tt=min(256, n), tm_e=512, tf_e=1792)
    return x2.reshape(bsz, seq, dm)
```

```python
import functools

import jax
import jax.numpy as jnp
import numpy as np
from jax import lax
from jax.experimental import pallas as pl
from jax.experimental.pallas import tpu as pltpu

F32 = jnp.float32
BF16 = jnp.bfloat16

GRID_W = 64
HEAD_DIM = 64
ROPE_THETA = 10000.0
NORM_EPS = 1e-6
NEG_INF = -1e30
A_HEADS = 4
A_Q_RANK = 256
A_KV_RANK = 128
A_NOPE = 64
A_ROPE = 32
A_V = 64
C_PATTERNS = ((128, 1), (512, 4), (2048, 16))
D_HALF_WINDOW = 128
N_EXPERTS = 8
TOP_K = 2

LANES = 128
HALF = 64
VMEM_MB = 1024 * 1024
LOG2E = 1.4426950408889634
BAND_SUB = 128

PAIR = 2 * LANES
QKV_SEG = 3 * PAIR
SRC_A = A_Q_RANK + A_KV_RANK + A_ROPE
SRC_B = SRC_A + PAIR + 2 * LANES
SRC_C = SRC_B + len(C_PATTERNS) * QKV_SEG
SRC_D = SRC_C + PAIR + 2 * LANES

ZB = 0
ZC = 512
ZD = 512 + QKV_SEG
ZMAIN_COLS = ZD + QKV_SEG
A_SEG = 512
B_SEG = 512


def _cparams(sem, vmem_mb):
    return pltpu.CompilerParams(dimension_semantics=sem, vmem_limit_bytes=vmem_mb * VMEM_MB)


def _resident(shape):
    nd = len(shape)
    return pl.BlockSpec(shape, lambda *_: (0,) * nd, pipeline_mode=pl.Buffered(1))


def _rms(xf, g):
    return xf * lax.rsqrt(jnp.mean(xf * xf, axis=-1, keepdims=True) + NORM_EPS) * g


def _sigmoid(x):
    return 1.0 / (1.0 + jnp.exp(-x))


def _swap32(a):
    lane = lax.broadcasted_iota(jnp.int32, a.shape, 1)
    fwd = pltpu.roll(a, LANES - 32, 1)
    bwd = pltpu.roll(a, 32, 1)
    return jnp.where((lane & 32) == 0, fwd, bwd)


def _rope(a, cos, sin):
    outs = []
    for c in range(a.shape[1] // LANES):
        ch = a[:, c * LANES:(c + 1) * LANES]
        outs.append(ch * cos + _swap32(ch) * sin)
    return outs[0] if len(outs) == 1 else jnp.concatenate(outs, axis=1)


def _head_norm(a, g, bd):
    outs = []
    for c in range(a.shape[1] // LANES):
        ch = a[:, c * LANES:(c + 1) * LANES]
        sq = ch * ch
        hi = sq.astype(BF16)
        lo = (sq - hi.astype(F32)).astype(BF16)
        ms = (jnp.dot(hi, bd, preferred_element_type=F32)
              + jnp.dot(lo, bd, preferred_element_type=F32))
        outs.append(ch * lax.rsqrt(ms + NORM_EPS) * g)
    return outs[0] if len(outs) == 1 else jnp.concatenate(outs, axis=1)


def _in_proj_kernel(x_ref, g_ref, w_ref, tabf_ref, tabx_ref, taba_ref, tabq_ref,
                    bqg_ref, bkg_ref, aqg_ref, akvg_ref, wuq_ref, wk_ref, wv_ref, wvb_ref,
                    z_ref, qa_ref, ka_ref, vat_ref, vbt_ref, zc1_ref, zc2_ref, cls_scr):
    h = _rms(x_ref[...], g_ref[...]).astype(BF16)
    cf, sf = tabf_ref[0], tabf_ref[1]
    cx, sx = tabx_ref[0], tabx_ref[1]
    q_scale = HEAD_DIM ** -0.5

    acc = jnp.dot(h, w_ref[:, 0:A_SEG], preferred_element_type=F32)
    nq = _rms(acc[:, 0:A_Q_RANK], aqg_ref[...]).astype(BF16)
    nkv = _rms(acc[:, A_Q_RANK:A_Q_RANK + A_KV_RANK], akvg_ref[...]).astype(BF16)
    kr = _rope(acc[:, 384:512], taba_ref[0], taba_ref[1]).astype(BF16)
    qa = jnp.dot(nq, wuq_ref[...], preferred_element_type=F32)
    qa = _rope(qa, tabq_ref[0], tabq_ref[1]) * ((A_NOPE + A_ROPE) ** -0.5 * LOG2E)
    qa_ref[...] = qa.astype(qa_ref.dtype)
    r = lax.broadcasted_iota(jnp.int32, (LANES, A_HEADS * LANES), 0)
    c = lax.broadcasted_iota(jnp.int32, (LANES, A_HEADS * LANES), 1)
    place = jnp.where((r < HALF) & ((c & (LANES - 1)) == r + HALF), 1.0, 0.0).astype(BF16)
    ka = (jnp.dot(nkv, wk_ref[...], preferred_element_type=F32)
          + jnp.dot(kr, place, preferred_element_type=F32))
    ka_ref[...] = ka.astype(ka_ref.dtype)
    nt_dims = (((1,), (1,)), ((), ()))
    vat_ref[...] = lax.dot_general(wv_ref[...], nkv, nt_dims, preferred_element_type=F32).astype(vat_ref.dtype)
    vbt_ref[...] = lax.dot_general(wvb_ref[...], h, nt_dims, preferred_element_type=F32).astype(vbt_ref.dtype)

    rr = lax.broadcasted_iota(jnp.int32, (LANES, LANES), 0)
    cc = lax.broadcasted_iota(jnp.int32, (LANES, LANES), 1)
    bd = jnp.where((rr >> 6) == (cc >> 6), 1.0 / HEAD_DIM, 0.0).astype(BF16)
    acc = jnp.dot(h, w_ref[:, A_SEG:A_SEG + B_SEG], preferred_element_type=F32)
    q = _rope(_head_norm(acc[:, 0:256], bqg_ref[...], bd), cx, sx) * (q_scale * LOG2E)
    k = _rope(_head_norm(acc[:, 256:512], bkg_ref[...], bd), cx, sx)
    z_ref[:, ZB:ZB + 256] = q.astype(z_ref.dtype)
    z_ref[:, ZB + 256:ZB + 512] = k.astype(z_ref.dtype)

    tm = h.shape[0]
    for widx, zoff, cls_ref, dil in ((0, ZC, None, 1), (1, 0, zc1_ref, C_PATTERNS[1][1]),
                                     (2, 0, zc2_ref, C_PATTERNS[2][1]), (3, ZD, None, 1)):
        base = A_SEG + B_SEG + widx * QKV_SEG
        acc = jnp.dot(h, w_ref[:, base:base + QKV_SEG], preferred_element_type=F32)
        q = _rope(acc[:, 0:256], cf, sf) * q_scale
        k = _rope(acc[:, 256:512], cf, sf)
        if cls_ref is None:
            z_ref[:, zoff:zoff + 256] = q.astype(z_ref.dtype)
            z_ref[:, zoff + 256:zoff + 512] = k.astype(z_ref.dtype)
            z_ref[:, zoff + 512:zoff + QKV_SEG] = acc[:, 512:QKV_SEG].astype(z_ref.dtype)
        else:
            qkv = (q[:, 0:LANES], q[:, LANES:], k[:, 0:LANES], k[:, LANES:],
                   acc[:, 512:512 + LANES], acc[:, 512 + LANES:QKV_SEG])
            for ch, val in enumerate(qkv):
                cls_scr[ch] = val
            for c in range(dil):
                for ch in range(len(qkv)):
                    col = c * QKV_SEG + ch * LANES
                    cls_ref[:, col:col + LANES] = cls_scr[ch, pl.ds(c, tm // dil, stride=dil), :].astype(cls_ref.dtype)


def _in_proj(x2, gain, w, tabs, bqg, bkg, aqg, akvg, wuq, wk, wv, wvb, seq, tm):
    n, dm = x2.shape
    nt = seq // tm
    bsz = n // seq
    d1, d2 = C_PATTERNS[1][1], C_PATTERNS[2][1]
    tab_spec = pl.BlockSpec((2, tm, LANES), lambda i: (0, i % nt, 0))
    row = lambda cols: pl.BlockSpec((tm, cols), lambda i: (i, 0))
    vt_spec = pl.BlockSpec((None, 2 * LANES, tm), lambda i: (i // nt, 0, i % nt))
    cls_spec = lambda d: pl.BlockSpec((None, tm // d, d * QKV_SEG), lambda i: (i // nt, i % nt, 0))
    return pl.pallas_call(
        _in_proj_kernel,
        grid=(n // tm,),
        in_specs=[row(dm), _resident((1, dm)), _resident(w.shape),
                  tab_spec, tab_spec, tab_spec, tab_spec,
                  _resident((1, LANES)), _resident((1, LANES)),
                  _resident((1, A_Q_RANK)), _resident((1, A_KV_RANK)),
                  _resident(wuq.shape), _resident(wk.shape), _resident(wv.shape), _resident(wvb.shape)],
        out_specs=[row(ZMAIN_COLS), row(A_HEADS * LANES), row(A_HEADS * LANES),
                   vt_spec, vt_spec, cls_spec(d1), cls_spec(d2)],
        out_shape=[jax.ShapeDtypeStruct((n, ZMAIN_COLS), BF16),
                   jax.ShapeDtypeStruct((n, A_HEADS * LANES), BF16),
                   jax.ShapeDtypeStruct((n, A_HEADS * LANES), BF16),
                   jax.ShapeDtypeStruct((bsz, 2 * LANES, seq), BF16),
                   jax.ShapeDtypeStruct((bsz, 2 * LANES, seq), BF16),
                   jax.ShapeDtypeStruct((bsz, seq // d1, d1 * QKV_SEG), BF16),
                   jax.ShapeDtypeStruct((bsz, seq // d2, d2 * QKV_SEG), BF16)],
        scratch_shapes=[pltpu.VMEM((QKV_SEG // LANES, tm, LANES), F32)],
        compiler_params=_cparams(("parallel",), 48),
        name="in_proj",
    )(x2, gain, w, tabs["full"], tabs["axial"], tabs["a_k"], tabs["a_q"],
      bqg, bkg, aqg, akvg, wuq, wk, wv, wvb)


FLASH_SAFE_EXP = 64.0


def _head_queries(q_ref, packed):
    lane = lax.broadcasted_iota(jnp.int32, (1, LANES), 1)
    if packed:
        return [jnp.where((lane < HALF) == (r == 0), q_ref[...], 0) for r in range(2)], [0, 0]
    return [q_ref[:, r * LANES:(r + 1) * LANES] for r in range(2)], [0, LANES]


def _flash_single_pass(q_ref, k_ref, vt_ref, s_a, s_b, *, tk, packed):
    tq = q_ref.shape[0]
    nk = k_ref.shape[0] // tk
    qs, kcs = _head_queries(q_ref, packed)
    qts = [q.astype(F32).T.astype(BF16) for q in qs]
    row = lax.broadcasted_iota(jnp.int32, (LANES, tq), 0)
    klane = lax.broadcasted_iota(jnp.int32, (tk, LANES), 1)
    k_one = jnp.where(klane == 0, 1.0, 0.0).astype(BF16)
    v_one = jnp.ones((16, tk), BF16)

    def scores(c, refs, dst):
        ks = pl.multiple_of(c * tk, tk)
        for r in range(2):
            k = jnp.concatenate([k_ref[pl.ds(ks, tk), kcs[r]:kcs[r] + LANES], k_one], axis=1)
            bias = jnp.where(row == 0, -refs[r], 0.0).astype(BF16)
            qt = jnp.concatenate([qts[r], bias], axis=0)
            dst[r] = jnp.dot(k, qt, preferred_element_type=F32)

    def consume(c, src, refs, sts, first):
        ks = pl.multiple_of(c * tk, tk)
        new = []
        for r in range(2):
            big, base, l, acc, hi, lo = sts[r]
            s = src[r]
            cmax = jnp.max(s, axis=0, keepdims=True)
            p = jnp.exp2(s).astype(BF16)
            lhs = jnp.concatenate([vt_ref[r * HALF:(r + 1) * HALF, pl.ds(ks, tk)], v_one], axis=0)
            pv = jnp.dot(lhs, p, preferred_element_type=F32)
            alpha = jnp.exp2(base - refs[r])
            l = l * alpha + pv[HALF:HALF + 1]
            acc = acc * alpha + pv[0:HALF]
            big = jnp.maximum(big, refs[r] + cmax)
            hi = jnp.maximum(hi, cmax)
            if first:
                lo = jnp.minimum(lo, cmax)
            new.append((big, refs[r], l, acc, hi, lo))
        return tuple(new)

    def ref_of(sts):
        return [st[0].astype(BF16).astype(F32) for st in sts]

    zero = jnp.zeros((1, tq), F32)
    sts = tuple((jnp.full((1, tq), NEG_INF, F32), zero, zero, jnp.zeros((HALF, tq), F32),
                 jnp.full((1, tq), NEG_INF, F32), jnp.full((1, tq), -NEG_INF, F32)) for _ in range(2))
    zeros2 = [zero, zero]
    scores(0, zeros2, s_a)
    scores(1, zeros2, s_b)
    sts = consume(0, s_a, zeros2, sts, True)
    ra = ref_of(sts)
    scores(2, ra, s_a)
    sts = consume(1, s_b, zeros2, sts, True)

    def pair(jj, carry):
        sts, ra = carry
        c0 = 2 * jj
        rb = ref_of(sts)
        scores(c0 + 1, rb, s_b)
        sts = consume(c0, s_a, ra, sts, False)
        ra = ref_of(sts)
        scores(c0 + 2, ra, s_a)
        sts = consume(c0 + 1, s_b, rb, sts, False)
        return sts, ra

    sts, ra = lax.fori_loop(1, nk // 2 - 1, pair, (sts, ra))
    rb = ref_of(sts)
    scores(nk - 1, rb, s_b)
    sts = consume(nk - 2, s_a, ra, sts, False)
    sts = consume(nk - 1, s_b, rb, sts, False)
    outs, bad = [], None
    for r in range(2):
        _, _, l, acc, hi, lo = sts[r]
        outs.append(acc / l)
        b = (hi > FLASH_SAFE_EXP) | (lo < -FLASH_SAFE_EXP) | jnp.logical_not(l > 2.0 ** -FLASH_SAFE_EXP)
        bad = b if bad is None else (bad | b)
    return jnp.concatenate(outs, axis=0), bad


def _flash_kernel(q_ref, k_ref, vt_ref, o_ref, s_a, s_b, *, tk, packed):
    out, bad = _flash_single_pass(q_ref, k_ref, vt_ref, s_a, s_b, tk=tk, packed=packed)
    o_ref[...] = out.T.astype(o_ref.dtype)

    @pl.when(jnp.max(jnp.where(bad, 1.0, 0.0)) > 0.0)
    def _():
        o_ref[...] = _flash_two_pass(q_ref, k_ref, vt_ref, s_a, s_b, tk=tk, packed=packed).T.astype(o_ref.dtype)


def _flash_two_pass(q_ref, k_ref, vt_ref, s_a, s_b, *, tk, packed):
    tq = q_ref.shape[0]
    nk = k_ref.shape[0] // tk
    qs, kcs = _head_queries(q_ref, packed)

    def scores(kk, dst):
        ks = pl.multiple_of(kk * tk, tk)
        for r in range(2):
            k = k_ref[pl.ds(ks, tk), kcs[r]:kcs[r] + LANES]
            dst[r] = lax.dot_general(k, qs[r], (((1,), (1,)), ((), ())), preferred_element_type=F32)

    def consume(kk, src, carry):
        ks = pl.multiple_of(kk * tk, tk)
        new = []
        for r in range(2):
            m, l, acc = carry[r]
            s = src[r]
            m_new = jnp.maximum(m, jnp.max(s, axis=0, keepdims=True))
            alpha = jnp.exp2(m - m_new)
            p = jnp.exp2(s - m_new)
            l = alpha * l + jnp.sum(p, axis=0, keepdims=True)
            vt = vt_ref[r * HALF:(r + 1) * HALF, pl.ds(ks, tk)]
            acc = alpha * acc + jnp.dot(vt, p.astype(BF16), preferred_element_type=F32)
            new.append((m_new, l, acc))
        return tuple(new)

    def pair(jj, carry):
        c0 = 2 * jj
        scores(c0 + 1, s_b)
        carry = consume(c0, s_a, carry)
        scores(c0 + 2, s_a)
        return consume(c0 + 1, s_b, carry)

    init = tuple((jnp.full((1, tq), NEG_INF, F32), jnp.zeros((1, tq), F32), jnp.zeros((HALF, tq), F32))
                 for _ in range(2))
    scores(0, s_a)
    carry = lax.fori_loop(0, nk // 2 - 1, pair, init)
    scores(nk - 1, s_b)
    carry = consume(nk - 2, s_a, carry)
    carry = consume(nk - 1, s_b, carry)
    return jnp.concatenate([acc / l for (_, l, acc) in carry], axis=0)


def _flash(q, k, vt, *, q_col0, k_col0, v_row0, seq, packed, tq, tk):
    n = q.shape[0]
    bsz = n // seq
    qw = LANES if packed else 2 * LANES
    nq = seq // tq
    assert seq % (2 * tk) == 0 and seq // tk >= 4, "the chunk pipeline needs an even number (>= 4) of key chunks"
    k3 = k.reshape(bsz, seq, k.shape[1])
    qb, kb, vb = q_col0 // qw, k_col0 // qw, v_row0 // LANES
    return pl.pallas_call(
        functools.partial(_flash_kernel, tk=tk, packed=packed),
        grid=(bsz, 2, nq),
        in_specs=[pl.BlockSpec((tq, qw), lambda b, j, i: (b * nq + i, qb + j)),
                  pl.BlockSpec((None, seq, qw), lambda b, j, i: (b, 0, kb + j)),
                  pl.BlockSpec((None, LANES, seq), lambda b, j, i: (b, vb + j, 0))],
        out_specs=pl.BlockSpec((tq, LANES), lambda b, j, i: (b * nq + i, j)),
        out_shape=jax.ShapeDtypeStruct((n, 2 * LANES), BF16),
        scratch_shapes=[pltpu.VMEM((2, tk, tq), F32), pltpu.VMEM((2, tk, tq), F32)],
        compiler_params=_cparams(("parallel", "parallel", "parallel"), 48),
        name="flash_packed" if packed else "flash_slots",
    )(q, k3, vt)


def _banded_kernel(*refs, hw, has_sink, want_lse):
    if has_sink:
        sink_ref, q_ref, k_ref, v_ref = refs[:4]
        outs = refs[4:]
    else:
        q_ref, k_ref, v_ref = refs[:3]
        outs = refs[3:]
    o_ref = outs[0]
    tq = q_ref.shape[0]
    length = k_ref.shape[0]
    sb = min(BAND_SUB, tq)
    win = min(sb + 2 * hw, length)
    i = pl.program_id(1)
    lane = lax.broadcasted_iota(jnp.int32, (1, LANES), 1)
    first = lane < HALF
    cols = [slice(j * LANES, (j + 1) * LANES) for j in range(2)]
    subs = list(range(tq // sb))
    kss, valids = [], []
    for u in subs:
        q0 = i * tq + u * sb
        ks = pl.multiple_of(jnp.clip(q0 - hw, 0, length - win), HALF)
        qpos = q0 + lax.broadcasted_iota(jnp.int32, (sb, win), 0)
        kpos = ks + lax.broadcasted_iota(jnp.int32, (sb, win), 1)
        kss.append(ks)
        valids.append(jnp.abs(qpos - kpos) <= hw)
    chains = [(u, j, r) for u in subs for j in range(2) for r in range(2)]
    kws = {(u, j): k_ref[pl.ds(kss[u], win), cols[j]] for u in subs for j in range(2)}
    vws = {(u, j): v_ref[pl.ds(kss[u], win), cols[j]] for u in subs for j in range(2)}
    ss = []
    for u, j, r in chains:
        q = jnp.where(first == (r == 0), q_ref[u * sb:(u + 1) * sb, cols[j]], 0)
        ss.append(lax.dot_general(q, kws[u, j], (((1,), (1,)), ((), ())), preferred_element_type=F32))
    ss = [jnp.where(valids[u], s, NEG_INF) for s, (u, j, r) in zip(ss, chains)]
    ms = [jnp.max(s, axis=-1, keepdims=True) for s in ss]
    if has_sink:
        sinks = [sink_ref[2 * j + r] for u, j, r in chains]
        ms = [jnp.maximum(m, sk) for m, sk in zip(ms, sinks)]
    es = [jnp.exp(s - m) for s, m in zip(ss, ms)]
    ls = [jnp.sum(e, axis=-1, keepdims=True) for e in es]
    if has_sink:
        ls = [l + jnp.exp(sk - m) for l, sk, m in zip(ls, sinks, ms)]
    os_ = [jnp.dot(e.astype(BF16), vws[u, j], preferred_element_type=F32) / l
           for e, l, (u, j, r) in zip(es, ls, chains)]
    for n in range(0, len(chains), 2):
        u, j, _ = chains[n]
        rows = slice(u * sb, (u + 1) * sb)
        o_ref[rows, cols[j]] = jnp.where(first, os_[n], os_[n + 1]).astype(o_ref.dtype)
        if want_lse:
            lses = [jnp.broadcast_to(ms[n + r] + jnp.log(ls[n + r]), (sb, LANES)) for r in range(2)]
            outs[1][rows, cols[j]] = jnp.where(first, lses[0], lses[1])


def _banded(q, k, v, *, q_col0, k_col0, v_col0, row_cols, dil, seq, hw, tq, sink=None,
            want_lse=False, out_dtype=BF16):
    bsz, ls, _ = q.shape
    nq = ls // tq
    pw = 2 * LANES
    rb = row_cols // pw
    qb, kb, vb = q_col0 // pw, k_col0 // pw, v_col0 // pw
    in_specs = [pl.BlockSpec((None, tq, pw), lambda bc, i: (bc // dil, i, (bc % dil) * rb + qb)),
                pl.BlockSpec((None, ls, pw), lambda bc, i: (bc // dil, 0, (bc % dil) * rb + kb)),
                pl.BlockSpec((None, ls, pw), lambda bc, i: (bc // dil, 0, (bc % dil) * rb + vb))]
    args = [q, k, v]
    if sink is not None:
        in_specs = [pl.BlockSpec(memory_space=pltpu.SMEM)] + in_specs
        args = [sink] + args
    o_spec = pl.BlockSpec((None, tq, pw), lambda bc, i: (bc // dil, i, bc % dil))
    out_specs = [o_spec]
    out_shape = [jax.ShapeDtypeStruct((bsz, ls, dil * pw), out_dtype)]
    if want_lse:
        out_specs.append(o_spec)
        out_shape.append(jax.ShapeDtypeStruct((bsz, ls, dil * pw), F32))
    res = pl.pallas_call(
        functools.partial(_banded_kernel, hw=hw, has_sink=sink is not None, want_lse=want_lse),
        grid=(bsz * dil, nq),
        in_specs=in_specs,
        out_specs=out_specs,
        out_shape=out_shape,
        compiler_params=_cparams(("parallel", "parallel"), 48),
        name="banded_d%d" % dil,
    )(*args)
    return res


def _merge_kernel(x_ref, oa_ref, ob_ref, oc0_ref, oc1_ref, oc2_ref, l0_ref, l1_ref, l2_ref,
                  od_ref, gpre_ref, wg_ref, wb_ref, wo_ref, gpost_ref, out_ref, tok_scr):
    xf = x_ref[...]
    tm, dm = xf.shape
    h = _rms(xf, gpre_ref[...]).astype(BF16)
    pw = 2 * LANES
    toks = []
    for n, src in enumerate((oc1_ref, l1_ref, oc2_ref, l2_ref)):
        dil = src.shape[1] // pw
        for c in range(dil):
            for hp in range(2):
                col = c * pw + hp * LANES
                tok_scr[2 * n + hp, pl.ds(c, tm // dil, stride=dil), :] = src[:, col:col + LANES]
        toks.append(jnp.concatenate([tok_scr[2 * n], tok_scr[2 * n + 1]], axis=1))
    oc1, l1, oc2, l2 = toks
    l0 = l0_ref[...]
    mx = jnp.maximum(jnp.maximum(l0, l1), l2)
    w0, w1, w2 = jnp.exp(l0 - mx), jnp.exp(l1 - mx), jnp.exp(l2 - mx)
    oc = (w0 * oc0_ref[...] + w1 * oc1 + w2 * oc2) / (w0 + w1 + w2)
    branches = (oa_ref[...], ob_ref[...], oc.astype(BF16), od_ref[...])
    merged = None
    for n, o in enumerate(branches):
        gate = _sigmoid(jnp.dot(h, wg_ref[:, n * dm:(n + 1) * dm], preferred_element_type=F32))
        term = gate * jnp.dot(o, wb_ref[n], preferred_element_type=F32)
        merged = term if merged is None else merged + term
    y = jnp.dot(merged.astype(BF16), wo_ref[...], preferred_element_type=F32)
    out_ref[...] = xf + _rms(y, gpost_ref[...])


def _merge(x2, oa, ob, oc, lc, od, gpre, wg, wb, wo, gpost, seq, tm):
    n, dm = x2.shape
    nt = seq // tm
    row = lambda cols: pl.BlockSpec((tm, cols), lambda i: (i, 0))
    bw = 2 * LANES
    cls = lambda a: pl.BlockSpec((None, tm // (a.shape[2] // bw), a.shape[2]), lambda i: (i // nt, i % nt, 0))
    return pl.pallas_call(
        _merge_kernel,
        grid=(n // tm,),
        in_specs=[row(dm), row(bw), row(bw), row(bw), cls(oc[1]), cls(oc[2]), row(bw), cls(lc[1]), cls(lc[2]),
                  row(bw), _resident((1, dm)), _resident(wg.shape), _resident(wb.shape),
                  _resident(wo.shape), _resident((1, dm))],
        out_specs=row(dm),
        out_shape=jax.ShapeDtypeStruct((n, dm), F32),
        scratch_shapes=[pltpu.VMEM((8, tm, LANES), F32)],
        compiler_params=_cparams(("parallel",), 56),
        name="merge",
    )(x2, oa, ob, oc[0], oc[1], oc[2], lc[0], lc[1], lc[2], od, gpre, wg, wb, wo, gpost)


def _ffn_kernel(x_ref, g_ref, wg_ref, wu_ref, wd_ref, p_ref, gf_ref, wpg_ref, wpp_ref, gp_ref,
                out_ref, h_scr, acc_scr):
    j = pl.program_id(1)

    @pl.when(j == 0)
    def _():
        h_scr[...] = _rms(x_ref[...], g_ref[...]).astype(BF16)
        acc_scr[...] = jnp.zeros_like(acc_scr)

    h = h_scr[...]
    a = jnp.dot(h, wg_ref[...], preferred_element_type=F32)
    u = jnp.dot(h, wu_ref[...], preferred_element_type=F32)
    act = (a * _sigmoid(a) * u).astype(BF16)
    acc_scr[...] += jnp.dot(act, wd_ref[...], preferred_element_type=F32)

    @pl.when(j == pl.num_programs(1) - 1)
    def _():
        _post_math(x_ref[...], acc_scr[...], p_ref, gf_ref, wpg_ref, wpp_ref, gp_ref, out_ref)


def _ffn(x2, gain, wg, wu, wd, p2, gf, wpg, wpp, gp, tm, tf):
    p3, layer = p2
    n, dm = x2.shape
    dff = wg.shape[1]
    return pl.pallas_call(
        _ffn_kernel,
        grid=(n // tm, dff // tf),
        in_specs=[pl.BlockSpec((tm, dm), lambda i, j: (i, 0)),
                  pl.BlockSpec((1, dm), lambda i, j: (0, 0)),
                  pl.BlockSpec((dm, tf), lambda i, j: (0, j)),
                  pl.BlockSpec((dm, tf), lambda i, j: (0, j)),
                  pl.BlockSpec((tf, dm), lambda i, j: (j, 0)),
                  pl.BlockSpec((None, tm, p3.shape[2]), lambda i, j: (layer, i, 0)),
                  _resident((1, dm)), _resident(wpg.shape), _resident(wpp.shape), _resident((1, dm))],
        out_specs=pl.BlockSpec((tm, dm), lambda i, j: (i, 0)),
        out_shape=jax.ShapeDtypeStruct((n, dm), F32),
        scratch_shapes=[pltpu.VMEM((tm, dm), BF16), pltpu.VMEM((tm, dm), F32)],
        compiler_params=_cparams(("parallel", "arbitrary"), 56),
        name="ffn",
    )(x2, gain, wg, wu, wd, p3, gf, wpg, wpp, gp)


def _post_math(xf, f, p_ref, gf_ref, wpg_ref, wpp_ref, gp_ref, out_ref):
    x2 = xf + _rms(f, gf_ref[...])
    gate = _sigmoid(jnp.dot(x2.astype(BF16), wpg_ref[...], preferred_element_type=F32))
    e = jnp.dot(p_ref[...].astype(BF16), wpp_ref[...], preferred_element_type=F32) * gate
    out_ref[...] = x2 + _rms(e, gp_ref[...])


def _router_kernel(x_ref, g_ref, wr_ref, br_ref, hs_ref, info_ref):
    hf = _rms(x_ref[...], g_ref[...])
    hb = hf.astype(BF16)
    tm, dm = hf.shape
    sub = dm // LANES
    for s in range(sub):
        hs_ref[pl.ds(s, tm, stride=sub), :] = hf[:, s * LANES:(s + 1) * LANES]
    hl = (hf - hb.astype(F32)).astype(BF16)
    whi, wlo = wr_ref[0], wr_ref[1]
    logits = (jnp.dot(hb, whi, preferred_element_type=F32) + jnp.dot(hl, whi, preferred_element_type=F32)
              + jnp.dot(hb, wlo, preferred_element_type=F32)) + br_ref[...]
    lane = lax.broadcasted_iota(jnp.int32, logits.shape, 1).astype(F32)
    m1 = jnp.max(logits, axis=-1, keepdims=True)
    i1 = jnp.min(jnp.where(logits == m1, lane, float(LANES)), axis=-1, keepdims=True)
    rest = jnp.where(lane == i1, NEG_INF, logits)
    m2 = jnp.max(rest, axis=-1, keepdims=True)
    i2 = jnp.min(jnp.where(rest == m2, lane, float(LANES)), axis=-1, keepdims=True)
    e2 = jnp.exp(m2 - m1)
    g1 = 1.0 / (1.0 + e2)
    g2 = e2 / (1.0 + e2)
    info = jnp.where(lane == 0.0, i1, jnp.where(lane == 1.0, i2, jnp.where(lane == 2.0, g1, g2)))
    info_ref[...] = info


def _router(x2, gain, wr, br, tm):
    n, dm = x2.shape
    row = lambda cols: pl.BlockSpec((tm, cols), lambda i: (i, 0))
    return pl.pallas_call(
        _router_kernel,
        grid=(n // tm,),
        in_specs=[row(dm), _resident((1, dm)), _resident(wr.shape), _resident((1, LANES))],
        out_specs=[pl.BlockSpec((tm * (dm // LANES), LANES), lambda i: (i, 0)), row(LANES)],
        out_shape=[jax.ShapeDtypeStruct((n * (dm // LANES), LANES), F32),
                   jax.ShapeDtypeStruct((n, LANES), F32)],
        compiler_params=_cparams(("parallel",), 32),
        name="router",
    )(x2, gain, wr, br)


def _experts_kernel(be_ref, src0_ref, srcn_ref, dstp_ref, dstl_ref, hs_hbm, wg_ref, wu_ref, wd_ref,
                    yt_hbm, xg, xb, acc, ys, gsem, ssem, *, tm, sub):
    i = pl.program_id(0)
    j = pl.program_id(1)
    nb = pl.num_programs(0)
    nf = pl.num_programs(1)
    slot = i % 2
    other = 1 - slot

    def gather(idx_ref, r, dslot):
        src = pl.multiple_of(idx_ref[0, r] * sub, sub)
        dst = pl.multiple_of(r * sub, sub)
        return pltpu.make_async_copy(hs_hbm.at[pl.ds(src, sub)], xg.at[dslot, pl.ds(dst, sub)],
                                     gsem.at[dslot])

    def scatter(idx_ref, r, sslot):
        src = pl.multiple_of(r * sub, sub)
        dst = pl.multiple_of(idx_ref[0, r] * sub, sub)
        return pltpu.make_async_copy(ys.at[sslot, pl.ds(src, sub)], yt_hbm.at[pl.ds(dst, sub)],
                                     ssem.at[sslot])

    def wait_gather(dslot):
        pltpu.make_async_copy(hs_hbm.at[pl.ds(0, tm * sub)], xg.at[dslot], gsem.at[dslot]).wait()

    def wait_scatter(sslot):
        pltpu.make_async_copy(ys.at[sslot], yt_hbm.at[pl.ds(0, tm * sub)], ssem.at[sslot]).wait()

    @pl.when((i == 0) & (j == 0))
    def _():
        ys[1] = jnp.zeros(ys.shape[1:], ys.dtype)

        def start(r, c):
            gather(src0_ref, r, 0).start()
            return c

        lax.fori_loop(0, tm, start, 0)

    @pl.when(j == 0)
    def _():
        wait_gather(slot)
        for s in range(sub):
            xb[:, s * LANES:(s + 1) * LANES] = xg[slot, pl.ds(s, tm, stride=sub), :].astype(BF16)
        acc[...] = jnp.zeros_like(acc)

    @pl.when(i < be_ref[nb])
    def _():
        x = xb[...]
        a = jnp.dot(x, wg_ref[...], preferred_element_type=F32)
        u = jnp.dot(x, wu_ref[...], preferred_element_type=F32)
        act = (a * _sigmoid(a) * u).astype(BF16)
        acc[...] += jnp.dot(act, wd_ref[...], preferred_element_type=F32)

    @pl.when(j == 0)
    def _():
        for t in range(tm):
            gather(srcn_ref, t, other).start()

    @pl.when(j == nf - 1)
    def _():
        for t in range(tm):
            scatter(dstp_ref, t, other).start()

    @pl.when(j == nf - 1)
    def _():
        @pl.when(i >= 1)
        def _():
            wait_scatter(slot)

        for s in range(sub):
            ys[slot, pl.ds(s, tm, stride=sub), :] = acc[:, s * LANES:(s + 1) * LANES]

        @pl.when(i == nb - 1)
        def _():
            def start(r, c):
                scatter(dstl_ref, r, slot).start()
                return c

            lax.fori_loop(0, tm, start, 0)
            wait_scatter(slot)
            wait_scatter(other)
            wait_gather(other)


def _experts(hs, blk_e, src_tok, dst_row, n_slabs, wg, wu, wd, tm, tf):
    nb = src_tok.shape[0]
    dm, dff = wg.shape[1], wg.shape[2]
    sub = dm // LANES
    nf = dff // tf
    smem = lambda imap: pl.BlockSpec((None, 1, tm), imap, memory_space=pltpu.SMEM)
    grid_spec = pltpu.PrefetchScalarGridSpec(
        num_scalar_prefetch=1,
        grid=(nb, nf),
        in_specs=[smem(lambda i, j, be: (0, 0, 0)),
                  smem(lambda i, j, be: (jnp.minimum(i + 1, nb - 1), 0, 0)),
                  smem(lambda i, j, be: (i, 0, 0)),
                  smem(lambda i, j, be: (nb, 0, 0)),
                  pl.BlockSpec(memory_space=pl.ANY),
                  pl.BlockSpec((None, dm, tf), lambda i, j, be: (be[i], 0, j)),
                  pl.BlockSpec((None, dm, tf), lambda i, j, be: (be[i], 0, j)),
                  pl.BlockSpec((None, tf, dm), lambda i, j, be: (be[i], j, 0))],
        out_specs=pl.BlockSpec(memory_space=pl.ANY),
        scratch_shapes=[pltpu.VMEM((2, tm * sub, LANES), F32), pltpu.VMEM((tm, dm), BF16),
                        pltpu.VMEM((tm, dm), F32), pltpu.VMEM((2, tm * sub, LANES), F32),
                        pltpu.SemaphoreType.DMA((2,)), pltpu.SemaphoreType.DMA((2,))],
    )
    return pl.pallas_call(
        functools.partial(_experts_kernel, tm=tm, sub=sub),
        grid_spec=grid_spec,
        out_shape=jax.ShapeDtypeStruct((n_slabs * sub, LANES), F32),
        compiler_params=_cparams(("arbitrary", "arbitrary"), 48),
        name="experts",
    )(blk_e, src_tok, src_tok, dst_row, dst_row, hs, wg, wu, wd)


def _combine_kernel(y_ref, x_ref, gt_ref, p_ref, gf_ref, wpg_ref, wpp_ref, gp_ref, out_ref):
    tt, dm = x_ref.shape
    sub = dm // LANES
    gt = gt_ref[...]

    def rows(slot):
        return jnp.concatenate([y_ref[pl.ds(slot * sub + s, tt, stride=TOP_K * sub), :] for s in range(sub)],
                               axis=1)

    f = gt[:, 2:3] * rows(0) + gt[:, 3:4] * rows(1)
    _post_math(x_ref[...], f, p_ref, gf_ref, wpg_ref, wpp_ref, gp_ref, out_ref)


def _combine(yt, info, x2, p2, gf, wpg, wpp, gp, tt):
    p3, layer = p2
    n, dm = x2.shape
    row = lambda cols: pl.BlockSpec((tt, cols), lambda i: (i, 0))
    return pl.pallas_call(
        _combine_kernel,
        grid=(n // tt,),
        in_specs=[pl.BlockSpec((tt * TOP_K * (dm // LANES), LANES), lambda i: (i, 0)),
                  row(dm), row(LANES), pl.BlockSpec((None, tt, p3.shape[2]), lambda i: (layer, i, 0)),
                  _resident((1, dm)),
                  _resident(wpg.shape), _resident(wpp.shape), _resident((1, dm))],
        out_specs=row(dm),
        out_shape=jax.ShapeDtypeStruct((n, dm), F32),
        compiler_params=_cparams(("parallel",), 48),
        name="combine",
    )(yt, x2, info, p3, gf, wpg, wpp, gp)


def _moe(x2, p2, gain, wr, br, wg, wu, wd, gf, wpg, wpp, gp, *, tm_r, tt, tm_e, tf_e):
    n, _ = x2.shape
    hs, info = _router(x2, gain, wr, br, tm_r)
    n_asg = n * TOP_K
    e_flat = jnp.concatenate([info[:, k].astype(jnp.int32) for k in range(TOP_K)])
    slab = jnp.concatenate([jnp.arange(n, dtype=jnp.int32) * TOP_K + k for k in range(TOP_K)])
    onehot = (e_flat[:, None] == jnp.arange(N_EXPERTS, dtype=jnp.int32)[None, :]).astype(jnp.int32)
    csum = jnp.cumsum(onehot, axis=0)
    rank = jnp.sum((csum - onehot) * onehot, axis=1)
    counts = csum[-1]
    padded = ((counts + tm_e - 1) // tm_e) * tm_e
    pend = jnp.cumsum(padded)
    pstart = pend - padded
    dest = (pstart[e_flat] + rank).astype(jnp.int32)
    n_blocks = -(-n_asg // tm_e) + N_EXPERTS
    n_rows = n_blocks * tm_e
    blk_start = jnp.arange(n_blocks, dtype=jnp.int32) * tm_e
    blk_e = jnp.sum((pend[None, :] <= blk_start[:, None]).astype(jnp.int32), axis=1)
    blk_e = jnp.minimum(blk_e, N_EXPERTS - 1)
    blk_e = jnp.concatenate([blk_e, (pend[-1:] // tm_e).astype(jnp.int32)])
    asg = jnp.full((n_rows,), -1, jnp.int32).at[dest].set(slab, unique_indices=True)
    is_pad = asg < 0
    pad_rank = jnp.cumsum(is_pad.astype(jnp.int32)) - 1
    src_tok = jnp.where(is_pad, 0, asg // TOP_K).reshape(n_blocks, 1, tm_e)
    dst_row = jnp.where(is_pad, n_asg + pad_rank, asg)
    spare = n_asg + (n_rows - n_asg) + jnp.arange(tm_e, dtype=jnp.int32)
    dst_row = jnp.concatenate([spare, dst_row]).reshape(n_blocks + 1, 1, tm_e)
    n_slabs = n_asg + (n_rows - n_asg) + tm_e
    yt = _experts(hs, blk_e, src_tok, dst_row, n_slabs, wg, wu, wd, tm_e, tf_e)
    return _combine(yt, info, x2, p2, gf, wpg, wpp, gp, tt)


def _tables(seq):
    pos = np.arange(seq, dtype=np.int32)

    def cs(p, half):
        inv = np.power(np.float32(ROPE_THETA), -np.arange(half, dtype=np.float32) / np.float32(half))
        ang = p.astype(np.float32)[:, None] * inv[None, :].astype(np.float32)
        return jnp.asarray(np.cos(ang), F32), jnp.asarray(np.sin(ang), F32)

    ones = lambda w: jnp.ones((seq, w), F32)
    zeros = lambda w: jnp.zeros((seq, w), F32)
    (c, s), (cr, sr), (cc, sc), (ca, sa) = (
        cs(pos, HEAD_DIM // 2), cs(pos // GRID_W, HEAD_DIM // 4), cs(pos % GRID_W, HEAD_DIM // 4),
        cs(pos, A_ROPE // 2))
    full = jnp.stack([jnp.tile(jnp.concatenate([c, c], 1), (1, 2)),
                      jnp.tile(jnp.concatenate([-s, s], 1), (1, 2))])
    axial = jnp.stack([jnp.tile(jnp.concatenate([cr, cc, cr, cc], 1), (1, 2)),
                       jnp.tile(jnp.concatenate([-sr, -sc, sr, sc], 1), (1, 2))])
    slot_c = jnp.concatenate([ca, ones(16), ca, ones(16)], 1)
    slot_s = jnp.concatenate([-sa, zeros(16), sa, zeros(16)], 1)
    a_k = jnp.stack([jnp.concatenate([slot_c, ones(HALF)], 1), jnp.concatenate([slot_s, zeros(HALF)], 1)])
    a_q = jnp.stack([jnp.concatenate([ones(HALF), slot_c], 1), jnp.concatenate([zeros(HALF), slot_s], 1)])
    return {"full": full, "axial": axial, "a_k": a_k, "a_q": a_q}


_AXIAL_PERM = tuple(list(range(0, 16)) + list(range(32, 48)) + list(range(16, 32)) + list(range(48, 64)))


def _dup_heads(w, n_heads, perm=None):
    rows = w.shape[0]
    w = w.reshape(rows, n_heads, HEAD_DIM)
    if perm is not None:
        w = w[:, :, perm]
    return jnp.stack([w, w], axis=2).reshape(rows, n_heads * 2 * HEAD_DIM)


def _assemble_w_in(w):
    dm = w.shape[0]
    perm = jnp.array(_AXIAL_PERM, jnp.int32)
    a, b, c, d = w[:, 0:SRC_A], w[:, SRC_A:SRC_B], w[:, SRC_B:SRC_C], w[:, SRC_C:SRC_D]
    z = lambda n: jnp.zeros((dm, n), w.dtype)
    kr = a[:, 384:416]
    a_seg = jnp.concatenate([a[:, 0:384], kr[:, 0:16], z(16), kr[:, 16:32], z(16), z(HALF)], axis=1)
    bq = b[:, 0:256].reshape(dm, 4, HEAD_DIM)[:, :, perm].reshape(dm, 256)
    b_seg = jnp.concatenate([bq, _dup_heads(b[:, 256:384], 2, perm)], axis=1)
    d_seg = jnp.concatenate([d[:, 0:256], _dup_heads(d[:, 256:384], 2), _dup_heads(d[:, 384:512], 2)], axis=1)
    w_vbt = _dup_heads(b[:, 384:512], 2).T.astype(BF16)
    return jnp.concatenate([a_seg, b_seg, c, d_seg], axis=1).astype(BF16), w_vbt


def _assemble_a(w_uq, w_ukv):
    zq = lambda n: jnp.zeros((w_uq.shape[0], n), w_uq.dtype)
    zk = lambda n: jnp.zeros((w_ukv.shape[0], n), w_ukv.dtype)
    dq = A_NOPE + A_ROPE
    q_cols, k_cols, v_cols = [], [], []
    for hh in range(A_HEADS):
        q = w_uq[:, hh * dq:(hh + 1) * dq]
        q_cols += [q[:, 0:A_NOPE], q[:, A_NOPE:A_NOPE + 16], zq(16), q[:, A_NOPE + 16:dq], zq(16)]
        kv = w_ukv[:, hh * (A_NOPE + A_V):(hh + 1) * (A_NOPE + A_V)]
        k_cols += [kv[:, 0:A_NOPE], zk(HALF)]
        v_cols += [kv[:, A_NOPE:A_NOPE + A_V]]
    cat = lambda cols: jnp.concatenate(cols, axis=1).astype(BF16)
    return cat(q_cols), cat(k_cols), cat(v_cols).T


def _gain_pair(g, perm=None):
    if perm is not None:
        g = g[jnp.array(perm, jnp.int32)]
    return jnp.tile(g, 2).reshape(1, LANES).astype(F32)


def kernel(x, p, w_in, a_qa_g, a_kva_g, a_w_uq, a_w_ukv, b_q_g, b_k_g, d_sink, w_branch, w_out,
           mix_pre_g, mix_post_g, ffn_pre_g, ffn_post_g, ffn_w_gate, ffn_w_up, ffn_w_down,
           router_w, router_b, moe_w_gate, moe_w_up, moe_w_down, ple_w_proj, ple_w_gate, ple_post_g):
    bsz, seq, dm = x.shape
    depth = w_in.shape[0]
    n = bsz * seq
    tm = min(512, seq)
    tq_flash = min(512, seq)
    tk_flash = min(1024, seq // 4)
    tabs = _tables(seq)
    row = lambda g: g.reshape(1, -1).astype(F32)
    x2 = x.reshape(n, dm)

    for i in range(depth):
        w_in16 = w_in[i].astype(BF16)
        w_all, wvb = _assemble_w_in(w_in16)
        wuq, wk, wv = _assemble_a(a_w_uq[i], a_w_ukv[i])
        z, qa, ka, vat, vbt, zc1, zc2 = _in_proj(
            x2, row(mix_pre_g[i]), w_all, tabs,
            _gain_pair(b_q_g[i], _AXIAL_PERM), _gain_pair(b_k_g[i], _AXIAL_PERM),
            row(a_qa_g[i]), row(a_kva_g[i]), wuq, wk, wv, wvb, seq, tm)
        z3 = z.reshape(bsz, seq, ZMAIN_COLS)
        o_a = _flash(qa, ka, vat, q_col0=0, k_col0=0, v_row0=0, seq=seq, packed=False,
                     tq=tq_flash, tk=tk_flash)
        o_b = _flash(z, z, vbt, q_col0=ZB, k_col0=ZB + 256, v_row0=0, seq=seq, packed=True,
                     tq=tq_flash, tk=tk_flash)
        o_c, l_c = [], []
        for (win, dil), (src, col0, rc) in zip(C_PATTERNS, ((z3, ZC, ZMAIN_COLS), (zc1, 0, QKV_SEG), (zc2, 0, QKV_SEG))):
            og, lg = _banded(src, src, src, q_col0=col0, k_col0=col0 + 256, v_col0=col0 + 512,
                             row_cols=rc, dil=dil, seq=seq, hw=win // (2 * dil), tq=min(512, seq // dil),
                             want_lse=True, out_dtype=F32)
            o_c.append(og)
            l_c.append(lg)
        o_c[0] = o_c[0].reshape(n, 2 * LANES)
        l_c[0] = l_c[0].reshape(n, 2 * LANES)
        (o_d,) = _banded(z3, z3, z3, q_col0=ZD, k_col0=ZD + 256, v_col0=ZD + 512,
                         row_cols=ZMAIN_COLS, dil=1, seq=seq, hw=D_HALF_WINDOW, tq=min(512, seq),
                         sink=d_sink[i].astype(F32))
        o_d = o_d.reshape(n, 2 * LANES)
        wg_gate = w_in16[:, SRC_D:]
        x2 = _merge(x2, o_a, o_b, o_c, l_c, o_d, row(mix_pre_g[i]), wg_gate,
                    w_branch[i].astype(BF16), w_out[i].astype(BF16), row(mix_post_g[i]), seq, tm)

        p2 = (p.reshape(depth, n, -1), i)
        wpg = ple_w_gate[i].astype(BF16)
        wpp = ple_w_proj[i].astype(BF16)
        j = i // 2
        if i % 2 == 0:
            x2 = _ffn(x2, row(ffn_pre_g[i]), ffn_w_gate[j].astype(BF16), ffn_w_up[j].astype(BF16),
                      ffn_w_down[j].astype(BF16), p2, row(ffn_post_g[i]), wpg, wpp, row(ple_post_g[i]),
                      min(1024, n), 512)
        else:
            wr32 = jnp.zeros((dm, LANES), F32).at[:, :N_EXPERTS].set(router_w[j].astype(F32))
            wr_hi = wr32.astype(BF16)
            wr = jnp.stack([wr_hi, (wr32 - wr_hi.astype(F32)).astype(BF16)])
            br = jnp.full((1, LANES), NEG_INF, F32).at[0, :N_EXPERTS].set(router_b[j].astype(F32))
            x2 = _moe(x2, p2, row(ffn_pre_g[i]), wr, br, moe_w_gate[j].astype(BF16),
                      moe_w_up[j].astype(BF16), moe_w_down[j].astype(BF16), row(ffn_post_g[i]),
                      wpg, wpp, row(ple_post_g[i]), tm_r=tm, tt=min(256, n), tm_e=512, tf_e=1792)
    return x2.reshape(bsz, seq, dm)
```

```python
import functools

import jax
import jax.numpy as jnp
import numpy as np
from jax import lax
from jax.experimental import pallas as pl
from jax.experimental.pallas import tpu as pltpu

F32 = jnp.float32
BF16 = jnp.bfloat16

GRID_W = 64
HEAD_DIM = 64
ROPE_THETA = 10000.0
NORM_EPS = 1e-6
NEG_INF = -1e30
A_HEADS = 4
A_Q_RANK = 256
A_KV_RANK = 128
A_NOPE = 64
A_ROPE = 32
A_V = 64
C_PATTERNS = ((128, 1), (512, 4), (2048, 16))
D_HALF_WINDOW = 128
N_EXPERTS = 8
TOP_K = 2

LANES = 128
HALF = 64
VMEM_MB = 1024 * 1024
LOG2E = 1.4426950408889634
BAND_SUB = 128

PAIR = 2 * LANES
QKV_SEG = 3 * PAIR
SRC_A = A_Q_RANK + A_KV_RANK + A_ROPE
SRC_B = SRC_A + PAIR + 2 * LANES
SRC_C = SRC_B + len(C_PATTERNS) * QKV_SEG
SRC_D = SRC_C + PAIR + 2 * LANES

ZB = 0
ZC = 512
ZD = 512 + QKV_SEG
ZMAIN_COLS = ZD + QKV_SEG
A_SEG = 512
B_SEG = 512


def _cparams(sem, vmem_mb):
    return pltpu.CompilerParams(dimension_semantics=sem, vmem_limit_bytes=vmem_mb * VMEM_MB)


def _resident(shape):
    nd = len(shape)
    return pl.BlockSpec(shape, lambda *_: (0,) * nd, pipeline_mode=pl.Buffered(1))


def _rms(xf, g):
    return xf * lax.rsqrt(jnp.mean(xf * xf, axis=-1, keepdims=True) + NORM_EPS) * g


def _sigmoid(x):
    return 1.0 / (1.0 + jnp.exp(-x))


def _swap32(a):
    lane = lax.broadcasted_iota(jnp.int32, a.shape, 1)
    fwd = pltpu.roll(a, LANES - 32, 1)
    bwd = pltpu.roll(a, 32, 1)
    return jnp.where((lane & 32) == 0, fwd, bwd)


def _rope(a, cos, sin):
    outs = []
    for c in range(a.shape[1] // LANES):
        ch = a[:, c * LANES:(c + 1) * LANES]
        outs.append(ch * cos + _swap32(ch) * sin)
    return outs[0] if len(outs) == 1 else jnp.concatenate(outs, axis=1)


def _head_norm(a, g, bd):
    outs = []
    for c in range(a.shape[1] // LANES):
        ch = a[:, c * LANES:(c + 1) * LANES]
        sq = ch * ch
        hi = sq.astype(BF16)
        lo = (sq - hi.astype(F32)).astype(BF16)
        ms = (jnp.dot(hi, bd, preferred_element_type=F32)
              + jnp.dot(lo, bd, preferred_element_type=F32))
        outs.append(ch * lax.rsqrt(ms + NORM_EPS) * g)
    return outs[0] if len(outs) == 1 else jnp.concatenate(outs, axis=1)


def _in_proj_kernel(x_ref, g_ref, w_ref, tabf_ref, tabx_ref, taba_ref, tabq_ref,
                    bqg_ref, bkg_ref, aqg_ref, akvg_ref, wuq_ref, wk_ref, wv_ref, wvb_ref,
                    z_ref, qa_ref, ka_ref, vat_ref, vbt_ref, zc1_ref, zc2_ref, cls_scr):
    h = _rms(x_ref[...], g_ref[...]).astype(BF16)
    cf, sf = tabf_ref[0], tabf_ref[1]
    cx, sx = tabx_ref[0], tabx_ref[1]
    q_scale = HEAD_DIM ** -0.5

    acc = jnp.dot(h, w_ref[:, 0:A_SEG], preferred_element_type=F32)
    nq = _rms(acc[:, 0:A_Q_RANK], aqg_ref[...]).astype(BF16)
    nkv = _rms(acc[:, A_Q_RANK:A_Q_RANK + A_KV_RANK], akvg_ref[...]).astype(BF16)
    kr = _rope(acc[:, 384:512], taba_ref[0], taba_ref[1]).astype(BF16)
    qa = jnp.dot(nq, wuq_ref[...], preferred_element_type=F32)
    qa = _rope(qa, tabq_ref[0], tabq_ref[1]) * ((A_NOPE + A_ROPE) ** -0.5 * LOG2E)
    qa_ref[...] = qa.astype(qa_ref.dtype)
    r = lax.broadcasted_iota(jnp.int32, (LANES, A_HEADS * LANES), 0)
    c = lax.broadcasted_iota(jnp.int32, (LANES, A_HEADS * LANES), 1)
    place = jnp.where((r < HALF) & ((c & (LANES - 1)) == r + HALF), 1.0, 0.0).astype(BF16)
    ka = (jnp.dot(nkv, wk_ref[...], preferred_element_type=F32)
          + jnp.dot(kr, place, preferred_element_type=F32))
    ka_ref[...] = ka.astype(ka_ref.dtype)
    nt_dims = (((1,), (1,)), ((), ()))
    vat_ref[...] = lax.dot_general(wv_ref[...], nkv, nt_dims, preferred_element_type=F32).astype(vat_ref.dtype)
    vbt_ref[...] = lax.dot_general(wvb_ref[...], h, nt_dims, preferred_element_type=F32).astype(vbt_ref.dtype)

    rr = lax.broadcasted_iota(jnp.int32, (LANES, LANES), 0)
    cc = lax.broadcasted_iota(jnp.int32, (LANES, LANES), 1)
    bd = jnp.where((rr >> 6) == (cc >> 6), 1.0 / HEAD_DIM, 0.0).astype(BF16)
    acc = jnp.dot(h, w_ref[:, A_SEG:A_SEG + B_SEG], preferred_element_type=F32)
    q = _rope(_head_norm(acc[:, 0:256], bqg_ref[...], bd), cx, sx) * (q_scale * LOG2E)
    k = _rope(_head_norm(acc[:, 256:512], bkg_ref[...], bd), cx, sx)
    z_ref[:, ZB:ZB + 256] = q.astype(z_ref.dtype)
    z_ref[:, ZB + 256:ZB + 512] = k.astype(z_ref.dtype)

    tm = h.shape[0]
    for widx, zoff, cls_ref, dil in ((0, ZC, None, 1), (1, 0, zc1_ref, C_PATTERNS[1][1]),
                                     (2, 0, zc2_ref, C_PATTERNS[2][1]), (3, ZD, None, 1)):
        base = A_SEG + B_SEG + widx * QKV_SEG
        acc = jnp.dot(h, w_ref[:, base:base + QKV_SEG], preferred_element_type=F32)
        q = _rope(acc[:, 0:256], cf, sf) * q_scale
        k = _rope(acc[:, 256:512], cf, sf)
        if cls_ref is None:
            z_ref[:, zoff:zoff + 256] = q.astype(z_ref.dtype)
            z_ref[:, zoff + 256:zoff + 512] = k.astype(z_ref.dtype)
            z_ref[:, zoff + 512:zoff + QKV_SEG] = acc[:, 512:QKV_SEG].astype(z_ref.dtype)
        else:
            qkv = (q[:, 0:LANES], q[:, LANES:], k[:, 0:LANES], k[:, LANES:],
                   acc[:, 512:512 + LANES], acc[:, 512 + LANES:QKV_SEG])
            for ch, val in enumerate(qkv):
                cls_scr[ch] = val
            for c in range(dil):
                for ch in range(len(qkv)):
                    col = c * QKV_SEG + ch * LANES
                    cls_ref[:, col:col + LANES] = cls_scr[ch, pl.ds(c, tm // dil, stride=dil), :].astype(cls_ref.dtype)


def _in_proj(x2, gain, w, tabs, bqg, bkg, aqg, akvg, wuq, wk, wv, wvb, seq, tm):
    n, dm = x2.shape
    nt = seq // tm
    bsz = n // seq
    d1, d2 = C_PATTERNS[1][1], C_PATTERNS[2][1]
    tab_spec = pl.BlockSpec((2, tm, LANES), lambda i: (0, i % nt, 0))
    row = lambda cols: pl.BlockSpec((tm, cols), lambda i: (i, 0))
    vt_spec = pl.BlockSpec((None, 2 * LANES, tm), lambda i: (i // nt, 0, i % nt))
    cls_spec = lambda d: pl.BlockSpec((None, tm // d, d * QKV_SEG), lambda i: (i // nt, i % nt, 0))
    return pl.pallas_call(
        _in_proj_kernel,
        grid=(n // tm,),
        in_specs=[row(dm), _resident((1, dm)), _resident(w.shape),
                  tab_spec, tab_spec, tab_spec, tab_spec,
                  _resident((1, LANES)), _resident((1, LANES)),
                  _resident((1, A_Q_RANK)), _resident((1, A_KV_RANK)),
                  _resident(wuq.shape), _resident(wk.shape), _resident(wv.shape), _resident(wvb.shape)],
        out_specs=[row(ZMAIN_COLS), row(A_HEADS * LANES), row(A_HEADS * LANES),
                   vt_spec, vt_spec, cls_spec(d1), cls_spec(d2)],
        out_shape=[jax.ShapeDtypeStruct((n, ZMAIN_COLS), BF16),
                   jax.ShapeDtypeStruct((n, A_HEADS * LANES), BF16),
                   jax.ShapeDtypeStruct((n, A_HEADS * LANES), BF16),
                   jax.ShapeDtypeStruct((bsz, 2 * LANES, seq), BF16),
                   jax.ShapeDtypeStruct((bsz, 2 * LANES, seq), BF16),
                   jax.ShapeDtypeStruct((bsz, seq // d1, d1 * QKV_SEG), BF16),
                   jax.ShapeDtypeStruct((bsz, seq // d2, d2 * QKV_SEG), BF16)],
        scratch_shapes=[pltpu.VMEM((QKV_SEG // LANES, tm, LANES), F32)],
        compiler_params=_cparams(("parallel",), 48),
        name="in_proj",
    )(x2, gain, w, tabs["full"], tabs["axial"], tabs["a_k"], tabs["a_q"],
      bqg, bkg, aqg, akvg, wuq, wk, wv, wvb)


FLASH_SAFE_EXP = 64.0


def _head_queries(q_ref, packed):
    lane = lax.broadcasted_iota(jnp.int32, (1, LANES), 1)
    if packed:
        return [jnp.where((lane < HALF) == (r == 0), q_ref[...], 0) for r in range(2)], [0, 0]
    return [q_ref[:, r * LANES:(r + 1) * LANES] for r in range(2)], [0, LANES]


def _flash_single_pass(q_ref, k_ref, vt_ref, s_a, s_b, *, tk, packed):
    tq = q_ref.shape[0]
    nk = k_ref.shape[0] // tk
    qs, kcs = _head_queries(q_ref, packed)
    qts = [q.astype(F32).T.astype(BF16) for q in qs]
    row = lax.broadcasted_iota(jnp.int32, (LANES, tq), 0)
    klane = lax.broadcasted_iota(jnp.int32, (tk, LANES), 1)
    k_one = jnp.where(klane == 0, 1.0, 0.0).astype(BF16)
    v_one = jnp.ones((16, tk), BF16)

    def scores(c, refs, dst):
        ks = pl.multiple_of(c * tk, tk)
        for r in range(2):
            k = jnp.concatenate([k_ref[pl.ds(ks, tk), kcs[r]:kcs[r] + LANES], k_one], axis=1)
            bias = jnp.where(row == 0, -refs[r], 0.0).astype(BF16)
            qt = jnp.concatenate([qts[r], bias], axis=0)
            dst[r] = jnp.dot(k, qt, preferred_element_type=F32)

    def consume(c, src, refs, sts, first):
        ks = pl.multiple_of(c * tk, tk)
        new = []
        for r in range(2):
            big, base, l, acc, hi, lo = sts[r]
            s = src[r]
            cmax = jnp.max(s, axis=0, keepdims=True)
            p = jnp.exp2(s).astype(BF16)
            lhs = jnp.concatenate([vt_ref[r * HALF:(r + 1) * HALF, pl.ds(ks, tk)], v_one], axis=0)
            pv = jnp.dot(lhs, p, preferred_element_type=F32)
            alpha = jnp.exp2(base - refs[r])
            l = l * alpha + pv[HALF:HALF + 1]
            acc = acc * alpha + pv[0:HALF]
            big = jnp.maximum(big, refs[r] + cmax)
            hi = jnp.maximum(hi, cmax)
            if first:
                lo = jnp.minimum(lo, cmax)
            new.append((big, refs[r], l, acc, hi, lo))
        return tuple(new)

    def ref_of(sts):
        return [st[0].astype(BF16).astype(F32) for st in sts]

    zero = jnp.zeros((1, tq), F32)
    sts = tuple((jnp.full((1, tq), NEG_INF, F32), zero, zero, jnp.zeros((HALF, tq), F32),
                 jnp.full((1, tq), NEG_INF, F32), jnp.full((1, tq), -NEG_INF, F32)) for _ in range(2))
    zeros2 = [zero, zero]
    scores(0, zeros2, s_a)
    scores(1, zeros2, s_b)
    sts = consume(0, s_a, zeros2, sts, True)
    ra = ref_of(sts)
    scores(2, ra, s_a)
    sts = consume(1, s_b, zeros2, sts, True)

    def pair(jj, carry):
        sts, ra = carry
        c0 = 2 * jj
        rb = ref_of(sts)
        scores(c0 + 1, rb, s_b)
        sts = consume(c0, s_a, ra, sts, False)
        ra = ref_of(sts)
        scores(c0 + 2, ra, s_a)
        sts = consume(c0 + 1, s_b, rb, sts, False)
        return sts, ra

    sts, ra = lax.fori_loop(1, nk // 2 - 1, pair, (sts, ra))
    rb = ref_of(sts)
    scores(nk - 1, rb, s_b)
    sts = consume(nk - 2, s_a, ra, sts, False)
    sts = consume(nk - 1, s_b, rb, sts, False)
    outs, bad = [], None
    for r in range(2):
        _, _, l, acc, hi, lo = sts[r]
        outs.append(acc / l)
        b = (hi > FLASH_SAFE_EXP) | (lo < -FLASH_SAFE_EXP) | jnp.logical_not(l > 2.0 ** -FLASH_SAFE_EXP)
        bad = b if bad is None else (bad | b)
    return jnp.concatenate(outs, axis=0), bad


def _flash_kernel(q_ref, k_ref, vt_ref, o_ref, s_a, s_b, *, tk, packed):
    out, bad = _flash_single_pass(q_ref, k_ref, vt_ref, s_a, s_b, tk=tk, packed=packed)
    o_ref[...] = out.T.astype(o_ref.dtype)

    @pl.when(jnp.max(jnp.where(bad, 1.0, 0.0)) > 0.0)
    def _():
        o_ref[...] = _flash_two_pass(q_ref, k_ref, vt_ref, s_a, s_b, tk=tk, packed=packed).T.astype(o_ref.dtype)


def _flash_two_pass(q_ref, k_ref, vt_ref, s_a, s_b, *, tk, packed):
    tq = q_ref.shape[0]
    nk = k_ref.shape[0] // tk
    qs, kcs = _head_queries(q_ref, packed)

    def scores(kk, dst):
        ks = pl.multiple_of(kk * tk, tk)
        for r in range(2):
            k = k_ref[pl.ds(ks, tk), kcs[r]:kcs[r] + LANES]
            dst[r] = lax.dot_general(k, qs[r], (((1,), (1,)), ((), ())), preferred_element_type=F32)

    def consume(kk, src, carry):
        ks = pl.multiple_of(kk * tk, tk)
        new = []
        for r in range(2):
            m, l, acc = carry[r]
            s = src[r]
            m_new = jnp.maximum(m, jnp.max(s, axis=0, keepdims=True))
            alpha = jnp.exp2(m - m_new)
            p = jnp.exp2(s - m_new)
            l = alpha * l + jnp.sum(p, axis=0, keepdims=True)
            vt = vt_ref[r * HALF:(r + 1) * HALF, pl.ds(ks, tk)]
            acc = alpha * acc + jnp.dot(vt, p.astype(BF16), preferred_element_type=F32)
            new.append((m_new, l, acc))
        return tuple(new)

    def pair(jj, carry):
        c0 = 2 * jj
        scores(c0 + 1, s_b)
        carry = consume(c0, s_a, carry)
        scores(c0 + 2, s_a)
        return consume(c0 + 1, s_b, carry)

    init = tuple((jnp.full((1, tq), NEG_INF, F32), jnp.zeros((1, tq), F32), jnp.zeros((HALF, tq), F32))
                 for _ in range(2))
    scores(0, s_a)
    carry = lax.fori_loop(0, nk // 2 - 1, pair, init)
    scores(nk - 1, s_b)
    carry = consume(nk - 2, s_a, carry)
    carry = consume(nk - 1, s_b, carry)
    return jnp.concatenate([acc / l for (_, l, acc) in carry], axis=0)


def _flash(q, k, vt, *, q_col0, k_col0, v_row0, seq, packed, tq, tk):
    n = q.shape[0]
    bsz = n // seq
    qw = LANES if packed else 2 * LANES
    nq = seq // tq
    assert seq % (2 * tk) == 0 and seq // tk >= 4, "the chunk pipeline needs an even number (>= 4) of key chunks"
    k3 = k.reshape(bsz, seq, k.shape[1])
    qb, kb, vb = q_col0 // qw, k_col0 // qw, v_row0 // LANES
    return pl.pallas_call(
        functools.partial(_flash_kernel, tk=tk, packed=packed),
        grid=(bsz, 2, nq),
        in_specs=[pl.BlockSpec((tq, qw), lambda b, j, i: (b * nq + i, qb + j)),
                  pl.BlockSpec((None, seq, qw), lambda b, j, i: (b, 0, kb + j)),
                  pl.BlockSpec((None, LANES, seq), lambda b, j, i: (b, vb + j, 0))],
        out_specs=pl.BlockSpec((tq, LANES), lambda b, j, i: (b * nq + i, j)),
        out_shape=jax.ShapeDtypeStruct((n, 2 * LANES), BF16),
        scratch_shapes=[pltpu.VMEM((2, tk, tq), F32), pltpu.VMEM((2, tk, tq), F32)],
        compiler_params=_cparams(("parallel", "parallel", "parallel"), 48),
        name="flash_packed" if packed else "flash_slots",
    )(q, k3, vt)


def _banded_kernel(*refs, hw, has_sink, want_lse):
    if has_sink:
        sink_ref, q_ref, k_ref, v_ref = refs[:4]
        outs = refs[4:]
    else:
        q_ref, k_ref, v_ref = refs[:3]
        outs = refs[3:]
    o_ref = outs[0]
    tq = q_ref.shape[0]
    length = k_ref.shape[0]
    sb = min(BAND_SUB, tq)
    win = min(sb + 2 * hw, length)
    i = pl.program_id(1)
    lane = lax.broadcasted_iota(jnp.int32, (1, LANES), 1)
    first = lane < HALF
    cols = [slice(j * LANES, (j + 1) * LANES) for j in range(2)]
    subs = list(range(tq // sb))
    kss, valids = [], []
    for u in subs:
        q0 = i * tq + u * sb
        ks = pl.multiple_of(jnp.clip(q0 - hw, 0, length - win), HALF)
        qpos = q0 + lax.broadcasted_iota(jnp.int32, (sb, win), 0)
        kpos = ks + lax.broadcasted_iota(jnp.int32, (sb, win), 1)
        kss.append(ks)
        valids.append(jnp.abs(qpos - kpos) <= hw)
    chains = [(u, j, r) for u in subs for j in range(2) for r in range(2)]
    kws = {(u, j): k_ref[pl.ds(kss[u], win), cols[j]] for u in subs for j in range(2)}
    vws = {(u, j): v_ref[pl.ds(kss[u], win), cols[j]] for u in subs for j in range(2)}
    ss = []
    for u, j, r in chains:
        q = jnp.where(first == (r == 0), q_ref[u * sb:(u + 1) * sb, cols[j]], 0)
        ss.append(lax.dot_general(q, kws[u, j], (((1,), (1,)), ((), ())), preferred_element_type=F32))
    ss = [jnp.where(valids[u], s, NEG_INF) for s, (u, j, r) in zip(ss, chains)]
    ms = [jnp.max(s, axis=-1, keepdims=True) for s in ss]
    if has_sink:
        sinks = [sink_ref[2 * j + r] for u, j, r in chains]
        ms = [jnp.maximum(m, sk) for m, sk in zip(ms, sinks)]
    es = [jnp.exp(s - m) for s, m in zip(ss, ms)]
    ls = [jnp.sum(e, axis=-1, keepdims=True) for e in es]
    if has_sink:
        ls = [l + jnp.exp(sk - m) for l, sk, m in zip(ls, sinks, ms)]
    os_ = [jnp.dot(e.astype(BF16), vws[u, j], preferred_element_type=F32) / l
           for e, l, (u, j, r) in zip(es, ls, chains)]
    for n in range(0, len(chains), 2):
        u, j, _ = chains[n]
        rows = slice(u * sb, (u + 1) * sb)
        o_ref[rows, cols[j]] = jnp.where(first, os_[n], os_[n + 1]).astype(o_ref.dtype)
        if want_lse:
            lses = [jnp.broadcast_to(ms[n + r] + jnp.log(ls[n + r]), (sb, LANES)) for r in range(2)]
            outs[1][rows, cols[j]] = jnp.where(first, lses[0], lses[1])


def _banded(q, k, v, *, q_col0, k_col0, v_col0, row_cols, dil, seq, hw, tq, sink=None,
            want_lse=False, out_dtype=BF16):
    bsz, ls, _ = q.shape
    nq = ls // tq
    pw = 2 * LANES
    rb = row_cols // pw
    qb, kb, vb = q_col0 // pw, k_col0 // pw, v_col0 // pw
    in_specs = [pl.BlockSpec((None, tq, pw), lambda bc, i: (bc // dil, i, (bc % dil) * rb + qb)),
                pl.BlockSpec((None, ls, pw), lambda bc, i: (bc // dil, 0, (bc % dil) * rb + kb)),
                pl.BlockSpec((None, ls, pw), lambda bc, i: (bc // dil, 0, (bc % dil) * rb + vb))]
    args = [q, k, v]
    if sink is not None:
        in_specs = [pl.BlockSpec(memory_space=pltpu.SMEM)] + in_specs
        args = [sink] + args
    o_spec = pl.BlockSpec((None, tq, pw), lambda bc, i: (bc // dil, i, bc % dil))
    out_specs = [o_spec]
    out_shape = [jax.ShapeDtypeStruct((bsz, ls, dil * pw), out_dtype)]
    if want_lse:
        out_specs.append(o_spec)
        out_shape.append(jax.ShapeDtypeStruct((bsz, ls, dil * pw), F32))
    res = pl.pallas_call(
        functools.partial(_banded_kernel, hw=hw, has_sink=sink is not None, want_lse=want_lse),
        grid=(bsz * dil, nq),
        in_specs=in_specs,
        out_specs=out_specs,
        out_shape=out_shape,
        compiler_params=_cparams(("parallel", "parallel"), 48),
        name="banded_d%d" % dil,
    )(*args)
    return res


def _merge_kernel(x_ref, oa_ref, ob_ref, oc0_ref, oc1_ref, oc2_ref, l0_ref, l1_ref, l2_ref,
                  od_ref, gpre_ref, wg_ref, wb_ref, wo_ref, gpost_ref, out_ref, tok_scr):
    xf = x_ref[...]
    tm, dm = xf.shape
    h = _rms(xf, gpre_ref[...]).astype(BF16)
    pw = 2 * LANES
    toks = []
    for n, src in enumerate((oc1_ref, l1_ref, oc2_ref, l2_ref)):
        dil = src.shape[1] // pw
        for c in range(dil):
            for hp in range(2):
                col = c * pw + hp * LANES
                tok_scr[2 * n + hp, pl.ds(c, tm // dil, stride=dil), :] = src[:, col:col + LANES]
        toks.append(jnp.concatenate([tok_scr[2 * n], tok_scr[2 * n + 1]], axis=1))
    oc1, l1, oc2, l2 = toks
    l0 = l0_ref[...]
    mx = jnp.maximum(jnp.maximum(l0, l1), l2)
    w0, w1, w2 = jnp.exp(l0 - mx), jnp.exp(l1 - mx), jnp.exp(l2 - mx)
    oc = (w0 * oc0_ref[...] + w1 * oc1 + w2 * oc2) / (w0 + w1 + w2)
    branches = (oa_ref[...], ob_ref[...], oc.astype(BF16), od_ref[...])
    merged = None
    for n, o in enumerate(branches):
        gate = _sigmoid(jnp.dot(h, wg_ref[:, n * dm:(n + 1) * dm], preferred_element_type=F32))
        term = gate * jnp.dot(o, wb_ref[n], preferred_element_type=F32)
        merged = term if merged is None else merged + term
    y = jnp.dot(merged.astype(BF16), wo_ref[...], preferred_element_type=F32)
    out_ref[...] = xf + _rms(y, gpost_ref[...])


def _merge(x2, oa, ob, oc, lc, od, gpre, wg, wb, wo, gpost, seq, tm):
    n, dm = x2.shape
    nt = seq // tm
    row = lambda cols: pl.BlockSpec((tm, cols), lambda i: (i, 0))
    bw = 2 * LANES
    cls = lambda a: pl.BlockSpec((None, tm // (a.shape[2] // bw), a.shape[2]), lambda i: (i // nt, i % nt, 0))
    return pl.pallas_call(
        _merge_kernel,
        grid=(n // tm,),
        in_specs=[row(dm), row(bw), row(bw), row(bw), cls(oc[1]), cls(oc[2]), row(bw), cls(lc[1]), cls(lc[2]),
                  row(bw), _resident((1, dm)), _resident(wg.shape), _resident(wb.shape),
                  _resident(wo.shape), _resident((1, dm))],
        out_specs=row(dm),
        out_shape=jax.ShapeDtypeStruct((n, dm), F32),
        scratch_shapes=[pltpu.VMEM((8, tm, LANES), F32)],
        compiler_params=_cparams(("parallel",), 56),
        name="merge",
    )(x2, oa, ob, oc[0], oc[1], oc[2], lc[0], lc[1], lc[2], od, gpre, wg, wb, wo, gpost)


def _ffn_kernel(x_ref, g_ref, wg_ref, wu_ref, wd_ref, p_ref, gf_ref, wpg_ref, wpp_ref, gp_ref,
                out_ref, h_scr, acc_scr):
    j = pl.program_id(1)

    @pl.when(j == 0)
    def _():
        h_scr[...] = _rms(x_ref[...], g_ref[...]).astype(BF16)
        acc_scr[...] = jnp.zeros_like(acc_scr)

    h = h_scr[...]
    a = jnp.dot(h, wg_ref[...], preferred_element_type=F32)
    u = jnp.dot(h, wu_ref[...], preferred_element_type=F32)
    act = (a * _sigmoid(a) * u).astype(BF16)
    acc_scr[...] += jnp.dot(act, wd_ref[...], preferred_element_type=F32)

    @pl.when(j == pl.num_programs(1) - 1)
    def _():
        _post_math(x_ref[...], acc_scr[...], p_ref, gf_ref, wpg_ref, wpp_ref, gp_ref, out_ref)


def _ffn(x2, gain, wg, wu, wd, p2, gf, wpg, wpp, gp, tm, tf):
    p3, layer = p2
    n, dm = x2.shape
    dff = wg.shape[1]
    return pl.pallas_call(
        _ffn_kernel,
        grid=(n // tm, dff // tf),
        in_specs=[pl.BlockSpec((tm, dm), lambda i, j: (i, 0)),
                  pl.BlockSpec((1, dm), lambda i, j: (0, 0)),
                  pl.BlockSpec((dm, tf), lambda i, j: (0, j)),
                  pl.BlockSpec((dm, tf), lambda i, j: (0, j)),
                  pl.BlockSpec((tf, dm), lambda i, j: (j, 0)),
                  pl.BlockSpec((None, tm, p3.shape[2]), lambda i, j: (layer, i, 0)),
                  _resident((1, dm)), _resident(wpg.shape), _resident(wpp.shape), _resident((1, dm))],
        out_specs=pl.BlockSpec((tm, dm), lambda i, j: (i, 0)),
        out_shape=jax.ShapeDtypeStruct((n, dm), F32),
        scratch_shapes=[pltpu.VMEM((tm, dm), BF16), pltpu.VMEM((tm, dm), F32)],
        compiler_params=_cparams(("parallel", "arbitrary"), 56),
        name="ffn",
    )(x2, gain, wg, wu, wd, p3, gf, wpg, wpp, gp)


def _post_math(xf, f, p_ref, gf_ref, wpg_ref, wpp_ref, gp_ref, out_ref):
    x2 = xf + _rms(f, gf_ref[...])
    gate = _sigmoid(jnp.dot(x2.astype(BF16), wpg_ref[...], preferred_element_type=F32))
    e = jnp.dot(p_ref[...].astype(BF16), wpp_ref[...], preferred_element_type=F32) * gate
    out_ref[...] = x2 + _rms(e, gp_ref[...])


def _router_kernel(x_ref, g_ref, wr_ref, br_ref, hs_ref, info_ref):
    hf = _rms(x_ref[...], g_ref[...])
    hb = hf.astype(BF16)
    tm, dm = hf.shape
    sub = dm // LANES
    for s in range(sub):
        hs_ref[pl.ds(s, tm, stride=sub), :] = hf[:, s * LANES:(s + 1) * LANES]
    hl = (hf - hb.astype(F32)).astype(BF16)
    whi, wlo = wr_ref[0], wr_ref[1]
    logits = (jnp.dot(hb, whi, preferred_element_type=F32) + jnp.dot(hl, whi, preferred_element_type=F32)
              + jnp.dot(hb, wlo, preferred_element_type=F32)) + br_ref[...]
    lane = lax.broadcasted_iota(jnp.int32, logits.shape, 1).astype(F32)
    m1 = jnp.max(logits, axis=-1, keepdims=True)
    i1 = jnp.min(jnp.where(logits == m1, lane, float(LANES)), axis=-1, keepdims=True)
    rest = jnp.where(lane == i1, NEG_INF, logits)
    m2 = jnp.max(rest, axis=-1, keepdims=True)
    i2 = jnp.min(jnp.where(rest == m2, lane, float(LANES)), axis=-1, keepdims=True)
    e2 = jnp.exp(m2 - m1)
    g1 = 1.0 / (1.0 + e2)
    g2 = e2 / (1.0 + e2)
    info = jnp.where(lane == 0.0, i1, jnp.where(lane == 1.0, i2, jnp.where(lane == 2.0, g1, g2)))
    info_ref[...] = info


def _router(x2, gain, wr, br, tm):
    n, dm = x2.shape
    row = lambda cols: pl.BlockSpec((tm, cols), lambda i: (i, 0))
    return pl.pallas_call(
        _router_kernel,
        grid=(n // tm,),
        in_specs=[row(dm), _resident((1, dm)), _resident(wr.shape), _resident((1, LANES))],
        out_specs=[pl.BlockSpec((tm * (dm // LANES), LANES), lambda i: (i, 0)), row(LANES)],
        out_shape=[jax.ShapeDtypeStruct((n * (dm // LANES), LANES), F32),
                   jax.ShapeDtypeStruct((n, LANES), F32)],
        compiler_params=_cparams(("parallel",), 32),
        name="router",
    )(x2, gain, wr, br)


def _experts_kernel(be_ref, src0_ref, srcn_ref, dstp_ref, dstl_ref, hs_hbm, wg_ref, wu_ref, wd_ref,
                    yt_hbm, xg, xb, acc, ys, gsem, ssem, *, tm, sub):
    i = pl.program_id(0)
    j = pl.program_id(1)
    nb = pl.num_programs(0)
    nf = pl.num_programs(1)
    slot = i % 2
    other = 1 - slot

    def gather(idx_ref, r, dslot):
        src = pl.multiple_of(idx_ref[0, r] * sub, sub)
        dst = pl.multiple_of(r * sub, sub)
        return pltpu.make_async_copy(hs_hbm.at[pl.ds(src, sub)], xg.at[dslot, pl.ds(dst, sub)],
                                     gsem.at[dslot])

    def scatter(idx_ref, r, sslot):
        src = pl.multiple_of(r * sub, sub)
        dst = pl.multiple_of(idx_ref[0, r] * sub, sub)
        return pltpu.make_async_copy(ys.at[sslot, pl.ds(src, sub)], yt_hbm.at[pl.ds(dst, sub)],
                                     ssem.at[sslot])

    def wait_gather(dslot):
        pltpu.make_async_copy(hs_hbm.at[pl.ds(0, tm * sub)], xg.at[dslot], gsem.at[dslot]).wait()

    def wait_scatter(sslot):
        pltpu.make_async_copy(ys.at[sslot], yt_hbm.at[pl.ds(0, tm * sub)], ssem.at[sslot]).wait()

    @pl.when((i == 0) & (j == 0))
    def _():
        ys[1] = jnp.zeros(ys.shape[1:], ys.dtype)

        def start(r, c):
            gather(src0_ref, r, 0).start()
            return c

        lax.fori_loop(0, tm, start, 0)

    @pl.when(j == 0)
    def _():
        wait_gather(slot)
        for s in range(sub):
            xb[:, s * LANES:(s + 1) * LANES] = xg[slot, pl.ds(s, tm, stride=sub), :].astype(BF16)
        acc[...] = jnp.zeros_like(acc)

    @pl.when(i < be_ref[nb])
    def _():
        x = xb[...]
        a = jnp.dot(x, wg_ref[...], preferred_element_type=F32)
        u = jnp.dot(x, wu_ref[...], preferred_element_type=F32)
        act = (a * _sigmoid(a) * u).astype(BF16)
        acc[...] += jnp.dot(act, wd_ref[...], preferred_element_type=F32)

    @pl.when(j == 0)
    def _():
        for t in range(tm):
            gather(srcn_ref, t, other).start()

    @pl.when(j == nf - 1)
    def _():
        for t in range(tm):
            scatter(dstp_ref, t, other).start()

    @pl.when(j == nf - 1)
    def _():
        @pl.when(i >= 1)
        def _():
            wait_scatter(slot)

        for s in range(sub):
            ys[slot, pl.ds(s, tm, stride=sub), :] = acc[:, s * LANES:(s + 1) * LANES]

        @pl.when(i == nb - 1)
        def _():
            def start(r, c):
                scatter(dstl_ref, r, slot).start()
                return c

            lax.fori_loop(0, tm, start, 0)
            wait_scatter(slot)
            wait_scatter(other)
            wait_gather(other)


def _experts(hs, blk_e, src_tok, dst_row, n_slabs, wg, wu, wd, tm, tf):
    nb = src_tok.shape[0]
    dm, dff = wg.shape[1], wg.shape[2]
    sub = dm // LANES
    nf = dff // tf
    smem = lambda imap: pl.BlockSpec((None, 1, tm), imap, memory_space=pltpu.SMEM)
    grid_spec = pltpu.PrefetchScalarGridSpec(
        num_scalar_prefetch=1,
        grid=(nb, nf),
        in_specs=[smem(lambda i, j, be: (0, 0, 0)),
                  smem(lambda i, j, be: (jnp.minimum(i + 1, nb - 1), 0, 0)),
                  smem(lambda i, j, be: (i, 0, 0)),
                  smem(lambda i, j, be: (nb, 0, 0)),
                  pl.BlockSpec(memory_space=pl.ANY),
                  pl.BlockSpec((None, dm, tf), lambda i, j, be: (be[i], 0, j)),
                  pl.BlockSpec((None, dm, tf), lambda i, j, be: (be[i], 0, j)),
                  pl.BlockSpec((None, tf, dm), lambda i, j, be: (be[i], j, 0))],
        out_specs=pl.BlockSpec(memory_space=pl.ANY),
        scratch_shapes=[pltpu.VMEM((2, tm * sub, LANES), F32), pltpu.VMEM((tm, dm), BF16),
                        pltpu.VMEM((tm, dm), F32), pltpu.VMEM((2, tm * sub, LANES), F32),
                        pltpu.SemaphoreType.DMA((2,)), pltpu.SemaphoreType.DMA((2,))],
    )
    return pl.pallas_call(
        functools.partial(_experts_kernel, tm=tm, sub=sub),
        grid_spec=grid_spec,
        out_shape=jax.ShapeDtypeStruct((n_slabs * sub, LANES), F32),
        compiler_params=_cparams(("arbitrary", "arbitrary"), 48),
        name="experts",
    )(blk_e, src_tok, src_tok, dst_row, dst_row, hs, wg, wu, wd)


def _combine_kernel(y_ref, x_ref, gt_ref, p_ref, gf_ref, wpg_ref, wpp_ref, gp_ref, out_ref):
    tt, dm = x_ref.shape
    sub = dm // LANES
    gt = gt_ref[...]

    def rows(slot):
        return jnp.concatenate([y_ref[pl.ds(slot * sub + s, tt, stride=TOP_K * sub), :] for s in range(sub)],
                               axis=1)

    f = gt[:, 2:3] * rows(0) + gt[:, 3:4] * rows(1)
    _post_math(x_ref[...], f, p_ref, gf_ref, wpg_ref, wpp_ref, gp_ref, out_ref)


def _combine(yt, info, x2, p2, gf, wpg, wpp, gp, tt):
    p3, layer = p2
    n, dm = x2.shape
    row = lambda cols: pl.BlockSpec((tt, cols), lambda i: (i, 0))
    return pl.pallas_call(
        _combine_kernel,
        grid=(n // tt,),
        in_specs=[pl.BlockSpec((tt * TOP_K * (dm // LANES), LANES), lambda i: (i, 0)),
                  row(dm), row(LANES), pl.BlockSpec((None, tt, p3.shape[2]), lambda i: (layer, i, 0)),
                  _resident((1, dm)),
                  _resident(wpg.shape), _resident(wpp.shape), _resident((1, dm))],
        out_specs=row(dm),
        out_shape=jax.ShapeDtypeStruct((n, dm), F32),
        compiler_params=_cparams(("parallel",), 48),
        name="combine",
    )(yt, x2, info, p3, gf, wpg, wpp, gp)


def _moe(x2, p2, gain, wr, br, wg, wu, wd, gf, wpg, wpp, gp, *, tm_r, tt, tm_e, tf_e):
    n, _ = x2.shape
    hs, info = _router(x2, gain, wr, br, tm_r)
    n_asg = n * TOP_K
    e_flat = jnp.concatenate([info[:, k].astype(jnp.int32) for k in range(TOP_K)])
    slab = jnp.concatenate([jnp.arange(n, dtype=jnp.int32) * TOP_K + k for k in range(TOP_K)])
    onehot = (e_flat[:, None] == jnp.arange(N_EXPERTS, dtype=jnp.int32)[None, :]).astype(jnp.int32)
    csum = jnp.cumsum(onehot, axis=0)
    rank = jnp.sum((csum - onehot) * onehot, axis=1)
    counts = csum[-1]
    padded = ((counts + tm_e - 1) // tm_e) * tm_e
    pend = jnp.cumsum(padded)
    pstart = pend - padded
    dest = (pstart[e_flat] + rank).astype(jnp.int32)
    n_blocks = -(-n_asg // tm_e) + N_EXPERTS
    n_rows = n_blocks * tm_e
    blk_start = jnp.arange(n_blocks, dtype=jnp.int32) * tm_e
    blk_e = jnp.sum((pend[None, :] <= blk_start[:, None]).astype(jnp.int32), axis=1)
    blk_e = jnp.minimum(blk_e, N_EXPERTS - 1)
    blk_e = jnp.concatenate([blk_e, (pend[-1:] // tm_e).astype(jnp.int32)])
    _, by_row = lax.sort_key_val(dest, slab)
    gap = pstart - (jnp.cumsum(counts) - counts)
    spread = jnp.concatenate([jnp.zeros((n_rows,), jnp.int32), by_row, jnp.zeros((n_rows,), jnp.int32)])
    row_id = jnp.arange(n_rows, dtype=jnp.int32)
    asg = jnp.full((n_rows,), -1, jnp.int32)
    for e in range(N_EXPERTS):
        shifted = lax.dynamic_slice(spread, (n_rows - gap[e],), (n_rows,))
        real = (row_id >= pstart[e]) & (row_id < pstart[e] + counts[e])
        asg = jnp.where(real, shifted, asg)
    is_pad = asg < 0
    pad_rank = jnp.cumsum(is_pad.astype(jnp.int32)) - 1
    src_tok = jnp.where(is_pad, 0, asg // TOP_K).reshape(n_blocks, 1, tm_e)
    dst_row = jnp.where(is_pad, n_asg + pad_rank, asg)
    spare = n_asg + (n_rows - n_asg) + jnp.arange(tm_e, dtype=jnp.int32)
    dst_row = jnp.concatenate([spare, dst_row]).reshape(n_blocks + 1, 1, tm_e)
    n_slabs = n_asg + (n_rows - n_asg) + tm_e
    yt = _experts(hs, blk_e, src_tok, dst_row, n_slabs, wg, wu, wd, tm_e, tf_e)
    return _combine(yt, info, x2, p2, gf, wpg, wpp, gp, tt)


def _tables(seq):
    pos = np.arange(seq, dtype=np.int32)

    def cs(p, half):
        inv = np.power(np.float32(ROPE_THETA), -np.arange(half, dtype=np.float32) / np.float32(half))
        ang = p.astype(np.float32)[:, None] * inv[None, :].astype(np.float32)
        return jnp.asarray(np.cos(ang), F32), jnp.asarray(np.sin(ang), F32)

    ones = lambda w: jnp.ones((seq, w), F32)
    zeros = lambda w: jnp.zeros((seq, w), F32)
    (c, s), (cr, sr), (cc, sc), (ca, sa) = (
        cs(pos, HEAD_DIM // 2), cs(pos // GRID_W, HEAD_DIM // 4), cs(pos % GRID_W, HEAD_DIM // 4),
        cs(pos, A_ROPE // 2))
    full = jnp.stack([jnp.tile(jnp.concatenate([c, c], 1), (1, 2)),
                      jnp.tile(jnp.concatenate([-s, s], 1), (1, 2))])
    axial = jnp.stack([jnp.tile(jnp.concatenate([cr, cc, cr, cc], 1), (1, 2)),
                       jnp.tile(jnp.concatenate([-sr, -sc, sr, sc], 1), (1, 2))])
    slot_c = jnp.concatenate([ca, ones(16), ca, ones(16)], 1)
    slot_s = jnp.concatenate([-sa, zeros(16), sa, zeros(16)], 1)
    a_k = jnp.stack([jnp.concatenate([slot_c, ones(HALF)], 1), jnp.concatenate([slot_s, zeros(HALF)], 1)])
    a_q = jnp.stack([jnp.concatenate([ones(HALF), slot_c], 1), jnp.concatenate([zeros(HALF), slot_s], 1)])
    return {"full": full, "axial": axial, "a_k": a_k, "a_q": a_q}


_AXIAL_PERM = tuple(list(range(0, 16)) + list(range(32, 48)) + list(range(16, 32)) + list(range(48, 64)))


def _dup_heads(w, n_heads, perm=None):
    rows = w.shape[0]
    w = w.reshape(rows, n_heads, HEAD_DIM)
    if perm is not None:
        w = w[:, :, perm]
    return jnp.stack([w, w], axis=2).reshape(rows, n_heads * 2 * HEAD_DIM)


def _assemble_w_in(w):
    dm = w.shape[0]
    perm = jnp.array(_AXIAL_PERM, jnp.int32)
    a, b, c, d = w[:, 0:SRC_A], w[:, SRC_A:SRC_B], w[:, SRC_B:SRC_C], w[:, SRC_C:SRC_D]
    z = lambda n: jnp.zeros((dm, n), w.dtype)
    kr = a[:, 384:416]
    a_seg = jnp.concatenate([a[:, 0:384], kr[:, 0:16], z(16), kr[:, 16:32], z(16), z(HALF)], axis=1)
    bq = b[:, 0:256].reshape(dm, 4, HEAD_DIM)[:, :, perm].reshape(dm, 256)
    b_seg = jnp.concatenate([bq, _dup_heads(b[:, 256:384], 2, perm)], axis=1)
    d_seg = jnp.concatenate([d[:, 0:256], _dup_heads(d[:, 256:384], 2), _dup_heads(d[:, 384:512], 2)], axis=1)
    w_vbt = _dup_heads(b[:, 384:512], 2).T.astype(BF16)
    return jnp.concatenate([a_seg, b_seg, c, d_seg], axis=1).astype(BF16), w_vbt


def _assemble_a(w_uq, w_ukv):
    zq = lambda n: jnp.zeros((w_uq.shape[0], n), w_uq.dtype)
    zk = lambda n: jnp.zeros((w_ukv.shape[0], n), w_ukv.dtype)
    dq = A_NOPE + A_ROPE
    q_cols, k_cols, v_cols = [], [], []
    for hh in range(A_HEADS):
        q = w_uq[:, hh * dq:(hh + 1) * dq]
        q_cols += [q[:, 0:A_NOPE], q[:, A_NOPE:A_NOPE + 16], zq(16), q[:, A_NOPE + 16:dq], zq(16)]
        kv = w_ukv[:, hh * (A_NOPE + A_V):(hh + 1) * (A_NOPE + A_V)]
        k_cols += [kv[:, 0:A_NOPE], zk(HALF)]
        v_cols += [kv[:, A_NOPE:A_NOPE + A_V]]
    cat = lambda cols: jnp.concatenate(cols, axis=1).astype(BF16)
    return cat(q_cols), cat(k_cols), cat(v_cols).T


def _gain_pair(g, perm=None):
    if perm is not None:
        g = g[jnp.array(perm, jnp.int32)]
    return jnp.tile(g, 2).reshape(1, LANES).astype(F32)


def kernel(x, p, w_in, a_qa_g, a_kva_g, a_w_uq, a_w_ukv, b_q_g, b_k_g, d_sink, w_branch, w_out,
           mix_pre_g, mix_post_g, ffn_pre_g, ffn_post_g, ffn_w_gate, ffn_w_up, ffn_w_down,
           router_w, router_b, moe_w_gate, moe_w_up, moe_w_down, ple_w_proj, ple_w_gate, ple_post_g):
    bsz, seq, dm = x.shape
    depth = w_in.shape[0]
    n = bsz * seq
    tm = min(512, seq)
    tq_flash = min(512, seq)
    tk_flash = min(1024, seq // 4)
    tabs = _tables(seq)
    row = lambda g: g.reshape(1, -1).astype(F32)
    x2 = x.reshape(n, dm)

    for i in range(depth):
        w_in16 = w_in[i].astype(BF16)
        w_all, wvb = _assemble_w_in(w_in16)
        wuq, wk, wv = _assemble_a(a_w_uq[i], a_w_ukv[i])
        z, qa, ka, vat, vbt, zc1, zc2 = _in_proj(
            x2, row(mix_pre_g[i]), w_all, tabs,
            _gain_pair(b_q_g[i], _AXIAL_PERM), _gain_pair(b_k_g[i], _AXIAL_PERM),
            row(a_qa_g[i]), row(a_kva_g[i]), wuq, wk, wv, wvb, seq, tm)
        z3 = z.reshape(bsz, seq, ZMAIN_COLS)
        o_a = _flash(qa, ka, vat, q_col0=0, k_col0=0, v_row0=0, seq=seq, packed=False,
                     tq=tq_flash, tk=tk_flash)
        o_b = _flash(z, z, vbt, q_col0=ZB, k_col0=ZB + 256, v_row0=0, seq=seq, packed=True,
                     tq=tq_flash, tk=tk_flash)
        o_c, l_c = [], []
        for (win, dil), (src, col0, rc) in zip(C_PATTERNS, ((z3, ZC, ZMAIN_COLS), (zc1, 0, QKV_SEG), (zc2, 0, QKV_SEG))):
            og, lg = _banded(src, src, src, q_col0=col0, k_col0=col0 + 256, v_col0=col0 + 512,
                             row_cols=rc, dil=dil, seq=seq, hw=win // (2 * dil), tq=min(512, seq // dil),
                             want_lse=True, out_dtype=F32)
            o_c.append(og)
            l_c.append(lg)
        o_c[0] = o_c[0].reshape(n, 2 * LANES)
        l_c[0] = l_c[0].reshape(n, 2 * LANES)
        (o_d,) = _banded(z3, z3, z3, q_col0=ZD, k_col0=ZD + 256, v_col0=ZD + 512,
                         row_cols=ZMAIN_COLS, dil=1, seq=seq, hw=D_HALF_WINDOW, tq=min(512, seq),
                         sink=d_sink[i].astype(F32))
        o_d = o_d.reshape(n, 2 * LANES)
        wg_gate = w_in16[:, SRC_D:]
        x2 = _merge(x2, o_a, o_b, o_c, l_c, o_d, row(mix_pre_g[i]), wg_gate,
                    w_branch[i].astype(BF16), w_out[i].astype(BF16), row(mix_post_g[i]), seq, tm)

        p2 = (p.reshape(depth, n, -1), i)
        wpg = ple_w_gate[i].astype(BF16)
        wpp = ple_w_proj[i].astype(BF16)
        j = i // 2
        if i % 2 == 0:
            x2 = _ffn(x2, row(ffn_pre_g[i]), ffn_w_gate[j].astype(BF16), ffn_w_up[j].astype(BF16),
                      ffn_w_down[j].astype(BF16), p2, row(ffn_post_g[i]), wpg, wpp, row(ple_post_g[i]),
                      min(1024, n), 512)
        else:
            wr32 = jnp.zeros((dm, LANES), F32).at[:, :N_EXPERTS].set(router_w[j].astype(F32))
            wr_hi = wr32.astype(BF16)
            wr = jnp.stack([wr_hi, (wr32 - wr_hi.astype(F32)).astype(BF16)])
            br = jnp.full((1, LANES), NEG_INF, F32).at[0, :N_EXPERTS].set(router_b[j].astype(F32))
            x2 = _moe(x2, p2, row(ffn_pre_g[i]), wr, br, moe_w_gate[j].astype(BF16),
                      moe_w_up[j].astype(BF16), moe_w_down[j].astype(BF16), row(ffn_post_g[i]),
                      wpg, wpp, row(ple_post_g[i]), tm_r=tm, tt=min(256, n), tm_e=512, tf_e=1792)
    return x2.reshape(bsz, seq, dm)
```

```python
import functools

import jax
import jax.numpy as jnp
import numpy as np
from jax import lax
from jax.experimental import pallas as pl
from jax.experimental.pallas import tpu as pltpu

F32 = jnp.float32
BF16 = jnp.bfloat16

GRID_W = 64
HEAD_DIM = 64
ROPE_THETA = 10000.0
NORM_EPS = 1e-6
NEG_INF = -1e30
A_HEADS = 4
A_Q_RANK = 256
A_KV_RANK = 128
A_NOPE = 64
A_ROPE = 32
A_V = 64
C_PATTERNS = ((128, 1), (512, 4), (2048, 16))
D_HALF_WINDOW = 128
N_EXPERTS = 8
TOP_K = 2

LANES = 128
HALF = 64
VMEM_MB = 1024 * 1024
LOG2E = 1.4426950408889634
BAND_SUB = 128

PAIR = 2 * LANES
QKV_SEG = 3 * PAIR
SRC_A = A_Q_RANK + A_KV_RANK + A_ROPE
SRC_B = SRC_A + PAIR + 2 * LANES
SRC_C = SRC_B + len(C_PATTERNS) * QKV_SEG
SRC_D = SRC_C + PAIR + 2 * LANES

ZB = 0
ZC = 512
ZD = 512 + QKV_SEG
ZMAIN_COLS = ZD + QKV_SEG
A_SEG = 512
B_SEG = 512


def _cparams(sem, vmem_mb):
    return pltpu.CompilerParams(dimension_semantics=sem, vmem_limit_bytes=vmem_mb * VMEM_MB)


def _resident(shape):
    nd = len(shape)
    return pl.BlockSpec(shape, lambda *_: (0,) * nd, pipeline_mode=pl.Buffered(1))


def _rms(xf, g):
    return xf * lax.rsqrt(jnp.mean(xf * xf, axis=-1, keepdims=True) + NORM_EPS) * g


def _sigmoid(x):
    return 1.0 / (1.0 + jnp.exp(-x))


def _swap32(a):
    lane = lax.broadcasted_iota(jnp.int32, a.shape, 1)
    fwd = pltpu.roll(a, LANES - 32, 1)
    bwd = pltpu.roll(a, 32, 1)
    return jnp.where((lane & 32) == 0, fwd, bwd)


def _rope(a, cos, sin):
    outs = []
    for c in range(a.shape[1] // LANES):
        ch = a[:, c * LANES:(c + 1) * LANES]
        outs.append(ch * cos + _swap32(ch) * sin)
    return outs[0] if len(outs) == 1 else jnp.concatenate(outs, axis=1)


def _head_norm(a, g, bd):
    outs = []
    for c in range(a.shape[1] // LANES):
        ch = a[:, c * LANES:(c + 1) * LANES]
        sq = ch * ch
        hi = sq.astype(BF16)
        lo = (sq - hi.astype(F32)).astype(BF16)
        ms = (jnp.dot(hi, bd, preferred_element_type=F32)
              + jnp.dot(lo, bd, preferred_element_type=F32))
        outs.append(ch * lax.rsqrt(ms + NORM_EPS) * g)
    return outs[0] if len(outs) == 1 else jnp.concatenate(outs, axis=1)


def _in_proj_kernel(x_ref, g_ref, w_ref, tabf_ref, tabx_ref, taba_ref, tabq_ref,
                    bqg_ref, bkg_ref, aqg_ref, akvg_ref, wuq_ref, wk_ref, wv_ref, wvb_ref,
                    z_ref, qa_ref, ka_ref, vat_ref, vbt_ref, zc1_ref, zc2_ref, cls_scr):
    h = _rms(x_ref[...], g_ref[...]).astype(BF16)
    cf, sf = tabf_ref[0], tabf_ref[1]
    cx, sx = tabx_ref[0], tabx_ref[1]
    q_scale = HEAD_DIM ** -0.5

    acc = jnp.dot(h, w_ref[:, 0:A_SEG], preferred_element_type=F32)
    nq = _rms(acc[:, 0:A_Q_RANK], aqg_ref[...]).astype(BF16)
    nkv = _rms(acc[:, A_Q_RANK:A_Q_RANK + A_KV_RANK], akvg_ref[...]).astype(BF16)
    kr = _rope(acc[:, 384:512], taba_ref[0], taba_ref[1]).astype(BF16)
    qa = jnp.dot(nq, wuq_ref[...], preferred_element_type=F32)
    qa = _rope(qa, tabq_ref[0], tabq_ref[1]) * ((A_NOPE + A_ROPE) ** -0.5 * LOG2E)
    qa_ref[...] = qa.astype(qa_ref.dtype)
    r = lax.broadcasted_iota(jnp.int32, (LANES, A_HEADS * LANES), 0)
    c = lax.broadcasted_iota(jnp.int32, (LANES, A_HEADS * LANES), 1)
    place = jnp.where((r < HALF) & ((c & (LANES - 1)) == r + HALF), 1.0, 0.0).astype(BF16)
    ka = (jnp.dot(nkv, wk_ref[...], preferred_element_type=F32)
          + jnp.dot(kr, place, preferred_element_type=F32))
    ka_ref[...] = ka.astype(ka_ref.dtype)
    nt_dims = (((1,), (1,)), ((), ()))
    vat_ref[...] = lax.dot_general(wv_ref[...], nkv, nt_dims, preferred_element_type=F32).astype(vat_ref.dtype)
    vbt_ref[...] = lax.dot_general(wvb_ref[...], h, nt_dims, preferred_element_type=F32).astype(vbt_ref.dtype)

    rr = lax.broadcasted_iota(jnp.int32, (LANES, LANES), 0)
    cc = lax.broadcasted_iota(jnp.int32, (LANES, LANES), 1)
    bd = jnp.where((rr >> 6) == (cc >> 6), 1.0 / HEAD_DIM, 0.0).astype(BF16)
    acc = jnp.dot(h, w_ref[:, A_SEG:A_SEG + B_SEG], preferred_element_type=F32)
    q = _rope(_head_norm(acc[:, 0:256], bqg_ref[...], bd), cx, sx) * (q_scale * LOG2E)
    k = _rope(_head_norm(acc[:, 256:512], bkg_ref[...], bd), cx, sx)
    z_ref[:, ZB:ZB + 256] = q.astype(z_ref.dtype)
    z_ref[:, ZB + 256:ZB + 512] = k.astype(z_ref.dtype)

    tm = h.shape[0]
    for widx, zoff, cls_ref, dil in ((0, ZC, None, 1), (1, 0, zc1_ref, C_PATTERNS[1][1]),
                                     (2, 0, zc2_ref, C_PATTERNS[2][1]), (3, ZD, None, 1)):
        base = A_SEG + B_SEG + widx * QKV_SEG
        acc = jnp.dot(h, w_ref[:, base:base + QKV_SEG], preferred_element_type=F32)
        q = _rope(acc[:, 0:256], cf, sf) * q_scale
        k = _rope(acc[:, 256:512], cf, sf)
        if cls_ref is None:
            z_ref[:, zoff:zoff + 256] = q.astype(z_ref.dtype)
            z_ref[:, zoff + 256:zoff + 512] = k.astype(z_ref.dtype)
            z_ref[:, zoff + 512:zoff + QKV_SEG] = acc[:, 512:QKV_SEG].astype(z_ref.dtype)
        else:
            qkv = (q[:, 0:LANES], q[:, LANES:], k[:, 0:LANES], k[:, LANES:],
                   acc[:, 512:512 + LANES], acc[:, 512 + LANES:QKV_SEG])
            for ch, val in enumerate(qkv):
                cls_scr[ch] = val
            for c in range(dil):
                for ch in range(len(qkv)):
                    col = c * QKV_SEG + ch * LANES
                    cls_ref[:, col:col + LANES] = cls_scr[ch, pl.ds(c, tm // dil, stride=dil), :].astype(cls_ref.dtype)


def _in_proj(x2, gain, w, tabs, bqg, bkg, aqg, akvg, wuq, wk, wv, wvb, seq, tm):
    n, dm = x2.shape
    nt = seq // tm
    bsz = n // seq
    d1, d2 = C_PATTERNS[1][1], C_PATTERNS[2][1]
    tab_spec = pl.BlockSpec((2, tm, LANES), lambda i: (0, i % nt, 0))
    row = lambda cols: pl.BlockSpec((tm, cols), lambda i: (i, 0))
    vt_spec = pl.BlockSpec((None, 2 * LANES, tm), lambda i: (i // nt, 0, i % nt))
    cls_spec = lambda d: pl.BlockSpec((None, tm // d, d * QKV_SEG), lambda i: (i // nt, i % nt, 0))
    return pl.pallas_call(
        _in_proj_kernel,
        grid=(n // tm,),
        in_specs=[row(dm), _resident((1, dm)), _resident(w.shape),
                  tab_spec, tab_spec, tab_spec, tab_spec,
                  _resident((1, LANES)), _resident((1, LANES)),
                  _resident((1, A_Q_RANK)), _resident((1, A_KV_RANK)),
                  _resident(wuq.shape), _resident(wk.shape), _resident(wv.shape), _resident(wvb.shape)],
        out_specs=[row(ZMAIN_COLS), row(A_HEADS * LANES), row(A_HEADS * LANES),
                   vt_spec, vt_spec, cls_spec(d1), cls_spec(d2)],
        out_shape=[jax.ShapeDtypeStruct((n, ZMAIN_COLS), BF16),
                   jax.ShapeDtypeStruct((n, A_HEADS * LANES), BF16),
                   jax.ShapeDtypeStruct((n, A_HEADS * LANES), BF16),
                   jax.ShapeDtypeStruct((bsz, 2 * LANES, seq), BF16),
                   jax.ShapeDtypeStruct((bsz, 2 * LANES, seq), BF16),
                   jax.ShapeDtypeStruct((bsz, seq // d1, d1 * QKV_SEG), BF16),
                   jax.ShapeDtypeStruct((bsz, seq // d2, d2 * QKV_SEG), BF16)],
        scratch_shapes=[pltpu.VMEM((QKV_SEG // LANES, tm, LANES), F32)],
        compiler_params=_cparams(("parallel",), 48),
        name="in_proj",
    )(x2, gain, w, tabs["full"], tabs["axial"], tabs["a_k"], tabs["a_q"],
      bqg, bkg, aqg, akvg, wuq, wk, wv, wvb)


FLASH_SAFE_EXP = 64.0


def _head_queries(q_ref, packed):
    lane = lax.broadcasted_iota(jnp.int32, (1, LANES), 1)
    if packed:
        return [jnp.where((lane < HALF) == (r == 0), q_ref[...], 0) for r in range(2)], [0, 0]
    return [q_ref[:, r * LANES:(r + 1) * LANES] for r in range(2)], [0, LANES]


def _flash_single_pass(q_ref, k_ref, vt_ref, s_a, s_b, *, tk, packed):
    tq = q_ref.shape[0]
    nk = k_ref.shape[0] // tk
    qs, kcs = _head_queries(q_ref, packed)
    qts = [q.astype(F32).T.astype(BF16) for q in qs]
    row = lax.broadcasted_iota(jnp.int32, (LANES, tq), 0)
    klane = lax.broadcasted_iota(jnp.int32, (tk, LANES), 1)
    k_one = jnp.where(klane == 0, 1.0, 0.0).astype(BF16)
    v_one = jnp.ones((16, tk), BF16)

    def scores(c, refs, dst):
        ks = pl.multiple_of(c * tk, tk)
        for r in range(2):
            k = jnp.concatenate([k_ref[pl.ds(ks, tk), kcs[r]:kcs[r] + LANES], k_one], axis=1)
            bias = jnp.where(row == 0, -refs[r], 0.0).astype(BF16)
            qt = jnp.concatenate([qts[r], bias], axis=0)
            dst[r] = jnp.dot(k, qt, preferred_element_type=F32)

    def consume(c, src, refs, sts, first):
        ks = pl.multiple_of(c * tk, tk)
        new = []
        for r in range(2):
            big, base, l, acc, hi, lo = sts[r]
            s = src[r]
            cmax = jnp.max(s, axis=0, keepdims=True)
            p = jnp.exp2(s).astype(BF16)
            lhs = jnp.concatenate([vt_ref[r * HALF:(r + 1) * HALF, pl.ds(ks, tk)], v_one], axis=0)
            pv = jnp.dot(lhs, p, preferred_element_type=F32)
            alpha = jnp.exp2(base - refs[r])
            l = l * alpha + pv[HALF:HALF + 1]
            acc = acc * alpha + pv[0:HALF]
            big = jnp.maximum(big, refs[r] + cmax)
            hi = jnp.maximum(hi, cmax)
            if first:
                lo = jnp.minimum(lo, cmax)
            new.append((big, refs[r], l, acc, hi, lo))
        return tuple(new)

    def ref_of(sts):
        return [st[0].astype(BF16).astype(F32) for st in sts]

    zero = jnp.zeros((1, tq), F32)
    sts = tuple((jnp.full((1, tq), NEG_INF, F32), zero, zero, jnp.zeros((HALF, tq), F32),
                 jnp.full((1, tq), NEG_INF, F32), jnp.full((1, tq), -NEG_INF, F32)) for _ in range(2))
    zeros2 = [zero, zero]
    scores(0, zeros2, s_a)
    scores(1, zeros2, s_b)
    sts = consume(0, s_a, zeros2, sts, True)
    ra = ref_of(sts)
    scores(2, ra, s_a)
    sts = consume(1, s_b, zeros2, sts, True)

    def pair(jj, carry):
        sts, ra = carry
        c0 = 2 * jj
        rb = ref_of(sts)
        scores(c0 + 1, rb, s_b)
        sts = consume(c0, s_a, ra, sts, False)
        ra = ref_of(sts)
        scores(c0 + 2, ra, s_a)
        sts = consume(c0 + 1, s_b, rb, sts, False)
        return sts, ra

    sts, ra = lax.fori_loop(1, nk // 2 - 1, pair, (sts, ra))
    rb = ref_of(sts)
    scores(nk - 1, rb, s_b)
    sts = consume(nk - 2, s_a, ra, sts, False)
    sts = consume(nk - 1, s_b, rb, sts, False)
    outs, bad = [], None
    for r in range(2):
        _, _, l, acc, hi, lo = sts[r]
        outs.append(acc / l)
        b = (hi > FLASH_SAFE_EXP) | (lo < -FLASH_SAFE_EXP) | jnp.logical_not(l > 2.0 ** -FLASH_SAFE_EXP)
        bad = b if bad is None else (bad | b)
    return jnp.concatenate(outs, axis=0), bad


def _flash_kernel(q_ref, k_ref, vt_ref, o_ref, s_a, s_b, *, tk, packed):
    out, bad = _flash_single_pass(q_ref, k_ref, vt_ref, s_a, s_b, tk=tk, packed=packed)
    o_ref[...] = out.T.astype(o_ref.dtype)

    @pl.when(jnp.max(jnp.where(bad, 1.0, 0.0)) > 0.0)
    def _():
        o_ref[...] = _flash_two_pass(q_ref, k_ref, vt_ref, s_a, s_b, tk=tk, packed=packed).T.astype(o_ref.dtype)


def _flash_two_pass(q_ref, k_ref, vt_ref, s_a, s_b, *, tk, packed):
    tq = q_ref.shape[0]
    nk = k_ref.shape[0] // tk
    qs, kcs = _head_queries(q_ref, packed)

    def scores(kk, dst):
        ks = pl.multiple_of(kk * tk, tk)
        for r in range(2):
            k = k_ref[pl.ds(ks, tk), kcs[r]:kcs[r] + LANES]
            dst[r] = lax.dot_general(k, qs[r], (((1,), (1,)), ((), ())), preferred_element_type=F32)

    def consume(kk, src, carry):
        ks = pl.multiple_of(kk * tk, tk)
        new = []
        for r in range(2):
            m, l, acc = carry[r]
            s = src[r]
            m_new = jnp.maximum(m, jnp.max(s, axis=0, keepdims=True))
            alpha = jnp.exp2(m - m_new)
            p = jnp.exp2(s - m_new)
            l = alpha * l + jnp.sum(p, axis=0, keepdims=True)
            vt = vt_ref[r * HALF:(r + 1) * HALF, pl.ds(ks, tk)]
            acc = alpha * acc + jnp.dot(vt, p.astype(BF16), preferred_element_type=F32)
            new.append((m_new, l, acc))
        return tuple(new)

    def pair(jj, carry):
        c0 = 2 * jj
        scores(c0 + 1, s_b)
        carry = consume(c0, s_a, carry)
        scores(c0 + 2, s_a)
        return consume(c0 + 1, s_b, carry)

    init = tuple((jnp.full((1, tq), NEG_INF, F32), jnp.zeros((1, tq), F32), jnp.zeros((HALF, tq), F32))
                 for _ in range(2))
    scores(0, s_a)
    carry = lax.fori_loop(0, nk // 2 - 1, pair, init)
    scores(nk - 1, s_b)
    carry = consume(nk - 2, s_a, carry)
    carry = consume(nk - 1, s_b, carry)
    return jnp.concatenate([acc / l for (_, l, acc) in carry], axis=0)


def _flash(q, k, vt, *, q_col0, k_col0, v_row0, seq, packed, tq, tk):
    n = q.shape[0]
    bsz = n // seq
    qw = LANES if packed else 2 * LANES
    nq = seq // tq
    assert seq % (2 * tk) == 0 and seq // tk >= 4, "the chunk pipeline needs an even number (>= 4) of key chunks"
    k3 = k.reshape(bsz, seq, k.shape[1])
    qb, kb, vb = q_col0 // qw, k_col0 // qw, v_row0 // LANES
    return pl.pallas_call(
        functools.partial(_flash_kernel, tk=tk, packed=packed),
        grid=(bsz, 2, nq),
        in_specs=[pl.BlockSpec((tq, qw), lambda b, j, i: (b * nq + i, qb + j)),
                  pl.BlockSpec((None, seq, qw), lambda b, j, i: (b, 0, kb + j)),
                  pl.BlockSpec((None, LANES, seq), lambda b, j, i: (b, vb + j, 0))],
        out_specs=pl.BlockSpec((tq, LANES), lambda b, j, i: (b * nq + i, j)),
        out_shape=jax.ShapeDtypeStruct((n, 2 * LANES), BF16),
        scratch_shapes=[pltpu.VMEM((2, tk, tq), F32), pltpu.VMEM((2, tk, tq), F32)],
        compiler_params=_cparams(("parallel", "parallel", "parallel"), 48),
        name="flash_packed" if packed else "flash_slots",
    )(q, k3, vt)


def _banded_kernel(*refs, hw, has_sink, want_lse):
    if has_sink:
        sink_ref, q_ref, k_ref, v_ref = refs[:4]
        outs = refs[4:]
    else:
        q_ref, k_ref, v_ref = refs[:3]
        outs = refs[3:]
    o_ref = outs[0]
    tq = q_ref.shape[0]
    length = k_ref.shape[0]
    sb = min(BAND_SUB, tq)
    win = min(sb + 2 * hw, length)
    i = pl.program_id(1)
    lane = lax.broadcasted_iota(jnp.int32, (1, LANES), 1)
    first = lane < HALF
    cols = [slice(j * LANES, (j + 1) * LANES) for j in range(2)]
    subs = list(range(tq // sb))
    kss, valids = [], []
    for u in subs:
        q0 = i * tq + u * sb
        ks = pl.multiple_of(jnp.clip(q0 - hw, 0, length - win), HALF)
        qpos = q0 + lax.broadcasted_iota(jnp.int32, (sb, win), 0)
        kpos = ks + lax.broadcasted_iota(jnp.int32, (sb, win), 1)
        kss.append(ks)
        valids.append(jnp.abs(qpos - kpos) <= hw)
    chains = [(u, j, r) for u in subs for j in range(2) for r in range(2)]
    kws = {(u, j): k_ref[pl.ds(kss[u], win), cols[j]] for u in subs for j in range(2)}
    vws = {(u, j): v_ref[pl.ds(kss[u], win), cols[j]] for u in subs for j in range(2)}
    ss = []
    for u, j, r in chains:
        q = jnp.where(first == (r == 0), q_ref[u * sb:(u + 1) * sb, cols[j]], 0)
        ss.append(lax.dot_general(q, kws[u, j], (((1,), (1,)), ((), ())), preferred_element_type=F32))
    ss = [jnp.where(valids[u], s, NEG_INF) for s, (u, j, r) in zip(ss, chains)]
    ms = [jnp.max(s, axis=-1, keepdims=True) for s in ss]
    if has_sink:
        sinks = [sink_ref[2 * j + r] for u, j, r in chains]
        ms = [jnp.maximum(m, sk) for m, sk in zip(ms, sinks)]
    es = [jnp.exp(s - m) for s, m in zip(ss, ms)]
    ls = [jnp.sum(e, axis=-1, keepdims=True) for e in es]
    if has_sink:
        ls = [l + jnp.exp(sk - m) for l, sk, m in zip(ls, sinks, ms)]
    os_ = [jnp.dot(e.astype(BF16), vws[u, j], preferred_element_type=F32) / l
           for e, l, (u, j, r) in zip(es, ls, chains)]
    for n in range(0, len(chains), 2):
        u, j, _ = chains[n]
        rows = slice(u * sb, (u + 1) * sb)
        o_ref[rows, cols[j]] = jnp.where(first, os_[n], os_[n + 1]).astype(o_ref.dtype)
        if want_lse:
            lses = [jnp.broadcast_to(ms[n + r] + jnp.log(ls[n + r]), (sb, LANES)) for r in range(2)]
            outs[1][rows, cols[j]] = jnp.where(first, lses[0], lses[1])


def _banded(q, k, v, *, q_col0, k_col0, v_col0, row_cols, dil, seq, hw, tq, sink=None,
            want_lse=False, out_dtype=BF16):
    bsz, ls, _ = q.shape
    nq = ls // tq
    pw = 2 * LANES
    rb = row_cols // pw
    qb, kb, vb = q_col0 // pw, k_col0 // pw, v_col0 // pw
    in_specs = [pl.BlockSpec((None, tq, pw), lambda bc, i: (bc // dil, i, (bc % dil) * rb + qb)),
                pl.BlockSpec((None, ls, pw), lambda bc, i: (bc // dil, 0, (bc % dil) * rb + kb)),
                pl.BlockSpec((None, ls, pw), lambda bc, i: (bc // dil, 0, (bc % dil) * rb + vb))]
    args = [q, k, v]
    if sink is not None:
        in_specs = [pl.BlockSpec(memory_space=pltpu.SMEM)] + in_specs
        args = [sink] + args
    o_spec = pl.BlockSpec((None, tq, pw), lambda bc, i: (bc // dil, i, bc % dil))
    out_specs = [o_spec]
    out_shape = [jax.ShapeDtypeStruct((bsz, ls, dil * pw), out_dtype)]
    if want_lse:
        out_specs.append(o_spec)
        out_shape.append(jax.ShapeDtypeStruct((bsz, ls, dil * pw), F32))
    res = pl.pallas_call(
        functools.partial(_banded_kernel, hw=hw, has_sink=sink is not None, want_lse=want_lse),
        grid=(bsz * dil, nq),
        in_specs=in_specs,
        out_specs=out_specs,
        out_shape=out_shape,
        compiler_params=_cparams(("parallel", "parallel"), 48),
        name="banded_d%d" % dil,
    )(*args)
    return res


def _merge_kernel(x_ref, oa_ref, ob_ref, oc0_ref, oc1_ref, oc2_ref, l0_ref, l1_ref, l2_ref,
                  od_ref, gpre_ref, wg_ref, wb_ref, wo_ref, gpost_ref, out_ref, tok_scr):
    xf = x_ref[...]
    tm, dm = xf.shape
    h = _rms(xf, gpre_ref[...]).astype(BF16)
    pw = 2 * LANES
    toks = []
    for n, src in enumerate((oc1_ref, l1_ref, oc2_ref, l2_ref)):
        dil = src.shape[1] // pw
        for c in range(dil):
            for hp in range(2):
                col = c * pw + hp * LANES
                tok_scr[2 * n + hp, pl.ds(c, tm // dil, stride=dil), :] = src[:, col:col + LANES]
        toks.append(jnp.concatenate([tok_scr[2 * n], tok_scr[2 * n + 1]], axis=1))
    oc1, l1, oc2, l2 = toks
    l0 = l0_ref[...]
    mx = jnp.maximum(jnp.maximum(l0, l1), l2)
    w0, w1, w2 = jnp.exp(l0 - mx), jnp.exp(l1 - mx), jnp.exp(l2 - mx)
    oc = (w0 * oc0_ref[...] + w1 * oc1 + w2 * oc2) / (w0 + w1 + w2)
    branches = (oa_ref[...], ob_ref[...], oc.astype(BF16), od_ref[...])
    merged = None
    for n, o in enumerate(branches):
        gate = _sigmoid(jnp.dot(h, wg_ref[:, n * dm:(n + 1) * dm], preferred_element_type=F32))
        term = gate * jnp.dot(o, wb_ref[n], preferred_element_type=F32)
        merged = term if merged is None else merged + term
    y = jnp.dot(merged.astype(BF16), wo_ref[...], preferred_element_type=F32)
    out_ref[...] = xf + _rms(y, gpost_ref[...])


def _merge(x2, oa, ob, oc, lc, od, gpre, wg, wb, wo, gpost, seq, tm):
    n, dm = x2.shape
    nt = seq // tm
    row = lambda cols: pl.BlockSpec((tm, cols), lambda i: (i, 0))
    bw = 2 * LANES
    cls = lambda a: pl.BlockSpec((None, tm // (a.shape[2] // bw), a.shape[2]), lambda i: (i // nt, i % nt, 0))
    return pl.pallas_call(
        _merge_kernel,
        grid=(n // tm,),
        in_specs=[row(dm), row(bw), row(bw), row(bw), cls(oc[1]), cls(oc[2]), row(bw), cls(lc[1]), cls(lc[2]),
                  row(bw), _resident((1, dm)), _resident(wg.shape), _resident(wb.shape),
                  _resident(wo.shape), _resident((1, dm))],
        out_specs=row(dm),
        out_shape=jax.ShapeDtypeStruct((n, dm), F32),
        scratch_shapes=[pltpu.VMEM((8, tm, LANES), F32)],
        compiler_params=_cparams(("parallel",), 56),
        name="merge",
    )(x2, oa, ob, oc[0], oc[1], oc[2], lc[0], lc[1], lc[2], od, gpre, wg, wb, wo, gpost)


def _ffn_kernel(x_ref, g_ref, wg_ref, wu_ref, wd_ref, p_ref, gf_ref, wpg_ref, wpp_ref, gp_ref,
                out_ref, h_scr, acc_scr):
    j = pl.program_id(1)

    @pl.when(j == 0)
    def _():
        h_scr[...] = _rms(x_ref[...], g_ref[...]).astype(BF16)
        acc_scr[...] = jnp.zeros_like(acc_scr)

    h = h_scr[...]
    a = jnp.dot(h, wg_ref[...], preferred_element_type=F32)
    u = jnp.dot(h, wu_ref[...], preferred_element_type=F32)
    act = (a * _sigmoid(a) * u).astype(BF16)
    acc_scr[...] += jnp.dot(act, wd_ref[...], preferred_element_type=F32)

    @pl.when(j == pl.num_programs(1) - 1)
    def _():
        _post_math(x_ref[...], acc_scr[...], p_ref, gf_ref, wpg_ref, wpp_ref, gp_ref, out_ref)


def _ffn(x2, gain, wg, wu, wd, p2, gf, wpg, wpp, gp, tm, tf):
    p3, layer = p2
    n, dm = x2.shape
    dff = wg.shape[1]
    return pl.pallas_call(
        _ffn_kernel,
        grid=(n // tm, dff // tf),
        in_specs=[pl.BlockSpec((tm, dm), lambda i, j: (i, 0)),
                  pl.BlockSpec((1, dm), lambda i, j: (0, 0)),
                  pl.BlockSpec((dm, tf), lambda i, j: (0, j)),
                  pl.BlockSpec((dm, tf), lambda i, j: (0, j)),
                  pl.BlockSpec((tf, dm), lambda i, j: (j, 0)),
                  pl.BlockSpec((None, tm, p3.shape[2]), lambda i, j: (layer, i, 0)),
                  _resident((1, dm)), _resident(wpg.shape), _resident(wpp.shape), _resident((1, dm))],
        out_specs=pl.BlockSpec((tm, dm), lambda i, j: (i, 0)),
        out_shape=jax.ShapeDtypeStruct((n, dm), F32),
        scratch_shapes=[pltpu.VMEM((tm, dm), BF16), pltpu.VMEM((tm, dm), F32)],
        compiler_params=_cparams(("parallel", "arbitrary"), 56),
        name="ffn",
    )(x2, gain, wg, wu, wd, p3, gf, wpg, wpp, gp)


def _post_math(xf, f, p_ref, gf_ref, wpg_ref, wpp_ref, gp_ref, out_ref):
    x2 = xf + _rms(f, gf_ref[...])
    gate = _sigmoid(jnp.dot(x2.astype(BF16), wpg_ref[...], preferred_element_type=F32))
    e = jnp.dot(p_ref[...].astype(BF16), wpp_ref[...], preferred_element_type=F32) * gate
    out_ref[...] = x2 + _rms(e, gp_ref[...])


def _router_kernel(x_ref, g_ref, wr_ref, br_ref, hs_ref, info_ref):
    hf = _rms(x_ref[...], g_ref[...])
    hb = hf.astype(BF16)
    tm, dm = hf.shape
    sub = dm // LANES
    for s in range(sub):
        hs_ref[pl.ds(s, tm, stride=sub), :] = hf[:, s * LANES:(s + 1) * LANES]
    hl = (hf - hb.astype(F32)).astype(BF16)
    whi, wlo = wr_ref[0], wr_ref[1]
    logits = (jnp.dot(hb, whi, preferred_element_type=F32) + jnp.dot(hl, whi, preferred_element_type=F32)
              + jnp.dot(hb, wlo, preferred_element_type=F32)) + br_ref[...]
    lane = lax.broadcasted_iota(jnp.int32, logits.shape, 1).astype(F32)
    m1 = jnp.max(logits, axis=-1, keepdims=True)
    i1 = jnp.min(jnp.where(logits == m1, lane, float(LANES)), axis=-1, keepdims=True)
    rest = jnp.where(lane == i1, NEG_INF, logits)
    m2 = jnp.max(rest, axis=-1, keepdims=True)
    i2 = jnp.min(jnp.where(rest == m2, lane, float(LANES)), axis=-1, keepdims=True)
    e2 = jnp.exp(m2 - m1)
    g1 = 1.0 / (1.0 + e2)
    g2 = e2 / (1.0 + e2)
    info = jnp.where(lane == 0.0, i1, jnp.where(lane == 1.0, i2, jnp.where(lane == 2.0, g1, g2)))
    info_ref[...] = info


def _router(x2, gain, wr, br, tm):
    n, dm = x2.shape
    row = lambda cols: pl.BlockSpec((tm, cols), lambda i: (i, 0))
    return pl.pallas_call(
        _router_kernel,
        grid=(n // tm,),
        in_specs=[row(dm), _resident((1, dm)), _resident(wr.shape), _resident((1, LANES))],
        out_specs=[pl.BlockSpec((tm * (dm // LANES), LANES), lambda i: (i, 0)), row(LANES)],
        out_shape=[jax.ShapeDtypeStruct((n * (dm // LANES), LANES), F32),
                   jax.ShapeDtypeStruct((n, LANES), F32)],
        compiler_params=_cparams(("parallel",), 32),
        name="router",
    )(x2, gain, wr, br)


def _experts_kernel(be_ref, src0_ref, srcn_ref, dstp_ref, dstl_ref, hs_hbm, wg_ref, wu_ref, wd_ref,
                    yt_hbm, xg, xb, acc, ys, gsem, ssem, *, tm, sub):
    i = pl.program_id(0)
    j = pl.program_id(1)
    nb = pl.num_programs(0)
    nf = pl.num_programs(1)
    slot = i % 2
    other = 1 - slot

    def gather(idx_ref, r, dslot):
        src = pl.multiple_of(idx_ref[0, r] * sub, sub)
        dst = pl.multiple_of(r * sub, sub)
        return pltpu.make_async_copy(hs_hbm.at[pl.ds(src, sub)], xg.at[dslot, pl.ds(dst, sub)],
                                     gsem.at[dslot])

    def scatter(idx_ref, r, sslot):
        src = pl.multiple_of(r * sub, sub)
        dst = pl.multiple_of(idx_ref[0, r] * sub, sub)
        return pltpu.make_async_copy(ys.at[sslot, pl.ds(src, sub)], yt_hbm.at[pl.ds(dst, sub)],
                                     ssem.at[sslot])

    def wait_gather(dslot):
        pltpu.make_async_copy(hs_hbm.at[pl.ds(0, tm * sub)], xg.at[dslot], gsem.at[dslot]).wait()

    def wait_scatter(sslot):
        pltpu.make_async_copy(ys.at[sslot], yt_hbm.at[pl.ds(0, tm * sub)], ssem.at[sslot]).wait()

    @pl.when((i == 0) & (j == 0))
    def _():
        ys[1] = jnp.zeros(ys.shape[1:], ys.dtype)

        def start(r, c):
            gather(src0_ref, r, 0).start()
            return c

        lax.fori_loop(0, tm, start, 0)

    @pl.when(j == 0)
    def _():
        wait_gather(slot)
        for s in range(sub):
            xb[:, s * LANES:(s + 1) * LANES] = xg[slot, pl.ds(s, tm, stride=sub), :].astype(BF16)
        acc[...] = jnp.zeros_like(acc)

    @pl.when(i < be_ref[nb])
    def _():
        x = xb[...]
        a = jnp.dot(x, wg_ref[...], preferred_element_type=F32)
        u = jnp.dot(x, wu_ref[...], preferred_element_type=F32)
        act = (a * _sigmoid(a) * u).astype(BF16)
        acc[...] += jnp.dot(act, wd_ref[...], preferred_element_type=F32)

    @pl.when(j == 0)
    def _():
        for t in range(tm):
            gather(srcn_ref, t, other).start(priority=t % 2)

    @pl.when(j == nf - 1)
    def _():
        for t in range(tm):
            scatter(dstp_ref, t, other).start(priority=t % 2)

    @pl.when(j == nf - 1)
    def _():
        @pl.when(i >= 1)
        def _():
            wait_scatter(slot)

        for s in range(sub):
            ys[slot, pl.ds(s, tm, stride=sub), :] = acc[:, s * LANES:(s + 1) * LANES]

        @pl.when(i == nb - 1)
        def _():
            def start(r, c):
                scatter(dstl_ref, r, slot).start()
                return c

            lax.fori_loop(0, tm, start, 0)
            wait_scatter(slot)
            wait_scatter(other)
            wait_gather(other)


def _experts(hs, blk_e, src_tok, dst_row, n_slabs, wg, wu, wd, tm, tf):
    nb = src_tok.shape[0]
    dm, dff = wg.shape[1], wg.shape[2]
    sub = dm // LANES
    nf = dff // tf
    smem = lambda imap: pl.BlockSpec((None, 1, tm), imap, memory_space=pltpu.SMEM)
    grid_spec = pltpu.PrefetchScalarGridSpec(
        num_scalar_prefetch=1,
        grid=(nb, nf),
        in_specs=[smem(lambda i, j, be: (0, 0, 0)),
                  smem(lambda i, j, be: (jnp.minimum(i + 1, nb - 1), 0, 0)),
                  smem(lambda i, j, be: (i, 0, 0)),
                  smem(lambda i, j, be: (nb, 0, 0)),
                  pl.BlockSpec(memory_space=pl.ANY),
                  pl.BlockSpec((None, dm, tf), lambda i, j, be: (be[i], 0, j)),
                  pl.BlockSpec((None, dm, tf), lambda i, j, be: (be[i], 0, j)),
                  pl.BlockSpec((None, tf, dm), lambda i, j, be: (be[i], j, 0))],
        out_specs=pl.BlockSpec(memory_space=pl.ANY),
        scratch_shapes=[pltpu.VMEM((2, tm * sub, LANES), F32), pltpu.VMEM((tm, dm), BF16),
                        pltpu.VMEM((tm, dm), F32), pltpu.VMEM((2, tm * sub, LANES), F32),
                        pltpu.SemaphoreType.DMA((2,)), pltpu.SemaphoreType.DMA((2,))],
    )
    return pl.pallas_call(
        functools.partial(_experts_kernel, tm=tm, sub=sub),
        grid_spec=grid_spec,
        out_shape=jax.ShapeDtypeStruct((n_slabs * sub, LANES), F32),
        compiler_params=_cparams(("arbitrary", "arbitrary"), 48),
        name="experts",
    )(blk_e, src_tok, src_tok, dst_row, dst_row, hs, wg, wu, wd)


def _combine_kernel(y_ref, x_ref, gt_ref, p_ref, gf_ref, wpg_ref, wpp_ref, gp_ref, out_ref):
    tt, dm = x_ref.shape
    sub = dm // LANES
    gt = gt_ref[...]

    def rows(slot):
        return jnp.concatenate([y_ref[pl.ds(slot * sub + s, tt, stride=TOP_K * sub), :] for s in range(sub)],
                               axis=1)

    f = gt[:, 2:3] * rows(0) + gt[:, 3:4] * rows(1)
    _post_math(x_ref[...], f, p_ref, gf_ref, wpg_ref, wpp_ref, gp_ref, out_ref)


def _combine(yt, info, x2, p2, gf, wpg, wpp, gp, tt):
    p3, layer = p2
    n, dm = x2.shape
    row = lambda cols: pl.BlockSpec((tt, cols), lambda i: (i, 0))
    return pl.pallas_call(
        _combine_kernel,
        grid=(n // tt,),
        in_specs=[pl.BlockSpec((tt * TOP_K * (dm // LANES), LANES), lambda i: (i, 0)),
                  row(dm), row(LANES), pl.BlockSpec((None, tt, p3.shape[2]), lambda i: (layer, i, 0)),
                  _resident((1, dm)),
                  _resident(wpg.shape), _resident(wpp.shape), _resident((1, dm))],
        out_specs=row(dm),
        out_shape=jax.ShapeDtypeStruct((n, dm), F32),
        compiler_params=_cparams(("parallel",), 48),
        name="combine",
    )(yt, x2, info, p3, gf, wpg, wpp, gp)


def _moe(x2, p2, gain, wr, br, wg, wu, wd, gf, wpg, wpp, gp, *, tm_r, tt, tm_e, tf_e):
    n, _ = x2.shape
    hs, info = _router(x2, gain, wr, br, tm_r)
    n_asg = n * TOP_K
    e_flat = jnp.concatenate([info[:, k].astype(jnp.int32) for k in range(TOP_K)])
    slab = jnp.concatenate([jnp.arange(n, dtype=jnp.int32) * TOP_K + k for k in range(TOP_K)])
    onehot = (e_flat[:, None] == jnp.arange(N_EXPERTS, dtype=jnp.int32)[None, :]).astype(jnp.int32)
    csum = jnp.cumsum(onehot, axis=0)
    rank = jnp.sum((csum - onehot) * onehot, axis=1)
    counts = csum[-1]
    padded = ((counts + tm_e - 1) // tm_e) * tm_e
    pend = jnp.cumsum(padded)
    pstart = pend - padded
    dest = (pstart[e_flat] + rank).astype(jnp.int32)
    n_blocks = -(-n_asg // tm_e) + N_EXPERTS
    n_rows = n_blocks * tm_e
    blk_start = jnp.arange(n_blocks, dtype=jnp.int32) * tm_e
    blk_e = jnp.sum((pend[None, :] <= blk_start[:, None]).astype(jnp.int32), axis=1)
    blk_e = jnp.minimum(blk_e, N_EXPERTS - 1)
    blk_e = jnp.concatenate([blk_e, (pend[-1:] // tm_e).astype(jnp.int32)])
    _, by_row = lax.sort_key_val(dest, slab)
    gap = pstart - (jnp.cumsum(counts) - counts)
    spread = jnp.concatenate([jnp.zeros((n_rows,), jnp.int32), by_row, jnp.zeros((n_rows,), jnp.int32)])
    row_id = jnp.arange(n_rows, dtype=jnp.int32)
    asg = jnp.full((n_rows,), -1, jnp.int32)
    for e in range(N_EXPERTS):
        shifted = lax.dynamic_slice(spread, (n_rows - gap[e],), (n_rows,))
        real = (row_id >= pstart[e]) & (row_id < pstart[e] + counts[e])
        asg = jnp.where(real, shifted, asg)
    is_pad = asg < 0
    pad_rank = jnp.cumsum(is_pad.astype(jnp.int32)) - 1
    src_tok = jnp.where(is_pad, 0, asg // TOP_K).reshape(n_blocks, 1, tm_e)
    dst_row = jnp.where(is_pad, n_asg + pad_rank, asg)
    spare = n_asg + (n_rows - n_asg) + jnp.arange(tm_e, dtype=jnp.int32)
    dst_row = jnp.concatenate([spare, dst_row]).reshape(n_blocks + 1, 1, tm_e)
    n_slabs = n_asg + (n_rows - n_asg) + tm_e
    yt = _experts(hs, blk_e, src_tok, dst_row, n_slabs, wg, wu, wd, tm_e, tf_e)
    return _combine(yt, info, x2, p2, gf, wpg, wpp, gp, tt)


def _tables(seq):
    pos = np.arange(seq, dtype=np.int32)

    def cs(p, half):
        inv = np.power(np.float32(ROPE_THETA), -np.arange(half, dtype=np.float32) / np.float32(half))
        ang = p.astype(np.float32)[:, None] * inv[None, :].astype(np.float32)
        return jnp.asarray(np.cos(ang), F32), jnp.asarray(np.sin(ang), F32)

    ones = lambda w: jnp.ones((seq, w), F32)
    zeros = lambda w: jnp.zeros((seq, w), F32)
    (c, s), (cr, sr), (cc, sc), (ca, sa) = (
        cs(pos, HEAD_DIM // 2), cs(pos // GRID_W, HEAD_DIM // 4), cs(pos % GRID_W, HEAD_DIM // 4),
        cs(pos, A_ROPE // 2))
    full = jnp.stack([jnp.tile(jnp.concatenate([c, c], 1), (1, 2)),
                      jnp.tile(jnp.concatenate([-s, s], 1), (1, 2))])
    axial = jnp.stack([jnp.tile(jnp.concatenate([cr, cc, cr, cc], 1), (1, 2)),
                       jnp.tile(jnp.concatenate([-sr, -sc, sr, sc], 1), (1, 2))])
    slot_c = jnp.concatenate([ca, ones(16), ca, ones(16)], 1)
    slot_s = jnp.concatenate([-sa, zeros(16), sa, zeros(16)], 1)
    a_k = jnp.stack([jnp.concatenate([slot_c, ones(HALF)], 1), jnp.concatenate([slot_s, zeros(HALF)], 1)])
    a_q = jnp.stack([jnp.concatenate([ones(HALF), slot_c], 1), jnp.concatenate([zeros(HALF), slot_s], 1)])
    return {"full": full, "axial": axial, "a_k": a_k, "a_q": a_q}


_AXIAL_PERM = tuple(list(range(0, 16)) + list(range(32, 48)) + list(range(16, 32)) + list(range(48, 64)))


def _dup_heads(w, n_heads, perm=None):
    rows = w.shape[0]
    w = w.reshape(rows, n_heads, HEAD_DIM)
    if perm is not None:
        w = w[:, :, perm]
    return jnp.stack([w, w], axis=2).reshape(rows, n_heads * 2 * HEAD_DIM)


def _assemble_w_in(w):
    dm = w.shape[0]
    perm = jnp.array(_AXIAL_PERM, jnp.int32)
    a, b, c, d = w[:, 0:SRC_A], w[:, SRC_A:SRC_B], w[:, SRC_B:SRC_C], w[:, SRC_C:SRC_D]
    z = lambda n: jnp.zeros((dm, n), w.dtype)
    kr = a[:, 384:416]
    a_seg = jnp.concatenate([a[:, 0:384], kr[:, 0:16], z(16), kr[:, 16:32], z(16), z(HALF)], axis=1)
    bq = b[:, 0:256].reshape(dm, 4, HEAD_DIM)[:, :, perm].reshape(dm, 256)
    b_seg = jnp.concatenate([bq, _dup_heads(b[:, 256:384], 2, perm)], axis=1)
    d_seg = jnp.concatenate([d[:, 0:256], _dup_heads(d[:, 256:384], 2), _dup_heads(d[:, 384:512], 2)], axis=1)
    w_vbt = _dup_heads(b[:, 384:512], 2).T.astype(BF16)
    return jnp.concatenate([a_seg, b_seg, c, d_seg], axis=1).astype(BF16), w_vbt


def _assemble_a(w_uq, w_ukv):
    zq = lambda n: jnp.zeros((w_uq.shape[0], n), w_uq.dtype)
    zk = lambda n: jnp.zeros((w_ukv.shape[0], n), w_ukv.dtype)
    dq = A_NOPE + A_ROPE
    q_cols, k_cols, v_cols = [], [], []
    for hh in range(A_HEADS):
        q = w_uq[:, hh * dq:(hh + 1) * dq]
        q_cols += [q[:, 0:A_NOPE], q[:, A_NOPE:A_NOPE + 16], zq(16), q[:, A_NOPE + 16:dq], zq(16)]
        kv = w_ukv[:, hh * (A_NOPE + A_V):(hh + 1) * (A_NOPE + A_V)]
        k_cols += [kv[:, 0:A_NOPE], zk(HALF)]
        v_cols += [kv[:, A_NOPE:A_NOPE + A_V]]
    cat = lambda cols: jnp.concatenate(cols, axis=1).astype(BF16)
    return cat(q_cols), cat(k_cols), cat(v_cols).T


def _gain_pair(g, perm=None):
    if perm is not None:
        g = g[jnp.array(perm, jnp.int32)]
    return jnp.tile(g, 2).reshape(1, LANES).astype(F32)


def kernel(x, p, w_in, a_qa_g, a_kva_g, a_w_uq, a_w_ukv, b_q_g, b_k_g, d_sink, w_branch, w_out,
           mix_pre_g, mix_post_g, ffn_pre_g, ffn_post_g, ffn_w_gate, ffn_w_up, ffn_w_down,
           router_w, router_b, moe_w_gate, moe_w_up, moe_w_down, ple_w_proj, ple_w_gate, ple_post_g):
    bsz, seq, dm = x.shape
    depth = w_in.shape[0]
    n = bsz * seq
    tm = min(512, seq)
    tq_flash = min(512, seq)
    tk_flash = min(1024, seq // 4)
    tabs = _tables(seq)
    row = lambda g: g.reshape(1, -1).astype(F32)
    x2 = x.reshape(n, dm)

    for i in range(depth):
        w_in16 = w_in[i].astype(BF16)
        w_all, wvb = _assemble_w_in(w_in16)
        wuq, wk, wv = _assemble_a(a_w_uq[i], a_w_ukv[i])
        z, qa, ka, vat, vbt, zc1, zc2 = _in_proj(
            x2, row(mix_pre_g[i]), w_all, tabs,
            _gain_pair(b_q_g[i], _AXIAL_PERM), _gain_pair(b_k_g[i], _AXIAL_PERM),
            row(a_qa_g[i]), row(a_kva_g[i]), wuq, wk, wv, wvb, seq, tm)
        z3 = z.reshape(bsz, seq, ZMAIN_COLS)
        o_a = _flash(qa, ka, vat, q_col0=0, k_col0=0, v_row0=0, seq=seq, packed=False,
                     tq=tq_flash, tk=tk_flash)
        o_b = _flash(z, z, vbt, q_col0=ZB, k_col0=ZB + 256, v_row0=0, seq=seq, packed=True,
                     tq=tq_flash, tk=tk_flash)
        o_c, l_c = [], []
        for (win, dil), (src, col0, rc) in zip(C_PATTERNS, ((z3, ZC, ZMAIN_COLS), (zc1, 0, QKV_SEG), (zc2, 0, QKV_SEG))):
            og, lg = _banded(src, src, src, q_col0=col0, k_col0=col0 + 256, v_col0=col0 + 512,
                             row_cols=rc, dil=dil, seq=seq, hw=win // (2 * dil), tq=min(512, seq // dil),
                             want_lse=True, out_dtype=F32)
            o_c.append(og)
            l_c.append(lg)
        o_c[0] = o_c[0].reshape(n, 2 * LANES)
        l_c[0] = l_c[0].reshape(n, 2 * LANES)
        (o_d,) = _banded(z3, z3, z3, q_col0=ZD, k_col0=ZD + 256, v_col0=ZD + 512,
                         row_cols=ZMAIN_COLS, dil=1, seq=seq, hw=D_HALF_WINDOW, tq=min(512, seq),
                         sink=d_sink[i].astype(F32))
        o_d = o_d.reshape(n, 2 * LANES)
        wg_gate = w_in16[:, SRC_D:]
        x2 = _merge(x2, o_a, o_b, o_c, l_c, o_d, row(mix_pre_g[i]), wg_gate,
                    w_branch[i].astype(BF16), w_out[i].astype(BF16), row(mix_post_g[i]), seq, tm)

        p2 = (p.reshape(depth, n, -1), i)
        wpg = ple_w_gate[i].astype(BF16)
        wpp = ple_w_proj[i].astype(BF16)
        j = i // 2
        if i % 2 == 0:
            x2 = _ffn(x2, row(ffn_pre_g[i]), ffn_w_gate[j].astype(BF16), ffn_w_up[j].astype(BF16),
                      ffn_w_down[j].astype(BF16), p2, row(ffn_post_g[i]), wpg, wpp, row(ple_post_g[i]),
                      min(1024, n), 512)
        else:
            wr32 = jnp.zeros((dm, LANES), F32).at[:, :N_EXPERTS].set(router_w[j].astype(F32))
            wr_hi = wr32.astype(BF16)
            wr = jnp.stack([wr_hi, (wr32 - wr_hi.astype(F32)).astype(BF16)])
            br = jnp.full((1, LANES), NEG_INF, F32).at[0, :N_EXPERTS].set(router_b[j].astype(F32))
            x2 = _moe(x2, p2, row(ffn_pre_g[i]), wr, br, moe_w_gate[j].astype(BF16),
                      moe_w_up[j].astype(BF16), moe_w_down[j].astype(BF16), row(ffn_post_g[i]),
                      wpg, wpp, row(ple_post_g[i]), tm_r=tm, tt=min(256, n), tm_e=512, tf_e=1792)
    return x2.reshape(bsz, seq, dm)
```
